```python
import math
import jax, jax.numpy as jnp
from jax import lax
import numpy as np

D_MODEL = 2048
BATCH = 8
SEQ = 8192
DEPTH = 2

HEAD_DIM = 128
DIL_GROUPS = ((128, 1), (512, 4), (2048, 16))
N_GROUPS = 3
HEADS_PER_GROUP = 4
DIL_WIDTH = N_GROUPS * HEADS_PER_GROUP * HEAD_DIM
DIL_OUT = HEADS_PER_GROUP * HEAD_DIM
SB_HEADS = 8
SB_WIDTH = SB_HEADS * HEAD_DIM
N_BRANCHES = 2
N_IN = 3 * DIL_WIDTH + 3 * SB_WIDTH + N_BRANCHES * D_MODEL
D_FF = 4 * D_MODEL
ROPE_THETA = 10000.0
BLOCK = 128
EPS = 1e-6

kernel_name = "hybrid_dilated_stickbreak_block"


def rms_norm(x, g):
    xf = x.astype(jnp.float32)
    y = xf * lax.rsqrt(jnp.mean(xf * xf, axis=-1, keepdims=True) + EPS)
    return (y * g.astype(jnp.float32)).astype(x.dtype)


def rotary(x):
    s = x.shape[1]
    half = HEAD_DIM // 2
    inv_freq = ROPE_THETA ** (-jnp.arange(half, dtype=jnp.float32) / half)
    ang = jnp.arange(s, dtype=jnp.float32)[:, None] * inv_freq[None, :]
    cos = jnp.cos(ang)[None, :, None, :]
    sin = jnp.sin(ang)[None, :, None, :]
    xf = x.astype(jnp.float32)
    x1, x2 = xf[..., :half], xf[..., half:]
    out = jnp.concatenate([x1 * cos - x2 * sin, x2 * cos + x1 * sin], axis=-1)
    return out.astype(x.dtype)


def dilated_window_attention(q, k, v, window, dilation):
    b, s, h, d = q.shape
    span = window // dilation
    length = s // dilation
    nb = -(-length // BLOCK)
    lp = nb * BLOCK

    def to_sub(t):
        t = t.reshape(b, length, dilation, h, d).transpose(0, 2, 3, 1, 4)
        t = jnp.pad(t, ((0, 0), (0, 0), (0, 0), (0, lp - length), (0, 0)))
        return t.reshape(b, dilation, h, nb, BLOCK, d)

    def with_prev(t):
        prev = jnp.pad(t[:, :, :, :-1], ((0, 0), (0, 0), (0, 0), (1, 0), (0, 0), (0, 0)))
        return jnp.concatenate([prev, t], axis=4)

    qb = to_sub(q)
    kw = with_prev(to_sub(k))
    vw = with_prev(to_sub(v))
    scores = jnp.einsum('brhnqd,brhnkd->brhnqk', qb, kw).astype(jnp.float32) / math.sqrt(d)
    blk = jnp.arange(nb)
    qi = blk[:, None, None] * BLOCK + jnp.arange(BLOCK)[None, :, None]
    ki = (blk[:, None, None] - 1) * BLOCK + jnp.arange(2 * BLOCK)[None, None, :]
    off = qi - ki
    valid = (off >= 0) & (off <= span) & (ki >= 0)
    scores = jnp.where(valid, scores, -jnp.inf)
    m = jnp.max(scores, axis=-1, keepdims=True)
    p = jnp.exp(scores - m)
    l = jnp.sum(p, axis=-1, keepdims=True)
    o = jnp.einsum('brhnqk,brhnkd->brhnqd', (p / l).astype(v.dtype), vw)
    log_den = (m + jnp.log(l))[..., 0]
    o = o.reshape(b, dilation, h, lp, d)[:, :, :, :length]
    o = o.transpose(0, 3, 1, 2, 4).reshape(b, s, h, d)
    log_den = log_den.reshape(b, dilation, h, lp)[..., :length]
    log_den = log_den.transpose(0, 3, 1, 2).reshape(b, s, h)
    return o, log_den


def stick_breaking_attention(q, k, v):
    b, s, h, d = q.shape
    nb = s // BLOCK
    qb = q.reshape(b, nb, BLOCK, h, d).transpose(1, 0, 3, 2, 4)
    kt = k.transpose(0, 2, 1, 3)
    vt = v.transpose(0, 2, 1, 3)
    key_pos = jnp.arange(s)

    def block(args):
        q_blk, start = args
        z = jnp.einsum('bhqd,bhkd->bhqk', q_blk, kt).astype(jnp.float32) / math.sqrt(d)
        qpos = start + jnp.arange(BLOCK)
        mask = key_pos[None, :] < qpos[:, None]
        log_keep = jnp.where(mask, -jax.nn.softplus(z), 0.0)
        between = lax.cumsum(log_keep, axis=3, reverse=True) - log_keep
        a = jnp.where(mask, jnp.exp(jax.nn.log_sigmoid(z) + between), 0.0)
        return jnp.einsum('bhqk,bhkd->bhqd', a.astype(vt.dtype), vt)

    starts = jnp.arange(nb) * BLOCK
    o = lax.map(block, (qb, starts))
    return o.transpose(1, 0, 3, 2, 4).reshape(b, s, h, d)


def _fwd_setup_inputs(seed: int = 0) -> dict:
    key = jax.random.key(seed)
    ks = jax.random.split(key, 13)
    f32 = jnp.float32

    def nrm(k, shape, fan_in):
        return jax.random.normal(k, shape, f32) * (fan_in ** -0.5)

    def gain(k, shape):
        return 1.0 + 0.02 * jax.random.normal(k, shape, f32)

    return {
        "x": jax.random.normal(ks[0], (BATCH, SEQ, D_MODEL), f32),
        "norm1_g": gain(ks[1], (DEPTH, D_MODEL)),
        "w_in": nrm(ks[2], (DEPTH, D_MODEL, N_IN), D_MODEL),
        "q_norm_g": gain(ks[3], (DEPTH, N_GROUPS, HEAD_DIM)),
        "k_norm_g": gain(ks[4], (DEPTH, N_GROUPS, HEAD_DIM)),
        "w_up_dil": nrm(ks[5], (DEPTH, DIL_OUT, D_MODEL), DIL_OUT),
        "w_up_sb": nrm(ks[6], (DEPTH, SB_WIDTH, D_MODEL), SB_WIDTH),
        "gate_b": 0.01 * jax.random.normal(ks[7], (DEPTH, N_BRANCHES, D_MODEL), f32),
        "w_out": nrm(ks[8], (DEPTH, D_MODEL, D_MODEL), D_MODEL),
        "norm2_g": gain(ks[9], (DEPTH, D_MODEL)),
        "w_ff1": nrm(ks[10], (DEPTH, D_MODEL, D_FF), D_MODEL),
        "w_ff2": nrm(ks[11], (DEPTH, D_FF, D_MODEL), D_FF),
    }


def _fwd_reference(x, norm1_g, w_in, q_norm_g, k_norm_g, w_up_dil, w_up_sb, gate_b, w_out, norm2_g, w_ff1, w_ff2):
    b, s, _ = x.shape
    cuts = [DIL_WIDTH, 2 * DIL_WIDTH, 3 * DIL_WIDTH,
            3 * DIL_WIDTH + SB_WIDTH, 3 * DIL_WIDTH + 2 * SB_WIDTH, 3 * DIL_WIDTH + 3 * SB_WIDTH]
    for layer in range(DEPTH):
        h = rms_norm(x, norm1_g[layer])
        proj = h @ w_in[layer]
        q_d, k_d, v_d, q_s, k_s, v_s, g_pre = jnp.split(proj, cuts, axis=-1)
        q_d = q_d.reshape(b, s, N_GROUPS, HEADS_PER_GROUP, HEAD_DIM)
        k_d = k_d.reshape(b, s, N_GROUPS, HEADS_PER_GROUP, HEAD_DIM)
        v_d = v_d.reshape(b, s, N_GROUPS, HEADS_PER_GROUP, HEAD_DIM)

        outs, dens = [], []
        for g, (window, dilation) in enumerate(DIL_GROUPS):
            qg = rotary(rms_norm(q_d[:, :, g], q_norm_g[layer, g]))
            kg = rotary(rms_norm(k_d[:, :, g], k_norm_g[layer, g]))
            o_g, den_g = dilated_window_attention(qg, kg, v_d[:, :, g], window, dilation)
            outs.append(o_g)
            dens.append(den_g)
        wts = jax.nn.softmax(jnp.stack(dens, axis=0), axis=0)
        y_dil = jnp.sum(wts[..., None] * jnp.stack(outs, axis=0).astype(jnp.float32), axis=0)
        y_dil = y_dil.astype(x.dtype).reshape(b, s, DIL_OUT)

        y_sb = stick_breaking_attention(
            q_s.reshape(b, s, SB_HEADS, HEAD_DIM),
            k_s.reshape(b, s, SB_HEADS, HEAD_DIM),
            v_s.reshape(b, s, SB_HEADS, HEAD_DIM),
        ).reshape(b, s, SB_WIDTH)

        gates = jax.nn.sigmoid(g_pre.reshape(b, s, N_BRANCHES, D_MODEL) + gate_b[layer])
        mixed = gates[:, :, 0] * (y_dil @ w_up_dil[layer]) + gates[:, :, 1] * (y_sb @ w_up_sb[layer])
        x = x + mixed @ w_out[layer]

        h2 = rms_norm(x, norm2_g[layer])
        x = x + jnp.square(jax.nn.relu(h2 @ w_ff1[layer])) @ w_ff2[layer]
    return x


import jax as _jax
import jax.numpy as _jnp

TWIN_FORMAT = 'train_step'
FWD_PARAMS = ['x', 'norm1_g', 'w_in', 'q_norm_g', 'k_norm_g', 'w_up_dil', 'w_up_sb', 'gate_b', 'w_out', 'norm2_g', 'w_ff1', 'w_ff2']
TWIN_WEIGHTS = ['norm1_g', 'w_in', 'q_norm_g', 'k_norm_g', 'w_up_dil', 'w_up_sb', 'gate_b', 'w_out', 'norm2_g', 'w_ff1', 'w_ff2']
TWIN_DIFF_INPUT = 'x'
TWIN_INPUTS = ['x', 'norm1_g', 'w_in', 'q_norm_g', 'k_norm_g', 'w_up_dil', 'w_up_sb', 'gate_b', 'w_out', 'norm2_g', 'w_ff1', 'w_ff2', 'loss_target', 'm_norm1_g', 'm_w_in', 'm_q_norm_g', 'm_k_norm_g', 'm_w_up_dil', 'm_w_up_sb', 'm_gate_b', 'm_w_out', 'm_norm2_g', 'm_w_ff1', 'm_w_ff2', 'v_norm1_g', 'v_w_in', 'v_q_norm_g', 'v_k_norm_g', 'v_w_up_dil', 'v_w_up_sb', 'v_gate_b', 'v_w_out', 'v_norm2_g', 'v_w_ff1', 'v_w_ff2']
TWIN_OUTPUTS = ['loss', 'grad_x', 'grad_norm1_g', 'grad_w_in', 'grad_q_norm_g', 'grad_k_norm_g', 'grad_w_up_dil', 'grad_w_up_sb', 'grad_gate_b', 'grad_w_out', 'grad_norm2_g', 'grad_w_ff1', 'grad_w_ff2', 'delta_norm1_g', 'delta_w_in', 'delta_q_norm_g', 'delta_k_norm_g', 'delta_w_up_dil', 'delta_w_up_sb', 'delta_gate_b', 'delta_w_out', 'delta_norm2_g', 'delta_w_ff1', 'delta_w_ff2', 'new_m_norm1_g', 'new_m_w_in', 'new_m_q_norm_g', 'new_m_k_norm_g', 'new_m_w_up_dil', 'new_m_w_up_sb', 'new_m_gate_b', 'new_m_w_out', 'new_m_norm2_g', 'new_m_w_ff1', 'new_m_w_ff2', 'new_v_norm1_g', 'new_v_w_in', 'new_v_q_norm_g', 'new_v_k_norm_g', 'new_v_w_up_dil', 'new_v_w_up_sb', 'new_v_gate_b', 'new_v_w_out', 'new_v_norm2_g', 'new_v_w_ff1', 'new_v_w_ff2']
TWIN_LEAF_KINDS = {'loss': 'loss', 'grad_x': 'grad_x', 'grad_norm1_g': 'grad_w', 'grad_w_in': 'grad_w', 'grad_q_norm_g': 'grad_w', 'grad_k_norm_g': 'grad_w', 'grad_w_up_dil': 'grad_w', 'grad_w_up_sb': 'grad_w', 'grad_gate_b': 'grad_w', 'grad_w_out': 'grad_w', 'grad_norm2_g': 'grad_w', 'grad_w_ff1': 'grad_w', 'grad_w_ff2': 'grad_w', 'delta_norm1_g': 'delta_w', 'delta_w_in': 'delta_w', 'delta_q_norm_g': 'delta_w', 'delta_k_norm_g': 'delta_w', 'delta_w_up_dil': 'delta_w', 'delta_w_up_sb': 'delta_w', 'delta_gate_b': 'delta_w', 'delta_w_out': 'delta_w', 'delta_norm2_g': 'delta_w', 'delta_w_ff1': 'delta_w', 'delta_w_ff2': 'delta_w', 'new_m_norm1_g': 'new_m', 'new_m_w_in': 'new_m', 'new_m_q_norm_g': 'new_m', 'new_m_k_norm_g': 'new_m', 'new_m_w_up_dil': 'new_m', 'new_m_w_up_sb': 'new_m', 'new_m_gate_b': 'new_m', 'new_m_w_out': 'new_m', 'new_m_norm2_g': 'new_m', 'new_m_w_ff1': 'new_m', 'new_m_w_ff2': 'new_m', 'new_v_norm1_g': 'new_v', 'new_v_w_in': 'new_v', 'new_v_q_norm_g': 'new_v', 'new_v_k_norm_g': 'new_v', 'new_v_w_up_dil': 'new_v', 'new_v_w_up_sb': 'new_v', 'new_v_gate_b': 'new_v', 'new_v_w_out': 'new_v', 'new_v_norm2_g': 'new_v', 'new_v_w_ff1': 'new_v', 'new_v_w_ff2': 'new_v'}


def _forward(args):
    return _fwd_reference(*[args[k] for k in FWD_PARAMS])


def _output_shape():
    def fwd():
        inp = _fwd_setup_inputs(0)
        return _fwd_reference(*[inp[k] for k in FWD_PARAMS])
    out = _jax.eval_shape(fwd)
    return out.shape, out.dtype

N_MICROBATCH = 1
ADAM_LR = 0.001
ADAM_B1 = 0.9
ADAM_B2 = 0.999
ADAM_EPS = 1e-08
ADAM_WD = 0.01
ADAM_STEP = 10
PER_EXAMPLE_BATCH_AXIS = {'x': 0, 'loss_target': 0}
SHARED_INPUTS = []
_WEIGHT_DTYPES = {'norm1_g': _jnp.float32, 'w_in': _jnp.float32, 'q_norm_g': _jnp.float32, 'k_norm_g': _jnp.float32, 'w_up_dil': _jnp.float32, 'w_up_sb': _jnp.float32, 'gate_b': _jnp.float32, 'w_out': _jnp.float32, 'norm2_g': _jnp.float32, 'w_ff1': _jnp.float32, 'w_ff2': _jnp.float32}
MOMENT_SCALE = {'norm1_g': 7.273203e+00, 'w_in': 1.780653e+00, 'q_norm_g': 3.656104e-01, 'k_norm_g': 3.711147e-01, 'w_up_dil': 1.830699e+00, 'w_up_sb': 3.603300e+00, 'gate_b': 1.545453e+00, 'w_out': 3.844902e+00, 'norm2_g': 9.822844e+01, 'w_ff1': 4.188082e+00, 'w_ff2': 1.627106e+01}


def _to_microbatches(a, axis):
    t = _jnp.moveaxis(a, axis, 0)
    t = t.reshape((N_MICROBATCH, t.shape[0] // N_MICROBATCH) + t.shape[1:])
    return _jnp.moveaxis(t, 1, axis + 1)


def setup_inputs(seed: int = 0) -> dict:
    inp = _fwd_setup_inputs(seed)
    key = _jax.random.fold_in(_jax.random.key(seed), 7919)
    shape, _ = _output_shape()
    out = dict(inp)
    out["loss_target"] = _jax.random.normal(_jax.random.fold_in(key, 0), shape, _jnp.float32)
    for i, name in enumerate(TWIN_WEIGHTS):
        w = inp[name].astype(_jnp.float32)
        if MOMENT_SCALE is None:
            s = _jnp.sqrt(_jnp.mean(_jnp.square(w)) + 1e-30)
        else:
            s = MOMENT_SCALE[name]
        km, kv = _jax.random.split(_jax.random.fold_in(key, i + 1))
        out[name] = w
        out["m_" + name] = s * _jax.random.normal(km, w.shape, _jnp.float32)
        out["v_" + name] = (s * s) * _jax.random.uniform(kv, w.shape, _jnp.float32, 0.5, 1.5)
    if N_MICROBATCH > 1:
        for name, axis in PER_EXAMPLE_BATCH_AXIS.items():
            out[name] = _to_microbatches(out[name], axis)
    return {'x': out['x'], 'norm1_g': out['norm1_g'], 'w_in': out['w_in'], 'q_norm_g': out['q_norm_g'], 'k_norm_g': out['k_norm_g'], 'w_up_dil': out['w_up_dil'], 'w_up_sb': out['w_up_sb'], 'gate_b': out['gate_b'], 'w_out': out['w_out'], 'norm2_g': out['norm2_g'], 'w_ff1': out['w_ff1'], 'w_ff2': out['w_ff2'], 'loss_target': out['loss_target'], 'm_norm1_g': out['m_norm1_g'], 'm_w_in': out['m_w_in'], 'm_q_norm_g': out['m_q_norm_g'], 'm_k_norm_g': out['m_k_norm_g'], 'm_w_up_dil': out['m_w_up_dil'], 'm_w_up_sb': out['m_w_up_sb'], 'm_gate_b': out['m_gate_b'], 'm_w_out': out['m_w_out'], 'm_norm2_g': out['m_norm2_g'], 'm_w_ff1': out['m_w_ff1'], 'm_w_ff2': out['m_w_ff2'], 'v_norm1_g': out['v_norm1_g'], 'v_w_in': out['v_w_in'], 'v_q_norm_g': out['v_q_norm_g'], 'v_k_norm_g': out['v_k_norm_g'], 'v_w_up_dil': out['v_w_up_dil'], 'v_w_up_sb': out['v_w_up_sb'], 'v_gate_b': out['v_gate_b'], 'v_w_out': out['v_w_out'], 'v_norm2_g': out['v_norm2_g'], 'v_w_ff1': out['v_w_ff1'], 'v_w_ff2': out['v_w_ff2']}


def _loss(weights, diff, rest, loss_target):
    with _jax.named_scope("forward"):
        args = {**rest, TWIN_DIFF_INPUT: diff, **{k: w.astype(_WEIGHT_DTYPES[k]) for k, w in weights.items()}}
        y = _forward(args)
    with _jax.named_scope("loss_head"):
        err = _jnp.square(y.astype(_jnp.float32) - loss_target)
        return 0.5 * _jnp.sum(_jnp.mean(err, axis=-1)) if err.ndim else 0.5 * err


def _adamw(w, g, m, v):
    m = ADAM_B1 * m + (1.0 - ADAM_B1) * g
    v = ADAM_B2 * v + (1.0 - ADAM_B2) * _jnp.square(g)
    m_hat = m / (1.0 - ADAM_B1 ** ADAM_STEP)
    v_hat = v / (1.0 - ADAM_B2 ** ADAM_STEP)
    delta = -ADAM_LR * (m_hat / (_jnp.sqrt(v_hat) + ADAM_EPS) + ADAM_WD * w)
    return delta, m, v


def reference(x, norm1_g, w_in, q_norm_g, k_norm_g, w_up_dil, w_up_sb, gate_b, w_out, norm2_g, w_ff1, w_ff2, loss_target, m_norm1_g, m_w_in, m_q_norm_g, m_k_norm_g, m_w_up_dil, m_w_up_sb, m_gate_b, m_w_out, m_norm2_g, m_w_ff1, m_w_ff2, v_norm1_g, v_w_in, v_q_norm_g, v_k_norm_g, v_w_up_dil, v_w_up_sb, v_gate_b, v_w_out, v_norm2_g, v_w_ff1, v_w_ff2):
    given = dict(x=x, norm1_g=norm1_g, w_in=w_in, q_norm_g=q_norm_g, k_norm_g=k_norm_g, w_up_dil=w_up_dil, w_up_sb=w_up_sb, gate_b=gate_b, w_out=w_out, norm2_g=norm2_g, w_ff1=w_ff1, w_ff2=w_ff2, loss_target=loss_target, m_norm1_g=m_norm1_g, m_w_in=m_w_in, m_q_norm_g=m_q_norm_g, m_k_norm_g=m_k_norm_g, m_w_up_dil=m_w_up_dil, m_w_up_sb=m_w_up_sb, m_gate_b=m_gate_b, m_w_out=m_w_out, m_norm2_g=m_norm2_g, m_w_ff1=m_w_ff1, m_w_ff2=m_w_ff2, v_norm1_g=v_norm1_g, v_w_in=v_w_in, v_q_norm_g=v_q_norm_g, v_k_norm_g=v_k_norm_g, v_w_up_dil=v_w_up_dil, v_w_up_sb=v_w_up_sb, v_gate_b=v_gate_b, v_w_out=v_w_out, v_norm2_g=v_norm2_g, v_w_ff1=v_w_ff1, v_w_ff2=v_w_ff2)
    weights = {n: given[n] for n in TWIN_WEIGHTS}
    shared = {n: given[n] for n in SHARED_INPUTS}
    per_example = {n: given[n] for n in ['x']}
    grad_fn = _jax.value_and_grad(_loss, argnums=(0, 1))

    def one_microbatch(ex, loss_target):
        ex = dict(ex)
        diff = ex.pop(TWIN_DIFF_INPUT)
        return grad_fn(weights, diff, {**shared, **ex}, loss_target)

    if N_MICROBATCH == 1:
        loss, (grad_w, grad_x) = one_microbatch(per_example, given["loss_target"])
    else:
        def body(carry, xs):
            loss_sum, grad_sum = carry
            l_k, (gw_k, gx_k) = one_microbatch(xs[0], xs[1])
            with _jax.named_scope("update"):
                return (loss_sum + l_k, _jax.tree.map(_jnp.add, grad_sum, gw_k)), gx_k

        init = (_jnp.zeros((), _jnp.float32), _jax.tree.map(_jnp.zeros_like, weights))
        (loss, grad_w), grad_x = _jax.lax.scan(body, init, (per_example, given["loss_target"]))
    with _jax.named_scope("update"):
        delta_w, new_m, new_v = {}, {}, {}
        for n in TWIN_WEIGHTS:
            delta_w[n], new_m[n], new_v[n] = _adamw(weights[n], grad_w[n], given["m_" + n], given["v_" + n])
    return (loss, grad_x, *[grad_w[n] for n in TWIN_WEIGHTS], *[delta_w[n] for n in TWIN_WEIGHTS],
            *[new_m[n] for n in TWIN_WEIGHTS], *[new_v[n] for n in TWIN_WEIGHTS])
```

```python
import functools
import math

import jax
import jax.numpy as jnp
from jax import lax
from jax.experimental import pallas as pl
from jax.experimental.pallas import tpu as pltpu

F32 = jnp.float32
BF16 = jnp.bfloat16

HEAD_DIM = 128
BLOCK = 128
N_GROUPS = 3
DILATIONS = (1, 4, 16)
ROPE_THETA = 10000.0
EPS = 1e-6
ADAM_LR = 0.001
ADAM_B1 = 0.9
ADAM_B2 = 0.999
ADAM_EPS = 1e-08
ADAM_WD = 0.01
ADAM_STEP = 10
N_DEV = 8
MESH = pl.DeviceIdType.MESH
VMEM_LIMIT_BYTES = 48 * 1024 * 1024
NEG = -1e30


def _params(sem):
    return pltpu.CompilerParams(dimension_semantics=sem, vmem_limit_bytes=VMEM_LIMIT_BYTES)


def _pick(n, options):
    for o in options:
        if n % o == 0:
            return o
    return n


_DIMS = {"nn": (((1,), (0,)), ((), ())), "nt": (((1,), (1,)), ((), ())), "tn": (((0,), (0,)), ((), ()))}


def _matmul(a, b, *, mode, out_dtype, name, epilogue=None, extra=None, tm=None, tn=None, tk=None):
    if mode == "nn":
        (M, K), (K2, N) = a.shape, b.shape
    elif mode == "nt":
        (M, K), (N, K2) = a.shape, b.shape
    else:
        (K, M), (K2, N) = a.shape, b.shape
    assert K == K2, (a.shape, b.shape, mode)
    tm = tm or _pick(M, (1024, 512, 256, 128))
    tn = tn or _pick(N, (512, 256, 128))
    tk = tk or _pick(K, (2048, 1024, 512, 256, 128))
    nk = K // tk
    dims = _DIMS[mode]
    n_extra = 0 if extra is None else 1
    n_out = 2 if epilogue == "relu2" else 1

    def body(*refs):
        a_ref, b_ref = refs[0], refs[1]
        extra_ref = refs[2] if n_extra else None
        outs = refs[2 + n_extra:2 + n_extra + n_out]
        acc_ref = refs[-1] if nk > 1 else None

        def finish(acc):
            if epilogue is None:
                outs[0][...] = acc.astype(outs[0].dtype)
            elif epilogue == "add":
                outs[0][...] = (acc + extra_ref[...]).astype(outs[0].dtype)
            elif epilogue == "relu2":
                r = jnp.maximum(acc, 0.0)
                outs[0][...] = r.astype(outs[0].dtype)
                outs[1][...] = (r * r).astype(outs[1].dtype)
            else:
                outs[0][...] = (acc * (2.0 * extra_ref[...].astype(F32))).astype(outs[0].dtype)

        prod = lax.dot_general(a_ref[...], b_ref[...], dims, preferred_element_type=F32)
        if nk == 1:
            finish(prod)
        else:
            k = pl.program_id(2)

            @pl.when(k == 0)
            def _():
                acc_ref[...] = prod

            @pl.when(k > 0)
            def _():
                acc_ref[...] += prod

            @pl.when(k == nk - 1)
            def _():
                finish(acc_ref[...])

    if mode == "nn":
        a_spec = pl.BlockSpec((tm, tk), lambda i, j, k: (i, k))
        b_spec = pl.BlockSpec((tk, tn), lambda i, j, k: (k, j))
    elif mode == "nt":
        a_spec = pl.BlockSpec((tm, tk), lambda i, j, k: (i, k))
        b_spec = pl.BlockSpec((tn, tk), lambda i, j, k: (j, k))
    else:
        a_spec = pl.BlockSpec((tk, tm), lambda i, j, k: (k, i))
        b_spec = pl.BlockSpec((tk, tn), lambda i, j, k: (k, j))
    o_spec = pl.BlockSpec((tm, tn), lambda i, j, k: (i, j))
    in_specs = [a_spec, b_spec] + ([o_spec] if n_extra else [])
    out_shape = [jax.ShapeDtypeStruct((M, N), out_dtype)] * n_out
    res = pl.pallas_call(
        body,
        name=name,
        grid=(M // tm, N // tn, nk),
        in_specs=in_specs,
        out_specs=[o_spec] * n_out,
        out_shape=out_shape,
        scratch_shapes=[pltpu.VMEM((tm, tn), F32)] if nk > 1 else [],
        compiler_params=_params(("parallel", "parallel", "arbitrary")),
    )(a, b, *([extra] if n_extra else []))
    return res if n_out > 1 else res[0]


def _rmsnorm_fwd(x, g, *, name):
    T, D = x.shape
    tm = _pick(T, (512, 256, 128))

    def body(x_ref, g_ref, o_ref):
        xf = x_ref[...]
        r = lax.rsqrt(jnp.mean(xf * xf, axis=-1, keepdims=True) + EPS)
        o_ref[...] = ((xf * r) * g_ref[...]).astype(o_ref.dtype)

    return pl.pallas_call(
        body,
        name=name,
        grid=(T // tm,),
        in_specs=[pl.BlockSpec((tm, D), lambda i: (i, 0)), pl.BlockSpec((1, D), lambda i: (0, 0))],
        out_specs=pl.BlockSpec((tm, D), lambda i: (i, 0)),
        out_shape=jax.ShapeDtypeStruct((T, D), BF16),
        compiler_params=_params(("parallel",)),
    )(x, g)


def _rmsnorm_bwd(x, g, dh, dres, *, name):
    T, D = x.shape
    tm = _pick(T, (256, 128))

    def body(x_ref, g_ref, dh_ref, dres_ref, dx_ref, dxb_ref, dg_ref):
        i = pl.program_id(0)
        xf = x_ref[...]
        r = lax.rsqrt(jnp.mean(xf * xf, axis=-1, keepdims=True) + EPS)
        y = xf * r
        dh_v = dh_ref[...]
        dy = dh_v * g_ref[...]
        c = jnp.mean(dy * y, axis=-1, keepdims=True)
        dx = r * (dy - y * c) + dres_ref[...]
        dx_ref[...] = dx
        dxb_ref[...] = dx.astype(BF16)
        part = jnp.sum(dh_v * y, axis=0, keepdims=True)

        @pl.when(i == 0)
        def _():
            dg_ref[...] = part

        @pl.when(i > 0)
        def _():
            dg_ref[...] += part

    row = pl.BlockSpec((tm, D), lambda i: (i, 0))
    vec = pl.BlockSpec((1, D), lambda i: (0, 0))
    return pl.pallas_call(
        body,
        name=name,
        grid=(T // tm,),
        in_specs=[row, vec, row, row],
        out_specs=[row, row, vec],
        out_shape=[jax.ShapeDtypeStruct((T, D), F32), jax.ShapeDtypeStruct((T, D), BF16), jax.ShapeDtypeStruct((1, D), F32)],
        compiler_params=_params(("arbitrary",)),
    )(x, g, dh, dres)


GROUP_W = 4 * HEAD_DIM
N_NORMED = 2 * N_GROUPS


def _head_rstd(xh):
    return lax.rsqrt(jnp.mean(xh * xh, axis=-1, keepdims=True) + EPS)


def _prep_fwd(proj, gains, cos, sin, n_qkv_blocks, *, name):
    T = proj.shape[0]
    tm = _pick(T, (512, 256, 128))

    def body(p_ref, gain_ref, cos_ref, sin_ref, o_ref):
        j = pl.program_id(1)

        @pl.when(j < N_NORMED)
        def _():
            cos_v, sin_v = cos_ref[...], sin_ref[...]
            for hh in range(4):
                sl = slice(hh * HEAD_DIM, (hh + 1) * HEAD_DIM)
                xh = p_ref[:, sl]
                y = (xh * _head_rstd(xh)) * gain_ref[0, :, sl]
                o_ref[:, sl] = (y * cos_v + pltpu.roll(y, HEAD_DIM // 2, 1) * sin_v).astype(BF16)

        @pl.when(j >= N_NORMED)
        def _():
            o_ref[...] = p_ref[...].astype(BF16)

    return pl.pallas_call(
        body,
        name=name,
        grid=(T // tm, n_qkv_blocks),
        in_specs=[
            pl.BlockSpec((tm, GROUP_W), lambda i, j: (i, j)),
            pl.BlockSpec((1, 1, GROUP_W), lambda i, j: (jnp.minimum(j, N_NORMED - 1), 0, 0)),
            pl.BlockSpec((tm, HEAD_DIM), lambda i, j: (i, 0)),
            pl.BlockSpec((tm, HEAD_DIM), lambda i, j: (i, 0)),
        ],
        out_specs=pl.BlockSpec((tm, GROUP_W), lambda i, j: (i, j)),
        out_shape=jax.ShapeDtypeStruct((T, n_qkv_blocks * GROUP_W), BF16),
        compiler_params=_params(("parallel", "parallel")),
    )(proj, gains, cos, sin)


def _prep_bwd(proj, dq_d, dk_d, dv_d, gains, cos, sin, *, name):
    T = proj.shape[0]
    tm = _pick(T, (512, 256, 128))
    n_blocks = 3 * N_GROUPS

    def body(p_ref, dq_ref, dk_ref, dv_ref, gain_ref, cos_ref, sin_ref, o_ref, dgain_ref):
        j, i = pl.program_id(0), pl.program_id(1)

        def normed_bwd(d_ref):
            cos_v, sin_v = cos_ref[...], sin_ref[...]
            part = jnp.zeros((1, HEAD_DIM), F32)
            for hh in range(4):
                sl = slice(hh * HEAD_DIM, (hh + 1) * HEAD_DIM)
                xh = p_ref[:, sl]
                r = _head_rstd(xh)
                y0 = xh * r
                d_out = d_ref[:, sl]
                d_yg = d_out * cos_v + pltpu.roll(d_out * sin_v, HEAD_DIM // 2, 1)
                part = part + jnp.sum(d_yg * y0, axis=0, keepdims=True)
                dy0 = d_yg * gain_ref[0, :, sl]
                c = jnp.mean(dy0 * y0, axis=-1, keepdims=True)
                o_ref[:, sl] = (r * (dy0 - y0 * c)).astype(BF16)

            @pl.when(i == 0)
            def _():
                dgain_ref[0] = part

            @pl.when(i > 0)
            def _():
                dgain_ref[0] += part

        @pl.when(j < N_GROUPS)
        def _():
            normed_bwd(dq_ref)

        @pl.when(jnp.logical_and(j >= N_GROUPS, j < N_NORMED))
        def _():
            normed_bwd(dk_ref)

        @pl.when(j >= N_NORMED)
        def _():
            o_ref[...] = dv_ref[...].astype(BF16)

    normed = lambda j, i: (jnp.minimum(j, N_NORMED - 1), 0, 0)

    def grad_spec(first):
        return pl.BlockSpec((tm, GROUP_W), lambda j, i: (i, jnp.clip(j - first, 0, N_GROUPS - 1)))

    return pl.pallas_call(
        body,
        name=name,
        grid=(n_blocks, T // tm),
        in_specs=[
            pl.BlockSpec((tm, GROUP_W), lambda j, i: (i, jnp.minimum(j, N_NORMED - 1))),
            grad_spec(0), grad_spec(N_GROUPS), grad_spec(2 * N_GROUPS),
            pl.BlockSpec((1, 1, GROUP_W), normed),
            pl.BlockSpec((tm, HEAD_DIM), lambda j, i: (i, 0)),
            pl.BlockSpec((tm, HEAD_DIM), lambda j, i: (i, 0)),
        ],
        out_specs=[pl.BlockSpec((tm, GROUP_W), lambda j, i: (i, j)), pl.BlockSpec((1, 1, HEAD_DIM), normed)],
        out_shape=[jax.ShapeDtypeStruct((T, n_blocks * GROUP_W), BF16), jax.ShapeDtypeStruct((N_NORMED, 1, HEAD_DIM), F32)],
        compiler_params=_params(("arbitrary", "arbitrary")),
    )(proj, dq_d, dk_d, dv_d, gains, cos, sin)


def _nt(a, b):
    return lax.dot_general(a, b, _DIMS["nt"], preferred_element_type=F32)


def _tn(a, b):
    return lax.dot_general(a, b, _DIMS["tn"], preferred_element_type=F32)


def _nn(a, b):
    return jnp.dot(a, b, preferred_element_type=F32)


def _window_masks():
    row = lax.broadcasted_iota(jnp.int32, (BLOCK, BLOCK), 0)
    col = lax.broadcasted_iota(jnp.int32, (BLOCK, BLOCK), 1)
    return row >= col, col >= row


def _dil_fwd(qkv, g, *, name):
    T, W = qkv.shape
    r = DILATIONS[g]
    L = T // r
    nb = L // BLOCK
    wb = W // HEAD_DIM
    scale = 1.0 / math.sqrt(HEAD_DIM)
    view = qkv.reshape(L, r * W)

    def body(q_ref, kc_ref, kp_ref, vc_ref, vp_ref, o_ref, ld_ref):
        n = pl.program_id(2)
        q = q_ref[...]
        m_cur, m_prev = _window_masks()
        s_c = jnp.where(m_cur, _nt(q, kc_ref[...]) * scale, NEG)
        s_p = jnp.where(jnp.logical_and(m_prev, n > 0), _nt(q, kp_ref[...]) * scale, NEG)
        m = jnp.maximum(jnp.max(s_c, axis=-1, keepdims=True), jnp.max(s_p, axis=-1, keepdims=True))
        p_c = jnp.exp(s_c - m)
        p_p = jnp.exp(s_p - m)
        l = jnp.sum(p_c, axis=-1, keepdims=True) + jnp.sum(p_p, axis=-1, keepdims=True)
        inv = 1.0 / l
        o = _nn((p_c * inv).astype(BF16), vc_ref[...]) + _nn((p_p * inv).astype(BF16), vp_ref[...])
        o_ref[...] = o
        ld_ref[...] = jnp.broadcast_to(m + jnp.log(l), (BLOCK, HEAD_DIM))

    def col(off):
        return lambda c, h, n: (n, c * wb + off + g * 4 + h)

    def col_prev(off):
        return lambda c, h, n: (jnp.maximum(n - 1, 0), c * wb + off + g * 4 + h)

    blk = (BLOCK, HEAD_DIM)
    out_spec = pl.BlockSpec(blk, lambda c, h, n: (n, c * 4 + h))
    o, ld = pl.pallas_call(
        body,
        name=name,
        grid=(r, 4, nb),
        in_specs=[
            pl.BlockSpec(blk, col(0)),
            pl.BlockSpec(blk, col(12)),
            pl.BlockSpec(blk, col_prev(12)),
            pl.BlockSpec(blk, col(24)),
            pl.BlockSpec(blk, col_prev(24)),
        ],
        out_specs=[out_spec, out_spec],
        out_shape=[jax.ShapeDtypeStruct((L, r * GROUP_W), F32)] * 2,
        compiler_params=_params(("parallel", "parallel", "arbitrary")),
    )(view, view, view, view, view)
    return o.reshape(T, GROUP_W), ld.reshape(T, GROUP_W)


def _dil_bwd(qkv, do, ld, dterm, prev, g, *, name):
    T, W = qkv.shape
    r = DILATIONS[g]
    L = T // r
    nb = L // BLOCK
    wb = W // HEAD_DIM
    ob = N_GROUPS * 4
    scale = 1.0 / math.sqrt(HEAD_DIM)
    view = qkv.reshape(L, r * W)
    do_v, ld_v, dt_v = (t.reshape(L, r * GROUP_W) for t in (do, ld, dterm))

    def body(q_ref, qn_ref, kc_ref, kp_ref, vc_ref, vp_ref, do_ref, don_ref, ld_ref, ldn_ref, dt_ref, dtn_ref, *rest):
        dq_ref, dk_ref, dv_ref = rest[-3:]
        n = pl.program_id(2)
        m_cur, m_prev = _window_masks()
        kc, vc = kc_ref[...], vc_ref[...]

        def tile(q, k, v, do_t, ld_t, dt_t, mask):
            s = _nt(q, k) * scale
            p = jnp.where(mask, jnp.exp(s - ld_t[:, 0:1]), 0.0)
            ds = p * (_nt(do_t, v) + dt_t[:, 0:1]) * scale
            return p.astype(BF16), ds.astype(BF16)

        q, do_t = q_ref[...], do_ref[...]
        p_cc, ds_cc = tile(q, kc, vc, do_t, ld_ref[...], dt_ref[...], m_cur)
        kp = kp_ref[...]
        _, ds_cp = tile(q, kp, vp_ref[...], do_t, ld_ref[...], dt_ref[...], jnp.logical_and(m_prev, n > 0))
        dq_ref[...] = _nn(ds_cc, kc) + _nn(ds_cp, kp)
        qn, don = qn_ref[...], don_ref[...]
        p_nc, ds_nc = tile(qn, kc, vc, don, ldn_ref[...], dtn_ref[...], jnp.logical_and(m_prev, n < nb - 1))
        dk_ref[...] = _tn(ds_cc, q) + _tn(ds_nc, qn)
        dv_ref[...] = _tn(p_cc, do_t) + _tn(p_nc, don)

    def qcol(off):
        return lambda c, h, n: (n, c * wb + off + g * 4 + h)

    def qcol_prev(off):
        return lambda c, h, n: (jnp.maximum(n - 1, 0), c * wb + off + g * 4 + h)

    def qcol_next(off):
        return lambda c, h, n: (jnp.minimum(n + 1, nb - 1), c * wb + off + g * 4 + h)

    cur = lambda c, h, n: (n, c * 4 + h)
    nxt = lambda c, h, n: (jnp.minimum(n + 1, nb - 1), c * 4 + h)
    blk = (BLOCK, HEAD_DIM)
    in_specs = [
        pl.BlockSpec(blk, qcol(0)), pl.BlockSpec(blk, qcol_next(0)),
        pl.BlockSpec(blk, qcol(12)), pl.BlockSpec(blk, qcol_prev(12)),
        pl.BlockSpec(blk, qcol(24)), pl.BlockSpec(blk, qcol_prev(24)),
        pl.BlockSpec(blk, cur), pl.BlockSpec(blk, nxt),
        pl.BlockSpec(blk, cur), pl.BlockSpec(blk, nxt),
        pl.BlockSpec(blk, cur), pl.BlockSpec(blk, nxt),
    ]
    args = [view, view, view, view, view, view, do_v, do_v, ld_v, ld_v, dt_v, dt_v]
    aliases = {}
    if prev is not None:
        in_specs += [pl.BlockSpec(memory_space=pl.ANY)] * 3
        args += [t.reshape(L, r * ob * HEAD_DIM) for t in prev]
        aliases = {12: 0, 13: 1, 14: 2}

    ocol = lambda c, h, n: (n, c * ob + g * 4 + h)
    dshape = jax.ShapeDtypeStruct((L, r * ob * HEAD_DIM), F32)
    outs = pl.pallas_call(
        body,
        name=name,
        grid=(r, 4, nb),
        in_specs=in_specs,
        out_specs=[pl.BlockSpec(blk, ocol)] * 3,
        out_shape=[dshape, dshape, dshape],
        input_output_aliases=aliases,
        compiler_params=_params(("parallel", "parallel", "arbitrary")),
    )(*args)
    return tuple(t.reshape(T, ob * HEAD_DIM) for t in outs)


def _group_weights(ld_refs):
    lds = [r[...] for r in ld_refs]
    m = jnp.maximum(jnp.maximum(lds[0], lds[1]), lds[2])
    es = [jnp.exp(v - m) for v in lds]
    inv = 1.0 / (es[0] + es[1] + es[2])
    return [e * inv for e in es]


def _merge_fwd(os_, lds, *, name):
    T = os_[0].shape[0]
    tm = _pick(T, (1024, 512, 256, 128))

    def body(o0, o1, o2, l0, l1, l2, y_ref):
        w = _group_weights((l0, l1, l2))
        y_ref[...] = (w[0] * o0[...] + w[1] * o1[...] + w[2] * o2[...]).astype(BF16)

    spec = pl.BlockSpec((tm, GROUP_W), lambda i: (i, 0))
    return pl.pallas_call(
        body, name=name, grid=(T // tm,), in_specs=[spec] * 6, out_specs=spec,
        out_shape=jax.ShapeDtypeStruct((T, GROUP_W), BF16), compiler_params=_params(("parallel",)),
    )(*os_, *lds)


def _merge_bwd(os_, lds, dy, *, name):
    T = dy.shape[0]
    tm = _pick(T, (512, 256, 128))

    def body(o0, o1, o2, l0, l1, l2, dy_ref, do0, do1, do2, dt0, dt1, dt2):
        w = _group_weights((l0, l1, l2))
        dy_v = dy_ref[...]
        y = w[0] * o0[...] + w[1] * o1[...] + w[2] * o2[...]
        prod = dy_v * y
        for hh in range(4):
            sl = slice(hh * HEAD_DIM, (hh + 1) * HEAD_DIM)
            s = jnp.sum(prod[:, sl], axis=-1, keepdims=True)
            for wg, dt in zip(w, (dt0, dt1, dt2)):
                dt[:, sl] = -wg[:, sl] * s
        for wg, do in zip(w, (do0, do1, do2)):
            do[...] = (wg * dy_v).astype(BF16)

    spec = pl.BlockSpec((tm, GROUP_W), lambda i: (i, 0))
    outs = pl.pallas_call(
        body, name=name, grid=(T // tm,), in_specs=[spec] * 7, out_specs=[spec] * 6,
        out_shape=[jax.ShapeDtypeStruct((T, GROUP_W), BF16)] * 3 + [jax.ShapeDtypeStruct((T, GROUP_W), F32)] * 3,
        compiler_params=_params(("parallel",)),
    )(*os_, *lds, dy)
    return outs[:3], outs[3:]


def _split_dot(x, tri):
    hi = x.astype(BF16)
    lo = (x - hi.astype(F32)).astype(BF16)
    return _nn(hi, tri) + _nn(lo, tri)


def _log_terms(z):
    t = jnp.log(1.0 + jnp.exp(-jnp.abs(z)))
    return -(jnp.maximum(z, 0.0) + t), jnp.minimum(z, 0.0) - t


def _sb_fwd(qkv, n_heads, col0, *, name):
    T = qkv.shape[0]
    nq = T // BLOCK
    scale = 1.0 / math.sqrt(HEAD_DIM)

    def body(q_ref, k_ref, v_ref, o_ref, ob_ref):
        i = pl.program_id(1)
        q = q_ref[...]
        row = lax.broadcasted_iota(jnp.int32, (BLOCK, BLOCK), 0)
        col = lax.broadcasted_iota(jnp.int32, (BLOCK, BLOCK), 1)
        tri = (row > col).astype(BF16)

        def tile(j, carry, acc, mask):
            rows = pl.ds(pl.multiple_of(j * BLOCK, BLOCK), BLOCK)
            z = _nt(q, k_ref[rows, :]) * scale
            lk, ls = _log_terms(z)
            if mask is not None:
                lk = jnp.where(mask, lk, 0.0)
            between = _split_dot(lk, tri) + carry
            a = jnp.exp(ls + between)
            if mask is not None:
                a = jnp.where(mask, a, 0.0)
            acc = acc + _nn(a.astype(BF16), v_ref[rows, :])
            return carry + jnp.sum(lk, axis=-1, keepdims=True), acc

        carry, acc = tile(i, jnp.zeros((BLOCK, 1), F32), jnp.zeros((BLOCK, HEAD_DIM), F32), col < row)

        def step(jj, state):
            return tile(i - 1 - jj, state[0], state[1], None)

        carry, acc = lax.fori_loop(0, i, step, (carry, acc))
        o_ref[...] = acc
        ob_ref[...] = acc.astype(BF16)

    blk = (BLOCK, HEAD_DIM)
    out_spec = pl.BlockSpec(blk, lambda h, i: (i, h))
    return pl.pallas_call(
        body,
        name=name,
        grid=(n_heads, nq),
        in_specs=[
            pl.BlockSpec(blk, lambda h, i: (i, col0 + h)),
            pl.BlockSpec((T, HEAD_DIM), lambda h, i: (0, col0 + n_heads + h)),
            pl.BlockSpec((T, HEAD_DIM), lambda h, i: (0, col0 + 2 * n_heads + h)),
        ],
        out_specs=[out_spec, out_spec],
        out_shape=[jax.ShapeDtypeStruct((T, n_heads * HEAD_DIM), F32), jax.ShapeDtypeStruct((T, n_heads * HEAD_DIM), BF16)],
        compiler_params=_params(("parallel", "arbitrary")),
    )(qkv, qkv, qkv)


def _sb_bwd(qkv, o32, do, n_heads, col0, *, name):
    T = qkv.shape[0]
    nq = T // BLOCK
    scale = 1.0 / math.sqrt(HEAD_DIM)

    def body(q_ref, k_ref, v_ref, o_ref, do_ref, dq_ref, dk_ref, dv_ref, dk_acc, dv_acc):
        i = pl.program_id(1)

        @pl.when(i == 0)
        def _():
            dk_acc[...] = jnp.zeros_like(dk_acc)
            dv_acc[...] = jnp.zeros_like(dv_acc)

        q, do_t = q_ref[...], do_ref[...]
        delta = jnp.sum(do_t.astype(F32) * o_ref[...], axis=-1, keepdims=True)
        row = lax.broadcasted_iota(jnp.int32, (BLOCK, BLOCK), 0)
        col = lax.broadcasted_iota(jnp.int32, (BLOCK, BLOCK), 1)
        tri = (row > col).astype(BF16)
        tri_incl = (row >= col).astype(BF16)

        def tile(j, carry_b, carry_g, dq, mask):
            rows = pl.ds(pl.multiple_of(j * BLOCK, BLOCK), BLOCK)
            k_t, v_t = k_ref[rows, :], v_ref[rows, :]
            z = _nt(q, k_t) * scale
            lk, ls = _log_terms(z)
            if mask is not None:
                lk = jnp.where(mask, lk, 0.0)
            a = jnp.exp(ls + _split_dot(lk, tri) + carry_b)
            if mask is not None:
                a = jnp.where(mask, a, 0.0)
            a_b = a.astype(BF16)
            g = a_b.astype(F32) * _nt(do_t, v_t)
            before = delta - (_split_dot(g, tri_incl) + carry_g)
            sig = jnp.exp(ls)
            dz = (g * (1.0 - sig) - sig * before) * scale
            if mask is not None:
                dz = jnp.where(mask, dz, 0.0)
            dz_b = dz.astype(BF16)
            dk_acc[rows, :] += _tn(dz_b, q)
            dv_acc[rows, :] += _tn(a_b, do_t)
            return (carry_b + jnp.sum(lk, axis=-1, keepdims=True), carry_g + jnp.sum(g, axis=-1, keepdims=True),
                    dq + _nn(dz_b, k_t))

        zero = jnp.zeros((BLOCK, 1), F32)
        state = tile(i, zero, zero, jnp.zeros((BLOCK, HEAD_DIM), F32), col < row)

        def step(jj, st):
            return tile(i - 1 - jj, st[0], st[1], st[2], None)

        state = lax.fori_loop(0, i, step, state)
        dq_ref[...] = state[2].astype(BF16)

        @pl.when(i == nq - 1)
        def _():
            dk_ref[...] = dk_acc[...].astype(BF16)
            dv_ref[...] = dv_acc[...].astype(BF16)

    blk = (BLOCK, HEAD_DIM)
    full = (T, HEAD_DIM)
    dshape = jax.ShapeDtypeStruct((T, n_heads * HEAD_DIM), BF16)
    return pl.pallas_call(
        body,
        name=name,
        grid=(n_heads, nq),
        in_specs=[
            pl.BlockSpec(blk, lambda h, i: (i, col0 + h)),
            pl.BlockSpec(full, lambda h, i: (0, col0 + n_heads + h)),
            pl.BlockSpec(full, lambda h, i: (0, col0 + 2 * n_heads + h)),
            pl.BlockSpec(blk, lambda h, i: (i, h)),
            pl.BlockSpec(blk, lambda h, i: (i, h)),
        ],
        out_specs=[pl.BlockSpec(blk, lambda h, i: (i, h)), pl.BlockSpec(full, lambda h, i: (0, h)),
                   pl.BlockSpec(full, lambda h, i: (0, h))],
        out_shape=[dshape, dshape, dshape],
        scratch_shapes=[pltpu.VMEM(full, F32), pltpu.VMEM(full, F32)],
        compiler_params=_params(("arbitrary", "arbitrary")),
    )(qkv, qkv, qkv, o32, do)


def _gate_fwd(y_dil, y_sb, w_up_dil, w_up_sb, proj, gate_b, gate_col0, *, name):
    T, D = y_dil.shape[0], w_up_dil.shape[1]
    tm = _pick(T, (512, 256, 128))
    tn = _pick(D, (512, 256, 128))
    c0, nbr = gate_col0 // tn, D // tn

    def body(yd_ref, ys_ref, wd_ref, ws_ref, gp0_ref, gp1_ref, b_ref, o_ref):
        g0 = jax.nn.sigmoid(gp0_ref[...] + b_ref[0:1, :])
        g1 = jax.nn.sigmoid(gp1_ref[...] + b_ref[1:2, :])
        o_ref[...] = (g0 * _nn(yd_ref[...], wd_ref[...]) + g1 * _nn(ys_ref[...], ws_ref[...])).astype(BF16)

    return pl.pallas_call(
        body,
        name=name,
        grid=(T // tm, nbr),
        in_specs=[
            pl.BlockSpec((tm, y_dil.shape[1]), lambda i, j: (i, 0)),
            pl.BlockSpec((tm, y_sb.shape[1]), lambda i, j: (i, 0)),
            pl.BlockSpec((w_up_dil.shape[0], tn), lambda i, j: (0, j)),
            pl.BlockSpec((w_up_sb.shape[0], tn), lambda i, j: (0, j)),
            pl.BlockSpec((tm, tn), lambda i, j: (i, c0 + j)),
            pl.BlockSpec((tm, tn), lambda i, j: (i, c0 + nbr + j)),
            pl.BlockSpec((2, tn), lambda i, j: (0, j)),
        ],
        out_specs=pl.BlockSpec((tm, tn), lambda i, j: (i, j)),
        out_shape=jax.ShapeDtypeStruct((T, D), BF16),
        compiler_params=_params(("parallel", "parallel")),
    )(y_dil, y_sb, w_up_dil, w_up_sb, proj, proj, gate_b)


def _gate_bwd(y, w_up, proj, gate_b, dmixed, branch, gate_col0, *, name):
    T, D = y.shape[0], w_up.shape[1]
    tm = _pick(T, (512, 256, 128))
    tn = _pick(D, (512, 256, 128))
    c0 = gate_col0 // tn + branch * (D // tn)

    def body(y_ref, w_ref, gp_ref, b_ref, dm_ref, dup_ref, dgp_ref, db_ref):
        i = pl.program_id(1)
        g = jax.nn.sigmoid(gp_ref[...] + b_ref[branch:branch + 1, :])
        dm = dm_ref[...]
        dup_ref[...] = (dm * g).astype(BF16)
        dgp = (dm * _nn(y_ref[...], w_ref[...])) * (g * (1.0 - g))
        dgp_ref[...] = dgp.astype(BF16)
        part = jnp.sum(dgp, axis=0, keepdims=True)

        @pl.when(i == 0)
        def _():
            db_ref[...] = part

        @pl.when(i > 0)
        def _():
            db_ref[...] += part

    tile = pl.BlockSpec((tm, tn), lambda j, i: (i, j))
    return pl.pallas_call(
        body,
        name=name,
        grid=(D // tn, T // tm),
        in_specs=[
            pl.BlockSpec((tm, y.shape[1]), lambda j, i: (i, 0)),
            pl.BlockSpec((w_up.shape[0], tn), lambda j, i: (0, j)),
            pl.BlockSpec((tm, tn), lambda j, i: (i, c0 + j)),
            pl.BlockSpec((2, tn), lambda j, i: (0, j)),
            tile,
        ],
        out_specs=[tile, tile, pl.BlockSpec((1, tn), lambda j, i: (0, j))],
        out_shape=[jax.ShapeDtypeStruct((T, D), BF16), jax.ShapeDtypeStruct((T, D), BF16), jax.ShapeDtypeStruct((1, D), F32)],
        compiler_params=_params(("parallel", "arbitrary")),
    )(y, w_up, proj, gate_b, dmixed)


def _loss_head(y, target, *, name):
    T, D = y.shape
    tm = _pick(T, (256, 128))

    def body(y_ref, t_ref, dy_ref, dyb_ref, l_ref):
        i = pl.program_id(0)
        err = y_ref[...] - t_ref[...]
        dy = err * (1.0 / D)
        dy_ref[...] = dy
        dyb_ref[...] = dy.astype(BF16)
        part = 0.5 * jnp.sum(jnp.mean(err * err, axis=-1, keepdims=True), axis=0, keepdims=True)

        @pl.when(i == 0)
        def _():
            l_ref[...] = part

        @pl.when(i > 0)
        def _():
            l_ref[...] += part

    row = pl.BlockSpec((tm, D), lambda i: (i, 0))
    return pl.pallas_call(
        body, name=name, grid=(T // tm,), in_specs=[row, row],
        out_specs=[row, row, pl.BlockSpec((1, 1), lambda i: (0, 0))],
        out_shape=[jax.ShapeDtypeStruct((T, D), F32), jax.ShapeDtypeStruct((T, D), BF16), jax.ShapeDtypeStruct((1, 1), F32)],
        compiler_params=_params(("arbitrary",)),
    )(y, target)


def _reduce_adamw(parts, w, m, v, *, name):
    R, C = w.shape
    tr = R
    for cand in (1024, 512, 256, 128, 64, 32, 16, 8):
        if R % cand == 0 and cand * C <= 256 * 1024:
            tr = cand
            break

    def body(p_ref, w_ref, m_ref, v_ref, g_ref, d_ref, nm_ref, nv_ref):
        g = p_ref[0].astype(F32)
        for s in range(1, N_DEV):
            g = g + p_ref[s].astype(F32)
        m_new = ADAM_B1 * m_ref[...] + (1.0 - ADAM_B1) * g
        v_new = ADAM_B2 * v_ref[...] + (1.0 - ADAM_B2) * (g * g)
        m_hat = m_new / (1.0 - ADAM_B1 ** ADAM_STEP)
        v_hat = v_new / (1.0 - ADAM_B2 ** ADAM_STEP)
        g_ref[...] = g
        d_ref[...] = -ADAM_LR * (m_hat / (jnp.sqrt(v_hat) + ADAM_EPS) + ADAM_WD * w_ref[...])
        nm_ref[...] = m_new
        nv_ref[...] = v_new

    row = pl.BlockSpec((tr, C), lambda i: (i, 0))
    return pl.pallas_call(
        body, name=name, grid=(R // tr,),
        in_specs=[pl.BlockSpec((N_DEV, tr, C), lambda i: (0, i, 0)), row, row, row],
        out_specs=[row] * 4, out_shape=[jax.ShapeDtypeStruct((R, C), F32)] * 4,
        compiler_params=_params(("parallel",)),
    )(parts, w, m, v)


_ANY = pl.BlockSpec(memory_space=pl.ANY)


def _place():
    return lax.axis_index("x"), lax.axis_index("y"), lax.axis_index("c")


def _slot(p):
    return 4 * p[0] + 2 * p[1] + p[2]


def _all_gather(shards, *, name):
    n = len(shards)

    def body(*refs):
        ins, outs = refs[:n], refs[n:2 * n]
        send_sems, recv_sems, local_sems = refs[2 * n:]
        x, y, c = _place()
        me, sibling = (x, y, c), (x, y, 1 - c)
        chips = [(1 - x, y), (x, 1 - y), (1 - x, 1 - y)]

        def copy(t, k, block, to, src=None):
            dst = outs[t].at[_slot(block)]
            return pltpu.make_async_remote_copy(
                src_ref=dst if src is None else src, dst_ref=dst, send_sem=send_sems.at[7 * t + k],
                recv_sem=recv_sems.at[7 * t + k], device_id=to, device_id_type=MESH)

        mine = [pltpu.make_async_copy(ins[t], outs[t].at[_slot(me)], local_sems.at[t]) for t in range(n)]
        for cp in mine:
            cp.start()
        first = []
        for t in range(n):
            first.append(copy(t, 0, me, sibling, src=ins[t]))
            first += [copy(t, 1 + j, me, (*chip, c), src=ins[t]) for j, chip in enumerate(chips)]
        for cp in first:
            cp.start()
        passed = []
        for j, chip in enumerate(chips):
            for t in range(n):
                copy(t, 1 + j, (*chip, c), me).wait_recv()
                cp = copy(t, 4 + j, (*chip, c), sibling)
                cp.start()
                passed.append(cp)
        for t in range(n):
            copy(t, 0, sibling, me).wait_recv()
            for j, chip in enumerate(chips):
                copy(t, 4 + j, (*chip, 1 - c), me).wait_recv()
        for cp in first + passed:
            cp.wait_send()
        for cp in mine:
            cp.wait()

    return pl.pallas_call(
        body,
        name=name,
        in_specs=[_ANY] * n,
        out_specs=[_ANY] * n,
        out_shape=[jax.ShapeDtypeStruct((N_DEV,) + s.shape, s.dtype) for s in shards],
        scratch_shapes=[pltpu.SemaphoreType.DMA((7 * n,)), pltpu.SemaphoreType.DMA((7 * n,)), pltpu.SemaphoreType.DMA((n,))],
        compiler_params=pltpu.CompilerParams(has_side_effects=True),
    )(*shards)


def _exchange(chunked, whole, *, name):
    arrays = list(chunked) + list(whole)
    n, n_chunked = len(arrays), len(chunked)

    def body(*refs):
        ins, outs = refs[:n], refs[n:2 * n]
        send_sems, recv_sems, local_sems = refs[2 * n:]
        x, y, c = _place()
        me = (x, y, c)
        flips = [(fx, fy, fc) for fx in (0, 1) for fy in (0, 1) for fc in (0, 1)][1:]
        peers = [tuple(1 - v if f else v for v, f in zip(me, flip)) for flip in flips]

        def src(t, dest):
            return ins[t].at[_slot(dest)] if t < n_chunked else ins[t]

        mine = [pltpu.make_async_copy(src(t, me), outs[t].at[_slot(me)], local_sems.at[t]) for t in range(n)]
        for cp in mine:
            cp.start()
        sends = []
        for t in range(n):
            for k, peer in enumerate(peers):
                sends.append(pltpu.make_async_remote_copy(
                    src_ref=src(t, peer), dst_ref=outs[t].at[_slot(me)], send_sem=send_sems.at[7 * t + k],
                    recv_sem=recv_sems.at[7 * t + k], device_id=peer, device_id_type=MESH))
        for cp in sends:
            cp.start()
        for t in range(n):
            for k, peer in enumerate(peers):
                landed = outs[t].at[_slot(peer)]
                pltpu.make_async_remote_copy(
                    src_ref=landed, dst_ref=landed, send_sem=send_sems.at[7 * t + k],
                    recv_sem=recv_sems.at[7 * t + k], device_id=peer, device_id_type=MESH).wait_recv()
        for cp in sends:
            cp.wait_send()
        for cp in mine:
            cp.wait()

    def out_shape(t, a):
        return jax.ShapeDtypeStruct(a.shape if t < n_chunked else (N_DEV,) + a.shape, a.dtype)

    return pl.pallas_call(
        body,
        name=name,
        in_specs=[_ANY] * n,
        out_specs=[_ANY] * n,
        out_shape=[out_shape(t, a) for t, a in enumerate(arrays)],
        scratch_shapes=[pltpu.SemaphoreType.DMA((7 * n,)), pltpu.SemaphoreType.DMA((7 * n,)), pltpu.SemaphoreType.DMA((n,))],
        compiler_params=pltpu.CompilerParams(has_side_effects=True),
    )(*arrays)


def _rope_tables(T):
    half = HEAD_DIM // 2
    inv_freq = ROPE_THETA ** (-jnp.arange(half, dtype=F32) / half)
    ang = jnp.arange(T, dtype=F32)[:, None] * inv_freq[None, :]
    cos, sin = jnp.cos(ang), jnp.sin(ang)
    return jnp.concatenate([cos, cos], axis=-1), jnp.concatenate([-sin, sin], axis=-1)


def _gain_table(q_gain, k_gain):
    return jnp.tile(jnp.concatenate([q_gain, k_gain], axis=0), (1, 4))[:, None, :]


def _layer_fwd(x, p, cos, sin, tag):
    dil_w = 3 * N_GROUPS * GROUP_W
    sb_heads = p["w_up_sb"].shape[0] // HEAD_DIM
    n_qkv_blocks = (dil_w + 3 * sb_heads * HEAD_DIM) // GROUP_W
    s = {"x": x}
    s["h"] = _rmsnorm_fwd(x, p["norm1"], name=f"norm1_fwd{tag}")
    s["proj"] = _matmul(s["h"], p["w_in"], mode="nn", out_dtype=F32, name=f"proj_fwd{tag}")
    s["qkv"] = _prep_fwd(s["proj"], p["gains"], cos, sin, n_qkv_blocks, name=f"prep_fwd{tag}")
    outs = [_dil_fwd(s["qkv"], g, name=f"dil{g}_fwd{tag}") for g in range(N_GROUPS)]
    s["o"], s["ld"] = [o for o, _ in outs], [ld for _, ld in outs]
    s["y_dil"] = _merge_fwd(s["o"], s["ld"], name=f"merge_fwd{tag}")
    s["y_sb32"], s["y_sb"] = _sb_fwd(s["qkv"], sb_heads, dil_w // HEAD_DIM, name=f"sb_fwd{tag}")
    s["mixed"] = _gate_fwd(s["y_dil"], s["y_sb"], p["w_up_dil"], p["w_up_sb"], s["proj"], p["gate_b"],
                           n_qkv_blocks * GROUP_W, name=f"gate_fwd{tag}")
    s["x1"] = _matmul(s["mixed"], p["w_out"], mode="nn", out_dtype=F32, epilogue="add", extra=x, name=f"out_fwd{tag}")
    s["h2"] = _rmsnorm_fwd(s["x1"], p["norm2"], name=f"norm2_fwd{tag}")
    s["f"], s["a"] = _matmul(s["h2"], p["w_ff1"], mode="nn", out_dtype=BF16, epilogue="relu2", name=f"ff1_fwd{tag}")
    x2 = _matmul(s["a"], p["w_ff2"], mode="nn", out_dtype=F32, epilogue="add", extra=s["x1"], name=f"ff2_fwd{tag}")
    return x2, s


def _layer_bwd(dx2, dx2_b, p, s, cos, sin, tag):
    dil_w = 3 * N_GROUPS * GROUP_W
    sb_heads = p["w_up_sb"].shape[0] // HEAD_DIM
    gate_col0 = dil_w + 3 * sb_heads * HEAD_DIM
    g = {}
    df = _matmul(dx2_b, p["w_ff2"], mode="nt", out_dtype=BF16, epilogue="relu2_bwd", extra=s["f"], name=f"ff2_bwd{tag}")
    g["w_ff2"] = _matmul(s["a"], dx2_b, mode="tn", out_dtype=BF16, name=f"ff2_wgrad{tag}")
    dh2 = _matmul(df, p["w_ff1"], mode="nt", out_dtype=F32, name=f"ff1_bwd{tag}")
    g["w_ff1"] = _matmul(s["h2"], df, mode="tn", out_dtype=BF16, name=f"ff1_wgrad{tag}")
    dx1, dx1_b, g["norm2"] = _rmsnorm_bwd(s["x1"], p["norm2"], dh2, dx2, name=f"norm2_bwd{tag}")
    dmixed = _matmul(dx1_b, p["w_out"], mode="nt", out_dtype=F32, name=f"out_bwd{tag}")
    g["w_out"] = _matmul(s["mixed"], dx1_b, mode="tn", out_dtype=BF16, name=f"out_wgrad{tag}")
    dup_dil, dgp0, db0 = _gate_bwd(s["y_dil"], p["w_up_dil"], s["proj"], p["gate_b"], dmixed, 0, gate_col0, name=f"gate0_bwd{tag}")
    dup_sb, dgp1, db1 = _gate_bwd(s["y_sb"], p["w_up_sb"], s["proj"], p["gate_b"], dmixed, 1, gate_col0, name=f"gate1_bwd{tag}")
    g["gate_b"] = jnp.concatenate([db0, db1], axis=0)
    dy_dil = _matmul(dup_dil, p["w_up_dil"], mode="nt", out_dtype=F32, name=f"updil_bwd{tag}")
    g["w_up_dil"] = _matmul(s["y_dil"], dup_dil, mode="tn", out_dtype=BF16, name=f"updil_wgrad{tag}")
    dy_sb = _matmul(dup_sb, p["w_up_sb"], mode="nt", out_dtype=BF16, name=f"upsb_bwd{tag}")
    g["w_up_sb"] = _matmul(s["y_sb"], dup_sb, mode="tn", out_dtype=BF16, name=f"upsb_wgrad{tag}")
    dos, dterms = _merge_bwd(s["o"], s["ld"], dy_dil, name=f"merge_bwd{tag}")
    prev = None
    for grp in range(N_GROUPS):
        prev = _dil_bwd(s["qkv"], dos[grp], s["ld"][grp], dterms[grp], prev, grp, name=f"dil{grp}_bwd{tag}")
    dproj_d, dgain = _prep_bwd(s["proj"], *prev, p["gains"], cos, sin, name=f"prep_bwd{tag}")
    g["q_gain"], g["k_gain"] = dgain[:N_GROUPS, 0], dgain[N_GROUPS:, 0]
    dq_s, dk_s, dv_s = _sb_bwd(s["qkv"], s["y_sb32"], dy_sb, sb_heads, dil_w // HEAD_DIM, name=f"sb_bwd{tag}")
    dproj = jnp.concatenate([dproj_d, dq_s, dk_s, dv_s, dgp0, dgp1], axis=1)
    dh = _matmul(dproj, p["w_in"], mode="nt", out_dtype=F32, name=f"proj_bwd{tag}")
    g["w_in"] = _matmul(s["h"], dproj, mode="tn", out_dtype=BF16, name=f"proj_wgrad{tag}")
    dx, dx_b, g["norm1"] = _rmsnorm_bwd(s["x"], p["norm1"], dh, dx1, name=f"norm1_bwd{tag}")
    return dx, dx_b, g


def _local_step(x, target, layers):
    cos, sin = _rope_tables(x.shape[0])
    saved = []
    for l, p in enumerate(layers):
        x, s = _layer_fwd(x, p, cos, sin, f"_l{l}")
        saved.append(s)
    dx, dx_b, loss = _loss_head(x, target, name="loss_head")
    grads = [None] * len(layers)
    for l in reversed(range(len(layers))):
        dx, dx_b, grads[l] = _layer_bwd(dx, dx_b, layers[l], saved[l], cos, sin, f"_l{l}")
    return loss, dx, grads


_MATRICES = ("w_in", "w_up_dil", "w_up_sb", "w_out", "w_ff1", "w_ff2")
_COLUMN_SHARDED = ("w_in", "w_up_dil", "w_up_sb", "gate_b", "w_ff1")
_SMALL = ("norm1_g", "norm2_g", "q_norm_g", "k_norm_g")


def _unshard(gathered, name, layer):
    blocks = gathered[:, layer]
    if name in _COLUMN_SHARDED:
        return jnp.transpose(blocks, (1, 0, 2)).reshape(blocks.shape[1], N_DEV * blocks.shape[2])
    return blocks.reshape(N_DEV * blocks.shape[1], blocks.shape[2])


def _to_chunks(per_layer, name):
    out = []
    for full in per_layer:
        if name in _COLUMN_SHARDED:
            r, cols = full.shape
            out.append(jnp.transpose(full.reshape(r, N_DEV, cols // N_DEV), (1, 0, 2)))
        else:
            out.append(full.reshape(N_DEV, full.shape[0] // N_DEV, full.shape[1]))
    return jnp.stack(out, axis=1)


def _pack_small(norm1, norm2, qg, kg):
    flat = jnp.concatenate([t.reshape(-1, HEAD_DIM) for t in (norm1, norm2, qg, kg)], axis=0)
    return jnp.pad(flat, ((0, -flat.shape[0] % 8), (0, 0)))


def _unpack_small(packed, shapes):
    out, row = [], 0
    for shape in shapes:
        rows = math.prod(shape) // HEAD_DIM
        out.append(packed[row:row + rows].reshape(shape))
        row += rows
    return out


def kernel(x, norm1_g, w_in, q_norm_g, k_norm_g, w_up_dil, w_up_sb, gate_b, w_out, norm2_g, w_ff1, w_ff2, loss_target, m_norm1_g, m_w_in, m_q_norm_g, m_k_norm_g, m_w_up_dil, m_w_up_sb, m_gate_b, m_w_out, m_norm2_g, m_w_ff1, m_w_ff2, v_norm1_g, v_w_in, v_q_norm_g, v_k_norm_g, v_w_up_dil, v_w_up_sb, v_gate_b, v_w_out, v_norm2_g, v_w_ff1, v_w_ff2):
    names = ("norm1_g", "w_in", "q_norm_g", "k_norm_g", "w_up_dil", "w_up_sb", "gate_b", "w_out", "norm2_g", "w_ff1", "w_ff2")
    w = dict(zip(names, (norm1_g, w_in, q_norm_g, k_norm_g, w_up_dil, w_up_sb, gate_b, w_out, norm2_g, w_ff1, w_ff2)))
    m = dict(zip(names, (m_norm1_g, m_w_in, m_q_norm_g, m_k_norm_g, m_w_up_dil, m_w_up_sb, m_gate_b, m_w_out, m_norm2_g, m_w_ff1, m_w_ff2)))
    v = dict(zip(names, (v_norm1_g, v_w_in, v_q_norm_g, v_k_norm_g, v_w_up_dil, v_w_up_sb, v_gate_b, v_w_out, v_norm2_g, v_w_ff1, v_w_ff2)))
    depth = norm1_g.shape[0]

    gathered = _all_gather([w[n].astype(BF16) for n in _MATRICES] + [gate_b], name="gather_weights")
    gathered = dict(zip(_MATRICES + ("gate_b",), gathered))
    layers = []
    for l in range(depth):
        p = {n: _unshard(gathered[n], n, l) for n in _MATRICES + ("gate_b",)}
        p["norm1"], p["norm2"] = norm1_g[l][None], norm2_g[l][None]
        p["gains"] = _gain_table(q_norm_g[l], k_norm_g[l])
        layers.append(p)

    loss_part, dx, grads = _local_step(x[0], loss_target[0], layers)
    loss = lax.psum(loss_part[0, 0], ("x", "y", "c"))

    sharded = _MATRICES + ("gate_b",)
    chunks = [_to_chunks([g[n] for g in grads], n) for n in sharded]
    small = _pack_small(jnp.concatenate([g["norm1"] for g in grads]), jnp.concatenate([g["norm2"] for g in grads]),
                        jnp.stack([g["q_gain"] for g in grads]), jnp.stack([g["k_gain"] for g in grads]))
    received = _exchange(chunks, [small], name="exchange_grads")
    out = {}
    for n, parts in zip(sharded, received):
        rows, cols = depth * w[n].shape[1], w[n].shape[2]
        res = _reduce_adamw(parts.reshape(N_DEV, rows, cols), w[n].reshape(rows, cols), m[n].reshape(rows, cols),
                            v[n].reshape(rows, cols), name=f"adamw_{n}")
        out[n] = [t.reshape(w[n].shape) for t in res]
    small_res = _reduce_adamw(received[-1], _pack_small(*(w[n] for n in _SMALL)), _pack_small(*(m[n] for n in _SMALL)),
                              _pack_small(*(v[n] for n in _SMALL)), name="adamw_small")
    small_shapes = [w[n].shape for n in _SMALL]
    for k, t in enumerate(small_res):
        for n, arr in zip(_SMALL, _unpack_small(t, small_shapes)):
            out.setdefault(n, [None] * 4)[k] = arr
    return (loss, dx[None], *(out[n][0] for n in names), *(out[n][1] for n in names), *(out[n][2] for n in names),
            *(out[n][3] for n in names))
```

```python
import functools
import math

import jax
import jax.numpy as jnp
from jax import lax
from jax.experimental import pallas as pl
from jax.experimental.pallas import tpu as pltpu

F32 = jnp.float32
BF16 = jnp.bfloat16

HEAD_DIM = 128
BLOCK = 128
N_GROUPS = 3
DILATIONS = (1, 4, 16)
ROPE_THETA = 10000.0
EPS = 1e-6
ADAM_LR = 0.001
ADAM_B1 = 0.9
ADAM_B2 = 0.999
ADAM_EPS = 1e-08
ADAM_WD = 0.01
ADAM_STEP = 10
N_DEV = 8
MESH = pl.DeviceIdType.MESH
VMEM_LIMIT_BYTES = 48 * 1024 * 1024
NEG = -1e30


def _params(sem):
    return pltpu.CompilerParams(dimension_semantics=sem, vmem_limit_bytes=VMEM_LIMIT_BYTES)


def _pick(n, options):
    for o in options:
        if n % o == 0:
            return o
    return n


_DIMS = {"nn": (((1,), (0,)), ((), ())), "nt": (((1,), (1,)), ((), ())), "tn": (((0,), (0,)), ((), ()))}


def _matmul(a, b, *, mode, out_dtype, name, epilogue=None, extra=None, tm=None, tn=None, tk=None):
    if mode == "nn":
        (M, K), (K2, N) = a.shape, b.shape
    elif mode == "nt":
        (M, K), (N, K2) = a.shape, b.shape
    else:
        (K, M), (K2, N) = a.shape, b.shape
    assert K == K2, (a.shape, b.shape, mode)
    tm = tm or _pick(M, (1024, 512, 256, 128))
    tn = tn or _pick(N, (512, 256, 128))
    tk = tk or _pick(K, (2048, 1024, 512, 256, 128))
    nk = K // tk
    dims = _DIMS[mode]
    n_extra = 0 if extra is None else 1
    n_out = 2 if epilogue == "relu2" else 1

    def body(*refs):
        a_ref, b_ref = refs[0], refs[1]
        extra_ref = refs[2] if n_extra else None
        outs = refs[2 + n_extra:2 + n_extra + n_out]
        acc_ref = refs[-1] if nk > 1 else None

        def finish(acc):
            if epilogue is None:
                outs[0][...] = acc.astype(outs[0].dtype)
            elif epilogue == "add":
                outs[0][...] = (acc + extra_ref[...]).astype(outs[0].dtype)
            elif epilogue == "relu2":
                r = jnp.maximum(acc, 0.0)
                outs[0][...] = r.astype(outs[0].dtype)
                outs[1][...] = (r * r).astype(outs[1].dtype)
            else:
                outs[0][...] = (acc * (2.0 * extra_ref[...].astype(F32))).astype(outs[0].dtype)

        prod = lax.dot_general(a_ref[...], b_ref[...], dims, preferred_element_type=F32)
        if nk == 1:
            finish(prod)
        else:
            k = pl.program_id(2)

            @pl.when(k == 0)
            def _():
                acc_ref[...] = prod

            @pl.when(k > 0)
            def _():
                acc_ref[...] += prod

            @pl.when(k == nk - 1)
            def _():
                finish(acc_ref[...])

    if mode == "nn":
        a_spec = pl.BlockSpec((tm, tk), lambda i, j, k: (i, k))
        b_spec = pl.BlockSpec((tk, tn), lambda i, j, k: (k, j))
    elif mode == "nt":
        a_spec = pl.BlockSpec((tm, tk), lambda i, j, k: (i, k))
        b_spec = pl.BlockSpec((tn, tk), lambda i, j, k: (j, k))
    else:
        a_spec = pl.BlockSpec((tk, tm), lambda i, j, k: (k, i))
        b_spec = pl.BlockSpec((tk, tn), lambda i, j, k: (k, j))
    o_spec = pl.BlockSpec((tm, tn), lambda i, j, k: (i, j))
    in_specs = [a_spec, b_spec] + ([o_spec] if n_extra else [])
    out_shape = [jax.ShapeDtypeStruct((M, N), out_dtype)] * n_out
    res = pl.pallas_call(
        body,
        name=name,
        grid=(M // tm, N // tn, nk),
        in_specs=in_specs,
        out_specs=[o_spec] * n_out,
        out_shape=out_shape,
        scratch_shapes=[pltpu.VMEM((tm, tn), F32)] if nk > 1 else [],
        compiler_params=_params(("parallel", "parallel", "arbitrary")),
    )(a, b, *([extra] if n_extra else []))
    return res if n_out > 1 else res[0]


def _rmsnorm_fwd(x, g, *, name):
    T, D = x.shape
    tm = _pick(T, (512, 256, 128))

    def body(x_ref, g_ref, o_ref):
        xf = x_ref[...]
        r = lax.rsqrt(jnp.mean(xf * xf, axis=-1, keepdims=True) + EPS)
        o_ref[...] = ((xf * r) * g_ref[...]).astype(o_ref.dtype)

    return pl.pallas_call(
        body,
        name=name,
        grid=(T // tm,),
        in_specs=[pl.BlockSpec((tm, D), lambda i: (i, 0)), pl.BlockSpec((1, D), lambda i: (0, 0))],
        out_specs=pl.BlockSpec((tm, D), lambda i: (i, 0)),
        out_shape=jax.ShapeDtypeStruct((T, D), BF16),
        compiler_params=_params(("parallel",)),
    )(x, g)


def _rmsnorm_bwd(x, g, dh, dres, *, name):
    T, D = x.shape
    tm = _pick(T, (256, 128))

    def body(x_ref, g_ref, dh_ref, dres_ref, dx_ref, dxb_ref, dg_ref):
        i = pl.program_id(0)
        xf = x_ref[...]
        r = lax.rsqrt(jnp.mean(xf * xf, axis=-1, keepdims=True) + EPS)
        y = xf * r
        dh_v = dh_ref[...]
        dy = dh_v * g_ref[...]
        c = jnp.mean(dy * y, axis=-1, keepdims=True)
        dx = r * (dy - y * c) + dres_ref[...]
        dx_ref[...] = dx
        dxb_ref[...] = dx.astype(BF16)
        part = jnp.sum(dh_v * y, axis=0, keepdims=True)

        @pl.when(i == 0)
        def _():
            dg_ref[...] = part

        @pl.when(i > 0)
        def _():
            dg_ref[...] += part

    row = pl.BlockSpec((tm, D), lambda i: (i, 0))
    vec = pl.BlockSpec((1, D), lambda i: (0, 0))
    return pl.pallas_call(
        body,
        name=name,
        grid=(T // tm,),
        in_specs=[row, vec, row, row],
        out_specs=[row, row, vec],
        out_shape=[jax.ShapeDtypeStruct((T, D), F32), jax.ShapeDtypeStruct((T, D), BF16), jax.ShapeDtypeStruct((1, D), F32)],
        compiler_params=_params(("arbitrary",)),
    )(x, g, dh, dres)


GROUP_W = 4 * HEAD_DIM
N_NORMED = 2 * N_GROUPS


def _head_rstd(xh):
    return lax.rsqrt(jnp.mean(xh * xh, axis=-1, keepdims=True) + EPS)


def _prep_fwd(proj, gains, cos, sin, n_qkv_blocks, *, name):
    T = proj.shape[0]
    tm = _pick(T, (512, 256, 128))

    def body(p_ref, gain_ref, cos_ref, sin_ref, o_ref):
        j = pl.program_id(1)

        @pl.when(j < N_NORMED)
        def _():
            cos_v, sin_v = cos_ref[...], sin_ref[...]
            for hh in range(4):
                sl = slice(hh * HEAD_DIM, (hh + 1) * HEAD_DIM)
                xh = p_ref[:, sl]
                y = (xh * _head_rstd(xh)) * gain_ref[0, :, sl]
                o_ref[:, sl] = (y * cos_v + pltpu.roll(y, HEAD_DIM // 2, 1) * sin_v).astype(BF16)

        @pl.when(j >= N_NORMED)
        def _():
            o_ref[...] = p_ref[...].astype(BF16)

    return pl.pallas_call(
        body,
        name=name,
        grid=(T // tm, n_qkv_blocks),
        in_specs=[
            pl.BlockSpec((tm, GROUP_W), lambda i, j: (i, j)),
            pl.BlockSpec((1, 1, GROUP_W), lambda i, j: (jnp.minimum(j, N_NORMED - 1), 0, 0)),
            pl.BlockSpec((tm, HEAD_DIM), lambda i, j: (i, 0)),
            pl.BlockSpec((tm, HEAD_DIM), lambda i, j: (i, 0)),
        ],
        out_specs=pl.BlockSpec((tm, GROUP_W), lambda i, j: (i, j)),
        out_shape=jax.ShapeDtypeStruct((T, n_qkv_blocks * GROUP_W), BF16),
        compiler_params=_params(("parallel", "parallel")),
    )(proj, gains, cos, sin)


def _prep_bwd(proj, dq_d, dk_d, dv_d, gains, cos, sin, *, name):
    T = proj.shape[0]
    tm = _pick(T, (512, 256, 128))
    n_blocks = 3 * N_GROUPS

    def body(p_ref, dq_ref, dk_ref, dv_ref, gain_ref, cos_ref, sin_ref, o_ref, dgain_ref):
        j, i = pl.program_id(0), pl.program_id(1)

        def normed_bwd(d_ref):
            cos_v, sin_v = cos_ref[...], sin_ref[...]
            part = jnp.zeros((1, HEAD_DIM), F32)
            for hh in range(4):
                sl = slice(hh * HEAD_DIM, (hh + 1) * HEAD_DIM)
                xh = p_ref[:, sl]
                r = _head_rstd(xh)
                y0 = xh * r
                d_out = d_ref[:, sl]
                d_yg = d_out * cos_v + pltpu.roll(d_out * sin_v, HEAD_DIM // 2, 1)
                part = part + jnp.sum(d_yg * y0, axis=0, keepdims=True)
                dy0 = d_yg * gain_ref[0, :, sl]
                c = jnp.mean(dy0 * y0, axis=-1, keepdims=True)
                o_ref[:, sl] = (r * (dy0 - y0 * c)).astype(BF16)

            @pl.when(i == 0)
            def _():
                dgain_ref[0] = part

            @pl.when(i > 0)
            def _():
                dgain_ref[0] += part

        @pl.when(j < N_GROUPS)
        def _():
            normed_bwd(dq_ref)

        @pl.when(jnp.logical_and(j >= N_GROUPS, j < N_NORMED))
        def _():
            normed_bwd(dk_ref)

        @pl.when(j >= N_NORMED)
        def _():
            o_ref[...] = dv_ref[...].astype(BF16)

    normed = lambda j, i: (jnp.minimum(j, N_NORMED - 1), 0, 0)

    def grad_spec(first):
        return pl.BlockSpec((tm, GROUP_W), lambda j, i: (i, jnp.clip(j - first, 0, N_GROUPS - 1)))

    return pl.pallas_call(
        body,
        name=name,
        grid=(n_blocks, T // tm),
        in_specs=[
            pl.BlockSpec((tm, GROUP_W), lambda j, i: (i, jnp.minimum(j, N_NORMED - 1))),
            grad_spec(0), grad_spec(N_GROUPS), grad_spec(2 * N_GROUPS),
            pl.BlockSpec((1, 1, GROUP_W), normed),
            pl.BlockSpec((tm, HEAD_DIM), lambda j, i: (i, 0)),
            pl.BlockSpec((tm, HEAD_DIM), lambda j, i: (i, 0)),
        ],
        out_specs=[pl.BlockSpec((tm, GROUP_W), lambda j, i: (i, j)), pl.BlockSpec((1, 1, HEAD_DIM), normed)],
        out_shape=[jax.ShapeDtypeStruct((T, n_blocks * GROUP_W), BF16), jax.ShapeDtypeStruct((N_NORMED, 1, HEAD_DIM), F32)],
        compiler_params=_params(("arbitrary", "arbitrary")),
    )(proj, dq_d, dk_d, dv_d, gains, cos, sin)


def _nt(a, b):
    return lax.dot_general(a, b, _DIMS["nt"], preferred_element_type=F32)


def _tn(a, b):
    return lax.dot_general(a, b, _DIMS["tn"], preferred_element_type=F32)


def _nn(a, b):
    return jnp.dot(a, b, preferred_element_type=F32)


def _window_masks():
    row = lax.broadcasted_iota(jnp.int32, (BLOCK, BLOCK), 0)
    col = lax.broadcasted_iota(jnp.int32, (BLOCK, BLOCK), 1)
    return row >= col, col >= row


def _dil_fwd(qkv, g, *, name):
    T, W = qkv.shape
    r = DILATIONS[g]
    L = T // r
    nb = L // BLOCK
    wb = W // HEAD_DIM
    scale = 1.0 / math.sqrt(HEAD_DIM)
    view = qkv.reshape(L, r * W)

    def body(q_ref, kc_ref, kp_ref, vc_ref, vp_ref, o_ref, ld_ref):
        n = pl.program_id(2)
        q = q_ref[...]
        m_cur, m_prev = _window_masks()
        s_c = jnp.where(m_cur, _nt(q, kc_ref[...]) * scale, NEG)
        s_p = jnp.where(jnp.logical_and(m_prev, n > 0), _nt(q, kp_ref[...]) * scale, NEG)
        m = jnp.maximum(jnp.max(s_c, axis=-1, keepdims=True), jnp.max(s_p, axis=-1, keepdims=True))
        p_c = jnp.exp(s_c - m)
        p_p = jnp.exp(s_p - m)
        l = jnp.sum(p_c, axis=-1, keepdims=True) + jnp.sum(p_p, axis=-1, keepdims=True)
        inv = 1.0 / l
        o = _nn((p_c * inv).astype(BF16), vc_ref[...]) + _nn((p_p * inv).astype(BF16), vp_ref[...])
        o_ref[...] = o
        ld_ref[...] = jnp.broadcast_to(m + jnp.log(l), (BLOCK, HEAD_DIM))

    def col(off):
        return lambda c, h, n: (n, c * wb + off + g * 4 + h)

    def col_prev(off):
        return lambda c, h, n: (jnp.maximum(n - 1, 0), c * wb + off + g * 4 + h)

    blk = (BLOCK, HEAD_DIM)
    out_spec = pl.BlockSpec(blk, lambda c, h, n: (n, c * 4 + h))
    o, ld = pl.pallas_call(
        body,
        name=name,
        grid=(r, 4, nb),
        in_specs=[
            pl.BlockSpec(blk, col(0)),
            pl.BlockSpec(blk, col(12)),
            pl.BlockSpec(blk, col_prev(12)),
            pl.BlockSpec(blk, col(24)),
            pl.BlockSpec(blk, col_prev(24)),
        ],
        out_specs=[out_spec, out_spec],
        out_shape=[jax.ShapeDtypeStruct((L, r * GROUP_W), F32)] * 2,
        compiler_params=_params(("parallel", "parallel", "arbitrary")),
    )(view, view, view, view, view)
    return o.reshape(T, GROUP_W), ld.reshape(T, GROUP_W)


def _dil_bwd(qkv, do, ld, dterm, prev, g, *, name):
    T, W = qkv.shape
    r = DILATIONS[g]
    L = T // r
    nb = L // BLOCK
    wb = W // HEAD_DIM
    ob = N_GROUPS * 4
    scale = 1.0 / math.sqrt(HEAD_DIM)
    view = qkv.reshape(L, r * W)
    do_v, ld_v, dt_v = (t.reshape(L, r * GROUP_W) for t in (do, ld, dterm))

    def body(q_ref, qn_ref, kc_ref, kp_ref, vc_ref, vp_ref, do_ref, don_ref, ld_ref, ldn_ref, dt_ref, dtn_ref, *rest):
        dq_ref, dk_ref, dv_ref = rest[-3:]
        n = pl.program_id(2)
        m_cur, m_prev = _window_masks()
        kc, vc = kc_ref[...], vc_ref[...]

        def tile(q, k, v, do_t, ld_t, dt_t, mask):
            s = _nt(q, k) * scale
            p = jnp.where(mask, jnp.exp(s - ld_t[:, 0:1]), 0.0)
            ds = p * (_nt(do_t, v) + dt_t[:, 0:1]) * scale
            return p.astype(BF16), ds.astype(BF16)

        q, do_t = q_ref[...], do_ref[...]
        p_cc, ds_cc = tile(q, kc, vc, do_t, ld_ref[...], dt_ref[...], m_cur)
        kp = kp_ref[...]
        _, ds_cp = tile(q, kp, vp_ref[...], do_t, ld_ref[...], dt_ref[...], jnp.logical_and(m_prev, n > 0))
        dq_ref[...] = _nn(ds_cc, kc) + _nn(ds_cp, kp)
        qn, don = qn_ref[...], don_ref[...]
        p_nc, ds_nc = tile(qn, kc, vc, don, ldn_ref[...], dtn_ref[...], jnp.logical_and(m_prev, n < nb - 1))
        dk_ref[...] = _tn(ds_cc, q) + _tn(ds_nc, qn)
        dv_ref[...] = _tn(p_cc, do_t) + _tn(p_nc, don)

    def qcol(off):
        return lambda c, h, n: (n, c * wb + off + g * 4 + h)

    def qcol_prev(off):
        return lambda c, h, n: (jnp.maximum(n - 1, 0), c * wb + off + g * 4 + h)

    def qcol_next(off):
        return lambda c, h, n: (jnp.minimum(n + 1, nb - 1), c * wb + off + g * 4 + h)

    cur = lambda c, h, n: (n, c * 4 + h)
    nxt = lambda c, h, n: (jnp.minimum(n + 1, nb - 1), c * 4 + h)
    blk = (BLOCK, HEAD_DIM)
    in_specs = [
        pl.BlockSpec(blk, qcol(0)), pl.BlockSpec(blk, qcol_next(0)),
        pl.BlockSpec(blk, qcol(12)), pl.BlockSpec(blk, qcol_prev(12)),
        pl.BlockSpec(blk, qcol(24)), pl.BlockSpec(blk, qcol_prev(24)),
        pl.BlockSpec(blk, cur), pl.BlockSpec(blk, nxt),
        pl.BlockSpec(blk, cur), pl.BlockSpec(blk, nxt),
        pl.BlockSpec(blk, cur), pl.BlockSpec(blk, nxt),
    ]
    args = [view, view, view, view, view, view, do_v, do_v, ld_v, ld_v, dt_v, dt_v]
    aliases = {}
    if prev is not None:
        in_specs += [pl.BlockSpec(memory_space=pl.ANY)] * 3
        args += [t.reshape(L, r * ob * HEAD_DIM) for t in prev]
        aliases = {12: 0, 13: 1, 14: 2}

    ocol = lambda c, h, n: (n, c * ob + g * 4 + h)
    dshape = jax.ShapeDtypeStruct((L, r * ob * HEAD_DIM), F32)
    outs = pl.pallas_call(
        body,
        name=name,
        grid=(r, 4, nb),
        in_specs=in_specs,
        out_specs=[pl.BlockSpec(blk, ocol)] * 3,
        out_shape=[dshape, dshape, dshape],
        input_output_aliases=aliases,
        compiler_params=_params(("parallel", "parallel", "arbitrary")),
    )(*args)
    return tuple(t.reshape(T, ob * HEAD_DIM) for t in outs)


def _group_weights(ld_refs):
    lds = [r[...] for r in ld_refs]
    m = jnp.maximum(jnp.maximum(lds[0], lds[1]), lds[2])
    es = [jnp.exp(v - m) for v in lds]
    inv = 1.0 / (es[0] + es[1] + es[2])
    return [e * inv for e in es]


def _merge_fwd(os_, lds, *, name):
    T = os_[0].shape[0]
    tm = _pick(T, (1024, 512, 256, 128))

    def body(o0, o1, o2, l0, l1, l2, y_ref):
        w = _group_weights((l0, l1, l2))
        y_ref[...] = (w[0] * o0[...] + w[1] * o1[...] + w[2] * o2[...]).astype(BF16)

    spec = pl.BlockSpec((tm, GROUP_W), lambda i: (i, 0))
    return pl.pallas_call(
        body, name=name, grid=(T // tm,), in_specs=[spec] * 6, out_specs=spec,
        out_shape=jax.ShapeDtypeStruct((T, GROUP_W), BF16), compiler_params=_params(("parallel",)),
    )(*os_, *lds)


def _merge_bwd(os_, lds, dy, *, name):
    T = dy.shape[0]
    tm = _pick(T, (512, 256, 128))

    def body(o0, o1, o2, l0, l1, l2, dy_ref, do0, do1, do2, dt0, dt1, dt2):
        w = _group_weights((l0, l1, l2))
        dy_v = dy_ref[...]
        y = w[0] * o0[...] + w[1] * o1[...] + w[2] * o2[...]
        prod = dy_v * y
        for hh in range(4):
            sl = slice(hh * HEAD_DIM, (hh + 1) * HEAD_DIM)
            s = jnp.sum(prod[:, sl], axis=-1, keepdims=True)
            for wg, dt in zip(w, (dt0, dt1, dt2)):
                dt[:, sl] = -wg[:, sl] * s
        for wg, do in zip(w, (do0, do1, do2)):
            do[...] = (wg * dy_v).astype(BF16)

    spec = pl.BlockSpec((tm, GROUP_W), lambda i: (i, 0))
    outs = pl.pallas_call(
        body, name=name, grid=(T // tm,), in_specs=[spec] * 7, out_specs=[spec] * 6,
        out_shape=[jax.ShapeDtypeStruct((T, GROUP_W), BF16)] * 3 + [jax.ShapeDtypeStruct((T, GROUP_W), F32)] * 3,
        compiler_params=_params(("parallel",)),
    )(*os_, *lds, dy)
    return outs[:3], outs[3:]


SB_ROWS = 512


def _sum_matrix(inclusive):
    j = lax.broadcasted_iota(jnp.int32, (2 * BLOCK, 2 * BLOCK), 0) % BLOCK
    s = lax.broadcasted_iota(jnp.int32, (2 * BLOCK, 2 * BLOCK), 1)
    later = (j >= s) if inclusive else (j > s)
    return jnp.logical_or(s >= BLOCK, later).astype(BF16)


def _block_sums(x, mat):
    hi = x.astype(BF16)
    lo = (x - hi.astype(F32)).astype(BF16)
    r = _nn(jnp.concatenate([hi, lo], axis=1), mat)
    return r[:, :BLOCK], r[:, BLOCK:]


def _log_terms(z):
    t = jnp.log(1.0 + jnp.exp(-jnp.abs(z)))
    return -(jnp.maximum(z, 0.0) + t), jnp.minimum(z, 0.0) - t


def _causal_mask(rows):
    row = lax.broadcasted_iota(jnp.int32, (rows, rows), 0)
    col = lax.broadcasted_iota(jnp.int32, (rows, rows), 1)
    return col < row


def _sb_fwd(qkv, n_heads, col0, *, name):
    T = qkv.shape[0]
    tq = _pick(T, (SB_ROWS, BLOCK))
    nq, nsub = T // tq, tq // BLOCK
    scale = 1.0 / math.sqrt(HEAD_DIM)

    def body(q_ref, k_ref, v_ref, o_ref, ob_ref):
        i = pl.program_id(1)
        q = q_ref[...]
        mat = _sum_matrix(False)

        def chunk(j, carry, acc, mask):
            rows = pl.ds(pl.multiple_of(j * tq, tq), tq)
            z = _nt(q, k_ref[rows, :]) * scale
            lk, ls = _log_terms(z)
            if mask is not None:
                lk = jnp.where(mask, lk, 0.0)
            a = []
            for b in reversed(range(nsub)):
                sl = slice(b * BLOCK, (b + 1) * BLOCK)
                later, total = _block_sums(lk[:, sl], mat)
                a.append(jnp.exp(ls[:, sl] + (later + carry)))
                carry = carry + total
            a = jnp.concatenate(a[::-1], axis=1)
            if mask is not None:
                a = jnp.where(mask, a, 0.0)
            return carry, acc + _nn(a.astype(BF16), v_ref[rows, :])

        zero = jnp.zeros((tq, HEAD_DIM), F32)
        carry, acc = chunk(i, zero, zero, _causal_mask(tq))

        def step(jj, state):
            return chunk(i - 1 - jj, state[0], state[1], None)

        carry, acc = lax.fori_loop(0, i, step, (carry, acc))
        o_ref[...] = acc
        ob_ref[...] = acc.astype(BF16)

    blk = (tq, HEAD_DIM)
    out_spec = pl.BlockSpec(blk, lambda h, i: (i, h))
    return pl.pallas_call(
        body,
        name=name,
        grid=(n_heads, nq),
        in_specs=[
            pl.BlockSpec(blk, lambda h, i: (i, col0 + h)),
            pl.BlockSpec((T, HEAD_DIM), lambda h, i: (0, col0 + n_heads + h)),
            pl.BlockSpec((T, HEAD_DIM), lambda h, i: (0, col0 + 2 * n_heads + h)),
        ],
        out_specs=[out_spec, out_spec],
        out_shape=[jax.ShapeDtypeStruct((T, n_heads * HEAD_DIM), F32), jax.ShapeDtypeStruct((T, n_heads * HEAD_DIM), BF16)],
        compiler_params=_params(("parallel", "arbitrary")),
    )(qkv, qkv, qkv)


def _sb_bwd(qkv, o32, do, n_heads, col0, *, name):
    T = qkv.shape[0]
    tq = _pick(T, (SB_ROWS, BLOCK))
    nq, nsub = T // tq, tq // BLOCK
    scale = 1.0 / math.sqrt(HEAD_DIM)

    def body(q_ref, k_ref, v_ref, o_ref, do_ref, dq_ref, dk_ref, dv_ref, dk_acc, dv_acc):
        i = pl.program_id(1)

        @pl.when(i == 0)
        def _():
            dk_acc[...] = jnp.zeros_like(dk_acc)
            dv_acc[...] = jnp.zeros_like(dv_acc)

        q, do_t = q_ref[...], do_ref[...]
        delta = jnp.broadcast_to(jnp.sum(do_t.astype(F32) * o_ref[...], axis=-1, keepdims=True), (tq, HEAD_DIM))
        mat, mat_incl = _sum_matrix(False), _sum_matrix(True)

        def chunk(j, carry_b, carry_g, dq, mask):
            rows = pl.ds(pl.multiple_of(j * tq, tq), tq)
            k_t, v_t = k_ref[rows, :], v_ref[rows, :]
            z = _nt(q, k_t) * scale
            lk, ls = _log_terms(z)
            if mask is not None:
                lk = jnp.where(mask, lk, 0.0)
            d_a = _nt(do_t, v_t)
            a_parts, dz_parts = [], []
            for b in reversed(range(nsub)):
                sl = slice(b * BLOCK, (b + 1) * BLOCK)
                later, total = _block_sums(lk[:, sl], mat)
                a = jnp.exp(ls[:, sl] + (later + carry_b))
                carry_b = carry_b + total
                if mask is not None:
                    a = jnp.where(mask[:, sl], a, 0.0)
                a_b = a.astype(BF16)
                g = a_b.astype(F32) * d_a[:, sl]
                from_here, total_g = _block_sums(g, mat_incl)
                before = delta - (from_here + carry_g)
                carry_g = carry_g + total_g
                sig = jnp.exp(ls[:, sl])
                dz = (g - sig * (g + before)) * scale
                a_parts.append(a_b)
                dz_parts.append(dz)
            dz = jnp.concatenate(dz_parts[::-1], axis=1)
            if mask is not None:
                dz = jnp.where(mask, dz, 0.0)
            dz_b = dz.astype(BF16)
            dk_acc[rows, :] += _tn(dz_b, q)
            dv_acc[rows, :] += _tn(jnp.concatenate(a_parts[::-1], axis=1), do_t)
            return carry_b, carry_g, dq + _nn(dz_b, k_t)

        zero = jnp.zeros((tq, HEAD_DIM), F32)
        state = chunk(i, zero, zero, zero, _causal_mask(tq))

        def step(jj, st):
            return chunk(i - 1 - jj, st[0], st[1], st[2], None)

        state = lax.fori_loop(0, i, step, state)
        dq_ref[...] = state[2].astype(BF16)

        @pl.when(i == nq - 1)
        def _():
            dk_ref[...] = dk_acc[...].astype(BF16)
            dv_ref[...] = dv_acc[...].astype(BF16)

    blk = (tq, HEAD_DIM)
    full = (T, HEAD_DIM)
    dshape = jax.ShapeDtypeStruct((T, n_heads * HEAD_DIM), BF16)
    return pl.pallas_call(
        body,
        name=name,
        grid=(n_heads, nq),
        in_specs=[
            pl.BlockSpec(blk, lambda h, i: (i, col0 + h)),
            pl.BlockSpec(full, lambda h, i: (0, col0 + n_heads + h)),
            pl.BlockSpec(full, lambda h, i: (0, col0 + 2 * n_heads + h)),
            pl.BlockSpec(blk, lambda h, i: (i, h)),
            pl.BlockSpec(blk, lambda h, i: (i, h)),
        ],
        out_specs=[pl.BlockSpec(blk, lambda h, i: (i, h)), pl.BlockSpec(full, lambda h, i: (0, h)),
                   pl.BlockSpec(full, lambda h, i: (0, h))],
        out_shape=[dshape, dshape, dshape],
        scratch_shapes=[pltpu.VMEM(full, F32), pltpu.VMEM(full, F32)],
        compiler_params=_params(("arbitrary", "arbitrary")),
    )(qkv, qkv, qkv, o32, do)


def _gate_fwd(y_dil, y_sb, w_up_dil, w_up_sb, proj, gate_b, gate_col0, *, name):
    T, D = y_dil.shape[0], w_up_dil.shape[1]
    tm = _pick(T, (512, 256, 128))
    tn = _pick(D, (512, 256, 128))
    c0, nbr = gate_col0 // tn, D // tn

    def body(yd_ref, ys_ref, wd_ref, ws_ref, gp0_ref, gp1_ref, b_ref, o_ref):
        g0 = jax.nn.sigmoid(gp0_ref[...] + b_ref[0:1, :])
        g1 = jax.nn.sigmoid(gp1_ref[...] + b_ref[1:2, :])
        o_ref[...] = (g0 * _nn(yd_ref[...], wd_ref[...]) + g1 * _nn(ys_ref[...], ws_ref[...])).astype(BF16)

    return pl.pallas_call(
        body,
        name=name,
        grid=(T // tm, nbr),
        in_specs=[
            pl.BlockSpec((tm, y_dil.shape[1]), lambda i, j: (i, 0)),
            pl.BlockSpec((tm, y_sb.shape[1]), lambda i, j: (i, 0)),
            pl.BlockSpec((w_up_dil.shape[0], tn), lambda i, j: (0, j)),
            pl.BlockSpec((w_up_sb.shape[0], tn), lambda i, j: (0, j)),
            pl.BlockSpec((tm, tn), lambda i, j: (i, c0 + j)),
            pl.BlockSpec((tm, tn), lambda i, j: (i, c0 + nbr + j)),
            pl.BlockSpec((2, tn), lambda i, j: (0, j)),
        ],
        out_specs=pl.BlockSpec((tm, tn), lambda i, j: (i, j)),
        out_shape=jax.ShapeDtypeStruct((T, D), BF16),
        compiler_params=_params(("parallel", "parallel")),
    )(y_dil, y_sb, w_up_dil, w_up_sb, proj, proj, gate_b)


def _gate_bwd(y, w_up, proj, gate_b, dmixed, branch, gate_col0, *, name):
    T, D = y.shape[0], w_up.shape[1]
    tm = _pick(T, (512, 256, 128))
    tn = _pick(D, (512, 256, 128))
    c0 = gate_col0 // tn + branch * (D // tn)

    def body(y_ref, w_ref, gp_ref, b_ref, dm_ref, dup_ref, dgp_ref, db_ref):
        i = pl.program_id(1)
        g = jax.nn.sigmoid(gp_ref[...] + b_ref[branch:branch + 1, :])
        dm = dm_ref[...]
        dup_ref[...] = (dm * g).astype(BF16)
        dgp = (dm * _nn(y_ref[...], w_ref[...])) * (g * (1.0 - g))
        dgp_ref[...] = dgp.astype(BF16)
        part = jnp.sum(dgp, axis=0, keepdims=True)

        @pl.when(i == 0)
        def _():
            db_ref[...] = part

        @pl.when(i > 0)
        def _():
            db_ref[...] += part

    tile = pl.BlockSpec((tm, tn), lambda j, i: (i, j))
    return pl.pallas_call(
        body,
        name=name,
        grid=(D // tn, T // tm),
        in_specs=[
            pl.BlockSpec((tm, y.shape[1]), lambda j, i: (i, 0)),
            pl.BlockSpec((w_up.shape[0], tn), lambda j, i: (0, j)),
            pl.BlockSpec((tm, tn), lambda j, i: (i, c0 + j)),
            pl.BlockSpec((2, tn), lambda j, i: (0, j)),
            tile,
        ],
        out_specs=[tile, tile, pl.BlockSpec((1, tn), lambda j, i: (0, j))],
        out_shape=[jax.ShapeDtypeStruct((T, D), BF16), jax.ShapeDtypeStruct((T, D), BF16), jax.ShapeDtypeStruct((1, D), F32)],
        compiler_params=_params(("parallel", "arbitrary")),
    )(y, w_up, proj, gate_b, dmixed)


def _loss_head(y, target, *, name):
    T, D = y.shape
    tm = _pick(T, (256, 128))

    def body(y_ref, t_ref, dy_ref, dyb_ref, l_ref):
        i = pl.program_id(0)
        err = y_ref[...] - t_ref[...]
        dy = err * (1.0 / D)
        dy_ref[...] = dy
        dyb_ref[...] = dy.astype(BF16)
        part = 0.5 * jnp.sum(jnp.mean(err * err, axis=-1, keepdims=True), axis=0, keepdims=True)

        @pl.when(i == 0)
        def _():
            l_ref[...] = part

        @pl.when(i > 0)
        def _():
            l_ref[...] += part

    row = pl.BlockSpec((tm, D), lambda i: (i, 0))
    return pl.pallas_call(
        body, name=name, grid=(T // tm,), in_specs=[row, row],
        out_specs=[row, row, pl.BlockSpec((1, 1), lambda i: (0, 0))],
        out_shape=[jax.ShapeDtypeStruct((T, D), F32), jax.ShapeDtypeStruct((T, D), BF16), jax.ShapeDtypeStruct((1, 1), F32)],
        compiler_params=_params(("arbitrary",)),
    )(y, target)


def _reduce_adamw(parts, w, m, v, *, name):
    R, C = w.shape
    tr = R
    for cand in (1024, 512, 256, 128, 64, 32, 16, 8):
        if R % cand == 0 and cand * C <= 256 * 1024:
            tr = cand
            break

    def body(p_ref, w_ref, m_ref, v_ref, g_ref, d_ref, nm_ref, nv_ref):
        g = p_ref[0].astype(F32)
        for s in range(1, N_DEV):
            g = g + p_ref[s].astype(F32)
        m_new = ADAM_B1 * m_ref[...] + (1.0 - ADAM_B1) * g
        v_new = ADAM_B2 * v_ref[...] + (1.0 - ADAM_B2) * (g * g)
        m_hat = m_new / (1.0 - ADAM_B1 ** ADAM_STEP)
        v_hat = v_new / (1.0 - ADAM_B2 ** ADAM_STEP)
        g_ref[...] = g
        d_ref[...] = -ADAM_LR * (m_hat / (jnp.sqrt(v_hat) + ADAM_EPS) + ADAM_WD * w_ref[...])
        nm_ref[...] = m_new
        nv_ref[...] = v_new

    row = pl.BlockSpec((tr, C), lambda i: (i, 0))
    return pl.pallas_call(
        body, name=name, grid=(R // tr,),
        in_specs=[pl.BlockSpec((N_DEV, tr, C), lambda i: (0, i, 0)), row, row, row],
        out_specs=[row] * 4, out_shape=[jax.ShapeDtypeStruct((R, C), F32)] * 4,
        compiler_params=_params(("parallel",)),
    )(parts, w, m, v)


_ANY = pl.BlockSpec(memory_space=pl.ANY)


def _place():
    return lax.axis_index("x"), lax.axis_index("y"), lax.axis_index("c")


def _slot(p):
    return 4 * p[0] + 2 * p[1] + p[2]


def _all_gather(shards, *, name):
    n = len(shards)

    def body(*refs):
        ins, outs = refs[:n], refs[n:2 * n]
        send_sems, recv_sems, local_sems = refs[2 * n:]
        x, y, c = _place()
        me, sibling = (x, y, c), (x, y, 1 - c)
        chips = [(1 - x, y), (x, 1 - y), (1 - x, 1 - y)]

        def copy(t, k, block, to, src=None):
            dst = outs[t].at[_slot(block)]
            return pltpu.make_async_remote_copy(
                src_ref=dst if src is None else src, dst_ref=dst, send_sem=send_sems.at[7 * t + k],
                recv_sem=recv_sems.at[7 * t + k], device_id=to, device_id_type=MESH)

        mine = [pltpu.make_async_copy(ins[t], outs[t].at[_slot(me)], local_sems.at[t]) for t in range(n)]
        for cp in mine:
            cp.start()
        first = []
        for t in range(n):
            first.append(copy(t, 0, me, sibling, src=ins[t]))
            first += [copy(t, 1 + j, me, (*chip, c), src=ins[t]) for j, chip in enumerate(chips)]
        for cp in first:
            cp.start()
        passed = []
        for j, chip in enumerate(chips):
            for t in range(n):
                copy(t, 1 + j, (*chip, c), me).wait_recv()
                cp = copy(t, 4 + j, (*chip, c), sibling)
                cp.start()
                passed.append(cp)
        for t in range(n):
            copy(t, 0, sibling, me).wait_recv()
            for j, chip in enumerate(chips):
                copy(t, 4 + j, (*chip, 1 - c), me).wait_recv()
        for cp in first + passed:
            cp.wait_send()
        for cp in mine:
            cp.wait()

    return pl.pallas_call(
        body,
        name=name,
        in_specs=[_ANY] * n,
        out_specs=[_ANY] * n,
        out_shape=[jax.ShapeDtypeStruct((N_DEV,) + s.shape, s.dtype) for s in shards],
        scratch_shapes=[pltpu.SemaphoreType.DMA((7 * n,)), pltpu.SemaphoreType.DMA((7 * n,)), pltpu.SemaphoreType.DMA((n,))],
        compiler_params=pltpu.CompilerParams(has_side_effects=True),
    )(*shards)


def _exchange(chunked, whole, *, name):
    arrays = list(chunked) + list(whole)
    n, n_chunked = len(arrays), len(chunked)

    def body(*refs):
        ins, outs = refs[:n], refs[n:2 * n]
        send_sems, recv_sems, local_sems = refs[2 * n:]
        x, y, c = _place()
        me = (x, y, c)
        flips = [(fx, fy, fc) for fx in (0, 1) for fy in (0, 1) for fc in (0, 1)][1:]
        peers = [tuple(1 - v if f else v for v, f in zip(me, flip)) for flip in flips]

        def src(t, dest):
            return ins[t].at[_slot(dest)] if t < n_chunked else ins[t]

        mine = [pltpu.make_async_copy(src(t, me), outs[t].at[_slot(me)], local_sems.at[t]) for t in range(n)]
        for cp in mine:
            cp.start()
        sends = []
        for t in range(n):
            for k, peer in enumerate(peers):
                sends.append(pltpu.make_async_remote_copy(
                    src_ref=src(t, peer), dst_ref=outs[t].at[_slot(me)], send_sem=send_sems.at[7 * t + k],
                    recv_sem=recv_sems.at[7 * t + k], device_id=peer, device_id_type=MESH))
        for cp in sends:
            cp.start()
        for t in range(n):
            for k, peer in enumerate(peers):
                landed = outs[t].at[_slot(peer)]
                pltpu.make_async_remote_copy(
                    src_ref=landed, dst_ref=landed, send_sem=send_sems.at[7 * t + k],
                    recv_sem=recv_sems.at[7 * t + k], device_id=peer, device_id_type=MESH).wait_recv()
        for cp in sends:
            cp.wait_send()
        for cp in mine:
            cp.wait()

    def out_shape(t, a):
        return jax.ShapeDtypeStruct(a.shape if t < n_chunked else (N_DEV,) + a.shape, a.dtype)

    return pl.pallas_call(
        body,
        name=name,
        in_specs=[_ANY] * n,
        out_specs=[_ANY] * n,
        out_shape=[out_shape(t, a) for t, a in enumerate(arrays)],
        scratch_shapes=[pltpu.SemaphoreType.DMA((7 * n,)), pltpu.SemaphoreType.DMA((7 * n,)), pltpu.SemaphoreType.DMA((n,))],
        compiler_params=pltpu.CompilerParams(has_side_effects=True),
    )(*arrays)


def _rope_tables(T):
    half = HEAD_DIM // 2
    inv_freq = ROPE_THETA ** (-jnp.arange(half, dtype=F32) / half)
    ang = jnp.arange(T, dtype=F32)[:, None] * inv_freq[None, :]
    cos, sin = jnp.cos(ang), jnp.sin(ang)
    return jnp.concatenate([cos, cos], axis=-1), jnp.concatenate([-sin, sin], axis=-1)


def _gain_table(q_gain, k_gain):
    return jnp.tile(jnp.concatenate([q_gain, k_gain], axis=0), (1, 4))[:, None, :]


def _layer_fwd(x, p, cos, sin, tag):
    dil_w = 3 * N_GROUPS * GROUP_W
    sb_heads = p["w_up_sb"].shape[0] // HEAD_DIM
    n_qkv_blocks = (dil_w + 3 * sb_heads * HEAD_DIM) // GROUP_W
    s = {"x": x}
    s["h"] = _rmsnorm_fwd(x, p["norm1"], name=f"norm1_fwd{tag}")
    s["proj"] = _matmul(s["h"], p["w_in"], mode="nn", out_dtype=F32, name=f"proj_fwd{tag}")
    s["qkv"] = _prep_fwd(s["proj"], p["gains"], cos, sin, n_qkv_blocks, name=f"prep_fwd{tag}")
    outs = [_dil_fwd(s["qkv"], g, name=f"dil{g}_fwd{tag}") for g in range(N_GROUPS)]
    s["o"], s["ld"] = [o for o, _ in outs], [ld for _, ld in outs]
    s["y_dil"] = _merge_fwd(s["o"], s["ld"], name=f"merge_fwd{tag}")
    s["y_sb32"], s["y_sb"] = _sb_fwd(s["qkv"], sb_heads, dil_w // HEAD_DIM, name=f"sb_fwd{tag}")
    s["mixed"] = _gate_fwd(s["y_dil"], s["y_sb"], p["w_up_dil"], p["w_up_sb"], s["proj"], p["gate_b"],
                           n_qkv_blocks * GROUP_W, name=f"gate_fwd{tag}")
    s["x1"] = _matmul(s["mixed"], p["w_out"], mode="nn", out_dtype=F32, epilogue="add", extra=x, name=f"out_fwd{tag}")
    s["h2"] = _rmsnorm_fwd(s["x1"], p["norm2"], name=f"norm2_fwd{tag}")
    s["f"], s["a"] = _matmul(s["h2"], p["w_ff1"], mode="nn", out_dtype=BF16, epilogue="relu2", name=f"ff1_fwd{tag}")
    x2 = _matmul(s["a"], p["w_ff2"], mode="nn", out_dtype=F32, epilogue="add", extra=s["x1"], name=f"ff2_fwd{tag}")
    return x2, s


def _layer_bwd(dx2, dx2_b, p, s, cos, sin, tag):
    dil_w = 3 * N_GROUPS * GROUP_W
    sb_heads = p["w_up_sb"].shape[0] // HEAD_DIM
    gate_col0 = dil_w + 3 * sb_heads * HEAD_DIM
    g = {}
    df = _matmul(dx2_b, p["w_ff2"], mode="nt", out_dtype=BF16, epilogue="relu2_bwd", extra=s["f"], name=f"ff2_bwd{tag}")
    g["w_ff2"] = _matmul(s["a"], dx2_b, mode="tn", out_dtype=BF16, name=f"ff2_wgrad{tag}")
    dh2 = _matmul(df, p["w_ff1"], mode="nt", out_dtype=F32, name=f"ff1_bwd{tag}")
    g["w_ff1"] = _matmul(s["h2"], df, mode="tn", out_dtype=BF16, name=f"ff1_wgrad{tag}")
    dx1, dx1_b, g["norm2"] = _rmsnorm_bwd(s["x1"], p["norm2"], dh2, dx2, name=f"norm2_bwd{tag}")
    dmixed = _matmul(dx1_b, p["w_out"], mode="nt", out_dtype=F32, name=f"out_bwd{tag}")
    g["w_out"] = _matmul(s["mixed"], dx1_b, mode="tn", out_dtype=BF16, name=f"out_wgrad{tag}")
    dup_dil, dgp0, db0 = _gate_bwd(s["y_dil"], p["w_up_dil"], s["proj"], p["gate_b"], dmixed, 0, gate_col0, name=f"gate0_bwd{tag}")
    dup_sb, dgp1, db1 = _gate_bwd(s["y_sb"], p["w_up_sb"], s["proj"], p["gate_b"], dmixed, 1, gate_col0, name=f"gate1_bwd{tag}")
    g["gate_b"] = jnp.concatenate([db0, db1], axis=0)
    dy_dil = _matmul(dup_dil, p["w_up_dil"], mode="nt", out_dtype=F32, name=f"updil_bwd{tag}")
    g["w_up_dil"] = _matmul(s["y_dil"], dup_dil, mode="tn", out_dtype=BF16, name=f"updil_wgrad{tag}")
    dy_sb = _matmul(dup_sb, p["w_up_sb"], mode="nt", out_dtype=BF16, name=f"upsb_bwd{tag}")
    g["w_up_sb"] = _matmul(s["y_sb"], dup_sb, mode="tn", out_dtype=BF16, name=f"upsb_wgrad{tag}")
    dos, dterms = _merge_bwd(s["o"], s["ld"], dy_dil, name=f"merge_bwd{tag}")
    prev = None
    for grp in range(N_GROUPS):
        prev = _dil_bwd(s["qkv"], dos[grp], s["ld"][grp], dterms[grp], prev, grp, name=f"dil{grp}_bwd{tag}")
    dproj_d, dgain = _prep_bwd(s["proj"], *prev, p["gains"], cos, sin, name=f"prep_bwd{tag}")
    g["q_gain"], g["k_gain"] = dgain[:N_GROUPS, 0], dgain[N_GROUPS:, 0]
    dq_s, dk_s, dv_s = _sb_bwd(s["qkv"], s["y_sb32"], dy_sb, sb_heads, dil_w // HEAD_DIM, name=f"sb_bwd{tag}")
    dproj = jnp.concatenate([dproj_d, dq_s, dk_s, dv_s, dgp0, dgp1], axis=1)
    dh = _matmul(dproj, p["w_in"], mode="nt", out_dtype=F32, name=f"proj_bwd{tag}")
    g["w_in"] = _matmul(s["h"], dproj, mode="tn", out_dtype=BF16, name=f"proj_wgrad{tag}")
    dx, dx_b, g["norm1"] = _rmsnorm_bwd(s["x"], p["norm1"], dh, dx1, name=f"norm1_bwd{tag}")
    return dx, dx_b, g


def _local_step(x, target, layers):
    cos, sin = _rope_tables(x.shape[0])
    saved = []
    for l, p in enumerate(layers):
        x, s = _layer_fwd(x, p, cos, sin, f"_l{l}")
        saved.append(s)
    dx, dx_b, loss = _loss_head(x, target, name="loss_head")
    grads = [None] * len(layers)
    for l in reversed(range(len(layers))):
        dx, dx_b, grads[l] = _layer_bwd(dx, dx_b, layers[l], saved[l], cos, sin, f"_l{l}")
    return loss, dx, grads


_MATRICES = ("w_in", "w_up_dil", "w_up_sb", "w_out", "w_ff1", "w_ff2")
_COLUMN_SHARDED = ("w_in", "w_up_dil", "w_up_sb", "gate_b", "w_ff1")
_SMALL = ("norm1_g", "norm2_g", "q_norm_g", "k_norm_g")


def _unshard(gathered, name, layer):
    blocks = gathered[:, layer]
    if name in _COLUMN_SHARDED:
        return jnp.transpose(blocks, (1, 0, 2)).reshape(blocks.shape[1], N_DEV * blocks.shape[2])
    return blocks.reshape(N_DEV * blocks.shape[1], blocks.shape[2])


def _to_chunks(per_layer, name):
    out = []
    for full in per_layer:
        if name in _COLUMN_SHARDED:
            r, cols = full.shape
            out.append(jnp.transpose(full.reshape(r, N_DEV, cols // N_DEV), (1, 0, 2)))
        else:
            out.append(full.reshape(N_DEV, full.shape[0] // N_DEV, full.shape[1]))
    return jnp.stack(out, axis=1)


def _pack_small(norm1, norm2, qg, kg):
    flat = jnp.concatenate([t.reshape(-1, HEAD_DIM) for t in (norm1, norm2, qg, kg)], axis=0)
    return jnp.pad(flat, ((0, -flat.shape[0] % 8), (0, 0)))


def _unpack_small(packed, shapes):
    out, row = [], 0
    for shape in shapes:
        rows = math.prod(shape) // HEAD_DIM
        out.append(packed[row:row + rows].reshape(shape))
        row += rows
    return out


def kernel(x, norm1_g, w_in, q_norm_g, k_norm_g, w_up_dil, w_up_sb, gate_b, w_out, norm2_g, w_ff1, w_ff2, loss_target, m_norm1_g, m_w_in, m_q_norm_g, m_k_norm_g, m_w_up_dil, m_w_up_sb, m_gate_b, m_w_out, m_norm2_g, m_w_ff1, m_w_ff2, v_norm1_g, v_w_in, v_q_norm_g, v_k_norm_g, v_w_up_dil, v_w_up_sb, v_gate_b, v_w_out, v_norm2_g, v_w_ff1, v_w_ff2):
    names = ("norm1_g", "w_in", "q_norm_g", "k_norm_g", "w_up_dil", "w_up_sb", "gate_b", "w_out", "norm2_g", "w_ff1", "w_ff2")
    w = dict(zip(names, (norm1_g, w_in, q_norm_g, k_norm_g, w_up_dil, w_up_sb, gate_b, w_out, norm2_g, w_ff1, w_ff2)))
    m = dict(zip(names, (m_norm1_g, m_w_in, m_q_norm_g, m_k_norm_g, m_w_up_dil, m_w_up_sb, m_gate_b, m_w_out, m_norm2_g, m_w_ff1, m_w_ff2)))
    v = dict(zip(names, (v_norm1_g, v_w_in, v_q_norm_g, v_k_norm_g, v_w_up_dil, v_w_up_sb, v_gate_b, v_w_out, v_norm2_g, v_w_ff1, v_w_ff2)))
    depth = norm1_g.shape[0]

    gathered = _all_gather([w[n].astype(BF16) for n in _MATRICES] + [gate_b], name="gather_weights")
    gathered = dict(zip(_MATRICES + ("gate_b",), gathered))
    layers = []
    for l in range(depth):
        p = {n: _unshard(gathered[n], n, l) for n in _MATRICES + ("gate_b",)}
        p["norm1"], p["norm2"] = norm1_g[l][None], norm2_g[l][None]
        p["gains"] = _gain_table(q_norm_g[l], k_norm_g[l])
        layers.append(p)

    loss_part, dx, grads = _local_step(x[0], loss_target[0], layers)
    loss = lax.psum(loss_part[0, 0], ("x", "y", "c"))

    sharded = _MATRICES + ("gate_b",)
    chunks = [_to_chunks([g[n] for g in grads], n) for n in sharded]
    small = _pack_small(jnp.concatenate([g["norm1"] for g in grads]), jnp.concatenate([g["norm2"] for g in grads]),
                        jnp.stack([g["q_gain"] for g in grads]), jnp.stack([g["k_gain"] for g in grads]))
    received = _exchange(chunks, [small], name="exchange_grads")
    out = {}
    for n, parts in zip(sharded, received):
        rows, cols = depth * w[n].shape[1], w[n].shape[2]
        res = _reduce_adamw(parts.reshape(N_DEV, rows, cols), w[n].reshape(rows, cols), m[n].reshape(rows, cols),
                            v[n].reshape(rows, cols), name=f"adamw_{n}")
        out[n] = [t.reshape(w[n].shape) for t in res]
    small_res = _reduce_adamw(received[-1], _pack_small(*(w[n] for n in _SMALL)), _pack_small(*(m[n] for n in _SMALL)),
                              _pack_small(*(v[n] for n in _SMALL)), name="adamw_small")
    small_shapes = [w[n].shape for n in _SMALL]
    for k, t in enumerate(small_res):
        for n, arr in zip(_SMALL, _unpack_small(t, small_shapes)):
            out.setdefault(n, [None] * 4)[k] = arr
    return (loss, dx[None], *(out[n][0] for n in names), *(out[n][1] for n in names), *(out[n][2] for n in names),
            *(out[n][3] for n in names))
```

```python
import functools
import math

import jax
import jax.numpy as jnp
from jax import lax
from jax.experimental import pallas as pl
from jax.experimental.pallas import tpu as pltpu

F32 = jnp.float32
BF16 = jnp.bfloat16

HEAD_DIM = 128
BLOCK = 128
N_GROUPS = 3
DILATIONS = (1, 4, 16)
ROPE_THETA = 10000.0
EPS = 1e-6
ADAM_LR = 0.001
ADAM_B1 = 0.9
ADAM_B2 = 0.999
ADAM_EPS = 1e-08
ADAM_WD = 0.01
ADAM_STEP = 10
N_DEV = 8
MESH = pl.DeviceIdType.MESH
VMEM_LIMIT_BYTES = 48 * 1024 * 1024
NEG = -1e30


def _params(sem):
    return pltpu.CompilerParams(dimension_semantics=sem, vmem_limit_bytes=VMEM_LIMIT_BYTES)


def _pick(n, options):
    for o in options:
        if n % o == 0:
            return o
    return n


_DIMS = {"nn": (((1,), (0,)), ((), ())), "nt": (((1,), (1,)), ((), ())), "tn": (((0,), (0,)), ((), ()))}


def _matmul(a, b, *, mode, out_dtype, name, epilogue=None, extra=None, tm=None, tn=None, tk=None):
    if mode == "nn":
        (M, K), (K2, N) = a.shape, b.shape
    elif mode == "nt":
        (M, K), (N, K2) = a.shape, b.shape
    else:
        (K, M), (K2, N) = a.shape, b.shape
    assert K == K2, (a.shape, b.shape, mode)
    tm = tm or _pick(M, (1024, 512, 256, 128))
    tn = tn or _pick(N, (512, 256, 128))
    tk = tk or _pick(K, (2048, 2944, 1024, 512, 256, 128))
    nk = K // tk
    dims = _DIMS[mode]
    n_extra = 0 if extra is None else 1
    n_out = 2 if epilogue == "relu2" else 1

    def body(*refs):
        a_ref, b_ref = refs[0], refs[1]
        extra_ref = refs[2] if n_extra else None
        outs = refs[2 + n_extra:2 + n_extra + n_out]
        acc_ref = refs[-1] if nk > 1 else None

        def finish(acc):
            if epilogue is None:
                outs[0][...] = acc.astype(outs[0].dtype)
            elif epilogue == "add":
                outs[0][...] = (acc + extra_ref[...]).astype(outs[0].dtype)
            elif epilogue == "relu2":
                r = jnp.maximum(acc, 0.0)
                outs[0][...] = r.astype(outs[0].dtype)
                outs[1][...] = (r * r).astype(outs[1].dtype)
            else:
                outs[0][...] = (acc * (2.0 * extra_ref[...].astype(F32))).astype(outs[0].dtype)

        prod = lax.dot_general(a_ref[...], b_ref[...], dims, preferred_element_type=F32)
        if nk == 1:
            finish(prod)
        else:
            k = pl.program_id(2)

            @pl.when(k == 0)
            def _():
                acc_ref[...] = prod

            @pl.when(k > 0)
            def _():
                acc_ref[...] += prod

            @pl.when(k == nk - 1)
            def _():
                finish(acc_ref[...])

    if mode == "nn":
        a_spec = pl.BlockSpec((tm, tk), lambda i, j, k: (i, k))
        b_spec = pl.BlockSpec((tk, tn), lambda i, j, k: (k, j))
    elif mode == "nt":
        a_spec = pl.BlockSpec((tm, tk), lambda i, j, k: (i, k))
        b_spec = pl.BlockSpec((tn, tk), lambda i, j, k: (j, k))
    else:
        a_spec = pl.BlockSpec((tk, tm), lambda i, j, k: (k, i))
        b_spec = pl.BlockSpec((tk, tn), lambda i, j, k: (k, j))
    o_spec = pl.BlockSpec((tm, tn), lambda i, j, k: (i, j))
    in_specs = [a_spec, b_spec] + ([o_spec] if n_extra else [])
    out_shape = [jax.ShapeDtypeStruct((M, N), out_dtype)] * n_out
    res = pl.pallas_call(
        body,
        name=name,
        grid=(M // tm, N // tn, nk),
        in_specs=in_specs,
        out_specs=[o_spec] * n_out,
        out_shape=out_shape,
        scratch_shapes=[pltpu.VMEM((tm, tn), F32)] if nk > 1 else [],
        compiler_params=_params(("parallel", "parallel", "arbitrary")),
    )(a, b, *([extra] if n_extra else []))
    return res if n_out > 1 else res[0]


def _rmsnorm_fwd(x, g, *, name):
    T, D = x.shape
    tm = _pick(T, (512, 256, 128))

    def body(x_ref, g_ref, o_ref):
        xf = x_ref[...]
        r = lax.rsqrt(jnp.mean(xf * xf, axis=-1, keepdims=True) + EPS)
        o_ref[...] = ((xf * r) * g_ref[...]).astype(o_ref.dtype)

    return pl.pallas_call(
        body,
        name=name,
        grid=(T // tm,),
        in_specs=[pl.BlockSpec((tm, D), lambda i: (i, 0)), pl.BlockSpec((1, D), lambda i: (0, 0))],
        out_specs=pl.BlockSpec((tm, D), lambda i: (i, 0)),
        out_shape=jax.ShapeDtypeStruct((T, D), BF16),
        compiler_params=_params(("parallel",)),
    )(x, g)


def _rmsnorm_bwd(x, g, dh, dres, *, name):
    T, D = x.shape
    tm = _pick(T, (256, 128))

    def body(x_ref, g_ref, dh_ref, dres_ref, dx_ref, dxb_ref, dg_ref):
        i = pl.program_id(0)
        xf = x_ref[...]
        r = lax.rsqrt(jnp.mean(xf * xf, axis=-1, keepdims=True) + EPS)
        y = xf * r
        dh_v = dh_ref[...]
        dy = dh_v * g_ref[...]
        c = jnp.mean(dy * y, axis=-1, keepdims=True)
        dx = r * (dy - y * c) + dres_ref[...]
        dx_ref[...] = dx
        dxb_ref[...] = dx.astype(BF16)
        part = jnp.sum(dh_v * y, axis=0, keepdims=True)

        @pl.when(i == 0)
        def _():
            dg_ref[...] = part

        @pl.when(i > 0)
        def _():
            dg_ref[...] += part

    row = pl.BlockSpec((tm, D), lambda i: (i, 0))
    vec = pl.BlockSpec((1, D), lambda i: (0, 0))
    return pl.pallas_call(
        body,
        name=name,
        grid=(T // tm,),
        in_specs=[row, vec, row, row],
        out_specs=[row, row, vec],
        out_shape=[jax.ShapeDtypeStruct((T, D), F32), jax.ShapeDtypeStruct((T, D), BF16), jax.ShapeDtypeStruct((1, D), F32)],
        compiler_params=_params(("arbitrary",)),
    )(x, g, dh, dres)


GROUP_W = 4 * HEAD_DIM
N_DIL_BLOCKS = 3 * N_GROUPS


def _regroup(w, axis):
    n = N_DIL_BLOCKS * GROUP_W
    head, rest = lax.slice_in_dim(w, 0, n, axis=axis), lax.slice_in_dim(w, n, w.shape[axis], axis=axis)
    shape = head.shape[:axis] + (3, N_GROUPS, GROUP_W) + head.shape[axis + 1:]
    head = jnp.swapaxes(head.reshape(shape), axis, axis + 1).reshape(head.shape)
    return jnp.concatenate([head, rest], axis=axis)


def _head_rstd(xh):
    return lax.rsqrt(jnp.mean(xh * xh, axis=-1, keepdims=True) + EPS)


def _gain_row(j):
    return jnp.minimum(j % 3, 1) * N_GROUPS + j // 3


def _prep_fwd(proj, gains, cos, sin, n_sb_blocks, *, name):
    T = proj.shape[0]
    tm = _pick(T, (512, 256, 128))

    def body(p_ref, gain_ref, cos_ref, sin_ref, o0_ref, o1_ref, o2_ref, os_ref):
        j = pl.program_id(1)
        for g, o_ref in enumerate((o0_ref, o1_ref, o2_ref)):
            @pl.when(jnp.logical_and(j // 3 == g, j % 3 < 2))
            def _():
                cos_v, sin_v = cos_ref[...], sin_ref[...]
                for hh in range(4):
                    sl = slice(hh * HEAD_DIM, (hh + 1) * HEAD_DIM)
                    xh = p_ref[:, sl]
                    y = (xh * _head_rstd(xh)) * gain_ref[0, :, sl]
                    o_ref[:, sl] = (y * cos_v + pltpu.roll(y, HEAD_DIM // 2, 1) * sin_v).astype(BF16)

            @pl.when(j == 3 * g + 2)
            def _():
                o_ref[...] = p_ref[...].astype(BF16)

        @pl.when(j >= N_DIL_BLOCKS)
        def _():
            os_ref[...] = p_ref[...].astype(BF16)

    def group_spec(g):
        return pl.BlockSpec((tm, GROUP_W), lambda i, j: (i, jnp.clip(j - 3 * g, 0, 2)))

    return pl.pallas_call(
        body,
        name=name,
        grid=(T // tm, N_DIL_BLOCKS + n_sb_blocks),
        in_specs=[
            pl.BlockSpec((tm, GROUP_W), lambda i, j: (i, j)),
            pl.BlockSpec((1, 1, GROUP_W), lambda i, j: (_gain_row(jnp.minimum(j, N_DIL_BLOCKS - 1)), 0, 0)),
            pl.BlockSpec((tm, HEAD_DIM), lambda i, j: (i, 0)),
            pl.BlockSpec((tm, HEAD_DIM), lambda i, j: (i, 0)),
        ],
        out_specs=[group_spec(0), group_spec(1), group_spec(2),
                   pl.BlockSpec((tm, GROUP_W), lambda i, j: (i, jnp.maximum(j - N_DIL_BLOCKS, 0)))],
        out_shape=[jax.ShapeDtypeStruct((T, 3 * GROUP_W), BF16)] * N_GROUPS
        + [jax.ShapeDtypeStruct((T, n_sb_blocks * GROUP_W), BF16)],
        compiler_params=_params(("parallel", "arbitrary")),
    )(proj, gains, cos, sin)


def _prep_bwd(proj, dqkv, gains, cos, sin, *, name):
    T = proj.shape[0]
    tm = _pick(T, (512, 256, 128))

    def body(p_ref, d0_ref, d1_ref, d2_ref, gain_ref, cos_ref, sin_ref, o_ref, dgain_ref):
        j, i = pl.program_id(0), pl.program_id(1)

        def normed_bwd(d_ref):
            cos_v, sin_v = cos_ref[...], sin_ref[...]
            part = jnp.zeros((1, HEAD_DIM), F32)
            for hh in range(4):
                sl = slice(hh * HEAD_DIM, (hh + 1) * HEAD_DIM)
                xh = p_ref[:, sl]
                r = _head_rstd(xh)
                y0 = xh * r
                d_out = d_ref[:, sl]
                d_yg = d_out * cos_v + pltpu.roll(d_out * sin_v, HEAD_DIM // 2, 1)
                part = part + jnp.sum(d_yg * y0, axis=0, keepdims=True)
                dy0 = d_yg * gain_ref[0, :, sl]
                c = jnp.mean(dy0 * y0, axis=-1, keepdims=True)
                o_ref[:, sl] = (r * (dy0 - y0 * c)).astype(BF16)

            @pl.when(i == 0)
            def _():
                dgain_ref[0] = part

            @pl.when(i > 0)
            def _():
                dgain_ref[0] += part

        for g, d_ref in enumerate((d0_ref, d1_ref, d2_ref)):
            @pl.when(jnp.logical_and(j // 3 == g, j % 3 < 2))
            def _():
                normed_bwd(d_ref)

            @pl.when(j == 3 * g + 2)
            def _():
                o_ref[...] = d_ref[...].astype(BF16)

    gain_row = lambda j, i: (_gain_row(j), 0, 0)

    def grad_spec(g):
        return pl.BlockSpec((tm, GROUP_W), lambda j, i: (i, jnp.clip(j - 3 * g, 0, 2)))

    return pl.pallas_call(
        body,
        name=name,
        grid=(N_DIL_BLOCKS, T // tm),
        in_specs=[
            pl.BlockSpec((tm, GROUP_W), lambda j, i: (i, j)),
            grad_spec(0), grad_spec(1), grad_spec(2),
            pl.BlockSpec((1, 1, GROUP_W), gain_row),
            pl.BlockSpec((tm, HEAD_DIM), lambda j, i: (i, 0)),
            pl.BlockSpec((tm, HEAD_DIM), lambda j, i: (i, 0)),
        ],
        out_specs=[pl.BlockSpec((tm, GROUP_W), lambda j, i: (i, j)), pl.BlockSpec((1, 1, HEAD_DIM), gain_row)],
        out_shape=[jax.ShapeDtypeStruct((T, N_DIL_BLOCKS * GROUP_W), BF16),
                   jax.ShapeDtypeStruct((2 * N_GROUPS, 1, HEAD_DIM), F32)],
        compiler_params=_params(("arbitrary", "arbitrary")),
    )(proj, *dqkv, gains, cos, sin)


def _nt(a, b):
    return lax.dot_general(a, b, _DIMS["nt"], preferred_element_type=F32)


def _tn(a, b):
    return lax.dot_general(a, b, _DIMS["tn"], preferred_element_type=F32)


def _nn(a, b):
    return jnp.dot(a, b, preferred_element_type=F32)


def _window_masks():
    row = lax.broadcasted_iota(jnp.int32, (BLOCK, BLOCK), 0)
    col = lax.broadcasted_iota(jnp.int32, (BLOCK, BLOCK), 1)
    return row >= col, col >= row


def _heads():
    return [slice(hh * HEAD_DIM, (hh + 1) * HEAD_DIM) for hh in range(GROUP_W // HEAD_DIM)]


def _dil_fwd(qkv, g, *, name):
    T = qkv.shape[0]
    r = DILATIONS[g]
    L = T // r
    nb = L // BLOCK
    scale = 1.0 / math.sqrt(HEAD_DIM)
    view = qkv.reshape(L, r * 3 * GROUP_W)

    def body(q_ref, kc_ref, kp_ref, vc_ref, vp_ref, o_ref, ld_ref):
        n = pl.program_id(1)
        m_cur, m_prev = _window_masks()
        m_prev = jnp.logical_and(m_prev, n > 0)
        for sl in _heads():
            q = q_ref[:, sl]
            s_c = jnp.where(m_cur, _nt(q, kc_ref[:, sl]) * scale, NEG)
            s_p = jnp.where(m_prev, _nt(q, kp_ref[:, sl]) * scale, NEG)
            m = jnp.maximum(jnp.max(s_c, axis=-1, keepdims=True), jnp.max(s_p, axis=-1, keepdims=True))
            p_c = jnp.exp(s_c - m)
            p_p = jnp.exp(s_p - m)
            l = jnp.sum(p_c, axis=-1, keepdims=True) + jnp.sum(p_p, axis=-1, keepdims=True)
            inv = 1.0 / l
            o_ref[:, sl] = _nn((p_c * inv).astype(BF16), vc_ref[:, sl]) + _nn((p_p * inv).astype(BF16), vp_ref[:, sl])
            ld_ref[:, sl] = jnp.broadcast_to(m + jnp.log(l), (BLOCK, HEAD_DIM))

    blk = (BLOCK, GROUP_W)
    cur = lambda kind: pl.BlockSpec(blk, lambda c, n: (n, 3 * c + kind))
    prev = lambda kind: pl.BlockSpec(blk, lambda c, n: (jnp.maximum(n - 1, 0), 3 * c + kind))
    out_spec = pl.BlockSpec(blk, lambda c, n: (n, c))
    o, ld = pl.pallas_call(
        body,
        name=name,
        grid=(r, nb),
        in_specs=[cur(0), cur(1), prev(1), cur(2), prev(2)],
        out_specs=[out_spec, out_spec],
        out_shape=[jax.ShapeDtypeStruct((L, r * GROUP_W), F32)] * 2,
        compiler_params=_params(("parallel", "arbitrary")),
    )(view, view, view, view, view)
    return o.reshape(T, GROUP_W), ld.reshape(T, GROUP_W)


def _dil_bwd(qkv, do, ld, dterm, g, *, name):
    T = qkv.shape[0]
    r = DILATIONS[g]
    L = T // r
    nb = L // BLOCK
    scale = 1.0 / math.sqrt(HEAD_DIM)
    view = qkv.reshape(L, r * 3 * GROUP_W)
    do_v, ld_v, dt_v = (t.reshape(L, r * GROUP_W) for t in (do, ld, dterm))

    def body(q_ref, qn_ref, kc_ref, kp_ref, vc_ref, vp_ref, do_ref, don_ref, ld_ref, ldn_ref, dt_ref, dtn_ref, out_ref):
        n = pl.program_id(1)
        m_cur, m_prev = _window_masks()
        has_prev, has_next = jnp.logical_and(m_prev, n > 0), jnp.logical_and(m_prev, n < nb - 1)

        def tile(q, k, v, do_t, ld_t, dt_t, mask):
            s = _nt(q, k) * scale
            p = jnp.where(mask, jnp.exp(s - ld_t[:, 0:1]), 0.0)
            ds = p * (_nt(do_t, v) + dt_t[:, 0:1]) * scale
            return p.astype(BF16), ds.astype(BF16)

        for hh, sl in enumerate(_heads()):
            kc, vc, kp = kc_ref[:, sl], vc_ref[:, sl], kp_ref[:, sl]
            q, do_t, ld_t, dt_t = q_ref[:, sl], do_ref[:, sl], ld_ref[:, sl], dt_ref[:, sl]
            p_cc, ds_cc = tile(q, kc, vc, do_t, ld_t, dt_t, m_cur)
            _, ds_cp = tile(q, kp, vp_ref[:, sl], do_t, ld_t, dt_t, has_prev)
            qn, don = qn_ref[:, sl], don_ref[:, sl]
            p_nc, ds_nc = tile(qn, kc, vc, don, ldn_ref[:, sl], dtn_ref[:, sl], has_next)
            at = lambda kind: slice(kind * GROUP_W + hh * HEAD_DIM, kind * GROUP_W + (hh + 1) * HEAD_DIM)
            out_ref[:, at(0)] = _nn(ds_cc, kc) + _nn(ds_cp, kp)
            out_ref[:, at(1)] = _tn(ds_cc, q) + _tn(ds_nc, qn)
            out_ref[:, at(2)] = _tn(p_cc, do_t) + _tn(p_nc, don)

    blk = (BLOCK, GROUP_W)
    qkv_spec = lambda kind, shift: pl.BlockSpec(blk, lambda c, n: (jnp.clip(n + shift, 0, nb - 1), 3 * c + kind))
    row_spec = lambda shift: pl.BlockSpec(blk, lambda c, n: (jnp.clip(n + shift, 0, nb - 1), c))
    out = pl.pallas_call(
        body,
        name=name,
        grid=(r, nb),
        in_specs=[qkv_spec(0, 0), qkv_spec(0, 1), qkv_spec(1, 0), qkv_spec(1, -1), qkv_spec(2, 0), qkv_spec(2, -1),
                  row_spec(0), row_spec(1), row_spec(0), row_spec(1), row_spec(0), row_spec(1)],
        out_specs=pl.BlockSpec((BLOCK, 3 * GROUP_W), lambda c, n: (n, c)),
        out_shape=jax.ShapeDtypeStruct((L, r * 3 * GROUP_W), F32),
        compiler_params=_params(("parallel", "arbitrary")),
    )(view, view, view, view, view, view, do_v, do_v, ld_v, ld_v, dt_v, dt_v)
    return out.reshape(T, 3 * GROUP_W)


def _group_weights(ld_refs):
    lds = [r[...] for r in ld_refs]
    m = jnp.maximum(jnp.maximum(lds[0], lds[1]), lds[2])
    es = [jnp.exp(v - m) for v in lds]
    inv = 1.0 / (es[0] + es[1] + es[2])
    return [e * inv for e in es]


def _merge_fwd(os_, lds, *, name):
    T = os_[0].shape[0]
    tm = _pick(T, (1024, 512, 256, 128))

    def body(o0, o1, o2, l0, l1, l2, y_ref):
        w = _group_weights((l0, l1, l2))
        y_ref[...] = (w[0] * o0[...] + w[1] * o1[...] + w[2] * o2[...]).astype(BF16)

    spec = pl.BlockSpec((tm, GROUP_W), lambda i: (i, 0))
    return pl.pallas_call(
        body, name=name, grid=(T // tm,), in_specs=[spec] * 6, out_specs=spec,
        out_shape=jax.ShapeDtypeStruct((T, GROUP_W), BF16), compiler_params=_params(("parallel",)),
    )(*os_, *lds)


def _merge_bwd(os_, lds, dy, *, name):
    T = dy.shape[0]
    tm = _pick(T, (512, 256, 128))

    def body(o0, o1, o2, l0, l1, l2, dy_ref, do0, do1, do2, dt0, dt1, dt2):
        w = _group_weights((l0, l1, l2))
        dy_v = dy_ref[...]
        y = w[0] * o0[...] + w[1] * o1[...] + w[2] * o2[...]
        prod = dy_v * y
        for hh in range(4):
            sl = slice(hh * HEAD_DIM, (hh + 1) * HEAD_DIM)
            s = jnp.sum(prod[:, sl], axis=-1, keepdims=True)
            for wg, dt in zip(w, (dt0, dt1, dt2)):
                dt[:, sl] = -wg[:, sl] * s
        for wg, do in zip(w, (do0, do1, do2)):
            do[...] = (wg * dy_v).astype(BF16)

    spec = pl.BlockSpec((tm, GROUP_W), lambda i: (i, 0))
    outs = pl.pallas_call(
        body, name=name, grid=(T // tm,), in_specs=[spec] * 7, out_specs=[spec] * 6,
        out_shape=[jax.ShapeDtypeStruct((T, GROUP_W), BF16)] * 3 + [jax.ShapeDtypeStruct((T, GROUP_W), F32)] * 3,
        compiler_params=_params(("parallel",)),
    )(*os_, *lds, dy)
    return outs[:3], outs[3:]


SB_ROWS = 512


def _sum_matrix(inclusive):
    j = lax.broadcasted_iota(jnp.int32, (2 * BLOCK, 2 * BLOCK), 0) % BLOCK
    s = lax.broadcasted_iota(jnp.int32, (2 * BLOCK, 2 * BLOCK), 1)
    later = (j >= s) if inclusive else (j > s)
    return jnp.logical_or(s >= BLOCK, later).astype(BF16)


def _block_sums(x, mat):
    hi = x.astype(BF16)
    lo = (x - hi.astype(F32)).astype(BF16)
    r = _nn(jnp.concatenate([hi, lo], axis=1), mat)
    return r[:, :BLOCK], r[:, BLOCK:]


def _log_terms(z):
    t = jnp.log(1.0 + jnp.exp(-jnp.abs(z)))
    return -(jnp.maximum(z, 0.0) + t), jnp.minimum(z, 0.0) - t


def _causal_mask(rows):
    row = lax.broadcasted_iota(jnp.int32, (rows, rows), 0)
    col = lax.broadcasted_iota(jnp.int32, (rows, rows), 1)
    return col < row


def _sb_fwd(qkv, n_heads, col0, *, name):
    T = qkv.shape[0]
    tq = _pick(T, (SB_ROWS, BLOCK))
    nq, nsub = T // tq, tq // BLOCK
    scale = 1.0 / math.sqrt(HEAD_DIM)

    def body(q_ref, k_ref, v_ref, o_ref, ob_ref):
        i = pl.program_id(1)
        q = q_ref[...]
        mat = _sum_matrix(False)

        def chunk(j, carry, acc, mask):
            rows = pl.ds(pl.multiple_of(j * tq, tq), tq)
            z = _nt(q, k_ref[rows, :]) * scale
            lk, ls = _log_terms(z)
            if mask is not None:
                lk = jnp.where(mask, lk, 0.0)
            a = []
            for b in reversed(range(nsub)):
                sl = slice(b * BLOCK, (b + 1) * BLOCK)
                later, total = _block_sums(lk[:, sl], mat)
                a.append(jnp.exp(ls[:, sl] + (later + carry)))
                carry = carry + total
            a = jnp.concatenate(a[::-1], axis=1)
            if mask is not None:
                a = jnp.where(mask, a, 0.0)
            return carry, acc + _nn(a.astype(BF16), v_ref[rows, :])

        zero = jnp.zeros((tq, HEAD_DIM), F32)
        carry, acc = chunk(i, zero, zero, _causal_mask(tq))

        def step(jj, state):
            return chunk(i - 1 - jj, state[0], state[1], None)

        carry, acc = lax.fori_loop(0, i, step, (carry, acc))
        o_ref[...] = acc
        ob_ref[...] = acc.astype(BF16)

    blk = (tq, HEAD_DIM)
    out_spec = pl.BlockSpec(blk, lambda h, i: (i, h))
    return pl.pallas_call(
        body,
        name=name,
        grid=(n_heads, nq),
        in_specs=[
            pl.BlockSpec(blk, lambda h, i: (i, col0 + h)),
            pl.BlockSpec((T, HEAD_DIM), lambda h, i: (0, col0 + n_heads + h)),
            pl.BlockSpec((T, HEAD_DIM), lambda h, i: (0, col0 + 2 * n_heads + h)),
        ],
        out_specs=[out_spec, out_spec],
        out_shape=[jax.ShapeDtypeStruct((T, n_heads * HEAD_DIM), F32), jax.ShapeDtypeStruct((T, n_heads * HEAD_DIM), BF16)],
        compiler_params=_params(("parallel", "arbitrary")),
    )(qkv, qkv, qkv)


def _sb_bwd(qkv, o32, do, n_heads, col0, *, name):
    T = qkv.shape[0]
    tq = _pick(T, (SB_ROWS, BLOCK))
    nq, nsub = T // tq, tq // BLOCK
    scale = 1.0 / math.sqrt(HEAD_DIM)

    def body(q_ref, k_ref, v_ref, o_ref, do_ref, dq_ref, dk_ref, dv_ref, dk_acc, dv_acc):
        i = pl.program_id(1)

        @pl.when(i == 0)
        def _():
            dk_acc[...] = jnp.zeros_like(dk_acc)
            dv_acc[...] = jnp.zeros_like(dv_acc)

        q, do_t = q_ref[...], do_ref[...]
        delta = jnp.broadcast_to(jnp.sum(do_t.astype(F32) * o_ref[...], axis=-1, keepdims=True), (tq, HEAD_DIM))
        mat, mat_incl = _sum_matrix(False), _sum_matrix(True)

        def chunk(j, carry_b, carry_g, dq, mask):
            rows = pl.ds(pl.multiple_of(j * tq, tq), tq)
            k_t, v_t = k_ref[rows, :], v_ref[rows, :]
            z = _nt(q, k_t) * scale
            lk, ls = _log_terms(z)
            if mask is not None:
                lk = jnp.where(mask, lk, 0.0)
            d_a = _nt(do_t, v_t)
            a_parts, dz_parts = [], []
            for b in reversed(range(nsub)):
                sl = slice(b * BLOCK, (b + 1) * BLOCK)
                later, total = _block_sums(lk[:, sl], mat)
                a = jnp.exp(ls[:, sl] + (later + carry_b))
                carry_b = carry_b + total
                if mask is not None:
                    a = jnp.where(mask[:, sl], a, 0.0)
                a_b = a.astype(BF16)
                g = a_b.astype(F32) * d_a[:, sl]
                from_here, total_g = _block_sums(g, mat_incl)
                before = delta - (from_here + carry_g)
                carry_g = carry_g + total_g
                sig = jnp.exp(ls[:, sl])
                dz = (g - sig * (g + before)) * scale
                a_parts.append(a_b)
                dz_parts.append(dz)
            dz = jnp.concatenate(dz_parts[::-1], axis=1)
            if mask is not None:
                dz = jnp.where(mask, dz, 0.0)
            dz_b = dz.astype(BF16)
            dk_acc[rows, :] += _tn(dz_b, q)
            dv_acc[rows, :] += _tn(jnp.concatenate(a_parts[::-1], axis=1), do_t)
            return carry_b, carry_g, dq + _nn(dz_b, k_t)

        zero = jnp.zeros((tq, HEAD_DIM), F32)
        state = chunk(i, zero, zero, zero, _causal_mask(tq))

        def step(jj, st):
            return chunk(i - 1 - jj, st[0], st[1], st[2], None)

        state = lax.fori_loop(0, i, step, state)
        dq_ref[...] = state[2].astype(BF16)

        @pl.when(i == nq - 1)
        def _():
            dk_ref[...] = dk_acc[...].astype(BF16)
            dv_ref[...] = dv_acc[...].astype(BF16)

    blk = (tq, HEAD_DIM)
    full = (T, HEAD_DIM)
    dshape = jax.ShapeDtypeStruct((T, n_heads * HEAD_DIM), BF16)
    return pl.pallas_call(
        body,
        name=name,
        grid=(n_heads, nq),
        in_specs=[
            pl.BlockSpec(blk, lambda h, i: (i, col0 + h)),
            pl.BlockSpec(full, lambda h, i: (0, col0 + n_heads + h)),
            pl.BlockSpec(full, lambda h, i: (0, col0 + 2 * n_heads + h)),
            pl.BlockSpec(blk, lambda h, i: (i, h)),
            pl.BlockSpec(blk, lambda h, i: (i, h)),
        ],
        out_specs=[pl.BlockSpec(blk, lambda h, i: (i, h)), pl.BlockSpec(full, lambda h, i: (0, h)),
                   pl.BlockSpec(full, lambda h, i: (0, h))],
        out_shape=[dshape, dshape, dshape],
        scratch_shapes=[pltpu.VMEM(full, F32), pltpu.VMEM(full, F32)],
        compiler_params=_params(("arbitrary", "arbitrary")),
    )(qkv, qkv, qkv, o32, do)


def _gate_fwd(y_dil, y_sb, w_up_dil, w_up_sb, proj, gate_b, gate_col0, *, name):
    T, D = y_dil.shape[0], w_up_dil.shape[1]
    tm = _pick(T, (512, 256, 128))
    tn = _pick(D, (512, 256, 128))
    c0, nbr = gate_col0 // tn, D // tn

    def body(yd_ref, ys_ref, wd_ref, ws_ref, gp0_ref, gp1_ref, b_ref, o_ref):
        g0 = jax.nn.sigmoid(gp0_ref[...] + b_ref[0:1, :])
        g1 = jax.nn.sigmoid(gp1_ref[...] + b_ref[1:2, :])
        o_ref[...] = (g0 * _nn(yd_ref[...], wd_ref[...]) + g1 * _nn(ys_ref[...], ws_ref[...])).astype(BF16)

    return pl.pallas_call(
        body,
        name=name,
        grid=(T // tm, nbr),
        in_specs=[
            pl.BlockSpec((tm, y_dil.shape[1]), lambda i, j: (i, 0)),
            pl.BlockSpec((tm, y_sb.shape[1]), lambda i, j: (i, 0)),
            pl.BlockSpec((w_up_dil.shape[0], tn), lambda i, j: (0, j)),
            pl.BlockSpec((w_up_sb.shape[0], tn), lambda i, j: (0, j)),
            pl.BlockSpec((tm, tn), lambda i, j: (i, c0 + j)),
            pl.BlockSpec((tm, tn), lambda i, j: (i, c0 + nbr + j)),
            pl.BlockSpec((2, tn), lambda i, j: (0, j)),
        ],
        out_specs=pl.BlockSpec((tm, tn), lambda i, j: (i, j)),
        out_shape=jax.ShapeDtypeStruct((T, D), BF16),
        compiler_params=_params(("parallel", "parallel")),
    )(y_dil, y_sb, w_up_dil, w_up_sb, proj, proj, gate_b)


def _gate_bwd(y, w_up, proj, gate_b, dmixed, branch, gate_col0, *, name):
    T, D = y.shape[0], w_up.shape[1]
    tm = _pick(T, (512, 256, 128))
    tn = _pick(D, (512, 256, 128))
    c0 = gate_col0 // tn + branch * (D // tn)

    def body(y_ref, w_ref, gp_ref, b_ref, dm_ref, dup_ref, dgp_ref, db_ref):
        i = pl.program_id(1)
        g = jax.nn.sigmoid(gp_ref[...] + b_ref[branch:branch + 1, :])
        dm = dm_ref[...]
        dup_ref[...] = (dm * g).astype(BF16)
        dgp = (dm * _nn(y_ref[...], w_ref[...])) * (g * (1.0 - g))
        dgp_ref[...] = dgp.astype(BF16)
        part = jnp.sum(dgp, axis=0, keepdims=True)

        @pl.when(i == 0)
        def _():
            db_ref[...] = part

        @pl.when(i > 0)
        def _():
            db_ref[...] += part

    tile = pl.BlockSpec((tm, tn), lambda j, i: (i, j))
    return pl.pallas_call(
        body,
        name=name,
        grid=(D // tn, T // tm),
        in_specs=[
            pl.BlockSpec((tm, y.shape[1]), lambda j, i: (i, 0)),
            pl.BlockSpec((w_up.shape[0], tn), lambda j, i: (0, j)),
            pl.BlockSpec((tm, tn), lambda j, i: (i, c0 + j)),
            pl.BlockSpec((2, tn), lambda j, i: (0, j)),
            tile,
        ],
        out_specs=[tile, tile, pl.BlockSpec((1, tn), lambda j, i: (0, j))],
        out_shape=[jax.ShapeDtypeStruct((T, D), BF16), jax.ShapeDtypeStruct((T, D), BF16), jax.ShapeDtypeStruct((1, D), F32)],
        compiler_params=_params(("parallel", "arbitrary")),
    )(y, w_up, proj, gate_b, dmixed)


def _loss_head(y, target, *, name):
    T, D = y.shape
    tm = _pick(T, (256, 128))

    def body(y_ref, t_ref, dy_ref, dyb_ref, l_ref):
        i = pl.program_id(0)
        err = y_ref[...] - t_ref[...]
        dy = err * (1.0 / D)
        dy_ref[...] = dy
        dyb_ref[...] = dy.astype(BF16)
        part = 0.5 * jnp.sum(jnp.mean(err * err, axis=-1, keepdims=True), axis=0, keepdims=True)

        @pl.when(i == 0)
        def _():
            l_ref[...] = part

        @pl.when(i > 0)
        def _():
            l_ref[...] += part

    row = pl.BlockSpec((tm, D), lambda i: (i, 0))
    return pl.pallas_call(
        body, name=name, grid=(T // tm,), in_specs=[row, row],
        out_specs=[row, row, pl.BlockSpec((1, 1), lambda i: (0, 0))],
        out_shape=[jax.ShapeDtypeStruct((T, D), F32), jax.ShapeDtypeStruct((T, D), BF16), jax.ShapeDtypeStruct((1, 1), F32)],
        compiler_params=_params(("arbitrary",)),
    )(y, target)


def _reduce_adamw(parts, w, m, v, *, name):
    R, C = w.shape
    tr = R
    for cand in (1024, 512, 256, 128, 64, 32, 16, 8):
        if R % cand == 0 and cand * C <= 256 * 1024:
            tr = cand
            break

    def body(p_ref, w_ref, m_ref, v_ref, g_ref, d_ref, nm_ref, nv_ref):
        g = p_ref[0].astype(F32)
        for s in range(1, N_DEV):
            g = g + p_ref[s].astype(F32)
        m_new = ADAM_B1 * m_ref[...] + (1.0 - ADAM_B1) * g
        v_new = ADAM_B2 * v_ref[...] + (1.0 - ADAM_B2) * (g * g)
        m_hat = m_new / (1.0 - ADAM_B1 ** ADAM_STEP)
        v_hat = v_new / (1.0 - ADAM_B2 ** ADAM_STEP)
        g_ref[...] = g
        d_ref[...] = -ADAM_LR * (m_hat / (jnp.sqrt(v_hat) + ADAM_EPS) + ADAM_WD * w_ref[...])
        nm_ref[...] = m_new
        nv_ref[...] = v_new

    row = pl.BlockSpec((tr, C), lambda i: (i, 0))
    return pl.pallas_call(
        body, name=name, grid=(R // tr,),
        in_specs=[pl.BlockSpec((N_DEV, tr, C), lambda i: (0, i, 0)), row, row, row],
        out_specs=[row] * 4, out_shape=[jax.ShapeDtypeStruct((R, C), F32)] * 4,
        compiler_params=_params(("parallel",)),
    )(parts, w, m, v)


_ANY = pl.BlockSpec(memory_space=pl.ANY)


def _place():
    return lax.axis_index("x"), lax.axis_index("y"), lax.axis_index("c")


def _slot(p):
    return 4 * p[0] + 2 * p[1] + p[2]


def _all_gather(shards, *, name):
    n = len(shards)

    def body(*refs):
        ins, outs = refs[:n], refs[n:2 * n]
        send_sems, recv_sems, local_sems = refs[2 * n:]
        x, y, c = _place()
        me, sibling = (x, y, c), (x, y, 1 - c)
        chips = [(1 - x, y), (x, 1 - y), (1 - x, 1 - y)]

        def copy(t, k, block, to, src=None):
            dst = outs[t].at[_slot(block)]
            return pltpu.make_async_remote_copy(
                src_ref=dst if src is None else src, dst_ref=dst, send_sem=send_sems.at[7 * t + k],
                recv_sem=recv_sems.at[7 * t + k], device_id=to, device_id_type=MESH)

        mine = [pltpu.make_async_copy(ins[t], outs[t].at[_slot(me)], local_sems.at[t]) for t in range(n)]
        for cp in mine:
            cp.start()
        first = []
        for t in range(n):
            first.append(copy(t, 0, me, sibling, src=ins[t]))
            first += [copy(t, 1 + j, me, (*chip, c), src=ins[t]) for j, chip in enumerate(chips)]
        for cp in first:
            cp.start()
        passed = []
        for j, chip in enumerate(chips):
            for t in range(n):
                copy(t, 1 + j, (*chip, c), me).wait_recv()
                cp = copy(t, 4 + j, (*chip, c), sibling)
                cp.start()
                passed.append(cp)
        for t in range(n):
            copy(t, 0, sibling, me).wait_recv()
            for j, chip in enumerate(chips):
                copy(t, 4 + j, (*chip, 1 - c), me).wait_recv()
        for cp in first + passed:
            cp.wait_send()
        for cp in mine:
            cp.wait()

    return pl.pallas_call(
        body,
        name=name,
        in_specs=[_ANY] * n,
        out_specs=[_ANY] * n,
        out_shape=[jax.ShapeDtypeStruct((N_DEV,) + s.shape, s.dtype) for s in shards],
        scratch_shapes=[pltpu.SemaphoreType.DMA((7 * n,)), pltpu.SemaphoreType.DMA((7 * n,)), pltpu.SemaphoreType.DMA((n,))],
        compiler_params=pltpu.CompilerParams(has_side_effects=True),
    )(*shards)


def _exchange(chunked, whole, *, name):
    arrays = list(chunked) + list(whole)
    n, n_chunked = len(arrays), len(chunked)

    def body(*refs):
        ins, outs = refs[:n], refs[n:2 * n]
        send_sems, recv_sems, local_sems = refs[2 * n:]
        x, y, c = _place()
        me = (x, y, c)
        flips = [(fx, fy, fc) for fx in (0, 1) for fy in (0, 1) for fc in (0, 1)][1:]
        peers = [tuple(1 - v if f else v for v, f in zip(me, flip)) for flip in flips]

        def src(t, dest):
            return ins[t].at[_slot(dest)] if t < n_chunked else ins[t]

        mine = [pltpu.make_async_copy(src(t, me), outs[t].at[_slot(me)], local_sems.at[t]) for t in range(n)]
        for cp in mine:
            cp.start()
        sends = []
        for t in range(n):
            for k, peer in enumerate(peers):
                sends.append(pltpu.make_async_remote_copy(
                    src_ref=src(t, peer), dst_ref=outs[t].at[_slot(me)], send_sem=send_sems.at[7 * t + k],
                    recv_sem=recv_sems.at[7 * t + k], device_id=peer, device_id_type=MESH))
        for cp in sends:
            cp.start()
        for t in range(n):
            for k, peer in enumerate(peers):
                landed = outs[t].at[_slot(peer)]
                pltpu.make_async_remote_copy(
                    src_ref=landed, dst_ref=landed, send_sem=send_sems.at[7 * t + k],
                    recv_sem=recv_sems.at[7 * t + k], device_id=peer, device_id_type=MESH).wait_recv()
        for cp in sends:
            cp.wait_send()
        for cp in mine:
            cp.wait()

    def out_shape(t, a):
        return jax.ShapeDtypeStruct(a.shape if t < n_chunked else (N_DEV,) + a.shape, a.dtype)

    return pl.pallas_call(
        body,
        name=name,
        in_specs=[_ANY] * n,
        out_specs=[_ANY] * n,
        out_shape=[out_shape(t, a) for t, a in enumerate(arrays)],
        scratch_shapes=[pltpu.SemaphoreType.DMA((7 * n,)), pltpu.SemaphoreType.DMA((7 * n,)), pltpu.SemaphoreType.DMA((n,))],
        compiler_params=pltpu.CompilerParams(has_side_effects=True),
    )(*arrays)


def _rope_tables(T):
    half = HEAD_DIM // 2
    inv_freq = ROPE_THETA ** (-jnp.arange(half, dtype=F32) / half)
    ang = jnp.arange(T, dtype=F32)[:, None] * inv_freq[None, :]
    cos, sin = jnp.cos(ang), jnp.sin(ang)
    return jnp.concatenate([cos, cos], axis=-1), jnp.concatenate([-sin, sin], axis=-1)


def _gain_table(q_gain, k_gain):
    return jnp.tile(jnp.concatenate([q_gain, k_gain], axis=0), (1, 4))[:, None, :]


def _layer_fwd(x, p, cos, sin, tag):
    sb_heads = p["w_up_sb"].shape[0] // HEAD_DIM
    n_sb_blocks = 3 * sb_heads * HEAD_DIM // GROUP_W
    s = {"x": x}
    s["h"] = _rmsnorm_fwd(x, p["norm1"], name=f"norm1_fwd{tag}")
    s["proj"] = _matmul(s["h"], p["w_in"], mode="nn", out_dtype=F32, name=f"proj_fwd{tag}")
    *s["qkv_d"], s["qkv_s"] = _prep_fwd(s["proj"], p["gains"], cos, sin, n_sb_blocks, name=f"prep_fwd{tag}")
    outs = [_dil_fwd(s["qkv_d"][g], g, name=f"dil{g}_fwd{tag}") for g in range(N_GROUPS)]
    s["o"], s["ld"] = [o for o, _ in outs], [ld for _, ld in outs]
    s["y_dil"] = _merge_fwd(s["o"], s["ld"], name=f"merge_fwd{tag}")
    s["y_sb32"], s["y_sb"] = _sb_fwd(s["qkv_s"], sb_heads, 0, name=f"sb_fwd{tag}")
    s["mixed"] = _gate_fwd(s["y_dil"], s["y_sb"], p["w_up_dil"], p["w_up_sb"], s["proj"], p["gate_b"],
                           (N_DIL_BLOCKS + n_sb_blocks) * GROUP_W, name=f"gate_fwd{tag}")
    s["x1"] = _matmul(s["mixed"], p["w_out"], mode="nn", out_dtype=F32, epilogue="add", extra=x, name=f"out_fwd{tag}")
    s["h2"] = _rmsnorm_fwd(s["x1"], p["norm2"], name=f"norm2_fwd{tag}")
    s["f"], s["a"] = _matmul(s["h2"], p["w_ff1"], mode="nn", out_dtype=BF16, epilogue="relu2", name=f"ff1_fwd{tag}")
    x2 = _matmul(s["a"], p["w_ff2"], mode="nn", out_dtype=F32, epilogue="add", extra=s["x1"], name=f"ff2_fwd{tag}")
    return x2, s


def _layer_bwd(dx2, dx2_b, p, s, cos, sin, tag):
    sb_heads = p["w_up_sb"].shape[0] // HEAD_DIM
    gate_col0 = N_DIL_BLOCKS * GROUP_W + 3 * sb_heads * HEAD_DIM
    g = {}
    df = _matmul(dx2_b, p["w_ff2"], mode="nt", out_dtype=BF16, epilogue="relu2_bwd", extra=s["f"], name=f"ff2_bwd{tag}")
    g["w_ff2"] = _matmul(s["a"], dx2_b, mode="tn", out_dtype=BF16, name=f"ff2_wgrad{tag}")
    dh2 = _matmul(df, p["w_ff1"], mode="nt", out_dtype=F32, name=f"ff1_bwd{tag}")
    g["w_ff1"] = _matmul(s["h2"], df, mode="tn", out_dtype=BF16, name=f"ff1_wgrad{tag}")
    dx1, dx1_b, g["norm2"] = _rmsnorm_bwd(s["x1"], p["norm2"], dh2, dx2, name=f"norm2_bwd{tag}")
    dmixed = _matmul(dx1_b, p["w_out"], mode="nt", out_dtype=F32, name=f"out_bwd{tag}")
    g["w_out"] = _matmul(s["mixed"], dx1_b, mode="tn", out_dtype=BF16, name=f"out_wgrad{tag}")
    dup_dil, dgp0, db0 = _gate_bwd(s["y_dil"], p["w_up_dil"], s["proj"], p["gate_b"], dmixed, 0, gate_col0, name=f"gate0_bwd{tag}")
    dup_sb, dgp1, db1 = _gate_bwd(s["y_sb"], p["w_up_sb"], s["proj"], p["gate_b"], dmixed, 1, gate_col0, name=f"gate1_bwd{tag}")
    g["gate_b"] = jnp.concatenate([db0, db1], axis=0)
    dy_dil = _matmul(dup_dil, p["w_up_dil"], mode="nt", out_dtype=F32, name=f"updil_bwd{tag}")
    g["w_up_dil"] = _matmul(s["y_dil"], dup_dil, mode="tn", out_dtype=BF16, name=f"updil_wgrad{tag}")
    dy_sb = _matmul(dup_sb, p["w_up_sb"], mode="nt", out_dtype=BF16, name=f"upsb_bwd{tag}")
    g["w_up_sb"] = _matmul(s["y_sb"], dup_sb, mode="tn", out_dtype=BF16, name=f"upsb_wgrad{tag}")
    dos, dterms = _merge_bwd(s["o"], s["ld"], dy_dil, name=f"merge_bwd{tag}")
    dqkv = [_dil_bwd(s["qkv_d"][grp], dos[grp], s["ld"][grp], dterms[grp], grp, name=f"dil{grp}_bwd{tag}")
            for grp in range(N_GROUPS)]
    dproj_d, dgain = _prep_bwd(s["proj"], dqkv, p["gains"], cos, sin, name=f"prep_bwd{tag}")
    g["q_gain"], g["k_gain"] = dgain[:N_GROUPS, 0], dgain[N_GROUPS:, 0]
    dq_s, dk_s, dv_s = _sb_bwd(s["qkv_s"], s["y_sb32"], dy_sb, sb_heads, 0, name=f"sb_bwd{tag}")
    dproj = jnp.concatenate([dproj_d, dq_s, dk_s, dv_s, dgp0, dgp1], axis=1)
    dh = _matmul(dproj, p["w_in"], mode="nt", out_dtype=F32, name=f"proj_bwd{tag}")
    g["w_in"] = _matmul(s["h"], dproj, mode="tn", out_dtype=BF16, name=f"proj_wgrad{tag}")
    dx, dx_b, g["norm1"] = _rmsnorm_bwd(s["x"], p["norm1"], dh, dx1, name=f"norm1_bwd{tag}")
    return dx, dx_b, g


def _local_step(x, target, layers):
    cos, sin = _rope_tables(x.shape[0])
    saved = []
    for l, p in enumerate(layers):
        x, s = _layer_fwd(x, p, cos, sin, f"_l{l}")
        saved.append(s)
    dx, dx_b, loss = _loss_head(x, target, name="loss_head")
    grads = [None] * len(layers)
    for l in reversed(range(len(layers))):
        dx, dx_b, grads[l] = _layer_bwd(dx, dx_b, layers[l], saved[l], cos, sin, f"_l{l}")
    return loss, dx, grads


_MATRICES = ("w_in", "w_up_dil", "w_up_sb", "w_out", "w_ff1", "w_ff2")
_COLUMN_SHARDED = ("w_in", "w_up_dil", "w_up_sb", "gate_b", "w_ff1")
_SMALL = ("norm1_g", "norm2_g", "q_norm_g", "k_norm_g")


def _unshard(gathered, name, layer):
    blocks = gathered[:, layer]
    if name in _COLUMN_SHARDED:
        return jnp.transpose(blocks, (1, 0, 2)).reshape(blocks.shape[1], N_DEV * blocks.shape[2])
    return blocks.reshape(N_DEV * blocks.shape[1], blocks.shape[2])


def _to_chunks(per_layer, name):
    out = []
    for full in per_layer:
        if name in _COLUMN_SHARDED:
            r, cols = full.shape
            out.append(jnp.transpose(full.reshape(r, N_DEV, cols // N_DEV), (1, 0, 2)))
        else:
            out.append(full.reshape(N_DEV, full.shape[0] // N_DEV, full.shape[1]))
    return jnp.stack(out, axis=1)


def _pack_small(norm1, norm2, qg, kg):
    flat = jnp.concatenate([t.reshape(-1, HEAD_DIM) for t in (norm1, norm2, qg, kg)], axis=0)
    return jnp.pad(flat, ((0, -flat.shape[0] % 8), (0, 0)))


def _unpack_small(packed, shapes):
    out, row = [], 0
    for shape in shapes:
        rows = math.prod(shape) // HEAD_DIM
        out.append(packed[row:row + rows].reshape(shape))
        row += rows
    return out


def kernel(x, norm1_g, w_in, q_norm_g, k_norm_g, w_up_dil, w_up_sb, gate_b, w_out, norm2_g, w_ff1, w_ff2, loss_target, m_norm1_g, m_w_in, m_q_norm_g, m_k_norm_g, m_w_up_dil, m_w_up_sb, m_gate_b, m_w_out, m_norm2_g, m_w_ff1, m_w_ff2, v_norm1_g, v_w_in, v_q_norm_g, v_k_norm_g, v_w_up_dil, v_w_up_sb, v_gate_b, v_w_out, v_norm2_g, v_w_ff1, v_w_ff2):
    names = ("norm1_g", "w_in", "q_norm_g", "k_norm_g", "w_up_dil", "w_up_sb", "gate_b", "w_out", "norm2_g", "w_ff1", "w_ff2")
    w = dict(zip(names, (norm1_g, w_in, q_norm_g, k_norm_g, w_up_dil, w_up_sb, gate_b, w_out, norm2_g, w_ff1, w_ff2)))
    m = dict(zip(names, (m_norm1_g, m_w_in, m_q_norm_g, m_k_norm_g, m_w_up_dil, m_w_up_sb, m_gate_b, m_w_out, m_norm2_g, m_w_ff1, m_w_ff2)))
    v = dict(zip(names, (v_norm1_g, v_w_in, v_q_norm_g, v_k_norm_g, v_w_up_dil, v_w_up_sb, v_gate_b, v_w_out, v_norm2_g, v_w_ff1, v_w_ff2)))
    depth = norm1_g.shape[0]

    gathered = _all_gather([w[n].astype(BF16) for n in _MATRICES] + [gate_b], name="gather_weights")
    gathered = dict(zip(_MATRICES + ("gate_b",), gathered))
    layers = []
    for l in range(depth):
        p = {n: _unshard(gathered[n], n, l) for n in _MATRICES + ("gate_b",)}
        p["w_in"] = _regroup(p["w_in"], 1)
        p["norm1"], p["norm2"] = norm1_g[l][None], norm2_g[l][None]
        p["gains"] = _gain_table(q_norm_g[l], k_norm_g[l])
        layers.append(p)

    loss_part, dx, grads = _local_step(x[0], loss_target[0], layers)
    loss = lax.psum(loss_part[0, 0], ("x", "y", "c"))

    sharded = _MATRICES + ("gate_b",)
    for g in grads:
        g["w_in"] = _regroup(g["w_in"], 1)
    chunks = [_to_chunks([g[n] for g in grads], n) for n in sharded]
    small = _pack_small(jnp.concatenate([g["norm1"] for g in grads]), jnp.concatenate([g["norm2"] for g in grads]),
                        jnp.stack([g["q_gain"] for g in grads]), jnp.stack([g["k_gain"] for g in grads]))
    received = _exchange(chunks, [small], name="exchange_grads")
    out = {}
    for n, parts in zip(sharded, received):
        rows, cols = depth * w[n].shape[1], w[n].shape[2]
        res = _reduce_adamw(parts.reshape(N_DEV, rows, cols), w[n].reshape(rows, cols), m[n].reshape(rows, cols),
                            v[n].reshape(rows, cols), name=f"adamw_{n}")
        out[n] = [t.reshape(w[n].shape) for t in res]
    small_res = _reduce_adamw(received[-1], _pack_small(*(w[n] for n in _SMALL)), _pack_small(*(m[n] for n in _SMALL)),
                              _pack_small(*(v[n] for n in _SMALL)), name="adamw_small")
    small_shapes = [w[n].shape for n in _SMALL]
    for k, t in enumerate(small_res):
        for n, arr in zip(_SMALL, _unpack_small(t, small_shapes)):
            out.setdefault(n, [None] * 4)[k] = arr
    return (loss, dx[None], *(out[n][0] for n in names), *(out[n][1] for n in names), *(out[n][2] for n in names),
            *(out[n][3] for n in names))
```

```python
import functools
import math

import jax
import jax.numpy as jnp
from jax import lax
from jax.experimental import pallas as pl
from jax.experimental.pallas import tpu as pltpu

F32 = jnp.float32
BF16 = jnp.bfloat16

HEAD_DIM = 128
BLOCK = 128
N_GROUPS = 3
DILATIONS = (1, 4, 16)
ROPE_THETA = 10000.0
EPS = 1e-6
ADAM_LR = 0.001
ADAM_B1 = 0.9
ADAM_B2 = 0.999
ADAM_EPS = 1e-08
ADAM_WD = 0.01
ADAM_STEP = 10
N_DEV = 8
MESH = pl.DeviceIdType.MESH
VMEM_LIMIT_BYTES = 48 * 1024 * 1024
NEG = -1e30


def _params(sem):
    return pltpu.CompilerParams(dimension_semantics=sem, vmem_limit_bytes=VMEM_LIMIT_BYTES)


def _pick(n, options):
    for o in options:
        if n % o == 0:
            return o
    return n


_DIMS = {"nn": (((1,), (0,)), ((), ())), "nt": (((1,), (1,)), ((), ())), "tn": (((0,), (0,)), ((), ()))}


def _matmul(a, b, *, mode, out_dtype, name, epilogue=None, extra=None, tm=None, tn=None, tk=None):
    if mode == "nn":
        (M, K), (K2, N) = a.shape, b.shape
    elif mode == "nt":
        (M, K), (N, K2) = a.shape, b.shape
    else:
        (K, M), (K2, N) = a.shape, b.shape
    assert K == K2, (a.shape, b.shape, mode)
    tm = tm or _pick(M, (1024, 512, 256, 128))
    tn = tn or _pick(N, (512, 256, 128))
    tk = tk or _pick(K, (2048, 2944, 1024, 512, 256, 128))
    nk = K // tk
    dims = _DIMS[mode]
    n_extra = 0 if extra is None else 1
    n_out = 2 if epilogue == "relu2" else 1

    def body(*refs):
        a_ref, b_ref = refs[0], refs[1]
        extra_ref = refs[2] if n_extra else None
        outs = refs[2 + n_extra:2 + n_extra + n_out]
        acc_ref = refs[-1] if nk > 1 else None

        def finish(acc):
            if epilogue is None:
                outs[0][...] = acc.astype(outs[0].dtype)
            elif epilogue == "add":
                outs[0][...] = (acc + extra_ref[...]).astype(outs[0].dtype)
            elif epilogue == "relu2":
                r = jnp.maximum(acc, 0.0)
                outs[0][...] = r.astype(outs[0].dtype)
                outs[1][...] = (r * r).astype(outs[1].dtype)
            else:
                outs[0][...] = (acc * (2.0 * extra_ref[...].astype(F32))).astype(outs[0].dtype)

        prod = lax.dot_general(a_ref[...], b_ref[...], dims, preferred_element_type=F32)
        if nk == 1:
            finish(prod)
        else:
            k = pl.program_id(2)

            @pl.when(k == 0)
            def _():
                acc_ref[...] = prod

            @pl.when(k > 0)
            def _():
                acc_ref[...] += prod

            @pl.when(k == nk - 1)
            def _():
                finish(acc_ref[...])

    if mode == "nn":
        a_spec = pl.BlockSpec((tm, tk), lambda i, j, k: (i, k))
        b_spec = pl.BlockSpec((tk, tn), lambda i, j, k: (k, j))
    elif mode == "nt":
        a_spec = pl.BlockSpec((tm, tk), lambda i, j, k: (i, k))
        b_spec = pl.BlockSpec((tn, tk), lambda i, j, k: (j, k))
    else:
        a_spec = pl.BlockSpec((tk, tm), lambda i, j, k: (k, i))
        b_spec = pl.BlockSpec((tk, tn), lambda i, j, k: (k, j))
    o_spec = pl.BlockSpec((tm, tn), lambda i, j, k: (i, j))
    in_specs = [a_spec, b_spec] + ([o_spec] if n_extra else [])
    out_shape = [jax.ShapeDtypeStruct((M, N), out_dtype)] * n_out
    res = pl.pallas_call(
        body,
        name=name,
        grid=(M // tm, N // tn, nk),
        in_specs=in_specs,
        out_specs=[o_spec] * n_out,
        out_shape=out_shape,
        scratch_shapes=[pltpu.VMEM((tm, tn), F32)] if nk > 1 else [],
        compiler_params=_params(("parallel", "parallel", "arbitrary")),
    )(a, b, *([extra] if n_extra else []))
    return res if n_out > 1 else res[0]


def _rmsnorm_fwd(x, g, *, name):
    T, D = x.shape
    tm = _pick(T, (512, 256, 128))

    def body(x_ref, g_ref, o_ref):
        xf = x_ref[...]
        r = lax.rsqrt(jnp.mean(xf * xf, axis=-1, keepdims=True) + EPS)
        o_ref[...] = ((xf * r) * g_ref[...]).astype(o_ref.dtype)

    return pl.pallas_call(
        body,
        name=name,
        grid=(T // tm,),
        in_specs=[pl.BlockSpec((tm, D), lambda i: (i, 0)), pl.BlockSpec((1, D), lambda i: (0, 0))],
        out_specs=pl.BlockSpec((tm, D), lambda i: (i, 0)),
        out_shape=jax.ShapeDtypeStruct((T, D), BF16),
        compiler_params=_params(("parallel",)),
    )(x, g)


def _rmsnorm_bwd(x, g, dh, dres, *, name):
    T, D = x.shape
    tm = _pick(T, (256, 128))

    def body(x_ref, g_ref, dh_ref, dres_ref, dx_ref, dxb_ref, dg_ref):
        i = pl.program_id(0)
        xf = x_ref[...]
        r = lax.rsqrt(jnp.mean(xf * xf, axis=-1, keepdims=True) + EPS)
        y = xf * r
        dh_v = dh_ref[...]
        dy = dh_v * g_ref[...]
        c = jnp.mean(dy * y, axis=-1, keepdims=True)
        dx = r * (dy - y * c) + dres_ref[...]
        dx_ref[...] = dx
        dxb_ref[...] = dx.astype(BF16)
        part = jnp.sum(dh_v * y, axis=0, keepdims=True)

        @pl.when(i == 0)
        def _():
            dg_ref[...] = part

        @pl.when(i > 0)
        def _():
            dg_ref[...] += part

    row = pl.BlockSpec((tm, D), lambda i: (i, 0))
    vec = pl.BlockSpec((1, D), lambda i: (0, 0))
    return pl.pallas_call(
        body,
        name=name,
        grid=(T // tm,),
        in_specs=[row, vec, row, row],
        out_specs=[row, row, vec],
        out_shape=[jax.ShapeDtypeStruct((T, D), F32), jax.ShapeDtypeStruct((T, D), BF16), jax.ShapeDtypeStruct((1, D), F32)],
        compiler_params=_params(("arbitrary",)),
    )(x, g, dh, dres)


GROUP_W = 4 * HEAD_DIM
N_DIL_BLOCKS = 3 * N_GROUPS
N_NORMED = 2 * N_GROUPS


def _kind_of_group(j, g):
    return jnp.clip((j - g) // N_GROUPS, 0, 2)


def _head_rstd(xh):
    return lax.rsqrt(jnp.mean(xh * xh, axis=-1, keepdims=True) + EPS)


def _prep_fwd(proj, gains, cos, sin, n_sb_blocks, *, name):
    T = proj.shape[0]
    tm = _pick(T, (512, 256, 128))

    def body(p_ref, gain_ref, cos_ref, sin_ref, o0_ref, o1_ref, o2_ref, os_ref):
        j = pl.program_id(1)
        for g, o_ref in enumerate((o0_ref, o1_ref, o2_ref)):
            @pl.when(jnp.logical_and(j % N_GROUPS == g, j < N_NORMED))
            def _():
                cos_v, sin_v = cos_ref[...], sin_ref[...]
                for hh in range(4):
                    sl = slice(hh * HEAD_DIM, (hh + 1) * HEAD_DIM)
                    xh = p_ref[:, sl]
                    y = (xh * _head_rstd(xh)) * gain_ref[0, :, sl]
                    o_ref[:, sl] = (y * cos_v + pltpu.roll(y, HEAD_DIM // 2, 1) * sin_v).astype(BF16)

            @pl.when(j == N_NORMED + g)
            def _():
                o_ref[...] = p_ref[...].astype(BF16)

        @pl.when(j >= N_DIL_BLOCKS)
        def _():
            os_ref[...] = p_ref[...].astype(BF16)

    def group_spec(g):
        return pl.BlockSpec((tm, GROUP_W), lambda i, j: (i, _kind_of_group(j, g)))

    return pl.pallas_call(
        body,
        name=name,
        grid=(T // tm, N_DIL_BLOCKS + n_sb_blocks),
        in_specs=[
            pl.BlockSpec((tm, GROUP_W), lambda i, j: (i, j)),
            pl.BlockSpec((1, 1, GROUP_W), lambda i, j: (jnp.minimum(j, N_NORMED - 1), 0, 0)),
            pl.BlockSpec((tm, HEAD_DIM), lambda i, j: (i, 0)),
            pl.BlockSpec((tm, HEAD_DIM), lambda i, j: (i, 0)),
        ],
        out_specs=[group_spec(0), group_spec(1), group_spec(2),
                   pl.BlockSpec((tm, GROUP_W), lambda i, j: (i, jnp.maximum(j - N_DIL_BLOCKS, 0)))],
        out_shape=[jax.ShapeDtypeStruct((T, 3 * GROUP_W), BF16)] * N_GROUPS
        + [jax.ShapeDtypeStruct((T, n_sb_blocks * GROUP_W), BF16)],
        compiler_params=_params(("parallel", "arbitrary")),
    )(proj, gains, cos, sin)


def _prep_bwd(proj, dqkv, gains, cos, sin, *, name):
    T = proj.shape[0]
    tm = _pick(T, (512, 256, 128))

    def body(p_ref, d0_ref, d1_ref, d2_ref, gain_ref, cos_ref, sin_ref, o_ref, dgain_ref):
        j, i = pl.program_id(0), pl.program_id(1)

        def normed_bwd(d_ref):
            cos_v, sin_v = cos_ref[...], sin_ref[...]
            part = jnp.zeros((1, HEAD_DIM), F32)
            for hh in range(4):
                sl = slice(hh * HEAD_DIM, (hh + 1) * HEAD_DIM)
                xh = p_ref[:, sl]
                r = _head_rstd(xh)
                y0 = xh * r
                d_out = d_ref[:, sl]
                d_yg = d_out * cos_v + pltpu.roll(d_out * sin_v, HEAD_DIM // 2, 1)
                part = part + jnp.sum(d_yg * y0, axis=0, keepdims=True)
                dy0 = d_yg * gain_ref[0, :, sl]
                c = jnp.mean(dy0 * y0, axis=-1, keepdims=True)
                o_ref[:, sl] = (r * (dy0 - y0 * c)).astype(BF16)

            @pl.when(i == 0)
            def _():
                dgain_ref[0] = part

            @pl.when(i > 0)
            def _():
                dgain_ref[0] += part

        for g, d_ref in enumerate((d0_ref, d1_ref, d2_ref)):
            @pl.when(jnp.logical_and(j % N_GROUPS == g, j < N_NORMED))
            def _():
                normed_bwd(d_ref)

            @pl.when(j == N_NORMED + g)
            def _():
                o_ref[...] = d_ref[...].astype(BF16)

    gain_row = lambda j, i: (jnp.minimum(j, N_NORMED - 1), 0, 0)

    def grad_spec(g):
        return pl.BlockSpec((tm, GROUP_W), lambda j, i: (i, _kind_of_group(j, g)))

    return pl.pallas_call(
        body,
        name=name,
        grid=(N_DIL_BLOCKS, T // tm),
        in_specs=[
            pl.BlockSpec((tm, GROUP_W), lambda j, i: (i, j)),
            grad_spec(0), grad_spec(1), grad_spec(2),
            pl.BlockSpec((1, 1, GROUP_W), gain_row),
            pl.BlockSpec((tm, HEAD_DIM), lambda j, i: (i, 0)),
            pl.BlockSpec((tm, HEAD_DIM), lambda j, i: (i, 0)),
        ],
        out_specs=[pl.BlockSpec((tm, GROUP_W), lambda j, i: (i, j)), pl.BlockSpec((1, 1, HEAD_DIM), gain_row)],
        out_shape=[jax.ShapeDtypeStruct((T, N_DIL_BLOCKS * GROUP_W), BF16),
                   jax.ShapeDtypeStruct((2 * N_GROUPS, 1, HEAD_DIM), F32)],
        compiler_params=_params(("arbitrary", "arbitrary")),
    )(proj, *dqkv, gains, cos, sin)


def _nt(a, b):
    return lax.dot_general(a, b, _DIMS["nt"], preferred_element_type=F32)


def _tn(a, b):
    return lax.dot_general(a, b, _DIMS["tn"], preferred_element_type=F32)


def _nn(a, b):
    return jnp.dot(a, b, preferred_element_type=F32)


def _window_masks():
    row = lax.broadcasted_iota(jnp.int32, (BLOCK, BLOCK), 0)
    col = lax.broadcasted_iota(jnp.int32, (BLOCK, BLOCK), 1)
    return row >= col, col >= row


def _heads():
    return [slice(hh * HEAD_DIM, (hh + 1) * HEAD_DIM) for hh in range(GROUP_W // HEAD_DIM)]


def _dil_fwd(qkv, g, *, name):
    T = qkv.shape[0]
    r = DILATIONS[g]
    L = T // r
    nb = L // BLOCK
    scale = 1.0 / math.sqrt(HEAD_DIM)
    view = qkv.reshape(L, r * 3 * GROUP_W)

    def body(q_ref, kc_ref, kp_ref, vc_ref, vp_ref, o_ref, ld_ref):
        n = pl.program_id(1)
        m_cur, m_prev = _window_masks()
        m_prev = jnp.logical_and(m_prev, n > 0)
        for sl in _heads():
            q = q_ref[:, sl]
            s_c = jnp.where(m_cur, _nt(q, kc_ref[:, sl]) * scale, NEG)
            s_p = jnp.where(m_prev, _nt(q, kp_ref[:, sl]) * scale, NEG)
            m = jnp.maximum(jnp.max(s_c, axis=-1, keepdims=True), jnp.max(s_p, axis=-1, keepdims=True))
            p_c = jnp.exp(s_c - m)
            p_p = jnp.exp(s_p - m)
            l = jnp.sum(p_c, axis=-1, keepdims=True) + jnp.sum(p_p, axis=-1, keepdims=True)
            inv = 1.0 / l
            o_ref[:, sl] = _nn((p_c * inv).astype(BF16), vc_ref[:, sl]) + _nn((p_p * inv).astype(BF16), vp_ref[:, sl])
            ld_ref[:, sl] = jnp.broadcast_to(m + jnp.log(l), (BLOCK, HEAD_DIM))

    blk = (BLOCK, GROUP_W)
    cur = lambda kind: pl.BlockSpec(blk, lambda c, n: (n, 3 * c + kind))
    prev = lambda kind: pl.BlockSpec(blk, lambda c, n: (jnp.maximum(n - 1, 0), 3 * c + kind))
    out_spec = pl.BlockSpec(blk, lambda c, n: (n, c))
    o, ld = pl.pallas_call(
        body,
        name=name,
        grid=(r, nb),
        in_specs=[cur(0), cur(1), prev(1), cur(2), prev(2)],
        out_specs=[out_spec, out_spec],
        out_shape=[jax.ShapeDtypeStruct((L, r * GROUP_W), F32)] * 2,
        compiler_params=_params(("parallel", "arbitrary")),
    )(view, view, view, view, view)
    return o.reshape(T, GROUP_W), ld.reshape(T, GROUP_W)


def _dil_bwd(qkv, do, ld, dterm, g, *, name):
    T = qkv.shape[0]
    r = DILATIONS[g]
    L = T // r
    nb = L // BLOCK
    scale = 1.0 / math.sqrt(HEAD_DIM)
    view = qkv.reshape(L, r * 3 * GROUP_W)
    do_v, ld_v, dt_v = (t.reshape(L, r * GROUP_W) for t in (do, ld, dterm))

    def body(q_ref, qn_ref, kc_ref, kp_ref, vc_ref, vp_ref, do_ref, don_ref, ld_ref, ldn_ref, dt_ref, dtn_ref, out_ref):
        n = pl.program_id(1)
        m_cur, m_prev = _window_masks()
        has_prev, has_next = jnp.logical_and(m_prev, n > 0), jnp.logical_and(m_prev, n < nb - 1)

        def tile(q, k, v, do_t, ld_t, dt_t, mask):
            s = _nt(q, k) * scale
            p = jnp.where(mask, jnp.exp(s - ld_t[:, 0:1]), 0.0)
            ds = p * (_nt(do_t, v) + dt_t[:, 0:1]) * scale
            return p.astype(BF16), ds.astype(BF16)

        for hh, sl in enumerate(_heads()):
            kc, vc, kp = kc_ref[:, sl], vc_ref[:, sl], kp_ref[:, sl]
            q, do_t, ld_t, dt_t = q_ref[:, sl], do_ref[:, sl], ld_ref[:, sl], dt_ref[:, sl]
            p_cc, ds_cc = tile(q, kc, vc, do_t, ld_t, dt_t, m_cur)
            _, ds_cp = tile(q, kp, vp_ref[:, sl], do_t, ld_t, dt_t, has_prev)
            qn, don = qn_ref[:, sl], don_ref[:, sl]
            p_nc, ds_nc = tile(qn, kc, vc, don, ldn_ref[:, sl], dtn_ref[:, sl], has_next)
            at = lambda kind: slice(kind * GROUP_W + hh * HEAD_DIM, kind * GROUP_W + (hh + 1) * HEAD_DIM)
            out_ref[:, at(0)] = _nn(ds_cc, kc) + _nn(ds_cp, kp)
            out_ref[:, at(1)] = _tn(ds_cc, q) + _tn(ds_nc, qn)
            out_ref[:, at(2)] = _tn(p_cc, do_t) + _tn(p_nc, don)

    blk = (BLOCK, GROUP_W)
    qkv_spec = lambda kind, shift: pl.BlockSpec(blk, lambda c, n: (jnp.clip(n + shift, 0, nb - 1), 3 * c + kind))
    row_spec = lambda shift: pl.BlockSpec(blk, lambda c, n: (jnp.clip(n + shift, 0, nb - 1), c))
    out = pl.pallas_call(
        body,
        name=name,
        grid=(r, nb),
        in_specs=[qkv_spec(0, 0), qkv_spec(0, 1), qkv_spec(1, 0), qkv_spec(1, -1), qkv_spec(2, 0), qkv_spec(2, -1),
                  row_spec(0), row_spec(1), row_spec(0), row_spec(1), row_spec(0), row_spec(1)],
        out_specs=pl.BlockSpec((BLOCK, 3 * GROUP_W), lambda c, n: (n, c)),
        out_shape=jax.ShapeDtypeStruct((L, r * 3 * GROUP_W), F32),
        compiler_params=_params(("parallel", "arbitrary")),
    )(view, view, view, view, view, view, do_v, do_v, ld_v, ld_v, dt_v, dt_v)
    return out.reshape(T, 3 * GROUP_W)


def _group_weights(ld_refs):
    lds = [r[...] for r in ld_refs]
    m = jnp.maximum(jnp.maximum(lds[0], lds[1]), lds[2])
    es = [jnp.exp(v - m) for v in lds]
    inv = 1.0 / (es[0] + es[1] + es[2])
    return [e * inv for e in es]


def _merge_fwd(os_, lds, *, name):
    T = os_[0].shape[0]
    tm = _pick(T, (1024, 512, 256, 128))

    def body(o0, o1, o2, l0, l1, l2, y_ref):
        w = _group_weights((l0, l1, l2))
        y_ref[...] = (w[0] * o0[...] + w[1] * o1[...] + w[2] * o2[...]).astype(BF16)

    spec = pl.BlockSpec((tm, GROUP_W), lambda i: (i, 0))
    return pl.pallas_call(
        body, name=name, grid=(T // tm,), in_specs=[spec] * 6, out_specs=spec,
        out_shape=jax.ShapeDtypeStruct((T, GROUP_W), BF16), compiler_params=_params(("parallel",)),
    )(*os_, *lds)


def _merge_bwd(os_, lds, dy, *, name):
    T = dy.shape[0]
    tm = _pick(T, (512, 256, 128))

    def body(o0, o1, o2, l0, l1, l2, dy_ref, do0, do1, do2, dt0, dt1, dt2):
        w = _group_weights((l0, l1, l2))
        dy_v = dy_ref[...]
        y = w[0] * o0[...] + w[1] * o1[...] + w[2] * o2[...]
        prod = dy_v * y
        for hh in range(4):
            sl = slice(hh * HEAD_DIM, (hh + 1) * HEAD_DIM)
            s = jnp.sum(prod[:, sl], axis=-1, keepdims=True)
            for wg, dt in zip(w, (dt0, dt1, dt2)):
                dt[:, sl] = -wg[:, sl] * s
        for wg, do in zip(w, (do0, do1, do2)):
            do[...] = (wg * dy_v).astype(BF16)

    spec = pl.BlockSpec((tm, GROUP_W), lambda i: (i, 0))
    outs = pl.pallas_call(
        body, name=name, grid=(T // tm,), in_specs=[spec] * 7, out_specs=[spec] * 6,
        out_shape=[jax.ShapeDtypeStruct((T, GROUP_W), BF16)] * 3 + [jax.ShapeDtypeStruct((T, GROUP_W), F32)] * 3,
        compiler_params=_params(("parallel",)),
    )(*os_, *lds, dy)
    return outs[:3], outs[3:]


SB_ROWS = 512


def _sum_matrix(inclusive):
    j = lax.broadcasted_iota(jnp.int32, (2 * BLOCK, 2 * BLOCK), 0) % BLOCK
    s = lax.broadcasted_iota(jnp.int32, (2 * BLOCK, 2 * BLOCK), 1)
    later = (j >= s) if inclusive else (j > s)
    return jnp.logical_or(s >= BLOCK, later).astype(BF16)


def _block_sums(x, mat):
    hi = x.astype(BF16)
    lo = (x - hi.astype(F32)).astype(BF16)
    r = _nn(jnp.concatenate([hi, lo], axis=1), mat)
    return r[:, :BLOCK], r[:, BLOCK:]


def _log_terms(z):
    t = jnp.log(1.0 + jnp.exp(-jnp.abs(z)))
    return -(jnp.maximum(z, 0.0) + t), jnp.minimum(z, 0.0) - t


def _causal_mask(rows):
    row = lax.broadcasted_iota(jnp.int32, (rows, rows), 0)
    col = lax.broadcasted_iota(jnp.int32, (rows, rows), 1)
    return col < row


def _sb_fwd(qkv, n_heads, col0, *, name):
    T = qkv.shape[0]
    tq = _pick(T, (SB_ROWS, BLOCK))
    nq, nsub = T // tq, tq // BLOCK
    scale = 1.0 / math.sqrt(HEAD_DIM)

    def body(q_ref, k_ref, v_ref, o_ref, ob_ref):
        i = pl.program_id(1)
        q = q_ref[...]
        mat = _sum_matrix(False)

        def chunk(j, carry, acc, mask):
            rows = pl.ds(pl.multiple_of(j * tq, tq), tq)
            z = _nt(q, k_ref[rows, :]) * scale
            lk, ls = _log_terms(z)
            if mask is not None:
                lk = jnp.where(mask, lk, 0.0)
            a = []
            for b in reversed(range(nsub)):
                sl = slice(b * BLOCK, (b + 1) * BLOCK)
                later, total = _block_sums(lk[:, sl], mat)
                a.append(jnp.exp(ls[:, sl] + (later + carry)))
                carry = carry + total
            a = jnp.concatenate(a[::-1], axis=1)
            if mask is not None:
                a = jnp.where(mask, a, 0.0)
            return carry, acc + _nn(a.astype(BF16), v_ref[rows, :])

        zero = jnp.zeros((tq, HEAD_DIM), F32)
        carry, acc = chunk(i, zero, zero, _causal_mask(tq))

        def step(jj, state):
            return chunk(i - 1 - jj, state[0], state[1], None)

        carry, acc = lax.fori_loop(0, i, step, (carry, acc))
        o_ref[...] = acc
        ob_ref[...] = acc.astype(BF16)

    blk = (tq, HEAD_DIM)
    out_spec = pl.BlockSpec(blk, lambda h, i: (i, h))
    return pl.pallas_call(
        body,
        name=name,
        grid=(n_heads, nq),
        in_specs=[
            pl.BlockSpec(blk, lambda h, i: (i, col0 + h)),
            pl.BlockSpec((T, HEAD_DIM), lambda h, i: (0, col0 + n_heads + h)),
            pl.BlockSpec((T, HEAD_DIM), lambda h, i: (0, col0 + 2 * n_heads + h)),
        ],
        out_specs=[out_spec, out_spec],
        out_shape=[jax.ShapeDtypeStruct((T, n_heads * HEAD_DIM), F32), jax.ShapeDtypeStruct((T, n_heads * HEAD_DIM), BF16)],
        compiler_params=_params(("parallel", "arbitrary")),
    )(qkv, qkv, qkv)


def _sb_bwd(qkv, o32, do, n_heads, col0, *, name):
    T = qkv.shape[0]
    tq = _pick(T, (SB_ROWS, BLOCK))
    nq, nsub = T // tq, tq // BLOCK
    scale = 1.0 / math.sqrt(HEAD_DIM)

    def body(q_ref, k_ref, v_ref, o_ref, do_ref, dq_ref, dk_ref, dv_ref, dk_acc, dv_acc):
        i = pl.program_id(1)

        @pl.when(i == 0)
        def _():
            dk_acc[...] = jnp.zeros_like(dk_acc)
            dv_acc[...] = jnp.zeros_like(dv_acc)

        q, do_t = q_ref[...], do_ref[...]
        delta = jnp.broadcast_to(jnp.sum(do_t.astype(F32) * o_ref[...], axis=-1, keepdims=True), (tq, HEAD_DIM))
        mat, mat_incl = _sum_matrix(False), _sum_matrix(True)

        def chunk(j, carry_b, carry_g, dq, mask):
            rows = pl.ds(pl.multiple_of(j * tq, tq), tq)
            k_t, v_t = k_ref[rows, :], v_ref[rows, :]
            z = _nt(q, k_t) * scale
            lk, ls = _log_terms(z)
            if mask is not None:
                lk = jnp.where(mask, lk, 0.0)
            d_a = _nt(do_t, v_t)
            a_parts, dz_parts = [], []
            for b in reversed(range(nsub)):
                sl = slice(b * BLOCK, (b + 1) * BLOCK)
                later, total = _block_sums(lk[:, sl], mat)
                a = jnp.exp(ls[:, sl] + (later + carry_b))
                carry_b = carry_b + total
                if mask is not None:
                    a = jnp.where(mask[:, sl], a, 0.0)
                a_b = a.astype(BF16)
                g = a_b.astype(F32) * d_a[:, sl]
                from_here, total_g = _block_sums(g, mat_incl)
                before = delta - (from_here + carry_g)
                carry_g = carry_g + total_g
                sig = jnp.exp(ls[:, sl])
                dz = (g - sig * (g + before)) * scale
                a_parts.append(a_b)
                dz_parts.append(dz)
            dz = jnp.concatenate(dz_parts[::-1], axis=1)
            if mask is not None:
                dz = jnp.where(mask, dz, 0.0)
            dz_b = dz.astype(BF16)
            dk_acc[rows, :] += _tn(dz_b, q)
            dv_acc[rows, :] += _tn(jnp.concatenate(a_parts[::-1], axis=1), do_t)
            return carry_b, carry_g, dq + _nn(dz_b, k_t)

        zero = jnp.zeros((tq, HEAD_DIM), F32)
        state = chunk(i, zero, zero, zero, _causal_mask(tq))

        def step(jj, st):
            return chunk(i - 1 - jj, st[0], st[1], st[2], None)

        state = lax.fori_loop(0, i, step, state)
        dq_ref[...] = state[2].astype(BF16)

        @pl.when(i == nq - 1)
        def _():
            dk_ref[...] = dk_acc[...].astype(BF16)
            dv_ref[...] = dv_acc[...].astype(BF16)

    blk = (tq, HEAD_DIM)
    full = (T, HEAD_DIM)
    dshape = jax.ShapeDtypeStruct((T, n_heads * HEAD_DIM), BF16)
    return pl.pallas_call(
        body,
        name=name,
        grid=(n_heads, nq),
        in_specs=[
            pl.BlockSpec(blk, lambda h, i: (i, col0 + h)),
            pl.BlockSpec(full, lambda h, i: (0, col0 + n_heads + h)),
            pl.BlockSpec(full, lambda h, i: (0, col0 + 2 * n_heads + h)),
            pl.BlockSpec(blk, lambda h, i: (i, h)),
            pl.BlockSpec(blk, lambda h, i: (i, h)),
        ],
        out_specs=[pl.BlockSpec(blk, lambda h, i: (i, h)), pl.BlockSpec(full, lambda h, i: (0, h)),
                   pl.BlockSpec(full, lambda h, i: (0, h))],
        out_shape=[dshape, dshape, dshape],
        scratch_shapes=[pltpu.VMEM(full, F32), pltpu.VMEM(full, F32)],
        compiler_params=_params(("arbitrary", "arbitrary")),
    )(qkv, qkv, qkv, o32, do)


def _gate_fwd(y_dil, y_sb, w_up_dil, w_up_sb, proj, gate_b, gate_col0, *, name):
    T, D = y_dil.shape[0], w_up_dil.shape[0]
    tm = _pick(T, (512, 256, 128))
    tn = _pick(D, (512, 256, 128))
    c0, nbr = gate_col0 // tn, D // tn

    def body(yd_ref, ys_ref, wd_ref, ws_ref, gp0_ref, gp1_ref, b_ref, o_ref):
        g0 = jax.nn.sigmoid(gp0_ref[...] + b_ref[0:1, :])
        g1 = jax.nn.sigmoid(gp1_ref[...] + b_ref[1:2, :])
        o_ref[...] = (g0 * _nt(yd_ref[...], wd_ref[...]) + g1 * _nt(ys_ref[...], ws_ref[...])).astype(BF16)

    return pl.pallas_call(
        body,
        name=name,
        grid=(T // tm, nbr),
        in_specs=[
            pl.BlockSpec((tm, y_dil.shape[1]), lambda i, j: (i, 0)),
            pl.BlockSpec((tm, y_sb.shape[1]), lambda i, j: (i, 0)),
            pl.BlockSpec((tn, w_up_dil.shape[1]), lambda i, j: (j, 0)),
            pl.BlockSpec((tn, w_up_sb.shape[1]), lambda i, j: (j, 0)),
            pl.BlockSpec((tm, tn), lambda i, j: (i, c0 + j)),
            pl.BlockSpec((tm, tn), lambda i, j: (i, c0 + nbr + j)),
            pl.BlockSpec((2, tn), lambda i, j: (0, j)),
        ],
        out_specs=pl.BlockSpec((tm, tn), lambda i, j: (i, j)),
        out_shape=jax.ShapeDtypeStruct((T, D), BF16),
        compiler_params=_params(("parallel", "parallel")),
    )(y_dil, y_sb, w_up_dil, w_up_sb, proj, proj, gate_b)


def _gate_bwd(y, w_up, proj, gate_b, dmixed, branch, gate_col0, *, name):
    T, D = y.shape[0], w_up.shape[0]
    tm = _pick(T, (512, 256, 128))
    tn = _pick(D, (512, 256, 128))
    c0 = gate_col0 // tn + branch * (D // tn)

    def body(y_ref, w_ref, gp_ref, b_ref, dm_ref, dup_ref, dgp_ref, db_ref):
        i = pl.program_id(1)
        g = jax.nn.sigmoid(gp_ref[...] + b_ref[branch:branch + 1, :])
        dm = dm_ref[...]
        dup_ref[...] = (dm * g).astype(BF16)
        dgp = (dm * _nt(y_ref[...], w_ref[...])) * (g * (1.0 - g))
        dgp_ref[...] = dgp.astype(BF16)
        part = jnp.sum(dgp, axis=0, keepdims=True)

        @pl.when(i == 0)
        def _():
            db_ref[...] = part

        @pl.when(i > 0)
        def _():
            db_ref[...] += part

    tile = pl.BlockSpec((tm, tn), lambda j, i: (i, j))
    return pl.pallas_call(
        body,
        name=name,
        grid=(D // tn, T // tm),
        in_specs=[
            pl.BlockSpec((tm, y.shape[1]), lambda j, i: (i, 0)),
            pl.BlockSpec((tn, w_up.shape[1]), lambda j, i: (j, 0)),
            pl.BlockSpec((tm, tn), lambda j, i: (i, c0 + j)),
            pl.BlockSpec((2, tn), lambda j, i: (0, j)),
            tile,
        ],
        out_specs=[tile, tile, pl.BlockSpec((1, tn), lambda j, i: (0, j))],
        out_shape=[jax.ShapeDtypeStruct((T, D), BF16), jax.ShapeDtypeStruct((T, D), BF16), jax.ShapeDtypeStruct((1, D), F32)],
        compiler_params=_params(("parallel", "arbitrary")),
    )(y, w_up, proj, gate_b, dmixed)


def _loss_head(y, target, *, name):
    T, D = y.shape
    tm = _pick(T, (256, 128))

    def body(y_ref, t_ref, dy_ref, dyb_ref, l_ref):
        i = pl.program_id(0)
        err = y_ref[...] - t_ref[...]
        dy = err * (1.0 / D)
        dy_ref[...] = dy
        dyb_ref[...] = dy.astype(BF16)
        part = 0.5 * jnp.sum(jnp.mean(err * err, axis=-1, keepdims=True), axis=0, keepdims=True)

        @pl.when(i == 0)
        def _():
            l_ref[...] = part

        @pl.when(i > 0)
        def _():
            l_ref[...] += part

    row = pl.BlockSpec((tm, D), lambda i: (i, 0))
    return pl.pallas_call(
        body, name=name, grid=(T // tm,), in_specs=[row, row],
        out_specs=[row, row, pl.BlockSpec((1, 1), lambda i: (0, 0))],
        out_shape=[jax.ShapeDtypeStruct((T, D), F32), jax.ShapeDtypeStruct((T, D), BF16), jax.ShapeDtypeStruct((1, 1), F32)],
        compiler_params=_params(("arbitrary",)),
    )(y, target)


def _reduce_adamw(parts, w, m, v, *, name):
    n_layers = len(parts)
    C = w.shape[1]
    R = w.shape[0] // n_layers
    tr = R
    for cand in (1024, 512, 256, 128, 64, 32, 16, 8):
        if R % cand == 0 and cand * C <= 256 * 1024:
            tr = cand
            break
    nr = R // tr

    def body(*refs):
        p_refs = refs[:n_layers]
        w_ref, m_ref, v_ref, g_ref, d_ref, nm_ref, nv_ref = refs[n_layers:]
        layer = pl.program_id(0)
        for l, p_ref in enumerate(p_refs):
            @pl.when(layer == l)
            def _():
                g = p_ref[0].astype(F32)
                for s in range(1, N_DEV):
                    g = g + p_ref[s].astype(F32)
                g_ref[...] = g

        _adamw_update(g_ref[...], w_ref, m_ref, v_ref, d_ref, nm_ref, nv_ref)

    def part_spec(l):
        return pl.BlockSpec((N_DEV, tr, C), lambda layer, i: (0, jnp.where(layer == l, i, (nr - 1) * (layer > l)), 0))

    row = pl.BlockSpec((tr, C), lambda layer, i: (layer * nr + i, 0))
    return pl.pallas_call(
        body, name=name, grid=(n_layers, nr),
        in_specs=[part_spec(l) for l in range(n_layers)] + [row, row, row],
        out_specs=[row] * 4, out_shape=[jax.ShapeDtypeStruct(w.shape, F32)] * 4,
        compiler_params=_params(("arbitrary", "arbitrary")),
    )(*parts, w, m, v)


def _adamw_update(g, w_ref, m_ref, v_ref, d_ref, nm_ref, nv_ref):
    m_new = ADAM_B1 * m_ref[...] + (1.0 - ADAM_B1) * g
    v_new = ADAM_B2 * v_ref[...] + (1.0 - ADAM_B2) * (g * g)
    m_hat = m_new / (1.0 - ADAM_B1 ** ADAM_STEP)
    v_hat = v_new / (1.0 - ADAM_B2 ** ADAM_STEP)
    d_ref[...] = -ADAM_LR * (m_hat / (jnp.sqrt(v_hat) + ADAM_EPS) + ADAM_WD * w_ref[...])
    nm_ref[...] = m_new
    nv_ref[...] = v_new


def _reduce_adamw_t(parts, w, m, v, *, name):
    n_layers = len(parts)
    _, n, K = parts[0].shape
    n_pad = w.shape[1]
    tm = _pick(K, (128,))
    nr = K // tm

    def body(*refs):
        p_refs = refs[:n_layers]
        w_ref, m_ref, v_ref, g_ref, d_ref, nm_ref, nv_ref = refs[n_layers:]
        layer = pl.program_id(0)
        for l, p_ref in enumerate(p_refs):
            @pl.when(layer == l)
            def _():
                g_t = p_ref[0].astype(F32)
                for s in range(1, N_DEV):
                    g_t = g_t + p_ref[s].astype(F32)
                if n_pad > n:
                    g_t = jnp.concatenate([g_t, jnp.zeros((n_pad - n, tm), F32)], axis=0)
                g_ref[...] = g_t.T

        _adamw_update(g_ref[...], w_ref, m_ref, v_ref, d_ref, nm_ref, nv_ref)

    def part_spec(l):
        return pl.BlockSpec((N_DEV, n, tm), lambda layer, i: (0, 0, jnp.where(layer == l, i, (nr - 1) * (layer > l))))

    row = pl.BlockSpec((tm, n_pad), lambda layer, i: (layer * nr + i, 0))
    return pl.pallas_call(
        body, name=name, grid=(n_layers, nr),
        in_specs=[part_spec(l) for l in range(n_layers)] + [row, row, row],
        out_specs=[row] * 4, out_shape=[jax.ShapeDtypeStruct(w.shape, F32)] * 4,
        compiler_params=_params(("arbitrary", "arbitrary")),
    )(*parts, w, m, v)


def _transpose_cast(x, *, name):
    R, C = x.shape
    tr, tc = _pick(R, (512, 256, 128)), _pick(C, (512, 256, 128))

    def body(x_ref, o_ref):
        o_ref[...] = x_ref[...].astype(F32).T.astype(BF16)

    return pl.pallas_call(
        body, name=name, grid=(R // tr, C // tc),
        in_specs=[pl.BlockSpec((tr, tc), lambda i, j: (i, j))],
        out_specs=pl.BlockSpec((tc, tr), lambda i, j: (j, i)),
        out_shape=jax.ShapeDtypeStruct((C, R), BF16),
        compiler_params=_params(("parallel", "parallel")),
    )(x)


_ANY = pl.BlockSpec(memory_space=pl.ANY)


def _place():
    return lax.axis_index("x"), lax.axis_index("y"), lax.axis_index("c")


def _slot(p):
    return 4 * p[0] + 2 * p[1] + p[2]


_HBM = pl.BlockSpec(memory_space=pltpu.HBM)
_SEM = pl.BlockSpec(memory_space=pltpu.SEMAPHORE)
_EFFECT = pltpu.SideEffectType.DATAFLOW_SIDE_EFFECTING
N_PEERS = N_DEV - 1


def _peers(me):
    flips = [(fx, fy, fc) for fx in (0, 1) for fy in (0, 1) for fc in (0, 1)][1:]
    return [tuple(1 - v if f else v for v, f in zip(me, flip)) for flip in flips]


def _peer_copy(src, lands, t, k, sender, to, send_sems, recv_sems):
    return pltpu.make_async_remote_copy(
        src_ref=src, dst_ref=lands[t].at[_slot(sender)], send_sem=send_sems.at[N_PEERS * t + k],
        recv_sem=recv_sems.at[N_PEERS * t + k], device_id=to, device_id_type=MESH)


class _Exchange:
    def __init__(self, chunked, whole, name):
        self.arrays = [pltpu.with_memory_space_constraint(a, pltpu.HBM) for a in list(chunked) + list(whole)]
        self.n, self.n_chunked, self.name = len(self.arrays), len(chunked), name

    def _src(self, ins, t, dest):
        return ins[t].at[_slot(dest)] if t < self.n_chunked else ins[t]

    def _land_shape(self, t):
        a = self.arrays[t]
        return a.shape if t < self.n_chunked else (N_DEV,) + a.shape

    def start(self, after=None):
        n = self.n

        def body(*refs):
            ins, lands = refs[:n], refs[n:2 * n]
            send_sems, recv_sems = refs[-2 * n - 3], refs[-2 * n - 2]
            token = refs[-1]
            me = _place()
            for t in range(n):
                for k, peer in enumerate(_peers(me)):
                    _peer_copy(self._src(ins, t, peer), lands, t, k, me, peer, send_sems, recv_sems).start()
            token[...] = jnp.zeros_like(token)

        lands = [pltpu.with_memory_space_constraint(lax.empty(self._land_shape(t), a.dtype), pltpu.HBM)
                 for t, a in enumerate(self.arrays)]
        sems = pltpu.SemaphoreType.DMA((N_PEERS * n,))
        outs = pl.pallas_call(
            body,
            name=self.name + "_start",
            in_specs=[_HBM] * (2 * n) + ([_ANY] if after is not None else []),
            out_specs=[_SEM, _SEM] + [_HBM] * (2 * n) + [pl.BlockSpec(memory_space=pltpu.VMEM)],
            out_shape=[sems, sems] + [pltpu.HBM(a.shape, a.dtype) for a in self.arrays + lands]
            + [jax.ShapeDtypeStruct((8, 128), F32)],
            input_output_aliases={t: 2 + t for t in range(2 * n)},
            compiler_params=pltpu.CompilerParams(has_side_effects=_EFFECT),
        )(*self.arrays, *lands, *([after] if after is not None else []))
        self.sems, self.thru, self.lands, self.token = outs[:2], outs[2:2 + n], outs[2 + n:2 + 2 * n], outs[-1]
        return self.token

    def finish(self, after):
        n = self.n

        def wait_body(*refs):
            ins, lands, (send_sems, recv_sems) = refs[:n], refs[n:2 * n], refs[2 * n:2 * n + 2]
            me = _place()
            for t in range(n):
                for k, peer in enumerate(_peers(me)):
                    cp = _peer_copy(self._src(ins, t, peer), lands, t, k, peer, peer, send_sems, recv_sems)
                    cp.wait_send()
                    cp.wait_recv()

        outs = pl.pallas_call(
            wait_body,
            name=self.name + "_wait",
            in_specs=[_HBM] * (2 * n) + [_SEM, _SEM, _ANY],
            out_specs=[_HBM] * (2 * n),
            out_shape=[pltpu.HBM(a.shape, a.dtype) for a in self.thru + self.lands],
            input_output_aliases={t: t for t in range(2 * n)},
            compiler_params=pltpu.CompilerParams(has_side_effects=_EFFECT),
        )(*self.thru, *self.lands, *self.sems, after)
        thru, lands = outs[:n], outs[n:]

        def own_body(*refs):
            ins, outs_ = refs[:n], refs[2 * n:3 * n]
            sems = refs[-1]
            me = _place()
            copies = [pltpu.make_async_copy(self._src(ins, t, me), outs_[t].at[_slot(me)], sems.at[t]) for t in range(n)]
            for cp in copies:
                cp.start()
            for cp in copies:
                cp.wait()

        return pl.pallas_call(
            own_body,
            name=self.name + "_own",
            in_specs=[_ANY] * (2 * n),
            out_specs=[_ANY] * n,
            out_shape=[jax.ShapeDtypeStruct(a.shape, a.dtype) for a in lands],
            input_output_aliases={n + t: t for t in range(n)},
            scratch_shapes=[pltpu.SemaphoreType.DMA((n,))],
        )(*thru, *lands)


def _rope_tables(T):
    half = HEAD_DIM // 2
    inv_freq = ROPE_THETA ** (-jnp.arange(half, dtype=F32) / half)
    ang = jnp.arange(T, dtype=F32)[:, None] * inv_freq[None, :]
    cos, sin = jnp.cos(ang), jnp.sin(ang)
    return jnp.concatenate([cos, cos], axis=-1), jnp.concatenate([-sin, sin], axis=-1)


def _gain_table(q_gain, k_gain):
    return jnp.tile(jnp.concatenate([q_gain, k_gain], axis=0), (1, 4))[:, None, :]


def _sb_heads(w_in):
    n_in, d_model = w_in.shape
    return (n_in - N_DIL_BLOCKS * GROUP_W - 2 * d_model) // (3 * HEAD_DIM)


def _layer_fwd(x, p, cos, sin, tag, early=None, late=None):
    if early is not None:
        p.update(early(x))
    sb_heads = _sb_heads(p["w_in"])
    n_sb_blocks = 3 * sb_heads * HEAD_DIM // GROUP_W
    s = {"x": x}
    s["h"] = _rmsnorm_fwd(x, p["norm1"], name=f"norm1_fwd{tag}")
    s["proj"] = _matmul(s["h"], p["w_in"], mode="nt", out_dtype=F32, name=f"proj_fwd{tag}")
    *s["qkv_d"], s["qkv_s"] = _prep_fwd(s["proj"], p["gains"], cos, sin, n_sb_blocks, name=f"prep_fwd{tag}")
    outs = [_dil_fwd(s["qkv_d"][g], g, name=f"dil{g}_fwd{tag}") for g in range(N_GROUPS)]
    s["o"], s["ld"] = [o for o, _ in outs], [ld for _, ld in outs]
    s["y_dil"] = _merge_fwd(s["o"], s["ld"], name=f"merge_fwd{tag}")
    s["y_sb32"], s["y_sb"] = _sb_fwd(s["qkv_s"], sb_heads, 0, name=f"sb_fwd{tag}")
    if late is not None:
        p.update(late(s["y_sb"]))
    s["mixed"] = _gate_fwd(s["y_dil"], s["y_sb"], p["w_up_dil"], p["w_up_sb"], s["proj"], p["gate_b"],
                           (N_DIL_BLOCKS + n_sb_blocks) * GROUP_W, name=f"gate_fwd{tag}")
    s["x1"] = _matmul(s["mixed"], p["w_out"], mode="nn", out_dtype=F32, epilogue="add", extra=x, name=f"out_fwd{tag}")
    s["h2"] = _rmsnorm_fwd(s["x1"], p["norm2"], name=f"norm2_fwd{tag}")
    s["f"], s["a"] = _matmul(s["h2"], p["w_ff1"], mode="nt", out_dtype=BF16, epilogue="relu2", name=f"ff1_fwd{tag}")
    x2 = _matmul(s["a"], p["w_ff2"], mode="nn", out_dtype=F32, epilogue="add", extra=s["x1"], name=f"ff2_fwd{tag}")
    return x2, s


def _layer_bwd(dx2, dx2_b, p, s, cos, sin, tag, mid_hook=None, end_hook=None):
    sb_heads = _sb_heads(p["w_in"])
    gate_col0 = N_DIL_BLOCKS * GROUP_W + 3 * sb_heads * HEAD_DIM
    g = {}
    df = _matmul(dx2_b, p["w_ff2"], mode="nt", out_dtype=BF16, epilogue="relu2_bwd", extra=s["f"], name=f"ff2_bwd{tag}")
    g["w_ff2"] = _matmul(s["a"], dx2_b, mode="tn", out_dtype=BF16, name=f"ff2_wgrad{tag}")
    dh2 = _matmul(df, p["w_ff1"], mode="nn", out_dtype=F32, name=f"ff1_bwd{tag}")
    g["w_ff1"] = _matmul(df, s["h2"], mode="tn", out_dtype=BF16, name=f"ff1_wgrad{tag}")
    dx1, dx1_b, g["norm2"] = _rmsnorm_bwd(s["x1"], p["norm2"], dh2, dx2, name=f"norm2_bwd{tag}")
    dmixed = _matmul(dx1_b, p["w_out"], mode="nt", out_dtype=F32, name=f"out_bwd{tag}")
    g["w_out"] = _matmul(s["mixed"], dx1_b, mode="tn", out_dtype=BF16, name=f"out_wgrad{tag}")
    gate_b = p["gate_b"] if mid_hook is None else p["gate_b"] + mid_hook(g)[0, 0]
    dup_dil, dgp0, db0 = _gate_bwd(s["y_dil"], p["w_up_dil"], s["proj"], gate_b, dmixed, 0, gate_col0, name=f"gate0_bwd{tag}")
    dup_sb, dgp1, db1 = _gate_bwd(s["y_sb"], p["w_up_sb"], s["proj"], gate_b, dmixed, 1, gate_col0, name=f"gate1_bwd{tag}")
    g["gate_b"] = jnp.concatenate([db0, db1], axis=0)
    dy_dil = _matmul(dup_dil, p["w_up_dil"], mode="nn", out_dtype=F32, name=f"updil_bwd{tag}")
    g["w_up_dil"] = _matmul(dup_dil, s["y_dil"], mode="tn", out_dtype=BF16, name=f"updil_wgrad{tag}")
    dy_sb = _matmul(dup_sb, p["w_up_sb"], mode="nn", out_dtype=BF16, name=f"upsb_bwd{tag}")
    g["w_up_sb"] = _matmul(dup_sb, s["y_sb"], mode="tn", out_dtype=BF16, name=f"upsb_wgrad{tag}")
    dos, dterms = _merge_bwd(s["o"], s["ld"], dy_dil, name=f"merge_bwd{tag}")
    dqkv = [_dil_bwd(s["qkv_d"][grp], dos[grp], s["ld"][grp], dterms[grp], grp, name=f"dil{grp}_bwd{tag}")
            for grp in range(N_GROUPS)]
    dproj_d, dgain = _prep_bwd(s["proj"], dqkv, p["gains"], cos, sin, name=f"prep_bwd{tag}")
    g["q_gain"], g["k_gain"] = dgain[:N_GROUPS, 0], dgain[N_GROUPS:, 0]
    dq_s, dk_s, dv_s = _sb_bwd(s["qkv_s"], s["y_sb32"], dy_sb, sb_heads, 0, name=f"sb_bwd{tag}")
    dproj = jnp.concatenate([dproj_d, dq_s, dk_s, dv_s, dgp0, dgp1], axis=1)
    dh = _matmul(dproj, p["w_in"], mode="nn", out_dtype=F32, name=f"proj_bwd{tag}")
    g["w_in"] = _matmul(dproj, s["h"], mode="tn", out_dtype=BF16, name=f"proj_wgrad{tag}")
    norm1 = p["norm1"] if end_hook is None else p["norm1"] + end_hook(g)[0, 0]
    dx, dx_b, g["norm1"] = _rmsnorm_bwd(s["x"], norm1, dh, dx1, name=f"norm1_bwd{tag}")
    return dx, dx_b, g


def _local_step(x, target, layers, early=None, late=None, mid_hooks=None, end_hooks=None):
    depth = len(layers)
    early, late, mid_hooks, end_hooks = (h or [None] * depth for h in (early, late, mid_hooks, end_hooks))
    cos, sin = _rope_tables(x.shape[0])
    saved = []
    for l, p in enumerate(layers):
        x, s = _layer_fwd(x, p, cos, sin, f"_l{l}", early[l], late[l])
        saved.append(s)
    dx, dx_b, loss = _loss_head(x, target, name="loss_head")
    grads = [None] * depth
    for l in reversed(range(depth)):
        dx, dx_b, grads[l] = _layer_bwd(dx, dx_b, layers[l], saved[l], cos, sin, f"_l{l}", mid_hooks[l], end_hooks[l])
    return loss, dx, grads


_MATRICES = ("w_in", "w_up_dil", "w_up_sb", "w_out", "w_ff1", "w_ff2")
_TRANSPOSED = ("w_in", "w_up_dil", "w_up_sb", "w_ff1")
_SMALL = ("norm1_g", "norm2_g", "q_norm_g", "k_norm_g")


def _unshard(blocks, name):
    if name == "gate_b":
        return jnp.transpose(blocks, (1, 0, 2)).reshape(blocks.shape[1], N_DEV * blocks.shape[2])
    return blocks.reshape(N_DEV * blocks.shape[1], blocks.shape[2])


def _to_chunks(full, name):
    if name == "gate_b":
        r, cols = full.shape
        return jnp.transpose(full.reshape(r, N_DEV, cols // N_DEV), (1, 0, 2))
    return full.reshape(N_DEV, full.shape[0] // N_DEV, full.shape[1])


def _lane_pad(n):
    return -n % HEAD_DIM


def _pack_small(norm1, norm2, qg, kg):
    flat = jnp.concatenate([t.reshape(-1, HEAD_DIM) for t in (norm1, norm2, qg, kg)], axis=0)
    return jnp.pad(flat, ((0, -flat.shape[0] % 8), (0, 0)))


def _unpack_small(packed, shapes):
    out, row = [], 0
    for shape in shapes:
        rows = math.prod(shape) // HEAD_DIM
        out.append(packed[row:row + rows].reshape(shape))
        row += rows
    return out


def kernel(x, norm1_g, w_in, q_norm_g, k_norm_g, w_up_dil, w_up_sb, gate_b, w_out, norm2_g, w_ff1, w_ff2, loss_target, m_norm1_g, m_w_in, m_q_norm_g, m_k_norm_g, m_w_up_dil, m_w_up_sb, m_gate_b, m_w_out, m_norm2_g, m_w_ff1, m_w_ff2, v_norm1_g, v_w_in, v_q_norm_g, v_k_norm_g, v_w_up_dil, v_w_up_sb, v_gate_b, v_w_out, v_norm2_g, v_w_ff1, v_w_ff2):
    names = ("norm1_g", "w_in", "q_norm_g", "k_norm_g", "w_up_dil", "w_up_sb", "gate_b", "w_out", "norm2_g", "w_ff1", "w_ff2")
    w = dict(zip(names, (norm1_g, w_in, q_norm_g, k_norm_g, w_up_dil, w_up_sb, gate_b, w_out, norm2_g, w_ff1, w_ff2)))
    m = dict(zip(names, (m_norm1_g, m_w_in, m_q_norm_g, m_k_norm_g, m_w_up_dil, m_w_up_sb, m_gate_b, m_w_out, m_norm2_g, m_w_ff1, m_w_ff2)))
    v = dict(zip(names, (v_norm1_g, v_w_in, v_q_norm_g, v_k_norm_g, v_w_up_dil, v_w_up_sb, v_gate_b, v_w_out, v_norm2_g, v_w_ff1, v_w_ff2)))
    depth = norm1_g.shape[0]
    assert depth == 2, "the exchange schedule below is written for two layers"
    sharded = _MATRICES + ("gate_b",)
    late_names = sharded[1:]

    def shards(layer, which):
        out = []
        for n in which:
            shard = w[n][layer]
            if n in _TRANSPOSED:
                cols = shard.shape[1]
                padded = jnp.pad(shard, ((0, 0), (0, _lane_pad(cols))))
                out.append(_transpose_cast(padded, name=f"shard_t_{n}_l{layer}")[:cols])
            else:
                out.append(shard if n == "gate_b" else shard.astype(BF16))
        return out

    def full(which, gathered):
        return {n: _unshard(blocks, n) for n, blocks in zip(which, gathered)}

    gather_in0 = _Exchange([], shards(0, sharded[:1]), "gather_in_l0")
    gather_rest0 = _Exchange([], shards(0, late_names), "gather_rest_l0")
    gather_1 = _Exchange([], shards(1, sharded), "gather_l1")
    token = gather_1.start(gather_rest0.start(gather_in0.start()))
    layers = []
    for l in range(depth):
        layers.append({"norm1": norm1_g[l][None], "norm2": norm2_g[l][None], "gains": _gain_table(q_norm_g[l], k_norm_g[l])})
    layers[0].update(full(sharded[:1], gather_in0.finish(token)))
    early = [None, lambda x_in: full(sharded, gather_1.finish(x_in))]
    late = [lambda y_sb: full(late_names, gather_rest0.finish(y_sb)), None]

    first_names = ("w_out", "w_ff1", "w_ff2")
    last_names = tuple(n for n in sharded if n not in first_names)
    exchanges = {}

    def start_exchange(key, which):
        def hook(g):
            exchanges[key] = _Exchange([_to_chunks(g[n], n) for n in which], [], "exchange_" + key)
            return exchanges[key].start()
        return hook

    loss_part, dx, grads = _local_step(
        x[0], loss_target[0], layers, early, late,
        mid_hooks=[start_exchange("first_l0", first_names), None], end_hooks=[None, start_exchange("l1", sharded)])
    loss = lax.psum(loss_part[0, 0], ("x", "y", "c"))
    small = _pack_small(jnp.concatenate([g["norm1"] for g in grads]), jnp.concatenate([g["norm2"] for g in grads]),
                        jnp.stack([g["q_gain"] for g in grads]), jnp.stack([g["k_gain"] for g in grads]))
    last = _Exchange([_to_chunks(grads[0][n], n) for n in last_names], [small], "exchange_last_l0")
    token = last.start()
    parts_1 = dict(zip(sharded, exchanges["l1"].finish(token)))
    parts_0 = dict(zip(first_names, exchanges["first_l0"].finish(parts_1["w_in"])))
    *received, small_parts = last.finish(parts_0["w_out"])
    parts_0.update(zip(last_names, received))

    out = {}
    for n in sharded:
        rows, cols = depth * w[n].shape[1], w[n].shape[2]
        parts = [parts_0[n], parts_1[n]]
        state = [t.reshape(rows, cols) for t in (w[n], m[n], v[n])]
        if n in _TRANSPOSED:
            pad = _lane_pad(cols)
            res = _reduce_adamw_t(parts, *(jnp.pad(t, ((0, 0), (0, pad))) for t in state), name=f"adamw_{n}")
            res = [t[:, :cols] for t in res]
        else:
            if w[n].shape[1] % 8:
                parts = [jnp.concatenate(parts, axis=1)]
            res = _reduce_adamw(parts, *state, name=f"adamw_{n}")
        out[n] = [t.reshape(w[n].shape) for t in res]
    small_res = _reduce_adamw([small_parts], _pack_small(*(w[n] for n in _SMALL)), _pack_small(*(m[n] for n in _SMALL)),
                              _pack_small(*(v[n] for n in _SMALL)), name="adamw_small")
    small_shapes = [w[n].shape for n in _SMALL]
    for k, t in enumerate(small_res):
        for n, arr in zip(_SMALL, _unpack_small(t, small_shapes)):
            out.setdefault(n, [None] * 4)[k] = arr
    return (loss, dx[None], *(out[n][0] for n in names), *(out[n][1] for n in names), *(out[n][2] for n in names),
            *(out[n][3] for n in names))
```

```python
import functools
import math

import jax
import jax.numpy as jnp
from jax import lax
from jax.experimental import pallas as pl
from jax.experimental.pallas import tpu as pltpu

F32 = jnp.float32
BF16 = jnp.bfloat16

HEAD_DIM = 128
BLOCK = 128
N_GROUPS = 3
DILATIONS = (1, 4, 16)
ROPE_THETA = 10000.0
EPS = 1e-6
ADAM_LR = 0.001
ADAM_B1 = 0.9
ADAM_B2 = 0.999
ADAM_EPS = 1e-08
ADAM_WD = 0.01
ADAM_STEP = 10
N_DEV = 8
MESH = pl.DeviceIdType.MESH
VMEM_LIMIT_BYTES = 48 * 1024 * 1024
NEG = -1e30


def _params(sem):
    return pltpu.CompilerParams(dimension_semantics=sem, vmem_limit_bytes=VMEM_LIMIT_BYTES)


def _pick(n, options):
    for o in options:
        if n % o == 0:
            return o
    return n


_DIMS = {"nn": (((1,), (0,)), ((), ())), "nt": (((1,), (1,)), ((), ())), "tn": (((0,), (0,)), ((), ()))}


def _matmul(a, b, *, mode, out_dtype, name, epilogue=None, extra=None, tm=None, tn=None, tk=None):
    if mode == "nn":
        (M, K), (K2, N) = a.shape, b.shape
    elif mode == "nt":
        (M, K), (N, K2) = a.shape, b.shape
    else:
        (K, M), (K2, N) = a.shape, b.shape
    assert K == K2, (a.shape, b.shape, mode)
    tm = tm or _pick(M, (1024, 512, 256, 128))
    tn = tn or _pick(N, (512, 256, 128))
    tk = tk or _pick(K, (2048, 2944, 1024, 512, 256, 128))
    nk = K // tk
    dims = _DIMS[mode]
    n_extra = 0 if extra is None else 1
    n_out = 2 if epilogue == "relu2" else 1

    def body(*refs):
        a_ref, b_ref = refs[0], refs[1]
        extra_ref = refs[2] if n_extra else None
        outs = refs[2 + n_extra:2 + n_extra + n_out]
        acc_ref = refs[-1] if nk > 1 else None

        def finish(acc):
            if epilogue is None:
                outs[0][...] = acc.astype(outs[0].dtype)
            elif epilogue == "add":
                outs[0][...] = (acc + extra_ref[...]).astype(outs[0].dtype)
            elif epilogue == "relu2":
                r = jnp.maximum(acc, 0.0)
                outs[0][...] = r.astype(outs[0].dtype)
                outs[1][...] = (r * r).astype(outs[1].dtype)
            else:
                outs[0][...] = (acc * (2.0 * extra_ref[...].astype(F32))).astype(outs[0].dtype)

        prod = lax.dot_general(a_ref[...], b_ref[...], dims, preferred_element_type=F32)
        if nk == 1:
            finish(prod)
        else:
            k = pl.program_id(2)

            @pl.when(k == 0)
            def _():
                acc_ref[...] = prod

            @pl.when(k > 0)
            def _():
                acc_ref[...] += prod

            @pl.when(k == nk - 1)
            def _():
                finish(acc_ref[...])

    if mode == "nn":
        a_spec = pl.BlockSpec((tm, tk), lambda i, j, k: (i, k))
        b_spec = pl.BlockSpec((tk, tn), lambda i, j, k: (k, j))
    elif mode == "nt":
        a_spec = pl.BlockSpec((tm, tk), lambda i, j, k: (i, k))
        b_spec = pl.BlockSpec((tn, tk), lambda i, j, k: (j, k))
    else:
        a_spec = pl.BlockSpec((tk, tm), lambda i, j, k: (k, i))
        b_spec = pl.BlockSpec((tk, tn), lambda i, j, k: (k, j))
    o_spec = pl.BlockSpec((tm, tn), lambda i, j, k: (i, j))
    in_specs = [a_spec, b_spec] + ([o_spec] if n_extra else [])
    out_shape = [jax.ShapeDtypeStruct((M, N), out_dtype)] * n_out
    res = pl.pallas_call(
        body,
        name=name,
        grid=(M // tm, N // tn, nk),
        in_specs=in_specs,
        out_specs=[o_spec] * n_out,
        out_shape=out_shape,
        scratch_shapes=[pltpu.VMEM((tm, tn), F32)] if nk > 1 else [],
        compiler_params=_params(("parallel", "parallel", "arbitrary")),
    )(a, b, *([extra] if n_extra else []))
    return res if n_out > 1 else res[0]


def _rmsnorm_fwd(x, g, *, name):
    T, D = x.shape
    tm = _pick(T, (512, 256, 128))

    def body(x_ref, g_ref, o_ref):
        xf = x_ref[...]
        r = lax.rsqrt(jnp.mean(xf * xf, axis=-1, keepdims=True) + EPS)
        o_ref[...] = ((xf * r) * g_ref[...]).astype(o_ref.dtype)

    return pl.pallas_call(
        body,
        name=name,
        grid=(T // tm,),
        in_specs=[pl.BlockSpec((tm, D), lambda i: (i, 0)), pl.BlockSpec((1, D), lambda i: (0, 0))],
        out_specs=pl.BlockSpec((tm, D), lambda i: (i, 0)),
        out_shape=jax.ShapeDtypeStruct((T, D), BF16),
        compiler_params=_params(("parallel",)),
    )(x, g)


def _rmsnorm_bwd(x, g, dh, dres, *, name):
    T, D = x.shape
    tm = _pick(T, (256, 128))

    def body(x_ref, g_ref, dh_ref, dres_ref, dx_ref, dxb_ref, dg_ref):
        i = pl.program_id(0)
        xf = x_ref[...]
        r = lax.rsqrt(jnp.mean(xf * xf, axis=-1, keepdims=True) + EPS)
        y = xf * r
        dh_v = dh_ref[...]
        dy = dh_v * g_ref[...]
        c = jnp.mean(dy * y, axis=-1, keepdims=True)
        dx = r * (dy - y * c) + dres_ref[...]
        dx_ref[...] = dx
        dxb_ref[...] = dx.astype(BF16)
        part = jnp.sum(dh_v * y, axis=0, keepdims=True)

        @pl.when(i == 0)
        def _():
            dg_ref[...] = part

        @pl.when(i > 0)
        def _():
            dg_ref[...] += part

    row = pl.BlockSpec((tm, D), lambda i: (i, 0))
    vec = pl.BlockSpec((1, D), lambda i: (0, 0))
    return pl.pallas_call(
        body,
        name=name,
        grid=(T // tm,),
        in_specs=[row, vec, row, row],
        out_specs=[row, row, vec],
        out_shape=[jax.ShapeDtypeStruct((T, D), F32), jax.ShapeDtypeStruct((T, D), BF16), jax.ShapeDtypeStruct((1, D), F32)],
        compiler_params=_params(("arbitrary",)),
    )(x, g, dh, dres)


GROUP_W = 4 * HEAD_DIM
N_DIL_BLOCKS = 3 * N_GROUPS
N_NORMED = 2 * N_GROUPS


def _kind_of_group(j, g):
    return jnp.clip((j - g) // N_GROUPS, 0, 2)


def _head_rstd(xh):
    return lax.rsqrt(jnp.mean(xh * xh, axis=-1, keepdims=True) + EPS)


def _prep_fwd(proj, gains, cos, sin, n_sb_blocks, *, name):
    T = proj.shape[0]
    tm = _pick(T, (512, 256, 128))

    def body(p_ref, gain_ref, cos_ref, sin_ref, o0_ref, o1_ref, o2_ref, os_ref):
        j = pl.program_id(1)
        for g, o_ref in enumerate((o0_ref, o1_ref, o2_ref)):
            @pl.when(jnp.logical_and(j % N_GROUPS == g, j < N_NORMED))
            def _():
                cos_v, sin_v = cos_ref[...], sin_ref[...]
                for hh in range(4):
                    sl = slice(hh * HEAD_DIM, (hh + 1) * HEAD_DIM)
                    xh = p_ref[:, sl]
                    y = (xh * _head_rstd(xh)) * gain_ref[0, :, sl]
                    o_ref[:, sl] = (y * cos_v + pltpu.roll(y, HEAD_DIM // 2, 1) * sin_v).astype(BF16)

            @pl.when(j == N_NORMED + g)
            def _():
                o_ref[...] = p_ref[...].astype(BF16)

        @pl.when(j >= N_DIL_BLOCKS)
        def _():
            os_ref[...] = p_ref[...].astype(BF16)

    def group_spec(g):
        return pl.BlockSpec((tm, GROUP_W), lambda i, j: (i, _kind_of_group(j, g)))

    return pl.pallas_call(
        body,
        name=name,
        grid=(T // tm, N_DIL_BLOCKS + n_sb_blocks),
        in_specs=[
            pl.BlockSpec((tm, GROUP_W), lambda i, j: (i, j)),
            pl.BlockSpec((1, 1, GROUP_W), lambda i, j: (jnp.minimum(j, N_NORMED - 1), 0, 0)),
            pl.BlockSpec((tm, HEAD_DIM), lambda i, j: (i, 0)),
            pl.BlockSpec((tm, HEAD_DIM), lambda i, j: (i, 0)),
        ],
        out_specs=[group_spec(0), group_spec(1), group_spec(2),
                   pl.BlockSpec((tm, GROUP_W), lambda i, j: (i, jnp.maximum(j - N_DIL_BLOCKS, 0)))],
        out_shape=[jax.ShapeDtypeStruct((T, 3 * GROUP_W), BF16)] * N_GROUPS
        + [jax.ShapeDtypeStruct((T, n_sb_blocks * GROUP_W), BF16)],
        compiler_params=_params(("parallel", "arbitrary")),
    )(proj, gains, cos, sin)


def _prep_bwd(proj, dqkv, gains, cos, sin, *, name):
    T = proj.shape[0]
    tm = _pick(T, (512, 256, 128))

    def body(p_ref, d0_ref, d1_ref, d2_ref, gain_ref, cos_ref, sin_ref, o_ref, dgain_ref):
        j, i = pl.program_id(0), pl.program_id(1)

        def normed_bwd(d_ref):
            cos_v, sin_v = cos_ref[...], sin_ref[...]
            part = jnp.zeros((1, HEAD_DIM), F32)
            for hh in range(4):
                sl = slice(hh * HEAD_DIM, (hh + 1) * HEAD_DIM)
                xh = p_ref[:, sl]
                r = _head_rstd(xh)
                y0 = xh * r
                d_out = d_ref[:, sl]
                d_yg = d_out * cos_v + pltpu.roll(d_out * sin_v, HEAD_DIM // 2, 1)
                part = part + jnp.sum(d_yg * y0, axis=0, keepdims=True)
                dy0 = d_yg * gain_ref[0, :, sl]
                c = jnp.mean(dy0 * y0, axis=-1, keepdims=True)
                o_ref[:, sl] = (r * (dy0 - y0 * c)).astype(BF16)

            @pl.when(i == 0)
            def _():
                dgain_ref[0] = part

            @pl.when(i > 0)
            def _():
                dgain_ref[0] += part

        for g, d_ref in enumerate((d0_ref, d1_ref, d2_ref)):
            @pl.when(jnp.logical_and(j % N_GROUPS == g, j < N_NORMED))
            def _():
                normed_bwd(d_ref)

            @pl.when(j == N_NORMED + g)
            def _():
                o_ref[...] = d_ref[...].astype(BF16)

    gain_row = lambda j, i: (jnp.minimum(j, N_NORMED - 1), 0, 0)

    def grad_spec(g):
        return pl.BlockSpec((tm, GROUP_W), lambda j, i: (i, _kind_of_group(j, g)))

    return pl.pallas_call(
        body,
        name=name,
        grid=(N_DIL_BLOCKS, T // tm),
        in_specs=[
            pl.BlockSpec((tm, GROUP_W), lambda j, i: (i, j)),
            grad_spec(0), grad_spec(1), grad_spec(2),
            pl.BlockSpec((1, 1, GROUP_W), gain_row),
            pl.BlockSpec((tm, HEAD_DIM), lambda j, i: (i, 0)),
            pl.BlockSpec((tm, HEAD_DIM), lambda j, i: (i, 0)),
        ],
        out_specs=[pl.BlockSpec((tm, GROUP_W), lambda j, i: (i, j)), pl.BlockSpec((1, 1, HEAD_DIM), gain_row)],
        out_shape=[jax.ShapeDtypeStruct((T, N_DIL_BLOCKS * GROUP_W), BF16),
                   jax.ShapeDtypeStruct((2 * N_GROUPS, 1, HEAD_DIM), F32)],
        compiler_params=_params(("arbitrary", "arbitrary")),
    )(proj, *dqkv, gains, cos, sin)


def _nt(a, b):
    return lax.dot_general(a, b, _DIMS["nt"], preferred_element_type=F32)


def _tn(a, b):
    return lax.dot_general(a, b, _DIMS["tn"], preferred_element_type=F32)


def _nn(a, b):
    return jnp.dot(a, b, preferred_element_type=F32)


def _window_masks():
    row = lax.broadcasted_iota(jnp.int32, (BLOCK, BLOCK), 0)
    col = lax.broadcasted_iota(jnp.int32, (BLOCK, BLOCK), 1)
    return row >= col, col >= row


def _heads():
    return [slice(hh * HEAD_DIM, (hh + 1) * HEAD_DIM) for hh in range(GROUP_W // HEAD_DIM)]


def _dil_fwd(qkv, g, *, name):
    T = qkv.shape[0]
    r = DILATIONS[g]
    L = T // r
    nb = L // BLOCK
    scale = 1.0 / math.sqrt(HEAD_DIM)
    view = qkv.reshape(L, r * 3 * GROUP_W)

    def body(q_ref, kc_ref, kp_ref, vc_ref, vp_ref, o_ref, ld_ref):
        n = pl.program_id(1)
        m_cur, m_prev = _window_masks()
        m_prev = jnp.logical_and(m_prev, n > 0)
        for sl in _heads():
            q = q_ref[:, sl]
            s_c = jnp.where(m_cur, _nt(q, kc_ref[:, sl]) * scale, NEG)
            s_p = jnp.where(m_prev, _nt(q, kp_ref[:, sl]) * scale, NEG)
            m = jnp.maximum(jnp.max(s_c, axis=-1, keepdims=True), jnp.max(s_p, axis=-1, keepdims=True))
            p_c = jnp.exp(s_c - m)
            p_p = jnp.exp(s_p - m)
            l = jnp.sum(p_c, axis=-1, keepdims=True) + jnp.sum(p_p, axis=-1, keepdims=True)
            inv = 1.0 / l
            o_ref[:, sl] = _nn((p_c * inv).astype(BF16), vc_ref[:, sl]) + _nn((p_p * inv).astype(BF16), vp_ref[:, sl])
            ld_ref[:, sl] = jnp.broadcast_to(m + jnp.log(l), (BLOCK, HEAD_DIM))

    blk = (BLOCK, GROUP_W)
    cur = lambda kind: pl.BlockSpec(blk, lambda c, n: (n, 3 * c + kind))
    prev = lambda kind: pl.BlockSpec(blk, lambda c, n: (jnp.maximum(n - 1, 0), 3 * c + kind))
    out_spec = pl.BlockSpec(blk, lambda c, n: (n, c))
    o, ld = pl.pallas_call(
        body,
        name=name,
        grid=(r, nb),
        in_specs=[cur(0), cur(1), prev(1), cur(2), prev(2)],
        out_specs=[out_spec, out_spec],
        out_shape=[jax.ShapeDtypeStruct((L, r * GROUP_W), F32)] * 2,
        compiler_params=_params(("parallel", "arbitrary")),
    )(view, view, view, view, view)
    return o.reshape(T, GROUP_W), ld.reshape(T, GROUP_W)


def _dil_bwd(qkv, do, ld, dterm, g, *, name):
    T = qkv.shape[0]
    r = DILATIONS[g]
    L = T // r
    nb = L // BLOCK
    scale = 1.0 / math.sqrt(HEAD_DIM)
    view = qkv.reshape(L, r * 3 * GROUP_W)
    do_v, ld_v, dt_v = (t.reshape(L, r * GROUP_W) for t in (do, ld, dterm))

    def body(q_ref, qn_ref, kc_ref, kp_ref, vc_ref, vp_ref, do_ref, don_ref, ld_ref, ldn_ref, dt_ref, dtn_ref, out_ref):
        n = pl.program_id(1)
        m_cur, m_prev = _window_masks()
        has_prev, has_next = jnp.logical_and(m_prev, n > 0), jnp.logical_and(m_prev, n < nb - 1)

        def tile(q, k, v, do_t, ld_t, dt_t, mask):
            s = _nt(q, k) * scale
            p = jnp.where(mask, jnp.exp(s - ld_t[:, 0:1]), 0.0)
            ds = p * (_nt(do_t, v) + dt_t[:, 0:1]) * scale
            return p.astype(BF16), ds.astype(BF16)

        for hh, sl in enumerate(_heads()):
            kc, vc, kp = kc_ref[:, sl], vc_ref[:, sl], kp_ref[:, sl]
            q, do_t, ld_t, dt_t = q_ref[:, sl], do_ref[:, sl], ld_ref[:, sl], dt_ref[:, sl]
            p_cc, ds_cc = tile(q, kc, vc, do_t, ld_t, dt_t, m_cur)
            _, ds_cp = tile(q, kp, vp_ref[:, sl], do_t, ld_t, dt_t, has_prev)
            qn, don = qn_ref[:, sl], don_ref[:, sl]
            p_nc, ds_nc = tile(qn, kc, vc, don, ldn_ref[:, sl], dtn_ref[:, sl], has_next)
            at = lambda kind: slice(kind * GROUP_W + hh * HEAD_DIM, kind * GROUP_W + (hh + 1) * HEAD_DIM)
            out_ref[:, at(0)] = _nn(ds_cc, kc) + _nn(ds_cp, kp)
            out_ref[:, at(1)] = _tn(ds_cc, q) + _tn(ds_nc, qn)
            out_ref[:, at(2)] = _tn(p_cc, do_t) + _tn(p_nc, don)

    blk = (BLOCK, GROUP_W)
    qkv_spec = lambda kind, shift: pl.BlockSpec(blk, lambda c, n: (jnp.clip(n + shift, 0, nb - 1), 3 * c + kind))
    row_spec = lambda shift: pl.BlockSpec(blk, lambda c, n: (jnp.clip(n + shift, 0, nb - 1), c))
    out = pl.pallas_call(
        body,
        name=name,
        grid=(r, nb),
        in_specs=[qkv_spec(0, 0), qkv_spec(0, 1), qkv_spec(1, 0), qkv_spec(1, -1), qkv_spec(2, 0), qkv_spec(2, -1),
                  row_spec(0), row_spec(1), row_spec(0), row_spec(1), row_spec(0), row_spec(1)],
        out_specs=pl.BlockSpec((BLOCK, 3 * GROUP_W), lambda c, n: (n, c)),
        out_shape=jax.ShapeDtypeStruct((L, r * 3 * GROUP_W), F32),
        compiler_params=_params(("parallel", "arbitrary")),
    )(view, view, view, view, view, view, do_v, do_v, ld_v, ld_v, dt_v, dt_v)
    return out.reshape(T, 3 * GROUP_W)


def _group_weights(ld_refs):
    lds = [r[...] for r in ld_refs]
    m = jnp.maximum(jnp.maximum(lds[0], lds[1]), lds[2])
    es = [jnp.exp(v - m) for v in lds]
    inv = 1.0 / (es[0] + es[1] + es[2])
    return [e * inv for e in es]


def _merge_fwd(os_, lds, *, name):
    T = os_[0].shape[0]
    tm = _pick(T, (1024, 512, 256, 128))

    def body(o0, o1, o2, l0, l1, l2, y_ref):
        w = _group_weights((l0, l1, l2))
        y_ref[...] = (w[0] * o0[...] + w[1] * o1[...] + w[2] * o2[...]).astype(BF16)

    spec = pl.BlockSpec((tm, GROUP_W), lambda i: (i, 0))
    return pl.pallas_call(
        body, name=name, grid=(T // tm,), in_specs=[spec] * 6, out_specs=spec,
        out_shape=jax.ShapeDtypeStruct((T, GROUP_W), BF16), compiler_params=_params(("parallel",)),
    )(*os_, *lds)


def _merge_bwd(os_, lds, dy, *, name):
    T = dy.shape[0]
    tm = _pick(T, (512, 256, 128))

    def body(o0, o1, o2, l0, l1, l2, dy_ref, do0, do1, do2, dt0, dt1, dt2):
        w = _group_weights((l0, l1, l2))
        dy_v = dy_ref[...]
        y = w[0] * o0[...] + w[1] * o1[...] + w[2] * o2[...]
        prod = dy_v * y
        for hh in range(4):
            sl = slice(hh * HEAD_DIM, (hh + 1) * HEAD_DIM)
            s = jnp.sum(prod[:, sl], axis=-1, keepdims=True)
            for wg, dt in zip(w, (dt0, dt1, dt2)):
                dt[:, sl] = -wg[:, sl] * s
        for wg, do in zip(w, (do0, do1, do2)):
            do[...] = (wg * dy_v).astype(BF16)

    spec = pl.BlockSpec((tm, GROUP_W), lambda i: (i, 0))
    outs = pl.pallas_call(
        body, name=name, grid=(T // tm,), in_specs=[spec] * 7, out_specs=[spec] * 6,
        out_shape=[jax.ShapeDtypeStruct((T, GROUP_W), BF16)] * 3 + [jax.ShapeDtypeStruct((T, GROUP_W), F32)] * 3,
        compiler_params=_params(("parallel",)),
    )(*os_, *lds, dy)
    return outs[:3], outs[3:]


SB_ROWS = 512


def _sum_matrix(inclusive):
    j = lax.broadcasted_iota(jnp.int32, (2 * BLOCK, 2 * BLOCK), 0) % BLOCK
    s = lax.broadcasted_iota(jnp.int32, (2 * BLOCK, 2 * BLOCK), 1)
    later = (j >= s) if inclusive else (j > s)
    return jnp.logical_or(s >= BLOCK, later).astype(BF16)


def _block_sums(x, mat):
    hi = x.astype(BF16)
    lo = (x - hi.astype(F32)).astype(BF16)
    r = _nn(jnp.concatenate([hi, lo], axis=1), mat)
    return r[:, :BLOCK], r[:, BLOCK:]


def _log_terms(z):
    t = jnp.log(1.0 + jnp.exp(-jnp.abs(z)))
    return -(jnp.maximum(z, 0.0) + t), jnp.minimum(z, 0.0) - t


def _causal_mask(rows):
    row = lax.broadcasted_iota(jnp.int32, (rows, rows), 0)
    col = lax.broadcasted_iota(jnp.int32, (rows, rows), 1)
    return col < row


def _sb_fwd(qkv, n_heads, col0, *, name):
    T = qkv.shape[0]
    tq = _pick(T, (SB_ROWS, BLOCK))
    nq, nsub = T // tq, tq // BLOCK
    scale = 1.0 / math.sqrt(HEAD_DIM)

    def body(q_ref, k_ref, v_ref, o_ref, ob_ref):
        i = pl.program_id(1)
        q = q_ref[...]
        mat = _sum_matrix(False)

        def chunk(j, carry, acc, mask):
            rows = pl.ds(pl.multiple_of(j * tq, tq), tq)
            z = _nt(q, k_ref[rows, :]) * scale
            lk, ls = _log_terms(z)
            if mask is not None:
                lk = jnp.where(mask, lk, 0.0)
            a = []
            for b in reversed(range(nsub)):
                sl = slice(b * BLOCK, (b + 1) * BLOCK)
                later, total = _block_sums(lk[:, sl], mat)
                a.append(jnp.exp(ls[:, sl] + (later + carry)))
                carry = carry + total
            a = jnp.concatenate(a[::-1], axis=1)
            if mask is not None:
                a = jnp.where(mask, a, 0.0)
            return carry, acc + _nn(a.astype(BF16), v_ref[rows, :])

        zero = jnp.zeros((tq, HEAD_DIM), F32)
        carry, acc = chunk(i, zero, zero, _causal_mask(tq))

        def step(jj, state):
            return chunk(i - 1 - jj, state[0], state[1], None)

        carry, acc = lax.fori_loop(0, i, step, (carry, acc))
        o_ref[...] = acc
        ob_ref[...] = acc.astype(BF16)

    blk = (tq, HEAD_DIM)
    out_spec = pl.BlockSpec(blk, lambda h, i: (i, h))
    return pl.pallas_call(
        body,
        name=name,
        grid=(n_heads, nq),
        in_specs=[
            pl.BlockSpec(blk, lambda h, i: (i, col0 + h)),
            pl.BlockSpec((T, HEAD_DIM), lambda h, i: (0, col0 + n_heads + h)),
            pl.BlockSpec((T, HEAD_DIM), lambda h, i: (0, col0 + 2 * n_heads + h)),
        ],
        out_specs=[out_spec, out_spec],
        out_shape=[jax.ShapeDtypeStruct((T, n_heads * HEAD_DIM), F32), jax.ShapeDtypeStruct((T, n_heads * HEAD_DIM), BF16)],
        compiler_params=_params(("parallel", "arbitrary")),
    )(qkv, qkv, qkv)


def _sb_bwd(qkv, o32, do, n_heads, col0, *, name):
    T = qkv.shape[0]
    tq = _pick(T, (SB_ROWS, BLOCK))
    nq, nsub = T // tq, tq // BLOCK
    scale = 1.0 / math.sqrt(HEAD_DIM)

    def body(q_ref, k_ref, v_ref, o_ref, do_ref, dq_ref, dk_ref, dv_ref, dk_acc, dv_acc):
        i = pl.program_id(1)

        @pl.when(i == 0)
        def _():
            dk_acc[...] = jnp.zeros_like(dk_acc)
            dv_acc[...] = jnp.zeros_like(dv_acc)

        q, do_t = q_ref[...], do_ref[...]
        delta = jnp.broadcast_to(jnp.sum(do_t.astype(F32) * o_ref[...], axis=-1, keepdims=True), (tq, HEAD_DIM))
        mat, mat_incl = _sum_matrix(False), _sum_matrix(True)

        def chunk(j, carry_b, carry_g, dq, mask):
            rows = pl.ds(pl.multiple_of(j * tq, tq), tq)
            k_t, v_t = k_ref[rows, :], v_ref[rows, :]
            z = _nt(q, k_t) * scale
            lk, ls = _log_terms(z)
            if mask is not None:
                lk = jnp.where(mask, lk, 0.0)
            d_a = _nt(do_t, v_t)
            a_parts, dz_parts = [], []
            for b in reversed(range(nsub)):
                sl = slice(b * BLOCK, (b + 1) * BLOCK)
                later, total = _block_sums(lk[:, sl], mat)
                a = jnp.exp(ls[:, sl] + (later + carry_b))
                carry_b = carry_b + total
                if mask is not None:
                    a = jnp.where(mask[:, sl], a, 0.0)
                a_b = a.astype(BF16)
                g = a_b.astype(F32) * d_a[:, sl]
                from_here, total_g = _block_sums(g, mat_incl)
                before = delta - (from_here + carry_g)
                carry_g = carry_g + total_g
                sig = jnp.exp(ls[:, sl])
                dz = (g - sig * (g + before)) * scale
                a_parts.append(a_b)
                dz_parts.append(dz)
            dz = jnp.concatenate(dz_parts[::-1], axis=1)
            if mask is not None:
                dz = jnp.where(mask, dz, 0.0)
            dz_b = dz.astype(BF16)
            dk_acc[rows, :] += _tn(dz_b, q)
            dv_acc[rows, :] += _tn(jnp.concatenate(a_parts[::-1], axis=1), do_t)
            return carry_b, carry_g, dq + _nn(dz_b, k_t)

        zero = jnp.zeros((tq, HEAD_DIM), F32)
        state = chunk(i, zero, zero, zero, _causal_mask(tq))

        def step(jj, st):
            return chunk(i - 1 - jj, st[0], st[1], st[2], None)

        state = lax.fori_loop(0, i, step, state)
        dq_ref[...] = state[2].astype(BF16)

        @pl.when(i == nq - 1)
        def _():
            dk_ref[...] = dk_acc[...].astype(BF16)
            dv_ref[...] = dv_acc[...].astype(BF16)

    blk = (tq, HEAD_DIM)
    full = (T, HEAD_DIM)
    dshape = jax.ShapeDtypeStruct((T, n_heads * HEAD_DIM), BF16)
    return pl.pallas_call(
        body,
        name=name,
        grid=(n_heads, nq),
        in_specs=[
            pl.BlockSpec(blk, lambda h, i: (i, col0 + h)),
            pl.BlockSpec(full, lambda h, i: (0, col0 + n_heads + h)),
            pl.BlockSpec(full, lambda h, i: (0, col0 + 2 * n_heads + h)),
            pl.BlockSpec(blk, lambda h, i: (i, h)),
            pl.BlockSpec(blk, lambda h, i: (i, h)),
        ],
        out_specs=[pl.BlockSpec(blk, lambda h, i: (i, h)), pl.BlockSpec(full, lambda h, i: (0, h)),
                   pl.BlockSpec(full, lambda h, i: (0, h))],
        out_shape=[dshape, dshape, dshape],
        scratch_shapes=[pltpu.VMEM(full, F32), pltpu.VMEM(full, F32)],
        compiler_params=_params(("arbitrary", "arbitrary")),
    )(qkv, qkv, qkv, o32, do)


def _gate_fwd(y_dil, y_sb, w_up_dil, w_up_sb, proj, gate_b, gate_col0, *, name):
    T, D = y_dil.shape[0], w_up_dil.shape[0]
    tm = _pick(T, (512, 256, 128))
    tn = _pick(D, (512, 256, 128))
    c0, nbr = gate_col0 // tn, D // tn

    def body(yd_ref, ys_ref, wd_ref, ws_ref, gp0_ref, gp1_ref, b_ref, o_ref):
        g0 = jax.nn.sigmoid(gp0_ref[...] + b_ref[0:1, :])
        g1 = jax.nn.sigmoid(gp1_ref[...] + b_ref[1:2, :])
        o_ref[...] = (g0 * _nt(yd_ref[...], wd_ref[...]) + g1 * _nt(ys_ref[...], ws_ref[...])).astype(BF16)

    return pl.pallas_call(
        body,
        name=name,
        grid=(T // tm, nbr),
        in_specs=[
            pl.BlockSpec((tm, y_dil.shape[1]), lambda i, j: (i, 0)),
            pl.BlockSpec((tm, y_sb.shape[1]), lambda i, j: (i, 0)),
            pl.BlockSpec((tn, w_up_dil.shape[1]), lambda i, j: (j, 0)),
            pl.BlockSpec((tn, w_up_sb.shape[1]), lambda i, j: (j, 0)),
            pl.BlockSpec((tm, tn), lambda i, j: (i, c0 + j)),
            pl.BlockSpec((tm, tn), lambda i, j: (i, c0 + nbr + j)),
            pl.BlockSpec((2, tn), lambda i, j: (0, j)),
        ],
        out_specs=pl.BlockSpec((tm, tn), lambda i, j: (i, j)),
        out_shape=jax.ShapeDtypeStruct((T, D), BF16),
        compiler_params=_params(("parallel", "parallel")),
    )(y_dil, y_sb, w_up_dil, w_up_sb, proj, proj, gate_b)


def _gate_bwd(y, w_up, proj, gate_b, dmixed, branch, gate_col0, *, name):
    T, D = y.shape[0], w_up.shape[0]
    tm = _pick(T, (512, 256, 128))
    tn = _pick(D, (512, 256, 128))
    c0 = gate_col0 // tn + branch * (D // tn)

    def body(y_ref, w_ref, gp_ref, b_ref, dm_ref, dup_ref, dgp_ref, db_ref):
        i = pl.program_id(1)
        g = jax.nn.sigmoid(gp_ref[...] + b_ref[branch:branch + 1, :])
        dm = dm_ref[...]
        dup_ref[...] = (dm * g).astype(BF16)
        dgp = (dm * _nt(y_ref[...], w_ref[...])) * (g * (1.0 - g))
        dgp_ref[...] = dgp.astype(BF16)
        part = jnp.sum(dgp, axis=0, keepdims=True)

        @pl.when(i == 0)
        def _():
            db_ref[...] = part

        @pl.when(i > 0)
        def _():
            db_ref[...] += part

    tile = pl.BlockSpec((tm, tn), lambda j, i: (i, j))
    return pl.pallas_call(
        body,
        name=name,
        grid=(D // tn, T // tm),
        in_specs=[
            pl.BlockSpec((tm, y.shape[1]), lambda j, i: (i, 0)),
            pl.BlockSpec((tn, w_up.shape[1]), lambda j, i: (j, 0)),
            pl.BlockSpec((tm, tn), lambda j, i: (i, c0 + j)),
            pl.BlockSpec((2, tn), lambda j, i: (0, j)),
            tile,
        ],
        out_specs=[tile, tile, pl.BlockSpec((1, tn), lambda j, i: (0, j))],
        out_shape=[jax.ShapeDtypeStruct((T, D), BF16), jax.ShapeDtypeStruct((T, D), BF16), jax.ShapeDtypeStruct((1, D), F32)],
        compiler_params=_params(("parallel", "arbitrary")),
    )(y, w_up, proj, gate_b, dmixed)


def _loss_head(y, target, *, name):
    T, D = y.shape
    tm = _pick(T, (256, 128))

    def body(y_ref, t_ref, dy_ref, dyb_ref, l_ref):
        i = pl.program_id(0)
        err = y_ref[...] - t_ref[...]
        dy = err * (1.0 / D)
        dy_ref[...] = dy
        dyb_ref[...] = dy.astype(BF16)
        part = 0.5 * jnp.sum(jnp.mean(err * err, axis=-1, keepdims=True), axis=0, keepdims=True)

        @pl.when(i == 0)
        def _():
            l_ref[...] = part

        @pl.when(i > 0)
        def _():
            l_ref[...] += part

    row = pl.BlockSpec((tm, D), lambda i: (i, 0))
    return pl.pallas_call(
        body, name=name, grid=(T // tm,), in_specs=[row, row],
        out_specs=[row, row, pl.BlockSpec((1, 1), lambda i: (0, 0))],
        out_shape=[jax.ShapeDtypeStruct((T, D), F32), jax.ShapeDtypeStruct((T, D), BF16), jax.ShapeDtypeStruct((1, 1), F32)],
        compiler_params=_params(("arbitrary",)),
    )(y, target)


def _sum_parts(p_ref, own):
    slot = _slot(_place())
    g = jnp.where(slot == 0, own, p_ref[0].astype(F32))
    for s in range(1, N_DEV):
        g = g + jnp.where(slot == s, own, p_ref[s].astype(F32))
    return g


def _held(step, l, last):
    layer, i = step
    return jnp.where(layer == l, i, last * (layer > l))


def _reduce_adamw(parts, own, w, m, v, *, own_chunked=True, name):
    n_layers = len(parts)
    C = w.shape[1]
    R = w.shape[0] // n_layers
    tr = R
    for cand in (1024, 512, 256, 128, 64, 32, 16, 8):
        if R % cand == 0 and cand * C <= 256 * 1024:
            tr = cand
            break
    nr = R // tr

    def body(*refs):
        p_refs, o_refs = refs[:n_layers], refs[n_layers:2 * n_layers]
        w_ref, m_ref, v_ref, g_ref, d_ref, nm_ref, nv_ref = refs[2 * n_layers:]
        layer = pl.program_id(0)
        for l, (p_ref, o_ref) in enumerate(zip(p_refs, o_refs)):
            @pl.when(layer == l)
            def _():
                g_ref[...] = _sum_parts(p_ref, (o_ref[0] if own_chunked else o_ref[...]).astype(F32))

        _adamw_update(g_ref[...], w_ref, m_ref, v_ref, d_ref, nm_ref, nv_ref)

    def part_spec(l):
        return pl.BlockSpec((N_DEV, tr, C), lambda *step: (0, _held(step, l, nr - 1), 0))

    def own_spec(l):
        if own_chunked:
            return pl.BlockSpec((1, tr, C), lambda *step: (_slot(_place()), _held(step, l, nr - 1), 0))
        return pl.BlockSpec((tr, C), lambda *step: (_held(step, l, nr - 1), 0))

    row = pl.BlockSpec((tr, C), lambda layer, i: (layer * nr + i, 0))
    return pl.pallas_call(
        body, name=name, grid=(n_layers, nr),
        in_specs=[part_spec(l) for l in range(n_layers)] + [own_spec(l) for l in range(n_layers)] + [row, row, row],
        out_specs=[row] * 4, out_shape=[jax.ShapeDtypeStruct(w.shape, F32)] * 4,
        compiler_params=_params(("arbitrary", "arbitrary")),
    )(*parts, *own, w, m, v)


def _adamw_update(g, w_ref, m_ref, v_ref, d_ref, nm_ref, nv_ref):
    m_new = ADAM_B1 * m_ref[...] + (1.0 - ADAM_B1) * g
    v_new = ADAM_B2 * v_ref[...] + (1.0 - ADAM_B2) * (g * g)
    m_hat = m_new / (1.0 - ADAM_B1 ** ADAM_STEP)
    v_hat = v_new / (1.0 - ADAM_B2 ** ADAM_STEP)
    d_ref[...] = -ADAM_LR * (m_hat / (jnp.sqrt(v_hat) + ADAM_EPS) + ADAM_WD * w_ref[...])
    nm_ref[...] = m_new
    nv_ref[...] = v_new


def _reduce_adamw_t(parts, own, w, m, v, *, name):
    n_layers = len(parts)
    _, n, K = parts[0].shape
    n_pad = w.shape[1]
    tm = _pick(K, (128,))
    nr = K // tm

    def body(*refs):
        p_refs, o_refs = refs[:n_layers], refs[n_layers:2 * n_layers]
        w_ref, m_ref, v_ref, g_ref, d_ref, nm_ref, nv_ref = refs[2 * n_layers:]
        layer = pl.program_id(0)
        for l, (p_ref, o_ref) in enumerate(zip(p_refs, o_refs)):
            @pl.when(layer == l)
            def _():
                g_t = _sum_parts(p_ref, o_ref[0].astype(F32))
                if n_pad > n:
                    g_t = jnp.concatenate([g_t, jnp.zeros((n_pad - n, tm), F32)], axis=0)
                g_ref[...] = g_t.T

        _adamw_update(g_ref[...], w_ref, m_ref, v_ref, d_ref, nm_ref, nv_ref)

    def part_spec(l):
        return pl.BlockSpec((N_DEV, n, tm), lambda *step: (0, 0, _held(step, l, nr - 1)))

    def own_spec(l):
        return pl.BlockSpec((1, n, tm), lambda *step: (_slot(_place()), 0, _held(step, l, nr - 1)))

    row = pl.BlockSpec((tm, n_pad), lambda layer, i: (layer * nr + i, 0))
    return pl.pallas_call(
        body, name=name, grid=(n_layers, nr),
        in_specs=[part_spec(l) for l in range(n_layers)] + [own_spec(l) for l in range(n_layers)] + [row, row, row],
        out_specs=[row] * 4, out_shape=[jax.ShapeDtypeStruct(w.shape, F32)] * 4,
        compiler_params=_params(("arbitrary", "arbitrary")),
    )(*parts, *own, w, m, v)


def _transpose_cast(x, *, name):
    R, C = x.shape
    tr, tc = _pick(R, (512, 256, 128)), _pick(C, (512, 256, 128))

    def body(x_ref, o_ref):
        o_ref[...] = x_ref[...].astype(F32).T.astype(BF16)

    return pl.pallas_call(
        body, name=name, grid=(R // tr, C // tc),
        in_specs=[pl.BlockSpec((tr, tc), lambda i, j: (i, j))],
        out_specs=pl.BlockSpec((tc, tr), lambda i, j: (j, i)),
        out_shape=jax.ShapeDtypeStruct((C, R), BF16),
        compiler_params=_params(("parallel", "parallel")),
    )(x)


_ANY = pl.BlockSpec(memory_space=pl.ANY)


def _place():
    return lax.axis_index("x"), lax.axis_index("y"), lax.axis_index("c")


def _slot(p):
    return 4 * p[0] + 2 * p[1] + p[2]


_HBM = pl.BlockSpec(memory_space=pltpu.HBM)
_SEM = pl.BlockSpec(memory_space=pltpu.SEMAPHORE)
_EFFECT = pltpu.SideEffectType.DATAFLOW_SIDE_EFFECTING
N_PEERS = N_DEV - 1


def _peers(me):
    flips = [(fx, fy, fc) for fx in (0, 1) for fy in (0, 1) for fc in (0, 1)][1:]
    return [tuple(1 - v if f else v for v, f in zip(me, flip)) for flip in flips]


def _peer_copy(src, lands, t, k, sender, to, send_sems, recv_sems):
    return pltpu.make_async_remote_copy(
        src_ref=src, dst_ref=lands[t].at[_slot(sender)], send_sem=send_sems.at[N_PEERS * t + k],
        recv_sem=recv_sems.at[N_PEERS * t + k], device_id=to, device_id_type=MESH)


class _Exchange:
    def __init__(self, chunked, whole, name):
        self.arrays = [pltpu.with_memory_space_constraint(a, pltpu.HBM) for a in list(chunked) + list(whole)]
        self.n, self.n_chunked, self.name = len(self.arrays), len(chunked), name

    def _src(self, ins, t, dest):
        return ins[t].at[_slot(dest)] if t < self.n_chunked else ins[t]

    def _land_shape(self, t):
        a = self.arrays[t]
        return a.shape if t < self.n_chunked else (N_DEV,) + a.shape

    def start(self, after=None):
        n = self.n

        def body(*refs):
            ins, lands = refs[:n], refs[n:2 * n]
            send_sems, recv_sems = refs[-2 * n - 3], refs[-2 * n - 2]
            token = refs[-1]
            me = _place()
            for t in range(n):
                for k, peer in enumerate(_peers(me)):
                    _peer_copy(self._src(ins, t, peer), lands, t, k, me, peer, send_sems, recv_sems).start()
            token[...] = jnp.zeros_like(token)

        lands = [pltpu.with_memory_space_constraint(lax.empty(self._land_shape(t), a.dtype), pltpu.HBM)
                 for t, a in enumerate(self.arrays)]
        sems = pltpu.SemaphoreType.DMA((N_PEERS * n,))
        outs = pl.pallas_call(
            body,
            name=self.name + "_start",
            in_specs=[_HBM] * (2 * n) + ([_ANY] if after is not None else []),
            out_specs=[_SEM, _SEM] + [_HBM] * (2 * n) + [pl.BlockSpec(memory_space=pltpu.VMEM)],
            out_shape=[sems, sems] + [pltpu.HBM(a.shape, a.dtype) for a in self.arrays + lands]
            + [jax.ShapeDtypeStruct((8, 128), F32)],
            input_output_aliases={t: 2 + t for t in range(2 * n)},
            compiler_params=pltpu.CompilerParams(has_side_effects=_EFFECT),
        )(*self.arrays, *lands, *([after] if after is not None else []))
        self.sems, self.thru, self.lands, self.token = outs[:2], outs[2:2 + n], outs[2 + n:2 + 2 * n], outs[-1]
        return self.token

    def finish(self, after):
        n = self.n

        def wait_body(*refs):
            ins, lands, (send_sems, recv_sems) = refs[:n], refs[n:2 * n], refs[2 * n:2 * n + 2]
            me = _place()
            for t in range(n):
                for k, peer in enumerate(_peers(me)):
                    cp = _peer_copy(self._src(ins, t, peer), lands, t, k, peer, peer, send_sems, recv_sems)
                    cp.wait_send()
                    cp.wait_recv()

        outs = pl.pallas_call(
            wait_body,
            name=self.name + "_wait",
            in_specs=[_HBM] * (2 * n) + [_SEM, _SEM, _ANY],
            out_specs=[_HBM] * (2 * n),
            out_shape=[pltpu.HBM(a.shape, a.dtype) for a in self.thru + self.lands],
            input_output_aliases={t: t for t in range(2 * n)},
            compiler_params=pltpu.CompilerParams(has_side_effects=_EFFECT),
        )(*self.thru, *self.lands, *self.sems, after)
        return outs[:n], outs[n:]


def _place_own(land, shard, *, name):
    n, K = shard.shape
    tr = n
    for cand in (1024, 736, 512, 256, 128):
        if n % cand == 0:
            tr = cand
            break

    def body(s_ref, land_ref, o_ref):
        del land_ref
        o_ref[0] = s_ref[...]

    return pl.pallas_call(
        body, name=name, grid=(n // tr,),
        in_specs=[pl.BlockSpec((tr, K), lambda i: (i, 0)), _ANY],
        out_specs=pl.BlockSpec((1, tr, K), lambda i: (_slot(_place()), i, 0)),
        out_shape=jax.ShapeDtypeStruct(land.shape, land.dtype),
        input_output_aliases={1: 0},
        compiler_params=_params(("arbitrary",)),
    )(shard, land)


def _rope_tables(T):
    half = HEAD_DIM // 2
    inv_freq = ROPE_THETA ** (-jnp.arange(half, dtype=F32) / half)
    ang = jnp.arange(T, dtype=F32)[:, None] * inv_freq[None, :]
    cos, sin = jnp.cos(ang), jnp.sin(ang)
    return jnp.concatenate([cos, cos], axis=-1), jnp.concatenate([-sin, sin], axis=-1)


def _gain_table(q_gain, k_gain):
    return jnp.tile(jnp.concatenate([q_gain, k_gain], axis=0), (1, 4))[:, None, :]


def _sb_heads(w_in):
    n_in, d_model = w_in.shape
    return (n_in - N_DIL_BLOCKS * GROUP_W - 2 * d_model) // (3 * HEAD_DIM)


def _layer_fwd(x, p, cos, sin, tag, early=None, late=None):
    if early is not None:
        p.update(early(x))
    sb_heads = _sb_heads(p["w_in"])
    n_sb_blocks = 3 * sb_heads * HEAD_DIM // GROUP_W
    s = {"x": x}
    s["h"] = _rmsnorm_fwd(x, p["norm1"], name=f"norm1_fwd{tag}")
    s["proj"] = _matmul(s["h"], p["w_in"], mode="nt", out_dtype=F32, name=f"proj_fwd{tag}")
    *s["qkv_d"], s["qkv_s"] = _prep_fwd(s["proj"], p["gains"], cos, sin, n_sb_blocks, name=f"prep_fwd{tag}")
    outs = [_dil_fwd(s["qkv_d"][g], g, name=f"dil{g}_fwd{tag}") for g in range(N_GROUPS)]
    s["o"], s["ld"] = [o for o, _ in outs], [ld for _, ld in outs]
    s["y_dil"] = _merge_fwd(s["o"], s["ld"], name=f"merge_fwd{tag}")
    s["y_sb32"], s["y_sb"] = _sb_fwd(s["qkv_s"], sb_heads, 0, name=f"sb_fwd{tag}")
    if late is not None:
        p.update(late(s["y_sb"]))
    s["mixed"] = _gate_fwd(s["y_dil"], s["y_sb"], p["w_up_dil"], p["w_up_sb"], s["proj"], p["gate_b"],
                           (N_DIL_BLOCKS + n_sb_blocks) * GROUP_W, name=f"gate_fwd{tag}")
    s["x1"] = _matmul(s["mixed"], p["w_out"], mode="nn", out_dtype=F32, epilogue="add", extra=x, name=f"out_fwd{tag}")
    s["h2"] = _rmsnorm_fwd(s["x1"], p["norm2"], name=f"norm2_fwd{tag}")
    s["f"], s["a"] = _matmul(s["h2"], p["w_ff1"], mode="nt", out_dtype=BF16, epilogue="relu2", name=f"ff1_fwd{tag}")
    x2 = _matmul(s["a"], p["w_ff2"], mode="nn", out_dtype=F32, epilogue="add", extra=s["x1"], name=f"ff2_fwd{tag}")
    return x2, s


def _layer_bwd(dx2, dx2_b, p, s, cos, sin, tag, mid_hook=None, end_hook=None):
    sb_heads = _sb_heads(p["w_in"])
    gate_col0 = N_DIL_BLOCKS * GROUP_W + 3 * sb_heads * HEAD_DIM
    g = {}
    df = _matmul(dx2_b, p["w_ff2"], mode="nt", out_dtype=BF16, epilogue="relu2_bwd", extra=s["f"], name=f"ff2_bwd{tag}")
    g["w_ff2"] = _matmul(s["a"], dx2_b, mode="tn", out_dtype=BF16, name=f"ff2_wgrad{tag}")
    dh2 = _matmul(df, p["w_ff1"], mode="nn", out_dtype=F32, name=f"ff1_bwd{tag}")
    g["w_ff1"] = _matmul(df, s["h2"], mode="tn", out_dtype=BF16, name=f"ff1_wgrad{tag}")
    dx1, dx1_b, g["norm2"] = _rmsnorm_bwd(s["x1"], p["norm2"], dh2, dx2, name=f"norm2_bwd{tag}")
    dmixed = _matmul(dx1_b, p["w_out"], mode="nt", out_dtype=F32, name=f"out_bwd{tag}")
    g["w_out"] = _matmul(s["mixed"], dx1_b, mode="tn", out_dtype=BF16, name=f"out_wgrad{tag}")
    gate_b = p["gate_b"] if mid_hook is None else p["gate_b"] + mid_hook(g)[0, 0]
    dup_dil, dgp0, db0 = _gate_bwd(s["y_dil"], p["w_up_dil"], s["proj"], gate_b, dmixed, 0, gate_col0, name=f"gate0_bwd{tag}")
    dup_sb, dgp1, db1 = _gate_bwd(s["y_sb"], p["w_up_sb"], s["proj"], gate_b, dmixed, 1, gate_col0, name=f"gate1_bwd{tag}")
    g["gate_b"] = jnp.concatenate([db0, db1], axis=0)
    dy_dil = _matmul(dup_dil, p["w_up_dil"], mode="nn", out_dtype=F32, name=f"updil_bwd{tag}")
    g["w_up_dil"] = _matmul(dup_dil, s["y_dil"], mode="tn", out_dtype=BF16, name=f"updil_wgrad{tag}")
    dy_sb = _matmul(dup_sb, p["w_up_sb"], mode="nn", out_dtype=BF16, name=f"upsb_bwd{tag}")
    g["w_up_sb"] = _matmul(dup_sb, s["y_sb"], mode="tn", out_dtype=BF16, name=f"upsb_wgrad{tag}")
    dos, dterms = _merge_bwd(s["o"], s["ld"], dy_dil, name=f"merge_bwd{tag}")
    dqkv = [_dil_bwd(s["qkv_d"][grp], dos[grp], s["ld"][grp], dterms[grp], grp, name=f"dil{grp}_bwd{tag}")
            for grp in range(N_GROUPS)]
    dproj_d, dgain = _prep_bwd(s["proj"], dqkv, p["gains"], cos, sin, name=f"prep_bwd{tag}")
    g["q_gain"], g["k_gain"] = dgain[:N_GROUPS, 0], dgain[N_GROUPS:, 0]
    dq_s, dk_s, dv_s = _sb_bwd(s["qkv_s"], s["y_sb32"], dy_sb, sb_heads, 0, name=f"sb_bwd{tag}")
    dproj = jnp.concatenate([dproj_d, dq_s, dk_s, dv_s, dgp0, dgp1], axis=1)
    dh = _matmul(dproj, p["w_in"], mode="nn", out_dtype=F32, name=f"proj_bwd{tag}")
    g["w_in"] = _matmul(dproj, s["h"], mode="tn", out_dtype=BF16, name=f"proj_wgrad{tag}")
    norm1 = p["norm1"] if end_hook is None else p["norm1"] + end_hook(g)[0, 0]
    dx, dx_b, g["norm1"] = _rmsnorm_bwd(s["x"], norm1, dh, dx1, name=f"norm1_bwd{tag}")
    return dx, dx_b, g


def _local_step(x, target, layers, early=None, late=None, mid_hooks=None, end_hooks=None):
    depth = len(layers)
    early, late, mid_hooks, end_hooks = (h or [None] * depth for h in (early, late, mid_hooks, end_hooks))
    cos, sin = _rope_tables(x.shape[0])
    saved = []
    for l, p in enumerate(layers):
        x, s = _layer_fwd(x, p, cos, sin, f"_l{l}", early[l], late[l])
        saved.append(s)
    dx, dx_b, loss = _loss_head(x, target, name="loss_head")
    grads = [None] * depth
    for l in reversed(range(depth)):
        dx, dx_b, grads[l] = _layer_bwd(dx, dx_b, layers[l], saved[l], cos, sin, f"_l{l}", mid_hooks[l], end_hooks[l])
    return loss, dx, grads


_MATRICES = ("w_in", "w_up_dil", "w_up_sb", "w_out", "w_ff1", "w_ff2")
_TRANSPOSED = ("w_in", "w_up_dil", "w_up_sb", "w_ff1")
_SMALL = ("norm1_g", "norm2_g", "q_norm_g", "k_norm_g")


def _unshard(blocks, name):
    if name == "gate_b":
        return jnp.transpose(blocks, (1, 0, 2)).reshape(blocks.shape[1], N_DEV * blocks.shape[2])
    return blocks.reshape(N_DEV * blocks.shape[1], blocks.shape[2])


def _to_chunks(full, name):
    if name == "gate_b":
        r, cols = full.shape
        return jnp.transpose(full.reshape(r, N_DEV, cols // N_DEV), (1, 0, 2))
    return full.reshape(N_DEV, full.shape[0] // N_DEV, full.shape[1])


def _lane_pad(n):
    return -n % HEAD_DIM


def _pack_small(norm1, norm2, qg, kg):
    flat = jnp.concatenate([t.reshape(-1, HEAD_DIM) for t in (norm1, norm2, qg, kg)], axis=0)
    return jnp.pad(flat, ((0, -flat.shape[0] % 8), (0, 0)))


def _unpack_small(packed, shapes):
    out, row = [], 0
    for shape in shapes:
        rows = math.prod(shape) // HEAD_DIM
        out.append(packed[row:row + rows].reshape(shape))
        row += rows
    return out


def kernel(x, norm1_g, w_in, q_norm_g, k_norm_g, w_up_dil, w_up_sb, gate_b, w_out, norm2_g, w_ff1, w_ff2, loss_target, m_norm1_g, m_w_in, m_q_norm_g, m_k_norm_g, m_w_up_dil, m_w_up_sb, m_gate_b, m_w_out, m_norm2_g, m_w_ff1, m_w_ff2, v_norm1_g, v_w_in, v_q_norm_g, v_k_norm_g, v_w_up_dil, v_w_up_sb, v_gate_b, v_w_out, v_norm2_g, v_w_ff1, v_w_ff2):
    names = ("norm1_g", "w_in", "q_norm_g", "k_norm_g", "w_up_dil", "w_up_sb", "gate_b", "w_out", "norm2_g", "w_ff1", "w_ff2")
    w = dict(zip(names, (norm1_g, w_in, q_norm_g, k_norm_g, w_up_dil, w_up_sb, gate_b, w_out, norm2_g, w_ff1, w_ff2)))
    m = dict(zip(names, (m_norm1_g, m_w_in, m_q_norm_g, m_k_norm_g, m_w_up_dil, m_w_up_sb, m_gate_b, m_w_out, m_norm2_g, m_w_ff1, m_w_ff2)))
    v = dict(zip(names, (v_norm1_g, v_w_in, v_q_norm_g, v_k_norm_g, v_w_up_dil, v_w_up_sb, v_gate_b, v_w_out, v_norm2_g, v_w_ff1, v_w_ff2)))
    depth = norm1_g.shape[0]
    assert depth == 2, "the exchange schedule below is written for two layers"
    sharded = _MATRICES + ("gate_b",)
    late_names = sharded[1:]

    def shards(layer, which):
        out = []
        for n in which:
            shard = w[n][layer]
            if n in _TRANSPOSED:
                cols = shard.shape[1]
                padded = jnp.pad(shard, ((0, 0), (0, _lane_pad(cols))))
                out.append(_transpose_cast(padded, name=f"shard_t_{n}_l{layer}")[:cols])
            else:
                out.append(shard if n == "gate_b" else shard.astype(BF16))
        return out

    def full(which, exchange, after):
        sent, landed = exchange.finish(after)
        return {n: _unshard(_place_own(land, shard, name=f"{exchange.name}_own_{n}"), n)
                for n, shard, land in zip(which, sent, landed)}

    gather_in0 = _Exchange([], shards(0, sharded[:1]), "gather_in_l0")
    gather_rest0 = _Exchange([], shards(0, late_names), "gather_rest_l0")
    gather_1 = _Exchange([], shards(1, sharded), "gather_l1")
    token = gather_1.start(gather_rest0.start(gather_in0.start()))
    layers = []
    for l in range(depth):
        layers.append({"norm1": norm1_g[l][None], "norm2": norm2_g[l][None], "gains": _gain_table(q_norm_g[l], k_norm_g[l])})
    layers[0].update(full(sharded[:1], gather_in0, token))
    early = [None, lambda x_in: full(sharded, gather_1, x_in)]
    late = [lambda y_sb: full(late_names, gather_rest0, y_sb), None]

    first_names = ("w_out", "w_ff1", "w_ff2")
    last_names = tuple(n for n in sharded if n not in first_names)
    exchanges = {}

    def start_exchange(key, which):
        def hook(g):
            exchanges[key] = _Exchange([_to_chunks(g[n], n) for n in which], [], "exchange_" + key)
            return exchanges[key].start()
        return hook

    loss_part, dx, grads = _local_step(
        x[0], loss_target[0], layers, early, late,
        mid_hooks=[start_exchange("first_l0", first_names), None], end_hooks=[None, start_exchange("l1", sharded)])
    loss = lax.psum(loss_part[0, 0], ("x", "y", "c"))
    small = _pack_small(jnp.concatenate([g["norm1"] for g in grads]), jnp.concatenate([g["norm2"] for g in grads]),
                        jnp.stack([g["q_gain"] for g in grads]), jnp.stack([g["k_gain"] for g in grads]))
    last = _Exchange([_to_chunks(grads[0][n], n) for n in last_names], [small], "exchange_last_l0")
    token = last.start()
    sent_1, landed_1 = exchanges["l1"].finish(token)
    sent_0, landed_0 = exchanges["first_l0"].finish(landed_1[0])
    sent_last, landed_last = last.finish(landed_0[0])
    moved = {n: [None, pair] for n, pair in zip(sharded, zip(sent_1, landed_1))}
    for n, pair in list(zip(first_names, zip(sent_0, landed_0))) + list(zip(last_names, zip(sent_last, landed_last))):
        moved[n][0] = pair

    out = {}
    for n in sharded:
        rows, cols = depth * w[n].shape[1], w[n].shape[2]
        own, parts = ([pair[k] for pair in moved[n]] for k in (0, 1))
        state = [t.reshape(rows, cols) for t in (w[n], m[n], v[n])]
        if n in _TRANSPOSED:
            pad = _lane_pad(cols)
            res = _reduce_adamw_t(parts, own, *(jnp.pad(t, ((0, 0), (0, pad))) for t in state), name=f"adamw_{n}")
            res = [t[:, :cols] for t in res]
        else:
            if w[n].shape[1] % 8:
                own, parts = [jnp.concatenate(own, axis=1)], [jnp.concatenate(parts, axis=1)]
            res = _reduce_adamw(parts, own, *state, name=f"adamw_{n}")
        out[n] = [t.reshape(w[n].shape) for t in res]
    small_res = _reduce_adamw([landed_last[-1]], [sent_last[-1]], _pack_small(*(w[n] for n in _SMALL)),
                              _pack_small(*(m[n] for n in _SMALL)), _pack_small(*(v[n] for n in _SMALL)),
                              own_chunked=False, name="adamw_small")
    small_shapes = [w[n].shape for n in _SMALL]
    for k, t in enumerate(small_res):
        for n, arr in zip(_SMALL, _unpack_small(t, small_shapes)):
            out.setdefault(n, [None] * 4)[k] = arr
    return (loss, dx[None], *(out[n][0] for n in names), *(out[n][1] for n in names), *(out[n][2] for n in names),
            *(out[n][3] for n in names))
```

```python
import functools
import math

import jax
import jax.numpy as jnp
from jax import lax
from jax.experimental import pallas as pl
from jax.experimental.pallas import tpu as pltpu

F32 = jnp.float32
BF16 = jnp.bfloat16

HEAD_DIM = 128
BLOCK = 128
N_GROUPS = 3
DILATIONS = (1, 4, 16)
ROPE_THETA = 10000.0
EPS = 1e-6
ADAM_LR = 0.001
ADAM_B1 = 0.9
ADAM_B2 = 0.999
ADAM_EPS = 1e-08
ADAM_WD = 0.01
ADAM_STEP = 10
N_DEV = 8
MESH = pl.DeviceIdType.MESH
VMEM_LIMIT_BYTES = 48 * 1024 * 1024
NEG = -1e30


def _params(sem):
    return pltpu.CompilerParams(dimension_semantics=sem, vmem_limit_bytes=VMEM_LIMIT_BYTES)


def _pick(n, options):
    for o in options:
        if n % o == 0:
            return o
    return n


_DIMS = {"nn": (((1,), (0,)), ((), ())), "nt": (((1,), (1,)), ((), ())), "tn": (((0,), (0,)), ((), ()))}


def _matmul(a, b, *, mode, out_dtype, name, epilogue=None, extra=None, tm=None, tn=None, tk=None):
    if mode == "nn":
        (M, K), (K2, N) = a.shape, b.shape
    elif mode == "nt":
        (M, K), (N, K2) = a.shape, b.shape
    else:
        (K, M), (K2, N) = a.shape, b.shape
    assert K == K2, (a.shape, b.shape, mode)
    tm = tm or _pick(M, (1024, 512, 256, 128))
    tn = tn or _pick(N, (512, 256, 128))
    tk = tk or _pick(K, (2048, 2944, 1024, 512, 256, 128))
    nk = K // tk
    dims = _DIMS[mode]
    n_extra = 0 if extra is None else 1
    n_out = 2 if epilogue == "relu2" else 1

    def body(*refs):
        a_ref, b_ref = refs[0], refs[1]
        extra_ref = refs[2] if n_extra else None
        outs = refs[2 + n_extra:2 + n_extra + n_out]
        acc_ref = refs[-1] if nk > 1 else None

        def finish(acc):
            if epilogue is None:
                outs[0][...] = acc.astype(outs[0].dtype)
            elif epilogue == "add":
                outs[0][...] = (acc + extra_ref[...]).astype(outs[0].dtype)
            elif epilogue == "relu2":
                r = jnp.maximum(acc, 0.0)
                outs[0][...] = r.astype(outs[0].dtype)
                outs[1][...] = (r * r).astype(outs[1].dtype)
            else:
                outs[0][...] = (acc * (2.0 * extra_ref[...].astype(F32))).astype(outs[0].dtype)

        prod = lax.dot_general(a_ref[...], b_ref[...], dims, preferred_element_type=F32)
        if nk == 1:
            finish(prod)
        else:
            k = pl.program_id(2)

            @pl.when(k == 0)
            def _():
                acc_ref[...] = prod

            @pl.when(k > 0)
            def _():
                acc_ref[...] += prod

            @pl.when(k == nk - 1)
            def _():
                finish(acc_ref[...])

    if mode == "nn":
        a_spec = pl.BlockSpec((tm, tk), lambda i, j, k: (i, k))
        b_spec = pl.BlockSpec((tk, tn), lambda i, j, k: (k, j))
    elif mode == "nt":
        a_spec = pl.BlockSpec((tm, tk), lambda i, j, k: (i, k))
        b_spec = pl.BlockSpec((tn, tk), lambda i, j, k: (j, k))
    else:
        a_spec = pl.BlockSpec((tk, tm), lambda i, j, k: (k, i))
        b_spec = pl.BlockSpec((tk, tn), lambda i, j, k: (k, j))
    o_spec = pl.BlockSpec((tm, tn), lambda i, j, k: (i, j))
    in_specs = [a_spec, b_spec] + ([o_spec] if n_extra else [])
    out_shape = [jax.ShapeDtypeStruct((M, N), out_dtype)] * n_out
    res = pl.pallas_call(
        body,
        name=name,
        grid=(M // tm, N // tn, nk),
        in_specs=in_specs,
        out_specs=[o_spec] * n_out,
        out_shape=out_shape,
        scratch_shapes=[pltpu.VMEM((tm, tn), F32)] if nk > 1 else [],
        compiler_params=_params(("parallel", "parallel", "arbitrary")),
    )(a, b, *([extra] if n_extra else []))
    return res if n_out > 1 else res[0]


def _rmsnorm_fwd(x, g, *, name):
    T, D = x.shape
    tm = _pick(T, (512, 256, 128))

    def body(x_ref, g_ref, o_ref):
        xf = x_ref[...]
        r = lax.rsqrt(jnp.mean(xf * xf, axis=-1, keepdims=True) + EPS)
        o_ref[...] = ((xf * r) * g_ref[...]).astype(o_ref.dtype)

    return pl.pallas_call(
        body,
        name=name,
        grid=(T // tm,),
        in_specs=[pl.BlockSpec((tm, D), lambda i: (i, 0)), pl.BlockSpec((1, D), lambda i: (0, 0))],
        out_specs=pl.BlockSpec((tm, D), lambda i: (i, 0)),
        out_shape=jax.ShapeDtypeStruct((T, D), BF16),
        compiler_params=_params(("parallel",)),
    )(x, g)


def _rmsnorm_bwd(x, g, dh, dres, *, name):
    T, D = x.shape
    tm = _pick(T, (256, 128))

    def body(x_ref, g_ref, dh_ref, dres_ref, dx_ref, dxb_ref, dg_ref):
        i = pl.program_id(0)
        xf = x_ref[...]
        r = lax.rsqrt(jnp.mean(xf * xf, axis=-1, keepdims=True) + EPS)
        y = xf * r
        dh_v = dh_ref[...]
        dy = dh_v * g_ref[...]
        c = jnp.mean(dy * y, axis=-1, keepdims=True)
        dx = r * (dy - y * c) + dres_ref[...]
        dx_ref[...] = dx
        dxb_ref[...] = dx.astype(BF16)
        part = jnp.sum(dh_v * y, axis=0, keepdims=True)

        @pl.when(i == 0)
        def _():
            dg_ref[...] = part

        @pl.when(i > 0)
        def _():
            dg_ref[...] += part

    row = pl.BlockSpec((tm, D), lambda i: (i, 0))
    vec = pl.BlockSpec((1, D), lambda i: (0, 0))
    return pl.pallas_call(
        body,
        name=name,
        grid=(T // tm,),
        in_specs=[row, vec, row, row],
        out_specs=[row, row, vec],
        out_shape=[jax.ShapeDtypeStruct((T, D), F32), jax.ShapeDtypeStruct((T, D), BF16), jax.ShapeDtypeStruct((1, D), F32)],
        compiler_params=_params(("arbitrary",)),
    )(x, g, dh, dres)


GROUP_W = 4 * HEAD_DIM
N_DIL_BLOCKS = 3 * N_GROUPS
N_NORMED = 2 * N_GROUPS


def _kind_of_group(j, g):
    return jnp.clip((j - g) // N_GROUPS, 0, 2)


def _head_rstd(xh):
    return lax.rsqrt(jnp.mean(xh * xh, axis=-1, keepdims=True) + EPS)


def _prep_fwd(proj, gains, cos, sin, n_sb_blocks, *, name):
    T = proj.shape[0]
    tm = _pick(T, (512, 256, 128))

    def body(p_ref, gain_ref, cos_ref, sin_ref, o0_ref, o1_ref, o2_ref, os_ref):
        j = pl.program_id(1)
        for g, o_ref in enumerate((o0_ref, o1_ref, o2_ref)):
            @pl.when(jnp.logical_and(j % N_GROUPS == g, j < N_NORMED))
            def _():
                cos_v, sin_v = cos_ref[...], sin_ref[...]
                for hh in range(4):
                    sl = slice(hh * HEAD_DIM, (hh + 1) * HEAD_DIM)
                    xh = p_ref[:, sl]
                    y = (xh * _head_rstd(xh)) * gain_ref[0, :, sl]
                    o_ref[:, sl] = (y * cos_v + pltpu.roll(y, HEAD_DIM // 2, 1) * sin_v).astype(BF16)

            @pl.when(j == N_NORMED + g)
            def _():
                o_ref[...] = p_ref[...].astype(BF16)

        @pl.when(j >= N_DIL_BLOCKS)
        def _():
            os_ref[...] = p_ref[...].astype(BF16)

    def group_spec(g):
        return pl.BlockSpec((tm, GROUP_W), lambda i, j: (i, _kind_of_group(j, g)))

    return pl.pallas_call(
        body,
        name=name,
        grid=(T // tm, N_DIL_BLOCKS + n_sb_blocks),
        in_specs=[
            pl.BlockSpec((tm, GROUP_W), lambda i, j: (i, j)),
            pl.BlockSpec((1, 1, GROUP_W), lambda i, j: (jnp.minimum(j, N_NORMED - 1), 0, 0)),
            pl.BlockSpec((tm, HEAD_DIM), lambda i, j: (i, 0)),
            pl.BlockSpec((tm, HEAD_DIM), lambda i, j: (i, 0)),
        ],
        out_specs=[group_spec(0), group_spec(1), group_spec(2),
                   pl.BlockSpec((tm, GROUP_W), lambda i, j: (i, jnp.maximum(j - N_DIL_BLOCKS, 0)))],
        out_shape=[jax.ShapeDtypeStruct((T, 3 * GROUP_W), BF16)] * N_GROUPS
        + [jax.ShapeDtypeStruct((T, n_sb_blocks * GROUP_W), BF16)],
        compiler_params=_params(("parallel", "arbitrary")),
    )(proj, gains, cos, sin)


def _prep_bwd(proj, dqkv, gains, cos, sin, *, name):
    T = proj.shape[0]
    tm = _pick(T, (512, 256, 128))

    def body(p_ref, d0_ref, d1_ref, d2_ref, gain_ref, cos_ref, sin_ref, o_ref, dgain_ref):
        j, i = pl.program_id(0), pl.program_id(1)

        def normed_bwd(d_ref):
            cos_v, sin_v = cos_ref[...], sin_ref[...]
            part = jnp.zeros((1, HEAD_DIM), F32)
            for hh in range(4):
                sl = slice(hh * HEAD_DIM, (hh + 1) * HEAD_DIM)
                xh = p_ref[:, sl]
                r = _head_rstd(xh)
                y0 = xh * r
                d_out = d_ref[:, sl]
                d_yg = d_out * cos_v + pltpu.roll(d_out * sin_v, HEAD_DIM // 2, 1)
                part = part + jnp.sum(d_yg * y0, axis=0, keepdims=True)
                dy0 = d_yg * gain_ref[0, :, sl]
                c = jnp.mean(dy0 * y0, axis=-1, keepdims=True)
                o_ref[:, sl] = (r * (dy0 - y0 * c)).astype(BF16)

            @pl.when(i == 0)
            def _():
                dgain_ref[0] = part

            @pl.when(i > 0)
            def _():
                dgain_ref[0] += part

        for g, d_ref in enumerate((d0_ref, d1_ref, d2_ref)):
            @pl.when(jnp.logical_and(j % N_GROUPS == g, j < N_NORMED))
            def _():
                normed_bwd(d_ref)

            @pl.when(j == N_NORMED + g)
            def _():
                o_ref[...] = d_ref[...].astype(BF16)

    gain_row = lambda j, i: (jnp.minimum(j, N_NORMED - 1), 0, 0)

    def grad_spec(g):
        return pl.BlockSpec((tm, GROUP_W), lambda j, i: (i, _kind_of_group(j, g)))

    return pl.pallas_call(
        body,
        name=name,
        grid=(N_DIL_BLOCKS, T // tm),
        in_specs=[
            pl.BlockSpec((tm, GROUP_W), lambda j, i: (i, j)),
            grad_spec(0), grad_spec(1), grad_spec(2),
            pl.BlockSpec((1, 1, GROUP_W), gain_row),
            pl.BlockSpec((tm, HEAD_DIM), lambda j, i: (i, 0)),
            pl.BlockSpec((tm, HEAD_DIM), lambda j, i: (i, 0)),
        ],
        out_specs=[pl.BlockSpec((tm, GROUP_W), lambda j, i: (i, j)), pl.BlockSpec((1, 1, HEAD_DIM), gain_row)],
        out_shape=[jax.ShapeDtypeStruct((T, N_DIL_BLOCKS * GROUP_W), BF16),
                   jax.ShapeDtypeStruct((2 * N_GROUPS, 1, HEAD_DIM), F32)],
        compiler_params=_params(("arbitrary", "arbitrary")),
    )(proj, *dqkv, gains, cos, sin)


def _nt(a, b):
    return lax.dot_general(a, b, _DIMS["nt"], preferred_element_type=F32)


def _tn(a, b):
    return lax.dot_general(a, b, _DIMS["tn"], preferred_element_type=F32)


def _nn(a, b):
    return jnp.dot(a, b, preferred_element_type=F32)


def _window_masks():
    row = lax.broadcasted_iota(jnp.int32, (BLOCK, BLOCK), 0)
    col = lax.broadcasted_iota(jnp.int32, (BLOCK, BLOCK), 1)
    return row >= col, col >= row


def _heads():
    return [slice(hh * HEAD_DIM, (hh + 1) * HEAD_DIM) for hh in range(GROUP_W // HEAD_DIM)]


def _dil_fwd(qkv, g, *, name):
    T = qkv.shape[0]
    r = DILATIONS[g]
    L = T // r
    nb = L // BLOCK
    scale = 1.0 / math.sqrt(HEAD_DIM)
    view = qkv.reshape(L, r * 3 * GROUP_W)

    def body(q_ref, kc_ref, kp_ref, vc_ref, vp_ref, o_ref, ld_ref):
        n = pl.program_id(1)
        m_cur, m_prev = _window_masks()
        m_prev = jnp.logical_and(m_prev, n > 0)
        for sl in _heads():
            q = q_ref[:, sl]
            s_c = jnp.where(m_cur, _nt(q, kc_ref[:, sl]) * scale, NEG)
            s_p = jnp.where(m_prev, _nt(q, kp_ref[:, sl]) * scale, NEG)
            m = jnp.maximum(jnp.max(s_c, axis=-1, keepdims=True), jnp.max(s_p, axis=-1, keepdims=True))
            p_c = jnp.exp(s_c - m)
            p_p = jnp.exp(s_p - m)
            l = jnp.sum(p_c, axis=-1, keepdims=True) + jnp.sum(p_p, axis=-1, keepdims=True)
            inv = 1.0 / l
            o_ref[:, sl] = _nn((p_c * inv).astype(BF16), vc_ref[:, sl]) + _nn((p_p * inv).astype(BF16), vp_ref[:, sl])
            ld_ref[:, sl] = jnp.broadcast_to(m + jnp.log(l), (BLOCK, HEAD_DIM))

    blk = (BLOCK, GROUP_W)
    cur = lambda kind: pl.BlockSpec(blk, lambda c, n: (n, 3 * c + kind))
    prev = lambda kind: pl.BlockSpec(blk, lambda c, n: (jnp.maximum(n - 1, 0), 3 * c + kind))
    out_spec = pl.BlockSpec(blk, lambda c, n: (n, c))
    o, ld = pl.pallas_call(
        body,
        name=name,
        grid=(r, nb),
        in_specs=[cur(0), cur(1), prev(1), cur(2), prev(2)],
        out_specs=[out_spec, out_spec],
        out_shape=[jax.ShapeDtypeStruct((L, r * GROUP_W), F32)] * 2,
        compiler_params=_params(("parallel", "arbitrary")),
    )(view, view, view, view, view)
    return o.reshape(T, GROUP_W), ld.reshape(T, GROUP_W)


def _dil_bwd(qkv, do, ld, dterm, g, *, name):
    T = qkv.shape[0]
    r = DILATIONS[g]
    L = T // r
    nb = L // BLOCK
    scale = 1.0 / math.sqrt(HEAD_DIM)
    view = qkv.reshape(L, r * 3 * GROUP_W)
    do_v, ld_v, dt_v = (t.reshape(L, r * GROUP_W) for t in (do, ld, dterm))

    def body(q_ref, qn_ref, kc_ref, kp_ref, vc_ref, vp_ref, do_ref, don_ref, ld_ref, ldn_ref, dt_ref, dtn_ref, out_ref):
        n = pl.program_id(1)
        m_cur, m_prev = _window_masks()
        has_prev, has_next = jnp.logical_and(m_prev, n > 0), jnp.logical_and(m_prev, n < nb - 1)

        def tile(q, k, v, do_t, ld_t, dt_t, mask):
            s = _nt(q, k) * scale
            p = jnp.where(mask, jnp.exp(s - ld_t[:, 0:1]), 0.0)
            ds = p * (_nt(do_t, v) + dt_t[:, 0:1]) * scale
            return p.astype(BF16), ds.astype(BF16)

        for hh, sl in enumerate(_heads()):
            kc, vc, kp = kc_ref[:, sl], vc_ref[:, sl], kp_ref[:, sl]
            q, do_t, ld_t, dt_t = q_ref[:, sl], do_ref[:, sl], ld_ref[:, sl], dt_ref[:, sl]
            p_cc, ds_cc = tile(q, kc, vc, do_t, ld_t, dt_t, m_cur)
            _, ds_cp = tile(q, kp, vp_ref[:, sl], do_t, ld_t, dt_t, has_prev)
            qn, don = qn_ref[:, sl], don_ref[:, sl]
            p_nc, ds_nc = tile(qn, kc, vc, don, ldn_ref[:, sl], dtn_ref[:, sl], has_next)
            at = lambda kind: slice(kind * GROUP_W + hh * HEAD_DIM, kind * GROUP_W + (hh + 1) * HEAD_DIM)
            out_ref[:, at(0)] = _nn(ds_cc, kc) + _nn(ds_cp, kp)
            out_ref[:, at(1)] = _tn(ds_cc, q) + _tn(ds_nc, qn)
            out_ref[:, at(2)] = _tn(p_cc, do_t) + _tn(p_nc, don)

    blk = (BLOCK, GROUP_W)
    qkv_spec = lambda kind, shift: pl.BlockSpec(blk, lambda c, n: (jnp.clip(n + shift, 0, nb - 1), 3 * c + kind))
    row_spec = lambda shift: pl.BlockSpec(blk, lambda c, n: (jnp.clip(n + shift, 0, nb - 1), c))
    out = pl.pallas_call(
        body,
        name=name,
        grid=(r, nb),
        in_specs=[qkv_spec(0, 0), qkv_spec(0, 1), qkv_spec(1, 0), qkv_spec(1, -1), qkv_spec(2, 0), qkv_spec(2, -1),
                  row_spec(0), row_spec(1), row_spec(0), row_spec(1), row_spec(0), row_spec(1)],
        out_specs=pl.BlockSpec((BLOCK, 3 * GROUP_W), lambda c, n: (n, c)),
        out_shape=jax.ShapeDtypeStruct((L, r * 3 * GROUP_W), F32),
        compiler_params=_params(("parallel", "arbitrary")),
    )(view, view, view, view, view, view, do_v, do_v, ld_v, ld_v, dt_v, dt_v)
    return out.reshape(T, 3 * GROUP_W)


def _group_weights(ld_refs):
    lds = [r[...] for r in ld_refs]
    m = jnp.maximum(jnp.maximum(lds[0], lds[1]), lds[2])
    es = [jnp.exp(v - m) for v in lds]
    inv = 1.0 / (es[0] + es[1] + es[2])
    return [e * inv for e in es]


def _merge_fwd(os_, lds, *, name):
    T = os_[0].shape[0]
    tm = _pick(T, (1024, 512, 256, 128))

    def body(o0, o1, o2, l0, l1, l2, y_ref):
        w = _group_weights((l0, l1, l2))
        y_ref[...] = (w[0] * o0[...] + w[1] * o1[...] + w[2] * o2[...]).astype(BF16)

    spec = pl.BlockSpec((tm, GROUP_W), lambda i: (i, 0))
    return pl.pallas_call(
        body, name=name, grid=(T // tm,), in_specs=[spec] * 6, out_specs=spec,
        out_shape=jax.ShapeDtypeStruct((T, GROUP_W), BF16), compiler_params=_params(("parallel",)),
    )(*os_, *lds)


def _merge_bwd(os_, lds, dy, *, name):
    T = dy.shape[0]
    tm = _pick(T, (512, 256, 128))

    def body(o0, o1, o2, l0, l1, l2, dy_ref, do0, do1, do2, dt0, dt1, dt2):
        w = _group_weights((l0, l1, l2))
        dy_v = dy_ref[...]
        y = w[0] * o0[...] + w[1] * o1[...] + w[2] * o2[...]
        prod = dy_v * y
        for hh in range(4):
            sl = slice(hh * HEAD_DIM, (hh + 1) * HEAD_DIM)
            s = jnp.sum(prod[:, sl], axis=-1, keepdims=True)
            for wg, dt in zip(w, (dt0, dt1, dt2)):
                dt[:, sl] = -wg[:, sl] * s
        for wg, do in zip(w, (do0, do1, do2)):
            do[...] = (wg * dy_v).astype(BF16)

    spec = pl.BlockSpec((tm, GROUP_W), lambda i: (i, 0))
    outs = pl.pallas_call(
        body, name=name, grid=(T // tm,), in_specs=[spec] * 7, out_specs=[spec] * 6,
        out_shape=[jax.ShapeDtypeStruct((T, GROUP_W), BF16)] * 3 + [jax.ShapeDtypeStruct((T, GROUP_W), F32)] * 3,
        compiler_params=_params(("parallel",)),
    )(*os_, *lds, dy)
    return outs[:3], outs[3:]


SB_ROWS = 512


def _sum_matrix(inclusive):
    j = lax.broadcasted_iota(jnp.int32, (2 * BLOCK, 2 * BLOCK), 0) % BLOCK
    s = lax.broadcasted_iota(jnp.int32, (2 * BLOCK, 2 * BLOCK), 1)
    later = (j >= s) if inclusive else (j > s)
    return jnp.logical_or(s >= BLOCK, later).astype(BF16)


def _block_sums(x, mat):
    hi = x.astype(BF16)
    lo = (x - hi.astype(F32)).astype(BF16)
    r = _nn(jnp.concatenate([hi, lo], axis=1), mat)
    return r[:, :BLOCK], r[:, BLOCK:]


def _log_terms(z):
    t = jnp.log(1.0 + jnp.exp(-jnp.abs(z)))
    return -(jnp.maximum(z, 0.0) + t), jnp.minimum(z, 0.0) - t


SB_DEAD = -105.0


def _sb_alive(n_chunks, state):
    return jnp.logical_and(state[0] < n_chunks, jnp.max(state[1]) > SB_DEAD)


def _causal_mask(rows):
    row = lax.broadcasted_iota(jnp.int32, (rows, rows), 0)
    col = lax.broadcasted_iota(jnp.int32, (rows, rows), 1)
    return col < row


def _sb_fwd(qkv, n_heads, col0, *, name):
    T = qkv.shape[0]
    tq = _pick(T, (SB_ROWS, BLOCK))
    nq, nsub = T // tq, tq // BLOCK
    scale = 1.0 / math.sqrt(HEAD_DIM)

    def body(q_ref, k_ref, v_ref, o_ref, ob_ref):
        i = pl.program_id(1)
        q = q_ref[...]
        mat = _sum_matrix(False)

        def chunk(j, carry, acc, mask):
            rows = pl.ds(pl.multiple_of(j * tq, tq), tq)
            z = _nt(q, k_ref[rows, :]) * scale
            lk, ls = _log_terms(z)
            if mask is not None:
                lk = jnp.where(mask, lk, 0.0)
            a = []
            for b in reversed(range(nsub)):
                sl = slice(b * BLOCK, (b + 1) * BLOCK)
                later, total = _block_sums(lk[:, sl], mat)
                a.append(jnp.exp(ls[:, sl] + (later + carry)))
                carry = carry + total
            a = jnp.concatenate(a[::-1], axis=1)
            if mask is not None:
                a = jnp.where(mask, a, 0.0)
            return carry, acc + _nn(a.astype(BF16), v_ref[rows, :])

        zero = jnp.zeros((tq, HEAD_DIM), F32)
        carry, acc = chunk(i, zero, zero, _causal_mask(tq))

        def step(state):
            carry, acc = chunk(i - 1 - state[0], state[1], state[2], None)
            return state[0] + 1, carry, acc

        _, carry, acc = lax.while_loop(functools.partial(_sb_alive, i), step, (0, carry, acc))
        o_ref[...] = acc
        ob_ref[...] = acc.astype(BF16)

    blk = (tq, HEAD_DIM)
    out_spec = pl.BlockSpec(blk, lambda h, i: (i, h))
    return pl.pallas_call(
        body,
        name=name,
        grid=(n_heads, nq),
        in_specs=[
            pl.BlockSpec(blk, lambda h, i: (i, col0 + h)),
            pl.BlockSpec((T, HEAD_DIM), lambda h, i: (0, col0 + n_heads + h)),
            pl.BlockSpec((T, HEAD_DIM), lambda h, i: (0, col0 + 2 * n_heads + h)),
        ],
        out_specs=[out_spec, out_spec],
        out_shape=[jax.ShapeDtypeStruct((T, n_heads * HEAD_DIM), F32), jax.ShapeDtypeStruct((T, n_heads * HEAD_DIM), BF16)],
        compiler_params=_params(("parallel", "arbitrary")),
    )(qkv, qkv, qkv)


def _sb_bwd(qkv, o32, do, n_heads, col0, *, name):
    T = qkv.shape[0]
    tq = _pick(T, (SB_ROWS, BLOCK))
    nq, nsub = T // tq, tq // BLOCK
    scale = 1.0 / math.sqrt(HEAD_DIM)

    def body(q_ref, k_ref, v_ref, o_ref, do_ref, dq_ref, dk_ref, dv_ref, dk_acc, dv_acc):
        i = pl.program_id(1)

        @pl.when(i == 0)
        def _():
            dk_acc[...] = jnp.zeros_like(dk_acc)
            dv_acc[...] = jnp.zeros_like(dv_acc)

        q, do_t = q_ref[...], do_ref[...]
        delta = jnp.broadcast_to(jnp.sum(do_t.astype(F32) * o_ref[...], axis=-1, keepdims=True), (tq, HEAD_DIM))
        mat, mat_incl = _sum_matrix(False), _sum_matrix(True)

        def chunk(j, carry_b, carry_g, dq, mask):
            rows = pl.ds(pl.multiple_of(j * tq, tq), tq)
            k_t, v_t = k_ref[rows, :], v_ref[rows, :]
            z = _nt(q, k_t) * scale
            lk, ls = _log_terms(z)
            if mask is not None:
                lk = jnp.where(mask, lk, 0.0)
            d_a = _nt(do_t, v_t)
            a_parts, dz_parts = [], []
            for b in reversed(range(nsub)):
                sl = slice(b * BLOCK, (b + 1) * BLOCK)
                later, total = _block_sums(lk[:, sl], mat)
                a = jnp.exp(ls[:, sl] + (later + carry_b))
                carry_b = carry_b + total
                if mask is not None:
                    a = jnp.where(mask[:, sl], a, 0.0)
                a_b = a.astype(BF16)
                g = a_b.astype(F32) * d_a[:, sl]
                from_here, total_g = _block_sums(g, mat_incl)
                before = delta - (from_here + carry_g)
                carry_g = carry_g + total_g
                sig = jnp.exp(ls[:, sl])
                dz = (g - sig * (g + before)) * scale
                a_parts.append(a_b)
                dz_parts.append(dz)
            dz = jnp.concatenate(dz_parts[::-1], axis=1)
            if mask is not None:
                dz = jnp.where(mask, dz, 0.0)
            dz_b = dz.astype(BF16)
            dk_acc[rows, :] += _tn(dz_b, q)
            dv_acc[rows, :] += _tn(jnp.concatenate(a_parts[::-1], axis=1), do_t)
            return carry_b, carry_g, dq + _nn(dz_b, k_t)

        zero = jnp.zeros((tq, HEAD_DIM), F32)
        state = chunk(i, zero, zero, zero, _causal_mask(tq))

        def step(st):
            return (st[0] + 1,) + chunk(i - 1 - st[0], st[1], st[2], st[3], None)

        state = lax.while_loop(functools.partial(_sb_alive, i), step, (0,) + state)
        dq_ref[...] = state[3].astype(BF16)

        @pl.when(i == nq - 1)
        def _():
            dk_ref[...] = dk_acc[...].astype(BF16)
            dv_ref[...] = dv_acc[...].astype(BF16)

    blk = (tq, HEAD_DIM)
    full = (T, HEAD_DIM)
    dshape = jax.ShapeDtypeStruct((T, n_heads * HEAD_DIM), BF16)
    return pl.pallas_call(
        body,
        name=name,
        grid=(n_heads, nq),
        in_specs=[
            pl.BlockSpec(blk, lambda h, i: (i, col0 + h)),
            pl.BlockSpec(full, lambda h, i: (0, col0 + n_heads + h)),
            pl.BlockSpec(full, lambda h, i: (0, col0 + 2 * n_heads + h)),
            pl.BlockSpec(blk, lambda h, i: (i, h)),
            pl.BlockSpec(blk, lambda h, i: (i, h)),
        ],
        out_specs=[pl.BlockSpec(blk, lambda h, i: (i, h)), pl.BlockSpec(full, lambda h, i: (0, h)),
                   pl.BlockSpec(full, lambda h, i: (0, h))],
        out_shape=[dshape, dshape, dshape],
        scratch_shapes=[pltpu.VMEM(full, F32), pltpu.VMEM(full, F32)],
        compiler_params=_params(("arbitrary", "arbitrary")),
    )(qkv, qkv, qkv, o32, do)


def _gate_fwd(y_dil, y_sb, w_up_dil, w_up_sb, proj, gate_b, gate_col0, *, name):
    T, D = y_dil.shape[0], w_up_dil.shape[0]
    tm = _pick(T, (512, 256, 128))
    tn = _pick(D, (512, 256, 128))
    c0, nbr = gate_col0 // tn, D // tn

    def body(yd_ref, ys_ref, wd_ref, ws_ref, gp0_ref, gp1_ref, b_ref, o_ref):
        g0 = jax.nn.sigmoid(gp0_ref[...] + b_ref[0:1, :])
        g1 = jax.nn.sigmoid(gp1_ref[...] + b_ref[1:2, :])
        o_ref[...] = (g0 * _nt(yd_ref[...], wd_ref[...]) + g1 * _nt(ys_ref[...], ws_ref[...])).astype(BF16)

    return pl.pallas_call(
        body,
        name=name,
        grid=(T // tm, nbr),
        in_specs=[
            pl.BlockSpec((tm, y_dil.shape[1]), lambda i, j: (i, 0)),
            pl.BlockSpec((tm, y_sb.shape[1]), lambda i, j: (i, 0)),
            pl.BlockSpec((tn, w_up_dil.shape[1]), lambda i, j: (j, 0)),
            pl.BlockSpec((tn, w_up_sb.shape[1]), lambda i, j: (j, 0)),
            pl.BlockSpec((tm, tn), lambda i, j: (i, c0 + j)),
            pl.BlockSpec((tm, tn), lambda i, j: (i, c0 + nbr + j)),
            pl.BlockSpec((2, tn), lambda i, j: (0, j)),
        ],
        out_specs=pl.BlockSpec((tm, tn), lambda i, j: (i, j)),
        out_shape=jax.ShapeDtypeStruct((T, D), BF16),
        compiler_params=_params(("parallel", "parallel")),
    )(y_dil, y_sb, w_up_dil, w_up_sb, proj, proj, gate_b)


def _gate_bwd(y, w_up, proj, gate_b, dmixed, branch, gate_col0, *, name):
    T, D = y.shape[0], w_up.shape[0]
    tm = _pick(T, (512, 256, 128))
    tn = _pick(D, (512, 256, 128))
    c0 = gate_col0 // tn + branch * (D // tn)

    def body(y_ref, w_ref, gp_ref, b_ref, dm_ref, dup_ref, dgp_ref, db_ref):
        i = pl.program_id(1)
        g = jax.nn.sigmoid(gp_ref[...] + b_ref[branch:branch + 1, :])
        dm = dm_ref[...]
        dup_ref[...] = (dm * g).astype(BF16)
        dgp = (dm * _nt(y_ref[...], w_ref[...])) * (g * (1.0 - g))
        dgp_ref[...] = dgp.astype(BF16)
        part = jnp.sum(dgp, axis=0, keepdims=True)

        @pl.when(i == 0)
        def _():
            db_ref[...] = part

        @pl.when(i > 0)
        def _():
            db_ref[...] += part

    tile = pl.BlockSpec((tm, tn), lambda j, i: (i, j))
    return pl.pallas_call(
        body,
        name=name,
        grid=(D // tn, T // tm),
        in_specs=[
            pl.BlockSpec((tm, y.shape[1]), lambda j, i: (i, 0)),
            pl.BlockSpec((tn, w_up.shape[1]), lambda j, i: (j, 0)),
            pl.BlockSpec((tm, tn), lambda j, i: (i, c0 + j)),
            pl.BlockSpec((2, tn), lambda j, i: (0, j)),
            tile,
        ],
        out_specs=[tile, tile, pl.BlockSpec((1, tn), lambda j, i: (0, j))],
        out_shape=[jax.ShapeDtypeStruct((T, D), BF16), jax.ShapeDtypeStruct((T, D), BF16), jax.ShapeDtypeStruct((1, D), F32)],
        compiler_params=_params(("parallel", "arbitrary")),
    )(y, w_up, proj, gate_b, dmixed)


def _loss_head(y, target, *, name):
    T, D = y.shape
    tm = _pick(T, (256, 128))

    def body(y_ref, t_ref, dy_ref, dyb_ref, l_ref):
        i = pl.program_id(0)
        err = y_ref[...] - t_ref[...]
        dy = err * (1.0 / D)
        dy_ref[...] = dy
        dyb_ref[...] = dy.astype(BF16)
        part = 0.5 * jnp.sum(jnp.mean(err * err, axis=-1, keepdims=True), axis=0, keepdims=True)

        @pl.when(i == 0)
        def _():
            l_ref[...] = part

        @pl.when(i > 0)
        def _():
            l_ref[...] += part

    row = pl.BlockSpec((tm, D), lambda i: (i, 0))
    return pl.pallas_call(
        body, name=name, grid=(T // tm,), in_specs=[row, row],
        out_specs=[row, row, pl.BlockSpec((1, 1), lambda i: (0, 0))],
        out_shape=[jax.ShapeDtypeStruct((T, D), F32), jax.ShapeDtypeStruct((T, D), BF16), jax.ShapeDtypeStruct((1, 1), F32)],
        compiler_params=_params(("arbitrary",)),
    )(y, target)


def _sum_parts(p_ref, own):
    slot = _slot(_place())
    g = jnp.where(slot == 0, own, p_ref[0].astype(F32))
    for s in range(1, N_DEV):
        g = g + jnp.where(slot == s, own, p_ref[s].astype(F32))
    return g


def _held(step, l, last):
    layer, i = step
    return jnp.where(layer == l, i, last * (layer > l))


def _reduce_adamw(parts, own, w, m, v, *, own_chunked=True, name):
    n_layers = len(parts)
    C = w.shape[1]
    R = w.shape[0] // n_layers
    tr = R
    for cand in (1024, 512, 256, 128, 64, 32, 16, 8):
        if R % cand == 0 and cand * C <= 256 * 1024:
            tr = cand
            break
    nr = R // tr

    def body(*refs):
        p_refs, o_refs = refs[:n_layers], refs[n_layers:2 * n_layers]
        w_ref, m_ref, v_ref, g_ref, d_ref, nm_ref, nv_ref = refs[2 * n_layers:]
        layer = pl.program_id(0)
        for l, (p_ref, o_ref) in enumerate(zip(p_refs, o_refs)):
            @pl.when(layer == l)
            def _():
                g_ref[...] = _sum_parts(p_ref, (o_ref[0] if own_chunked else o_ref[...]).astype(F32))

        _adamw_update(g_ref[...], w_ref, m_ref, v_ref, d_ref, nm_ref, nv_ref)

    def part_spec(l):
        return pl.BlockSpec((N_DEV, tr, C), lambda *step: (0, _held(step, l, nr - 1), 0))

    def own_spec(l):
        if own_chunked:
            return pl.BlockSpec((1, tr, C), lambda *step: (_slot(_place()), _held(step, l, nr - 1), 0))
        return pl.BlockSpec((tr, C), lambda *step: (_held(step, l, nr - 1), 0))

    row = pl.BlockSpec((tr, C), lambda layer, i: (layer * nr + i, 0))
    return pl.pallas_call(
        body, name=name, grid=(n_layers, nr),
        in_specs=[part_spec(l) for l in range(n_layers)] + [own_spec(l) for l in range(n_layers)] + [row, row, row],
        out_specs=[row] * 4, out_shape=[jax.ShapeDtypeStruct(w.shape, F32)] * 4,
        compiler_params=_params(("arbitrary", "arbitrary")),
    )(*parts, *own, w, m, v)


def _adamw_update(g, w_ref, m_ref, v_ref, d_ref, nm_ref, nv_ref):
    m_new = ADAM_B1 * m_ref[...] + (1.0 - ADAM_B1) * g
    v_new = ADAM_B2 * v_ref[...] + (1.0 - ADAM_B2) * (g * g)
    m_hat = m_new / (1.0 - ADAM_B1 ** ADAM_STEP)
    v_hat = v_new / (1.0 - ADAM_B2 ** ADAM_STEP)
    d_ref[...] = -ADAM_LR * (m_hat / (jnp.sqrt(v_hat) + ADAM_EPS) + ADAM_WD * w_ref[...])
    nm_ref[...] = m_new
    nv_ref[...] = v_new


def _reduce_adamw_t(parts, own, w, m, v, *, name):
    n_layers = len(parts)
    _, n, K = parts[0].shape
    n_pad = w.shape[1]
    tm = _pick(K, (128,))
    nr = K // tm

    def body(*refs):
        p_refs, o_refs = refs[:n_layers], refs[n_layers:2 * n_layers]
        w_ref, m_ref, v_ref, g_ref, d_ref, nm_ref, nv_ref = refs[2 * n_layers:]
        layer = pl.program_id(0)
        for l, (p_ref, o_ref) in enumerate(zip(p_refs, o_refs)):
            @pl.when(layer == l)
            def _():
                g_t = _sum_parts(p_ref, o_ref[0].astype(F32))
                if n_pad > n:
                    g_t = jnp.concatenate([g_t, jnp.zeros((n_pad - n, tm), F32)], axis=0)
                g_ref[...] = g_t.T

        _adamw_update(g_ref[...], w_ref, m_ref, v_ref, d_ref, nm_ref, nv_ref)

    def part_spec(l):
        return pl.BlockSpec((N_DEV, n, tm), lambda *step: (0, 0, _held(step, l, nr - 1)))

    def own_spec(l):
        return pl.BlockSpec((1, n, tm), lambda *step: (_slot(_place()), 0, _held(step, l, nr - 1)))

    row = pl.BlockSpec((tm, n_pad), lambda layer, i: (layer * nr + i, 0))
    return pl.pallas_call(
        body, name=name, grid=(n_layers, nr),
        in_specs=[part_spec(l) for l in range(n_layers)] + [own_spec(l) for l in range(n_layers)] + [row, row, row],
        out_specs=[row] * 4, out_shape=[jax.ShapeDtypeStruct(w.shape, F32)] * 4,
        compiler_params=_params(("arbitrary", "arbitrary")),
    )(*parts, *own, w, m, v)


def _transpose_cast(x, *, name):
    R, C = x.shape
    tr, tc = _pick(R, (512, 256, 128)), _pick(C, (512, 256, 128))

    def body(x_ref, o_ref):
        o_ref[...] = x_ref[...].astype(F32).T.astype(BF16)

    return pl.pallas_call(
        body, name=name, grid=(R // tr, C // tc),
        in_specs=[pl.BlockSpec((tr, tc), lambda i, j: (i, j))],
        out_specs=pl.BlockSpec((tc, tr), lambda i, j: (j, i)),
        out_shape=jax.ShapeDtypeStruct((C, R), BF16),
        compiler_params=_params(("parallel", "parallel")),
    )(x)


_ANY = pl.BlockSpec(memory_space=pl.ANY)


def _place():
    return lax.axis_index("x"), lax.axis_index("y"), lax.axis_index("c")


def _slot(p):
    return 4 * p[0] + 2 * p[1] + p[2]


_HBM = pl.BlockSpec(memory_space=pltpu.HBM)
_SEM = pl.BlockSpec(memory_space=pltpu.SEMAPHORE)
_EFFECT = pltpu.SideEffectType.DATAFLOW_SIDE_EFFECTING
N_PEERS = N_DEV - 1


def _peers(me):
    flips = [(fx, fy, fc) for fx in (0, 1) for fy in (0, 1) for fc in (0, 1)][1:]
    return [tuple(1 - v if f else v for v, f in zip(me, flip)) for flip in flips]


def _peer_copy(src, lands, t, k, sender, to, send_sems, recv_sems):
    return pltpu.make_async_remote_copy(
        src_ref=src, dst_ref=lands[t].at[_slot(sender)], send_sem=send_sems.at[N_PEERS * t + k],
        recv_sem=recv_sems.at[N_PEERS * t + k], device_id=to, device_id_type=MESH)


class _Exchange:
    def __init__(self, chunked, whole, name):
        self.arrays = [pltpu.with_memory_space_constraint(a, pltpu.HBM) for a in list(chunked) + list(whole)]
        self.n, self.n_chunked, self.name = len(self.arrays), len(chunked), name

    def _src(self, ins, t, dest):
        return ins[t].at[_slot(dest)] if t < self.n_chunked else ins[t]

    def _land_shape(self, t):
        a = self.arrays[t]
        return a.shape if t < self.n_chunked else (N_DEV,) + a.shape

    def start(self, after=None):
        n = self.n

        def body(*refs):
            ins, lands = refs[:n], refs[n:2 * n]
            send_sems, recv_sems = refs[-2 * n - 3], refs[-2 * n - 2]
            token = refs[-1]
            me = _place()
            for t in range(n):
                for k, peer in enumerate(_peers(me)):
                    _peer_copy(self._src(ins, t, peer), lands, t, k, me, peer, send_sems, recv_sems).start()
            token[...] = jnp.zeros_like(token)

        lands = [pltpu.with_memory_space_constraint(lax.empty(self._land_shape(t), a.dtype), pltpu.HBM)
                 for t, a in enumerate(self.arrays)]
        sems = pltpu.SemaphoreType.DMA((N_PEERS * n,))
        outs = pl.pallas_call(
            body,
            name=self.name + "_start",
            in_specs=[_HBM] * (2 * n) + ([_ANY] if after is not None else []),
            out_specs=[_SEM, _SEM] + [_HBM] * (2 * n) + [pl.BlockSpec(memory_space=pltpu.VMEM)],
            out_shape=[sems, sems] + [pltpu.HBM(a.shape, a.dtype) for a in self.arrays + lands]
            + [jax.ShapeDtypeStruct((8, 128), F32)],
            input_output_aliases={t: 2 + t for t in range(2 * n)},
            compiler_params=pltpu.CompilerParams(has_side_effects=_EFFECT),
        )(*self.arrays, *lands, *([after] if after is not None else []))
        self.sems, self.thru, self.lands, self.token = outs[:2], outs[2:2 + n], outs[2 + n:2 + 2 * n], outs[-1]
        return self.token

    def finish(self, after):
        n = self.n

        def wait_body(*refs):
            ins, lands, (send_sems, recv_sems) = refs[:n], refs[n:2 * n], refs[2 * n:2 * n + 2]
            me = _place()
            for t in range(n):
                for k, peer in enumerate(_peers(me)):
                    cp = _peer_copy(self._src(ins, t, peer), lands, t, k, peer, peer, send_sems, recv_sems)
                    cp.wait_send()
                    cp.wait_recv()

        outs = pl.pallas_call(
            wait_body,
            name=self.name + "_wait",
            in_specs=[_HBM] * (2 * n) + [_SEM, _SEM, _ANY],
            out_specs=[_HBM] * (2 * n),
            out_shape=[pltpu.HBM(a.shape, a.dtype) for a in self.thru + self.lands],
            input_output_aliases={t: t for t in range(2 * n)},
            compiler_params=pltpu.CompilerParams(has_side_effects=_EFFECT),
        )(*self.thru, *self.lands, *self.sems, after)
        return outs[:n], outs[n:]


def _exchange(chunked, whole, *, name):
    arrays = list(chunked) + list(whole)
    n, n_chunked = len(arrays), len(chunked)
    two_level = n_chunked == 0

    def body(*refs):
        ins, lands = refs[:n], refs[n:2 * n]
        send_sems, recv_sems, local_sems = refs[2 * n:]
        x, y, c = me = _place()

        def src(t, dest):
            return ins[t].at[_slot(dest)] if t < n_chunked else ins[t]

        mine = [pltpu.make_async_copy(src(t, me), lands[t].at[_slot(me)], local_sems.at[t]) for t in range(n)]
        for cp in mine:
            cp.start()
        if two_level:
            sibling = (x, y, 1 - c)
            chips = [(1 - x, y), (x, 1 - y), (1 - x, 1 - y)]

            def copy(t, k, block, to, source=None):
                dst = lands[t].at[_slot(block)]
                return pltpu.make_async_remote_copy(
                    src_ref=dst if source is None else source, dst_ref=dst, send_sem=send_sems.at[N_PEERS * t + k],
                    recv_sem=recv_sems.at[N_PEERS * t + k], device_id=to, device_id_type=MESH)

            first = []
            for t in range(n):
                first.append(copy(t, 0, me, sibling, ins[t]))
                first += [copy(t, 1 + j, me, (*chip, c), ins[t]) for j, chip in enumerate(chips)]
            for cp in first:
                cp.start()
            passed = []
            for j, chip in enumerate(chips):
                for t in range(n):
                    copy(t, 1 + j, (*chip, c), me).wait_recv()
                    cp = copy(t, 4 + j, (*chip, c), sibling)
                    cp.start()
                    passed.append(cp)
            for t in range(n):
                copy(t, 0, sibling, me).wait_recv()
                for j, chip in enumerate(chips):
                    copy(t, 4 + j, (*chip, 1 - c), me).wait_recv()
            sends = first + passed
        else:
            peers = _peers(me)
            sends = [_peer_copy(src(t, peer), lands, t, k, me, peer, send_sems, recv_sems)
                     for t in range(n) for k, peer in enumerate(peers)]
            for cp in sends:
                cp.start()
            for t in range(n):
                for k, peer in enumerate(peers):
                    _peer_copy(src(t, peer), lands, t, k, peer, peer, send_sems, recv_sems).wait_recv()
        for cp in sends:
            cp.wait_send()
        for cp in mine:
            cp.wait()

    def land_shape(t, a):
        return jax.ShapeDtypeStruct(a.shape if t < n_chunked else (N_DEV,) + a.shape, a.dtype)

    sems = pltpu.SemaphoreType.DMA((N_PEERS * n,))
    return pl.pallas_call(
        body,
        name=name,
        in_specs=[_ANY] * n,
        out_specs=[_ANY] * n,
        out_shape=[land_shape(t, a) for t, a in enumerate(arrays)],
        scratch_shapes=[sems, sems, pltpu.SemaphoreType.DMA((n,))],
        compiler_params=pltpu.CompilerParams(has_side_effects=True),
    )(*arrays)


def _place_own(land, shard, *, name):
    n, K = shard.shape
    tr = n
    for cand in (1024, 736, 512, 256, 128):
        if n % cand == 0:
            tr = cand
            break

    def body(s_ref, land_ref, o_ref):
        del land_ref
        o_ref[0] = s_ref[...]

    return pl.pallas_call(
        body, name=name, grid=(n // tr,),
        in_specs=[pl.BlockSpec((tr, K), lambda i: (i, 0)), _ANY],
        out_specs=pl.BlockSpec((1, tr, K), lambda i: (_slot(_place()), i, 0)),
        out_shape=jax.ShapeDtypeStruct(land.shape, land.dtype),
        input_output_aliases={1: 0},
        compiler_params=_params(("arbitrary",)),
    )(shard, land)


def _rope_tables(T):
    half = HEAD_DIM // 2
    inv_freq = ROPE_THETA ** (-jnp.arange(half, dtype=F32) / half)
    ang = jnp.arange(T, dtype=F32)[:, None] * inv_freq[None, :]
    cos, sin = jnp.cos(ang), jnp.sin(ang)
    return jnp.concatenate([cos, cos], axis=-1), jnp.concatenate([-sin, sin], axis=-1)


def _gain_table(q_gain, k_gain):
    return jnp.tile(jnp.concatenate([q_gain, k_gain], axis=0), (1, 4))[:, None, :]


def _sb_heads(w_in):
    n_in, d_model = w_in.shape
    return (n_in - N_DIL_BLOCKS * GROUP_W - 2 * d_model) // (3 * HEAD_DIM)


def _layer_fwd(x, p, cos, sin, tag, early=None, late=None):
    if early is not None:
        p.update(early(x))
    sb_heads = _sb_heads(p["w_in"])
    n_sb_blocks = 3 * sb_heads * HEAD_DIM // GROUP_W
    s = {"x": x}
    s["h"] = _rmsnorm_fwd(x, p["norm1"], name=f"norm1_fwd{tag}")
    s["proj"] = _matmul(s["h"], p["w_in"], mode="nt", out_dtype=F32, name=f"proj_fwd{tag}")
    *s["qkv_d"], s["qkv_s"] = _prep_fwd(s["proj"], p["gains"], cos, sin, n_sb_blocks, name=f"prep_fwd{tag}")
    outs = [_dil_fwd(s["qkv_d"][g], g, name=f"dil{g}_fwd{tag}") for g in range(N_GROUPS)]
    s["o"], s["ld"] = [o for o, _ in outs], [ld for _, ld in outs]
    s["y_dil"] = _merge_fwd(s["o"], s["ld"], name=f"merge_fwd{tag}")
    s["y_sb32"], s["y_sb"] = _sb_fwd(s["qkv_s"], sb_heads, 0, name=f"sb_fwd{tag}")
    if late is not None:
        p.update(late(s["y_sb"]))
    s["mixed"] = _gate_fwd(s["y_dil"], s["y_sb"], p["w_up_dil"], p["w_up_sb"], s["proj"], p["gate_b"],
                           (N_DIL_BLOCKS + n_sb_blocks) * GROUP_W, name=f"gate_fwd{tag}")
    s["x1"] = _matmul(s["mixed"], p["w_out"], mode="nn", out_dtype=F32, epilogue="add", extra=x, name=f"out_fwd{tag}")
    s["h2"] = _rmsnorm_fwd(s["x1"], p["norm2"], name=f"norm2_fwd{tag}")
    s["f"], s["a"] = _matmul(s["h2"], p["w_ff1"], mode="nt", out_dtype=BF16, epilogue="relu2", name=f"ff1_fwd{tag}")
    x2 = _matmul(s["a"], p["w_ff2"], mode="nn", out_dtype=F32, epilogue="add", extra=s["x1"], name=f"ff2_fwd{tag}")
    return x2, s


def _layer_bwd(dx2, dx2_b, p, s, cos, sin, tag, mid_hook=None, end_hook=None):
    sb_heads = _sb_heads(p["w_in"])
    gate_col0 = N_DIL_BLOCKS * GROUP_W + 3 * sb_heads * HEAD_DIM
    g = {}
    df = _matmul(dx2_b, p["w_ff2"], mode="nt", out_dtype=BF16, epilogue="relu2_bwd", extra=s["f"], name=f"ff2_bwd{tag}")
    g["w_ff2"] = _matmul(s["a"], dx2_b, mode="tn", out_dtype=BF16, name=f"ff2_wgrad{tag}")
    dh2 = _matmul(df, p["w_ff1"], mode="nn", out_dtype=F32, name=f"ff1_bwd{tag}")
    g["w_ff1"] = _matmul(df, s["h2"], mode="tn", out_dtype=BF16, name=f"ff1_wgrad{tag}")
    dx1, dx1_b, g["norm2"] = _rmsnorm_bwd(s["x1"], p["norm2"], dh2, dx2, name=f"norm2_bwd{tag}")
    dmixed = _matmul(dx1_b, p["w_out"], mode="nt", out_dtype=F32, name=f"out_bwd{tag}")
    g["w_out"] = _matmul(s["mixed"], dx1_b, mode="tn", out_dtype=BF16, name=f"out_wgrad{tag}")
    gate_b = p["gate_b"] if mid_hook is None else p["gate_b"] + mid_hook(g)[0, 0]
    dup_dil, dgp0, db0 = _gate_bwd(s["y_dil"], p["w_up_dil"], s["proj"], gate_b, dmixed, 0, gate_col0, name=f"gate0_bwd{tag}")
    dup_sb, dgp1, db1 = _gate_bwd(s["y_sb"], p["w_up_sb"], s["proj"], gate_b, dmixed, 1, gate_col0, name=f"gate1_bwd{tag}")
    g["gate_b"] = jnp.concatenate([db0, db1], axis=0)
    dy_dil = _matmul(dup_dil, p["w_up_dil"], mode="nn", out_dtype=F32, name=f"updil_bwd{tag}")
    g["w_up_dil"] = _matmul(dup_dil, s["y_dil"], mode="tn", out_dtype=BF16, name=f"updil_wgrad{tag}")
    dy_sb = _matmul(dup_sb, p["w_up_sb"], mode="nn", out_dtype=BF16, name=f"upsb_bwd{tag}")
    g["w_up_sb"] = _matmul(dup_sb, s["y_sb"], mode="tn", out_dtype=BF16, name=f"upsb_wgrad{tag}")
    dos, dterms = _merge_bwd(s["o"], s["ld"], dy_dil, name=f"merge_bwd{tag}")
    dqkv = [_dil_bwd(s["qkv_d"][grp], dos[grp], s["ld"][grp], dterms[grp], grp, name=f"dil{grp}_bwd{tag}")
            for grp in range(N_GROUPS)]
    dproj_d, dgain = _prep_bwd(s["proj"], dqkv, p["gains"], cos, sin, name=f"prep_bwd{tag}")
    g["q_gain"], g["k_gain"] = dgain[:N_GROUPS, 0], dgain[N_GROUPS:, 0]
    dq_s, dk_s, dv_s = _sb_bwd(s["qkv_s"], s["y_sb32"], dy_sb, sb_heads, 0, name=f"sb_bwd{tag}")
    dproj = jnp.concatenate([dproj_d, dq_s, dk_s, dv_s, dgp0, dgp1], axis=1)
    dh = _matmul(dproj, p["w_in"], mode="nn", out_dtype=F32, name=f"proj_bwd{tag}")
    g["w_in"] = _matmul(dproj, s["h"], mode="tn", out_dtype=BF16, name=f"proj_wgrad{tag}",
                        tn=_pick(s["h"].shape[1], (2048, 1024, 512, 256, 128)), tk=_pick(s["h"].shape[0], (1024, 512)))
    norm1 = p["norm1"] if end_hook is None else p["norm1"] + end_hook(g)[0, 0]
    dx, dx_b, g["norm1"] = _rmsnorm_bwd(s["x"], norm1, dh, dx1, name=f"norm1_bwd{tag}")
    return dx, dx_b, g


def _local_step(x, target, layers, early=None, late=None, mid_hooks=None, end_hooks=None):
    depth = len(layers)
    early, late, mid_hooks, end_hooks = (h or [None] * depth for h in (early, late, mid_hooks, end_hooks))
    cos, sin = _rope_tables(x.shape[0])
    saved = []
    for l, p in enumerate(layers):
        x, s = _layer_fwd(x, p, cos, sin, f"_l{l}", early[l], late[l])
        saved.append(s)
    dx, dx_b, loss = _loss_head(x, target, name="loss_head")
    grads = [None] * depth
    for l in reversed(range(depth)):
        dx, dx_b, grads[l] = _layer_bwd(dx, dx_b, layers[l], saved[l], cos, sin, f"_l{l}", mid_hooks[l], end_hooks[l])
    return loss, dx, grads


_MATRICES = ("w_in", "w_up_dil", "w_up_sb", "w_out", "w_ff1", "w_ff2")
_TRANSPOSED = ("w_in", "w_up_dil", "w_up_sb", "w_ff1")
_SMALL = ("norm1_g", "norm2_g", "q_norm_g", "k_norm_g")


def _unshard(blocks, name):
    if name == "gate_b":
        return jnp.transpose(blocks, (1, 0, 2)).reshape(blocks.shape[1], N_DEV * blocks.shape[2])
    return blocks.reshape(N_DEV * blocks.shape[1], blocks.shape[2])


def _to_chunks(full, name):
    if name == "gate_b":
        r, cols = full.shape
        return jnp.transpose(full.reshape(r, N_DEV, cols // N_DEV), (1, 0, 2))
    return full.reshape(N_DEV, full.shape[0] // N_DEV, full.shape[1])


def _lane_pad(n):
    return -n % HEAD_DIM


def _pack_small(norm1, norm2, qg, kg):
    flat = jnp.concatenate([t.reshape(-1, HEAD_DIM) for t in (norm1, norm2, qg, kg)], axis=0)
    return jnp.pad(flat, ((0, -flat.shape[0] % 8), (0, 0)))


def _unpack_small(packed, shapes):
    out, row = [], 0
    for shape in shapes:
        rows = math.prod(shape) // HEAD_DIM
        out.append(packed[row:row + rows].reshape(shape))
        row += rows
    return out


def kernel(x, norm1_g, w_in, q_norm_g, k_norm_g, w_up_dil, w_up_sb, gate_b, w_out, norm2_g, w_ff1, w_ff2, loss_target, m_norm1_g, m_w_in, m_q_norm_g, m_k_norm_g, m_w_up_dil, m_w_up_sb, m_gate_b, m_w_out, m_norm2_g, m_w_ff1, m_w_ff2, v_norm1_g, v_w_in, v_q_norm_g, v_k_norm_g, v_w_up_dil, v_w_up_sb, v_gate_b, v_w_out, v_norm2_g, v_w_ff1, v_w_ff2):
    names = ("norm1_g", "w_in", "q_norm_g", "k_norm_g", "w_up_dil", "w_up_sb", "gate_b", "w_out", "norm2_g", "w_ff1", "w_ff2")
    w = dict(zip(names, (norm1_g, w_in, q_norm_g, k_norm_g, w_up_dil, w_up_sb, gate_b, w_out, norm2_g, w_ff1, w_ff2)))
    m = dict(zip(names, (m_norm1_g, m_w_in, m_q_norm_g, m_k_norm_g, m_w_up_dil, m_w_up_sb, m_gate_b, m_w_out, m_norm2_g, m_w_ff1, m_w_ff2)))
    v = dict(zip(names, (v_norm1_g, v_w_in, v_q_norm_g, v_k_norm_g, v_w_up_dil, v_w_up_sb, v_gate_b, v_w_out, v_norm2_g, v_w_ff1, v_w_ff2)))
    depth = norm1_g.shape[0]
    assert depth == 2, "the exchange schedule below is written for two layers"
    sharded = _MATRICES + ("gate_b",)

    def shards(layer, which):
        out = []
        for n in which:
            shard = w[n][layer]
            if n in _TRANSPOSED:
                cols = shard.shape[1]
                padded = jnp.pad(shard, ((0, 0), (0, _lane_pad(cols))))
                out.append(_transpose_cast(padded, name=f"shard_t_{n}_l{layer}")[:cols])
            else:
                out.append(shard if n == "gate_b" else shard.astype(BF16))
        return out

    layers = []
    for l in range(depth):
        p = {"norm1": norm1_g[l][None], "norm2": norm2_g[l][None], "gains": _gain_table(q_norm_g[l], k_norm_g[l])}
        gathered = _exchange([], shards(l, sharded), name=f"gather_l{l}")
        p.update({n: _unshard(blocks, n) for n, blocks in zip(sharded, gathered)})
        layers.append(p)

    loss_part, dx, grads = _local_step(x[0], loss_target[0], layers)
    loss = lax.psum(loss_part[0, 0], ("x", "y", "c"))

    small = _pack_small(jnp.concatenate([g["norm1"] for g in grads]), jnp.concatenate([g["norm2"] for g in grads]),
                        jnp.stack([g["q_gain"] for g in grads]), jnp.stack([g["k_gain"] for g in grads]))
    sent = [[_to_chunks(g[n], n) for n in sharded] for g in grads]
    landed = [_exchange(sent[1], [], name="exchange_l1")]
    *landed_0, small_parts = _exchange(sent[0], [small], name="exchange_l0")
    landed.insert(0, landed_0)

    out = {}
    for k, n in enumerate(sharded):
        rows, cols = depth * w[n].shape[1], w[n].shape[2]
        own, parts = [s[k] for s in sent], [t[k] for t in landed]
        state = [t.reshape(rows, cols) for t in (w[n], m[n], v[n])]
        if n in _TRANSPOSED:
            pad = _lane_pad(cols)
            res = _reduce_adamw_t(parts, own, *(jnp.pad(t, ((0, 0), (0, pad))) for t in state), name=f"adamw_{n}")
            res = [t[:, :cols] for t in res]
        else:
            if w[n].shape[1] % 8:
                own, parts = [jnp.concatenate(own, axis=1)], [jnp.concatenate(parts, axis=1)]
            res = _reduce_adamw(parts, own, *state, name=f"adamw_{n}")
        out[n] = [t.reshape(w[n].shape) for t in res]
    small_res = _reduce_adamw([small_parts], [small], _pack_small(*(w[n] for n in _SMALL)),
                              _pack_small(*(m[n] for n in _SMALL)), _pack_small(*(v[n] for n in _SMALL)),
                              own_chunked=False, name="adamw_small")
    small_shapes = [w[n].shape for n in _SMALL]
    for k, t in enumerate(small_res):
        for n, arr in zip(_SMALL, _unpack_small(t, small_shapes)):
            out.setdefault(n, [None] * 4)[k] = arr
    return (loss, dx[None], *(out[n][0] for n in names), *(out[n][1] for n in names), *(out[n][2] for n in names),
            *(out[n][3] for n in names))
```

```python
import functools
import math

import jax
import jax.numpy as jnp
from jax import lax
from jax.experimental import pallas as pl
from jax.experimental.pallas import tpu as pltpu

F32 = jnp.float32
BF16 = jnp.bfloat16

HEAD_DIM = 128
BLOCK = 128
N_GROUPS = 3
DILATIONS = (1, 4, 16)
ROPE_THETA = 10000.0
EPS = 1e-6
ADAM_LR = 0.001
ADAM_B1 = 0.9
ADAM_B2 = 0.999
ADAM_EPS = 1e-08
ADAM_WD = 0.01
ADAM_STEP = 10
N_DEV = 8
MESH = pl.DeviceIdType.MESH
VMEM_LIMIT_BYTES = 48 * 1024 * 1024
NEG = -1e30


def _params(sem):
    return pltpu.CompilerParams(dimension_semantics=sem, vmem_limit_bytes=VMEM_LIMIT_BYTES)


def _pick(n, options):
    for o in options:
        if n % o == 0:
            return o
    return n


_DIMS = {"nn": (((1,), (0,)), ((), ())), "nt": (((1,), (1,)), ((), ())), "tn": (((0,), (0,)), ((), ()))}


def _matmul(a, b, *, mode, out_dtype, name, epilogue=None, extra=None, tm=None, tn=None, tk=None, ride=None):
    if mode == "nn":
        (M, K), (K2, N) = a.shape, b.shape
    elif mode == "nt":
        (M, K), (N, K2) = a.shape, b.shape
    else:
        (K, M), (K2, N) = a.shape, b.shape
    assert K == K2, (a.shape, b.shape, mode)
    tm = tm or _pick(M, (1024, 512, 256, 128))
    tn = tn or _pick(N, (512, 256, 128))
    tk = tk or _pick(K, (2048, 2944, 1024, 512, 256, 128))
    nk = K // tk
    dims = _DIMS[mode]
    n_extra = 0 if extra is None else 1
    n_out = 2 if epilogue == "relu2" else 1
    n_ride = 0 if ride is None else ride.n
    grid = (M // tm, N // tn, nk)

    def body(*refs):
        a_ref, b_ref = refs[0], refs[1]
        extra_ref = refs[2] if n_extra else None
        first_out = 2 + n_extra + n_ride
        outs = refs[first_out:first_out + n_out]
        acc_ref = refs[first_out + n_out + n_ride] if nk > 1 else None
        if ride is not None:
            ride_refs = (refs[2 + n_extra:first_out], refs[first_out + n_out:first_out + n_out + n_ride], refs[-3:])
            at = [pl.program_id(d) for d in range(3)]

            @pl.when(jnp.logical_and(jnp.logical_and(at[0] == 0, at[1] == 0), at[2] == 0))
            def _():
                ride.start(*ride_refs)

        def finish(acc):
            if epilogue is None:
                outs[0][...] = acc.astype(outs[0].dtype)
            elif epilogue == "add":
                outs[0][...] = (acc + extra_ref[...]).astype(outs[0].dtype)
            elif epilogue == "relu2":
                r = jnp.maximum(acc, 0.0)
                outs[0][...] = r.astype(outs[0].dtype)
                outs[1][...] = (r * r).astype(outs[1].dtype)
            else:
                outs[0][...] = (acc * (2.0 * extra_ref[...].astype(F32))).astype(outs[0].dtype)

        prod = lax.dot_general(a_ref[...], b_ref[...], dims, preferred_element_type=F32)
        if nk == 1:
            finish(prod)
        else:
            k = pl.program_id(2)

            @pl.when(k == 0)
            def _():
                acc_ref[...] = prod

            @pl.when(k > 0)
            def _():
                acc_ref[...] += prod

            @pl.when(k == nk - 1)
            def _():
                finish(acc_ref[...])

        if ride is not None:
            @pl.when(jnp.logical_and(jnp.logical_and(at[0] == grid[0] - 1, at[1] == grid[1] - 1), at[2] == nk - 1))
            def _():
                ride.finish(*ride_refs)

    if mode == "nn":
        a_spec = pl.BlockSpec((tm, tk), lambda i, j, k: (i, k))
        b_spec = pl.BlockSpec((tk, tn), lambda i, j, k: (k, j))
    elif mode == "nt":
        a_spec = pl.BlockSpec((tm, tk), lambda i, j, k: (i, k))
        b_spec = pl.BlockSpec((tn, tk), lambda i, j, k: (j, k))
    else:
        a_spec = pl.BlockSpec((tk, tm), lambda i, j, k: (k, i))
        b_spec = pl.BlockSpec((tk, tn), lambda i, j, k: (k, j))
    o_spec = pl.BlockSpec((tm, tn), lambda i, j, k: (i, j))
    in_specs = [a_spec, b_spec] + ([o_spec] if n_extra else []) + [_ANY] * n_ride
    out_shape = [jax.ShapeDtypeStruct((M, N), out_dtype)] * n_out + (ride.land_shapes() if ride else [])
    res = pl.pallas_call(
        body,
        name=name,
        grid=grid,
        in_specs=in_specs,
        out_specs=[o_spec] * n_out + [_ANY] * n_ride,
        out_shape=out_shape,
        scratch_shapes=([pltpu.VMEM((tm, tn), F32)] if nk > 1 else []) + (ride.semaphores() if ride else []),
        compiler_params=_params(("arbitrary",) * 3 if ride else ("parallel", "parallel", "arbitrary")),
    )(a, b, *([extra] if n_extra else []), *(ride.arrays if ride else []))
    if ride is None:
        return res if n_out > 1 else res[0]
    return (res[:n_out] if n_out > 1 else res[0]), res[n_out:]


def _rmsnorm_fwd(x, g, *, name):
    T, D = x.shape
    tm = _pick(T, (512, 256, 128))

    def body(x_ref, g_ref, o_ref):
        xf = x_ref[...]
        r = lax.rsqrt(jnp.mean(xf * xf, axis=-1, keepdims=True) + EPS)
        o_ref[...] = ((xf * r) * g_ref[...]).astype(o_ref.dtype)

    return pl.pallas_call(
        body,
        name=name,
        grid=(T // tm,),
        in_specs=[pl.BlockSpec((tm, D), lambda i: (i, 0)), pl.BlockSpec((1, D), lambda i: (0, 0))],
        out_specs=pl.BlockSpec((tm, D), lambda i: (i, 0)),
        out_shape=jax.ShapeDtypeStruct((T, D), BF16),
        compiler_params=_params(("parallel",)),
    )(x, g)


def _rmsnorm_bwd(x, g, dh, dres, *, name):
    T, D = x.shape
    tm = _pick(T, (256, 128))

    def body(x_ref, g_ref, dh_ref, dres_ref, dx_ref, dxb_ref, dg_ref):
        i = pl.program_id(0)
        xf = x_ref[...]
        r = lax.rsqrt(jnp.mean(xf * xf, axis=-1, keepdims=True) + EPS)
        y = xf * r
        dh_v = dh_ref[...]
        dy = dh_v * g_ref[...]
        c = jnp.mean(dy * y, axis=-1, keepdims=True)
        dx = r * (dy - y * c) + dres_ref[...]
        dx_ref[...] = dx
        dxb_ref[...] = dx.astype(BF16)
        part = jnp.sum(dh_v * y, axis=0, keepdims=True)

        @pl.when(i == 0)
        def _():
            dg_ref[...] = part

        @pl.when(i > 0)
        def _():
            dg_ref[...] += part

    row = pl.BlockSpec((tm, D), lambda i: (i, 0))
    vec = pl.BlockSpec((1, D), lambda i: (0, 0))
    return pl.pallas_call(
        body,
        name=name,
        grid=(T // tm,),
        in_specs=[row, vec, row, row],
        out_specs=[row, row, vec],
        out_shape=[jax.ShapeDtypeStruct((T, D), F32), jax.ShapeDtypeStruct((T, D), BF16), jax.ShapeDtypeStruct((1, D), F32)],
        compiler_params=_params(("arbitrary",)),
    )(x, g, dh, dres)


GROUP_W = 4 * HEAD_DIM
N_DIL_BLOCKS = 3 * N_GROUPS
N_NORMED = 2 * N_GROUPS


def _kind_of_group(j, g):
    return jnp.clip((j - g) // N_GROUPS, 0, 2)


def _head_rstd(xh):
    return lax.rsqrt(jnp.mean(xh * xh, axis=-1, keepdims=True) + EPS)


def _prep_fwd(proj, gains, cos, sin, n_sb_blocks, *, name):
    T = proj.shape[0]
    tm = _pick(T, (512, 256, 128))

    def body(p_ref, gain_ref, cos_ref, sin_ref, o0_ref, o1_ref, o2_ref, os_ref):
        j = pl.program_id(1)
        for g, o_ref in enumerate((o0_ref, o1_ref, o2_ref)):
            @pl.when(jnp.logical_and(j % N_GROUPS == g, j < N_NORMED))
            def _():
                cos_v, sin_v = cos_ref[...], sin_ref[...]
                for hh in range(4):
                    sl = slice(hh * HEAD_DIM, (hh + 1) * HEAD_DIM)
                    xh = p_ref[:, sl]
                    y = (xh * _head_rstd(xh)) * gain_ref[0, :, sl]
                    o_ref[:, sl] = (y * cos_v + pltpu.roll(y, HEAD_DIM // 2, 1) * sin_v).astype(BF16)

            @pl.when(j == N_NORMED + g)
            def _():
                o_ref[...] = p_ref[...].astype(BF16)

        @pl.when(j >= N_DIL_BLOCKS)
        def _():
            os_ref[...] = p_ref[...].astype(BF16)

    def group_spec(g):
        return pl.BlockSpec((tm, GROUP_W), lambda i, j: (i, _kind_of_group(j, g)))

    return pl.pallas_call(
        body,
        name=name,
        grid=(T // tm, N_DIL_BLOCKS + n_sb_blocks),
        in_specs=[
            pl.BlockSpec((tm, GROUP_W), lambda i, j: (i, j)),
            pl.BlockSpec((1, 1, GROUP_W), lambda i, j: (jnp.minimum(j, N_NORMED - 1), 0, 0)),
            pl.BlockSpec((tm, HEAD_DIM), lambda i, j: (i, 0)),
            pl.BlockSpec((tm, HEAD_DIM), lambda i, j: (i, 0)),
        ],
        out_specs=[group_spec(0), group_spec(1), group_spec(2),
                   pl.BlockSpec((tm, GROUP_W), lambda i, j: (i, jnp.maximum(j - N_DIL_BLOCKS, 0)))],
        out_shape=[jax.ShapeDtypeStruct((T, 3 * GROUP_W), BF16)] * N_GROUPS
        + [jax.ShapeDtypeStruct((T, n_sb_blocks * GROUP_W), BF16)],
        compiler_params=_params(("parallel", "arbitrary")),
    )(proj, gains, cos, sin)


def _prep_bwd(proj, dqkv, gains, cos, sin, *, name):
    T = proj.shape[0]
    tm = _pick(T, (512, 256, 128))

    def body(p_ref, d0_ref, d1_ref, d2_ref, gain_ref, cos_ref, sin_ref, o_ref, dgain_ref):
        j, i = pl.program_id(0), pl.program_id(1)

        def normed_bwd(d_ref):
            cos_v, sin_v = cos_ref[...], sin_ref[...]
            part = jnp.zeros((1, HEAD_DIM), F32)
            for hh in range(4):
                sl = slice(hh * HEAD_DIM, (hh + 1) * HEAD_DIM)
                xh = p_ref[:, sl]
                r = _head_rstd(xh)
                y0 = xh * r
                d_out = d_ref[:, sl]
                d_yg = d_out * cos_v + pltpu.roll(d_out * sin_v, HEAD_DIM // 2, 1)
                part = part + jnp.sum(d_yg * y0, axis=0, keepdims=True)
                dy0 = d_yg * gain_ref[0, :, sl]
                c = jnp.mean(dy0 * y0, axis=-1, keepdims=True)
                o_ref[:, sl] = (r * (dy0 - y0 * c)).astype(BF16)

            @pl.when(i == 0)
            def _():
                dgain_ref[0] = part

            @pl.when(i > 0)
            def _():
                dgain_ref[0] += part

        for g, d_ref in enumerate((d0_ref, d1_ref, d2_ref)):
            @pl.when(jnp.logical_and(j % N_GROUPS == g, j < N_NORMED))
            def _():
                normed_bwd(d_ref)

            @pl.when(j == N_NORMED + g)
            def _():
                o_ref[...] = d_ref[...].astype(BF16)

    gain_row = lambda j, i: (jnp.minimum(j, N_NORMED - 1), 0, 0)

    def grad_spec(g):
        return pl.BlockSpec((tm, GROUP_W), lambda j, i: (i, _kind_of_group(j, g)))

    return pl.pallas_call(
        body,
        name=name,
        grid=(N_DIL_BLOCKS, T // tm),
        in_specs=[
            pl.BlockSpec((tm, GROUP_W), lambda j, i: (i, j)),
            grad_spec(0), grad_spec(1), grad_spec(2),
            pl.BlockSpec((1, 1, GROUP_W), gain_row),
            pl.BlockSpec((tm, HEAD_DIM), lambda j, i: (i, 0)),
            pl.BlockSpec((tm, HEAD_DIM), lambda j, i: (i, 0)),
        ],
        out_specs=[pl.BlockSpec((tm, GROUP_W), lambda j, i: (i, j)), pl.BlockSpec((1, 1, HEAD_DIM), gain_row)],
        out_shape=[jax.ShapeDtypeStruct((T, N_DIL_BLOCKS * GROUP_W), BF16),
                   jax.ShapeDtypeStruct((2 * N_GROUPS, 1, HEAD_DIM), F32)],
        compiler_params=_params(("arbitrary", "arbitrary")),
    )(proj, *dqkv, gains, cos, sin)


def _nt(a, b):
    return lax.dot_general(a, b, _DIMS["nt"], preferred_element_type=F32)


def _tn(a, b):
    return lax.dot_general(a, b, _DIMS["tn"], preferred_element_type=F32)


def _nn(a, b):
    return jnp.dot(a, b, preferred_element_type=F32)


def _window_masks():
    row = lax.broadcasted_iota(jnp.int32, (BLOCK, BLOCK), 0)
    col = lax.broadcasted_iota(jnp.int32, (BLOCK, BLOCK), 1)
    return row >= col, col >= row


def _heads():
    return [slice(hh * HEAD_DIM, (hh + 1) * HEAD_DIM) for hh in range(GROUP_W // HEAD_DIM)]


def _dil_fwd(qkv, g, *, name):
    T = qkv.shape[0]
    r = DILATIONS[g]
    L = T // r
    nb = L // BLOCK
    scale = 1.0 / math.sqrt(HEAD_DIM)
    view = qkv.reshape(L, r * 3 * GROUP_W)

    def body(q_ref, kc_ref, kp_ref, vc_ref, vp_ref, o_ref, ld_ref):
        n = pl.program_id(1)
        m_cur, m_prev = _window_masks()
        m_prev = jnp.logical_and(m_prev, n > 0)
        for sl in _heads():
            q = q_ref[:, sl]
            s_c = jnp.where(m_cur, _nt(q, kc_ref[:, sl]) * scale, NEG)
            s_p = jnp.where(m_prev, _nt(q, kp_ref[:, sl]) * scale, NEG)
            m = jnp.maximum(jnp.max(s_c, axis=-1, keepdims=True), jnp.max(s_p, axis=-1, keepdims=True))
            p_c = jnp.exp(s_c - m)
            p_p = jnp.exp(s_p - m)
            l = jnp.sum(p_c, axis=-1, keepdims=True) + jnp.sum(p_p, axis=-1, keepdims=True)
            inv = 1.0 / l
            o_ref[:, sl] = _nn((p_c * inv).astype(BF16), vc_ref[:, sl]) + _nn((p_p * inv).astype(BF16), vp_ref[:, sl])
            ld_ref[:, sl] = jnp.broadcast_to(m + jnp.log(l), (BLOCK, HEAD_DIM))

    blk = (BLOCK, GROUP_W)
    cur = lambda kind: pl.BlockSpec(blk, lambda c, n: (n, 3 * c + kind))
    prev = lambda kind: pl.BlockSpec(blk, lambda c, n: (jnp.maximum(n - 1, 0), 3 * c + kind))
    out_spec = pl.BlockSpec(blk, lambda c, n: (n, c))
    o, ld = pl.pallas_call(
        body,
        name=name,
        grid=(r, nb),
        in_specs=[cur(0), cur(1), prev(1), cur(2), prev(2)],
        out_specs=[out_spec, out_spec],
        out_shape=[jax.ShapeDtypeStruct((L, r * GROUP_W), F32)] * 2,
        compiler_params=_params(("parallel", "arbitrary")),
    )(view, view, view, view, view)
    return o.reshape(T, GROUP_W), ld.reshape(T, GROUP_W)


def _dil_bwd(qkv, do, ld, dterm, g, *, name):
    T = qkv.shape[0]
    r = DILATIONS[g]
    L = T // r
    nb = L // BLOCK
    scale = 1.0 / math.sqrt(HEAD_DIM)
    view = qkv.reshape(L, r * 3 * GROUP_W)
    do_v, ld_v, dt_v = (t.reshape(L, r * GROUP_W) for t in (do, ld, dterm))

    def body(q_ref, qn_ref, kc_ref, kp_ref, vc_ref, vp_ref, do_ref, don_ref, ld_ref, ldn_ref, dt_ref, dtn_ref, out_ref):
        n = pl.program_id(1)
        m_cur, m_prev = _window_masks()
        has_prev, has_next = jnp.logical_and(m_prev, n > 0), jnp.logical_and(m_prev, n < nb - 1)

        def tile(q, k, v, do_t, ld_t, dt_t, mask):
            s = _nt(q, k) * scale
            p = jnp.where(mask, jnp.exp(s - ld_t[:, 0:1]), 0.0)
            ds = p * (_nt(do_t, v) + dt_t[:, 0:1]) * scale
            return p.astype(BF16), ds.astype(BF16)

        for hh, sl in enumerate(_heads()):
            kc, vc, kp = kc_ref[:, sl], vc_ref[:, sl], kp_ref[:, sl]
            q, do_t, ld_t, dt_t = q_ref[:, sl], do_ref[:, sl], ld_ref[:, sl], dt_ref[:, sl]
            p_cc, ds_cc = tile(q, kc, vc, do_t, ld_t, dt_t, m_cur)
            _, ds_cp = tile(q, kp, vp_ref[:, sl], do_t, ld_t, dt_t, has_prev)
            qn, don = qn_ref[:, sl], don_ref[:, sl]
            p_nc, ds_nc = tile(qn, kc, vc, don, ldn_ref[:, sl], dtn_ref[:, sl], has_next)
            at = lambda kind: slice(kind * GROUP_W + hh * HEAD_DIM, kind * GROUP_W + (hh + 1) * HEAD_DIM)
            out_ref[:, at(0)] = _nn(ds_cc, kc) + _nn(ds_cp, kp)
            out_ref[:, at(1)] = _tn(ds_cc, q) + _tn(ds_nc, qn)
            out_ref[:, at(2)] = _tn(p_cc, do_t) + _tn(p_nc, don)

    blk = (BLOCK, GROUP_W)
    qkv_spec = lambda kind, shift: pl.BlockSpec(blk, lambda c, n: (jnp.clip(n + shift, 0, nb - 1), 3 * c + kind))
    row_spec = lambda shift: pl.BlockSpec(blk, lambda c, n: (jnp.clip(n + shift, 0, nb - 1), c))
    out = pl.pallas_call(
        body,
        name=name,
        grid=(r, nb),
        in_specs=[qkv_spec(0, 0), qkv_spec(0, 1), qkv_spec(1, 0), qkv_spec(1, -1), qkv_spec(2, 0), qkv_spec(2, -1),
                  row_spec(0), row_spec(1), row_spec(0), row_spec(1), row_spec(0), row_spec(1)],
        out_specs=pl.BlockSpec((BLOCK, 3 * GROUP_W), lambda c, n: (n, c)),
        out_shape=jax.ShapeDtypeStruct((L, r * 3 * GROUP_W), F32),
        compiler_params=_params(("parallel", "arbitrary")),
    )(view, view, view, view, view, view, do_v, do_v, ld_v, ld_v, dt_v, dt_v)
    return out.reshape(T, 3 * GROUP_W)


def _group_weights(ld_refs):
    lds = [r[...] for r in ld_refs]
    m = jnp.maximum(jnp.maximum(lds[0], lds[1]), lds[2])
    es = [jnp.exp(v - m) for v in lds]
    inv = 1.0 / (es[0] + es[1] + es[2])
    return [e * inv for e in es]


def _merge_fwd(os_, lds, *, name):
    T = os_[0].shape[0]
    tm = _pick(T, (1024, 512, 256, 128))

    def body(o0, o1, o2, l0, l1, l2, y_ref):
        w = _group_weights((l0, l1, l2))
        y_ref[...] = (w[0] * o0[...] + w[1] * o1[...] + w[2] * o2[...]).astype(BF16)

    spec = pl.BlockSpec((tm, GROUP_W), lambda i: (i, 0))
    return pl.pallas_call(
        body, name=name, grid=(T // tm,), in_specs=[spec] * 6, out_specs=spec,
        out_shape=jax.ShapeDtypeStruct((T, GROUP_W), BF16), compiler_params=_params(("parallel",)),
    )(*os_, *lds)


def _merge_bwd(os_, lds, dy, *, name):
    T = dy.shape[0]
    tm = _pick(T, (512, 256, 128))

    def body(o0, o1, o2, l0, l1, l2, dy_ref, do0, do1, do2, dt0, dt1, dt2):
        w = _group_weights((l0, l1, l2))
        dy_v = dy_ref[...]
        y = w[0] * o0[...] + w[1] * o1[...] + w[2] * o2[...]
        prod = dy_v * y
        for hh in range(4):
            sl = slice(hh * HEAD_DIM, (hh + 1) * HEAD_DIM)
            s = jnp.sum(prod[:, sl], axis=-1, keepdims=True)
            for wg, dt in zip(w, (dt0, dt1, dt2)):
                dt[:, sl] = -wg[:, sl] * s
        for wg, do in zip(w, (do0, do1, do2)):
            do[...] = (wg * dy_v).astype(BF16)

    spec = pl.BlockSpec((tm, GROUP_W), lambda i: (i, 0))
    outs = pl.pallas_call(
        body, name=name, grid=(T // tm,), in_specs=[spec] * 7, out_specs=[spec] * 6,
        out_shape=[jax.ShapeDtypeStruct((T, GROUP_W), BF16)] * 3 + [jax.ShapeDtypeStruct((T, GROUP_W), F32)] * 3,
        compiler_params=_params(("parallel",)),
    )(*os_, *lds, dy)
    return outs[:3], outs[3:]


SB_ROWS = 512


def _sum_matrix(inclusive):
    j = lax.broadcasted_iota(jnp.int32, (2 * BLOCK, 2 * BLOCK), 0) % BLOCK
    s = lax.broadcasted_iota(jnp.int32, (2 * BLOCK, 2 * BLOCK), 1)
    later = (j >= s) if inclusive else (j > s)
    return jnp.logical_or(s >= BLOCK, later).astype(BF16)


def _block_sums(x, mat):
    hi = x.astype(BF16)
    lo = (x - hi.astype(F32)).astype(BF16)
    r = _nn(jnp.concatenate([hi, lo], axis=1), mat)
    return r[:, :BLOCK], r[:, BLOCK:]


def _log_terms(z):
    t = jnp.log(1.0 + jnp.exp(-jnp.abs(z)))
    return -(jnp.maximum(z, 0.0) + t), jnp.minimum(z, 0.0) - t


SB_DEAD = -105.0


def _sb_alive(n_chunks, state):
    return jnp.logical_and(state[0] < n_chunks, jnp.max(state[1]) > SB_DEAD)


def _causal_mask(rows):
    row = lax.broadcasted_iota(jnp.int32, (rows, rows), 0)
    col = lax.broadcasted_iota(jnp.int32, (rows, rows), 1)
    return col < row


def _sb_fwd(qkv, n_heads, col0, *, name):
    T = qkv.shape[0]
    tq = _pick(T, (SB_ROWS, BLOCK))
    nq, nsub = T // tq, tq // BLOCK
    scale = 1.0 / math.sqrt(HEAD_DIM)

    def body(q_ref, k_ref, v_ref, o_ref, ob_ref):
        i = pl.program_id(1)
        q = q_ref[...]
        mat = _sum_matrix(False)

        def chunk(j, carry, acc, mask):
            rows = pl.ds(pl.multiple_of(j * tq, tq), tq)
            z = _nt(q, k_ref[rows, :]) * scale
            lk, ls = _log_terms(z)
            if mask is not None:
                lk = jnp.where(mask, lk, 0.0)
            a = []
            for b in reversed(range(nsub)):
                sl = slice(b * BLOCK, (b + 1) * BLOCK)
                later, total = _block_sums(lk[:, sl], mat)
                a.append(jnp.exp(ls[:, sl] + (later + carry)))
                carry = carry + total
            a = jnp.concatenate(a[::-1], axis=1)
            if mask is not None:
                a = jnp.where(mask, a, 0.0)
            return carry, acc + _nn(a.astype(BF16), v_ref[rows, :])

        zero = jnp.zeros((tq, HEAD_DIM), F32)
        carry, acc = chunk(i, zero, zero, _causal_mask(tq))

        def step(state):
            carry, acc = chunk(i - 1 - state[0], state[1], state[2], None)
            return state[0] + 1, carry, acc

        _, carry, acc = lax.while_loop(functools.partial(_sb_alive, i), step, (0, carry, acc))
        o_ref[...] = acc
        ob_ref[...] = acc.astype(BF16)

    blk = (tq, HEAD_DIM)
    out_spec = pl.BlockSpec(blk, lambda h, i: (i, h))
    return pl.pallas_call(
        body,
        name=name,
        grid=(n_heads, nq),
        in_specs=[
            pl.BlockSpec(blk, lambda h, i: (i, col0 + h)),
            pl.BlockSpec((T, HEAD_DIM), lambda h, i: (0, col0 + n_heads + h)),
            pl.BlockSpec((T, HEAD_DIM), lambda h, i: (0, col0 + 2 * n_heads + h)),
        ],
        out_specs=[out_spec, out_spec],
        out_shape=[jax.ShapeDtypeStruct((T, n_heads * HEAD_DIM), F32), jax.ShapeDtypeStruct((T, n_heads * HEAD_DIM), BF16)],
        compiler_params=_params(("parallel", "arbitrary")),
    )(qkv, qkv, qkv)


def _sb_bwd(qkv, o32, do, n_heads, col0, *, name):
    T = qkv.shape[0]
    tq = _pick(T, (SB_ROWS, BLOCK))
    nq, nsub = T // tq, tq // BLOCK
    scale = 1.0 / math.sqrt(HEAD_DIM)

    def body(q_ref, k_ref, v_ref, o_ref, do_ref, dq_ref, dk_ref, dv_ref, dk_acc, dv_acc):
        i = pl.program_id(1)

        @pl.when(i == 0)
        def _():
            dk_acc[...] = jnp.zeros_like(dk_acc)
            dv_acc[...] = jnp.zeros_like(dv_acc)

        q, do_t = q_ref[...], do_ref[...]
        delta = jnp.broadcast_to(jnp.sum(do_t.astype(F32) * o_ref[...], axis=-1, keepdims=True), (tq, HEAD_DIM))
        mat, mat_incl = _sum_matrix(False), _sum_matrix(True)

        def chunk(j, carry_b, carry_g, dq, mask):
            rows = pl.ds(pl.multiple_of(j * tq, tq), tq)
            k_t, v_t = k_ref[rows, :], v_ref[rows, :]
            z = _nt(q, k_t) * scale
            lk, ls = _log_terms(z)
            if mask is not None:
                lk = jnp.where(mask, lk, 0.0)
            d_a = _nt(do_t, v_t)
            a_parts, dz_parts = [], []
            for b in reversed(range(nsub)):
                sl = slice(b * BLOCK, (b + 1) * BLOCK)
                later, total = _block_sums(lk[:, sl], mat)
                a = jnp.exp(ls[:, sl] + (later + carry_b))
                carry_b = carry_b + total
                if mask is not None:
                    a = jnp.where(mask[:, sl], a, 0.0)
                a_b = a.astype(BF16)
                g = a_b.astype(F32) * d_a[:, sl]
                from_here, total_g = _block_sums(g, mat_incl)
                before = delta - (from_here + carry_g)
                carry_g = carry_g + total_g
                sig = jnp.exp(ls[:, sl])
                dz = (g - sig * (g + before)) * scale
                a_parts.append(a_b)
                dz_parts.append(dz)
            dz = jnp.concatenate(dz_parts[::-1], axis=1)
            if mask is not None:
                dz = jnp.where(mask, dz, 0.0)
            dz_b = dz.astype(BF16)
            dk_acc[rows, :] += _tn(dz_b, q)
            dv_acc[rows, :] += _tn(jnp.concatenate(a_parts[::-1], axis=1), do_t)
            return carry_b, carry_g, dq + _nn(dz_b, k_t)

        zero = jnp.zeros((tq, HEAD_DIM), F32)
        state = chunk(i, zero, zero, zero, _causal_mask(tq))

        def step(st):
            return (st[0] + 1,) + chunk(i - 1 - st[0], st[1], st[2], st[3], None)

        state = lax.while_loop(functools.partial(_sb_alive, i), step, (0,) + state)
        dq_ref[...] = state[3].astype(BF16)

        @pl.when(i == nq - 1)
        def _():
            dk_ref[...] = dk_acc[...].astype(BF16)
            dv_ref[...] = dv_acc[...].astype(BF16)

    blk = (tq, HEAD_DIM)
    full = (T, HEAD_DIM)
    dshape = jax.ShapeDtypeStruct((T, n_heads * HEAD_DIM), BF16)
    return pl.pallas_call(
        body,
        name=name,
        grid=(n_heads, nq),
        in_specs=[
            pl.BlockSpec(blk, lambda h, i: (i, col0 + h)),
            pl.BlockSpec(full, lambda h, i: (0, col0 + n_heads + h)),
            pl.BlockSpec(full, lambda h, i: (0, col0 + 2 * n_heads + h)),
            pl.BlockSpec(blk, lambda h, i: (i, h)),
            pl.BlockSpec(blk, lambda h, i: (i, h)),
        ],
        out_specs=[pl.BlockSpec(blk, lambda h, i: (i, h)), pl.BlockSpec(full, lambda h, i: (0, h)),
                   pl.BlockSpec(full, lambda h, i: (0, h))],
        out_shape=[dshape, dshape, dshape],
        scratch_shapes=[pltpu.VMEM(full, F32), pltpu.VMEM(full, F32)],
        compiler_params=_params(("arbitrary", "arbitrary")),
    )(qkv, qkv, qkv, o32, do)


def _gate_fwd(y_dil, y_sb, w_up_dil, w_up_sb, proj, gate_b, gate_col0, *, name):
    T, D = y_dil.shape[0], w_up_dil.shape[0]
    tm = _pick(T, (512, 256, 128))
    tn = _pick(D, (512, 256, 128))
    c0, nbr = gate_col0 // tn, D // tn

    def body(yd_ref, ys_ref, wd_ref, ws_ref, gp0_ref, gp1_ref, b_ref, o_ref):
        g0 = jax.nn.sigmoid(gp0_ref[...] + b_ref[0:1, :])
        g1 = jax.nn.sigmoid(gp1_ref[...] + b_ref[1:2, :])
        o_ref[...] = (g0 * _nt(yd_ref[...], wd_ref[...]) + g1 * _nt(ys_ref[...], ws_ref[...])).astype(BF16)

    return pl.pallas_call(
        body,
        name=name,
        grid=(T // tm, nbr),
        in_specs=[
            pl.BlockSpec((tm, y_dil.shape[1]), lambda i, j: (i, 0)),
            pl.BlockSpec((tm, y_sb.shape[1]), lambda i, j: (i, 0)),
            pl.BlockSpec((tn, w_up_dil.shape[1]), lambda i, j: (j, 0)),
            pl.BlockSpec((tn, w_up_sb.shape[1]), lambda i, j: (j, 0)),
            pl.BlockSpec((tm, tn), lambda i, j: (i, c0 + j)),
            pl.BlockSpec((tm, tn), lambda i, j: (i, c0 + nbr + j)),
            pl.BlockSpec((2, tn), lambda i, j: (0, j)),
        ],
        out_specs=pl.BlockSpec((tm, tn), lambda i, j: (i, j)),
        out_shape=jax.ShapeDtypeStruct((T, D), BF16),
        compiler_params=_params(("parallel", "parallel")),
    )(y_dil, y_sb, w_up_dil, w_up_sb, proj, proj, gate_b)


def _gate_bwd(y, w_up, proj, gate_b, dmixed, branch, gate_col0, *, name):
    T, D = y.shape[0], w_up.shape[0]
    tm = _pick(T, (512, 256, 128))
    tn = _pick(D, (512, 256, 128))
    c0 = gate_col0 // tn + branch * (D // tn)

    def body(y_ref, w_ref, gp_ref, b_ref, dm_ref, dup_ref, dgp_ref, db_ref):
        i = pl.program_id(1)
        g = jax.nn.sigmoid(gp_ref[...] + b_ref[branch:branch + 1, :])
        dm = dm_ref[...]
        dup_ref[...] = (dm * g).astype(BF16)
        dgp = (dm * _nt(y_ref[...], w_ref[...])) * (g * (1.0 - g))
        dgp_ref[...] = dgp.astype(BF16)
        part = jnp.sum(dgp, axis=0, keepdims=True)

        @pl.when(i == 0)
        def _():
            db_ref[...] = part

        @pl.when(i > 0)
        def _():
            db_ref[...] += part

    tile = pl.BlockSpec((tm, tn), lambda j, i: (i, j))
    return pl.pallas_call(
        body,
        name=name,
        grid=(D // tn, T // tm),
        in_specs=[
            pl.BlockSpec((tm, y.shape[1]), lambda j, i: (i, 0)),
            pl.BlockSpec((tn, w_up.shape[1]), lambda j, i: (j, 0)),
            pl.BlockSpec((tm, tn), lambda j, i: (i, c0 + j)),
            pl.BlockSpec((2, tn), lambda j, i: (0, j)),
            tile,
        ],
        out_specs=[tile, tile, pl.BlockSpec((1, tn), lambda j, i: (0, j))],
        out_shape=[jax.ShapeDtypeStruct((T, D), BF16), jax.ShapeDtypeStruct((T, D), BF16), jax.ShapeDtypeStruct((1, D), F32)],
        compiler_params=_params(("parallel", "arbitrary")),
    )(y, w_up, proj, gate_b, dmixed)


def _loss_head(y, target, *, name):
    T, D = y.shape
    tm = _pick(T, (256, 128))

    def body(y_ref, t_ref, dy_ref, dyb_ref, l_ref):
        i = pl.program_id(0)
        err = y_ref[...] - t_ref[...]
        dy = err * (1.0 / D)
        dy_ref[...] = dy
        dyb_ref[...] = dy.astype(BF16)
        part = 0.5 * jnp.sum(jnp.mean(err * err, axis=-1, keepdims=True), axis=0, keepdims=True)

        @pl.when(i == 0)
        def _():
            l_ref[...] = part

        @pl.when(i > 0)
        def _():
            l_ref[...] += part

    row = pl.BlockSpec((tm, D), lambda i: (i, 0))
    return pl.pallas_call(
        body, name=name, grid=(T // tm,), in_specs=[row, row],
        out_specs=[row, row, pl.BlockSpec((1, 1), lambda i: (0, 0))],
        out_shape=[jax.ShapeDtypeStruct((T, D), F32), jax.ShapeDtypeStruct((T, D), BF16), jax.ShapeDtypeStruct((1, 1), F32)],
        compiler_params=_params(("arbitrary",)),
    )(y, target)


def _sum_parts(p_ref, own):
    slot = _slot(_place())
    g = jnp.where(slot == 0, own, p_ref[0].astype(F32))
    for s in range(1, N_DEV):
        g = g + jnp.where(slot == s, own, p_ref[s].astype(F32))
    return g


def _held(step, l, last):
    layer, i = step
    return jnp.where(layer == l, i, last * (layer > l))


def _reduce_adamw(parts, own, w, m, v, *, own_chunked=True, name):
    n_layers = len(parts)
    C = w.shape[1]
    R = w.shape[0] // n_layers
    tr = R
    for cand in (1024, 512, 256, 128, 64, 32, 16, 8):
        if R % cand == 0 and cand * C <= 256 * 1024:
            tr = cand
            break
    nr = R // tr

    def body(*refs):
        p_refs, o_refs = refs[:n_layers], refs[n_layers:2 * n_layers]
        w_ref, m_ref, v_ref, g_ref, d_ref, nm_ref, nv_ref = refs[2 * n_layers:]
        layer = pl.program_id(0)
        for l, (p_ref, o_ref) in enumerate(zip(p_refs, o_refs)):
            @pl.when(layer == l)
            def _():
                g_ref[...] = _sum_parts(p_ref, (o_ref[0] if own_chunked else o_ref[...]).astype(F32))

        _adamw_update(g_ref[...], w_ref, m_ref, v_ref, d_ref, nm_ref, nv_ref)

    def part_spec(l):
        return pl.BlockSpec((N_DEV, tr, C), lambda *step: (0, _held(step, l, nr - 1), 0))

    def own_spec(l):
        if own_chunked:
            return pl.BlockSpec((1, tr, C), lambda *step: (_slot(_place()), _held(step, l, nr - 1), 0))
        return pl.BlockSpec((tr, C), lambda *step: (_held(step, l, nr - 1), 0))

    row = pl.BlockSpec((tr, C), lambda layer, i: (layer * nr + i, 0))
    return pl.pallas_call(
        body, name=name, grid=(n_layers, nr),
        in_specs=[part_spec(l) for l in range(n_layers)] + [own_spec(l) for l in range(n_layers)] + [row, row, row],
        out_specs=[row] * 4, out_shape=[jax.ShapeDtypeStruct(w.shape, F32)] * 4,
        compiler_params=_params(("arbitrary", "arbitrary")),
    )(*parts, *own, w, m, v)


def _adamw_update(g, w_ref, m_ref, v_ref, d_ref, nm_ref, nv_ref):
    m_new = ADAM_B1 * m_ref[...] + (1.0 - ADAM_B1) * g
    v_new = ADAM_B2 * v_ref[...] + (1.0 - ADAM_B2) * (g * g)
    m_hat = m_new / (1.0 - ADAM_B1 ** ADAM_STEP)
    v_hat = v_new / (1.0 - ADAM_B2 ** ADAM_STEP)
    d_ref[...] = -ADAM_LR * (m_hat / (jnp.sqrt(v_hat) + ADAM_EPS) + ADAM_WD * w_ref[...])
    nm_ref[...] = m_new
    nv_ref[...] = v_new


def _reduce_adamw_t(parts, own, w, m, v, *, name):
    n_layers = len(parts)
    _, n, K = parts[0].shape
    n_pad = w.shape[1]
    tm = _pick(K, (128,))
    nr = K // tm

    def body(*refs):
        p_refs, o_refs = refs[:n_layers], refs[n_layers:2 * n_layers]
        w_ref, m_ref, v_ref, g_ref, d_ref, nm_ref, nv_ref = refs[2 * n_layers:]
        layer = pl.program_id(0)
        for l, (p_ref, o_ref) in enumerate(zip(p_refs, o_refs)):
            @pl.when(layer == l)
            def _():
                g_t = _sum_parts(p_ref, o_ref[0].astype(F32))
                if n_pad > n:
                    g_t = jnp.concatenate([g_t, jnp.zeros((n_pad - n, tm), F32)], axis=0)
                g_ref[...] = g_t.T

        _adamw_update(g_ref[...], w_ref, m_ref, v_ref, d_ref, nm_ref, nv_ref)

    def part_spec(l):
        return pl.BlockSpec((N_DEV, n, tm), lambda *step: (0, 0, _held(step, l, nr - 1)))

    def own_spec(l):
        return pl.BlockSpec((1, n, tm), lambda *step: (_slot(_place()), 0, _held(step, l, nr - 1)))

    row = pl.BlockSpec((tm, n_pad), lambda layer, i: (layer * nr + i, 0))
    return pl.pallas_call(
        body, name=name, grid=(n_layers, nr),
        in_specs=[part_spec(l) for l in range(n_layers)] + [own_spec(l) for l in range(n_layers)] + [row, row, row],
        out_specs=[row] * 4, out_shape=[jax.ShapeDtypeStruct(w.shape, F32)] * 4,
        compiler_params=_params(("arbitrary", "arbitrary")),
    )(*parts, *own, w, m, v)


def _transpose_cast(x, *, name):
    R, C = x.shape
    tr, tc = _pick(R, (512, 256, 128)), _pick(C, (512, 256, 128))

    def body(x_ref, o_ref):
        o_ref[...] = x_ref[...].astype(F32).T.astype(BF16)

    return pl.pallas_call(
        body, name=name, grid=(R // tr, C // tc),
        in_specs=[pl.BlockSpec((tr, tc), lambda i, j: (i, j))],
        out_specs=pl.BlockSpec((tc, tr), lambda i, j: (j, i)),
        out_shape=jax.ShapeDtypeStruct((C, R), BF16),
        compiler_params=_params(("parallel", "parallel")),
    )(x)


_ANY = pl.BlockSpec(memory_space=pl.ANY)


def _place():
    return lax.axis_index("x"), lax.axis_index("y"), lax.axis_index("c")


def _slot(p):
    return 4 * p[0] + 2 * p[1] + p[2]


_HBM = pl.BlockSpec(memory_space=pltpu.HBM)
_SEM = pl.BlockSpec(memory_space=pltpu.SEMAPHORE)
_EFFECT = pltpu.SideEffectType.DATAFLOW_SIDE_EFFECTING
N_PEERS = N_DEV - 1


def _peers(me):
    flips = [(fx, fy, fc) for fx in (0, 1) for fy in (0, 1) for fc in (0, 1)][1:]
    return [tuple(1 - v if f else v for v, f in zip(me, flip)) for flip in flips]


def _peer_copy(src, lands, t, k, sender, to, send_sems, recv_sems):
    return pltpu.make_async_remote_copy(
        src_ref=src, dst_ref=lands[t].at[_slot(sender)], send_sem=send_sems.at[N_PEERS * t + k],
        recv_sem=recv_sems.at[N_PEERS * t + k], device_id=to, device_id_type=MESH)


class _Exchange:
    def __init__(self, chunked, whole, name):
        self.arrays = [pltpu.with_memory_space_constraint(a, pltpu.HBM) for a in list(chunked) + list(whole)]
        self.n, self.n_chunked, self.name = len(self.arrays), len(chunked), name

    def _src(self, ins, t, dest):
        return ins[t].at[_slot(dest)] if t < self.n_chunked else ins[t]

    def _land_shape(self, t):
        a = self.arrays[t]
        return a.shape if t < self.n_chunked else (N_DEV,) + a.shape

    def start(self, after=None):
        n = self.n

        def body(*refs):
            ins, lands = refs[:n], refs[n:2 * n]
            send_sems, recv_sems = refs[-2 * n - 3], refs[-2 * n - 2]
            token = refs[-1]
            me = _place()
            for t in range(n):
                for k, peer in enumerate(_peers(me)):
                    _peer_copy(self._src(ins, t, peer), lands, t, k, me, peer, send_sems, recv_sems).start()
            token[...] = jnp.zeros_like(token)

        lands = [pltpu.with_memory_space_constraint(lax.empty(self._land_shape(t), a.dtype), pltpu.HBM)
                 for t, a in enumerate(self.arrays)]
        sems = pltpu.SemaphoreType.DMA((N_PEERS * n,))
        outs = pl.pallas_call(
            body,
            name=self.name + "_start",
            in_specs=[_HBM] * (2 * n) + ([_ANY] if after is not None else []),
            out_specs=[_SEM, _SEM] + [_HBM] * (2 * n) + [pl.BlockSpec(memory_space=pltpu.VMEM)],
            out_shape=[sems, sems] + [pltpu.HBM(a.shape, a.dtype) for a in self.arrays + lands]
            + [jax.ShapeDtypeStruct((8, 128), F32)],
            input_output_aliases={t: 2 + t for t in range(2 * n)},
            compiler_params=pltpu.CompilerParams(has_side_effects=_EFFECT),
        )(*self.arrays, *lands, *([after] if after is not None else []))
        self.sems, self.thru, self.lands, self.token = outs[:2], outs[2:2 + n], outs[2 + n:2 + 2 * n], outs[-1]
        return self.token

    def finish(self, after):
        n = self.n

        def wait_body(*refs):
            ins, lands, (send_sems, recv_sems) = refs[:n], refs[n:2 * n], refs[2 * n:2 * n + 2]
            me = _place()
            for t in range(n):
                for k, peer in enumerate(_peers(me)):
                    cp = _peer_copy(self._src(ins, t, peer), lands, t, k, peer, peer, send_sems, recv_sems)
                    cp.wait_send()
                    cp.wait_recv()

        outs = pl.pallas_call(
            wait_body,
            name=self.name + "_wait",
            in_specs=[_HBM] * (2 * n) + [_SEM, _SEM, _ANY],
            out_specs=[_HBM] * (2 * n),
            out_shape=[pltpu.HBM(a.shape, a.dtype) for a in self.thru + self.lands],
            input_output_aliases={t: t for t in range(2 * n)},
            compiler_params=pltpu.CompilerParams(has_side_effects=_EFFECT),
        )(*self.thru, *self.lands, *self.sems, after)
        return outs[:n], outs[n:]


def _exchange(chunked, whole, *, name):
    ride = _Ride(chunked, whole)
    n = ride.n

    def body(*refs):
        ride.start(refs[:n], refs[n:2 * n], refs[2 * n:])
        ride.finish(refs[:n], refs[n:2 * n], refs[2 * n:])

    return pl.pallas_call(
        body,
        name=name,
        in_specs=[_ANY] * n,
        out_specs=[_ANY] * n,
        out_shape=ride.land_shapes(),
        scratch_shapes=ride.semaphores(),
        compiler_params=pltpu.CompilerParams(has_side_effects=True),
    )(*ride.arrays)


class _Ride:
    def __init__(self, chunked, whole):
        self.arrays = list(chunked) + list(whole)
        self.n, self.n_chunked = len(self.arrays), len(chunked)
        self.gather = self.n_chunked == 0

    def land_shapes(self):
        return [jax.ShapeDtypeStruct(a.shape if t < self.n_chunked else (N_DEV,) + a.shape, a.dtype)
                for t, a in enumerate(self.arrays)]

    def semaphores(self):
        sems = pltpu.SemaphoreType.DMA((N_PEERS * self.n,))
        return [sems, sems, pltpu.SemaphoreType.DMA((self.n,))]

    def _src(self, ins, t, dest):
        return ins[t].at[_slot(dest)] if t < self.n_chunked else ins[t]

    def _copies(self, ins, lands, sems):
        send_sems, recv_sems, local_sems = sems
        x, y, c = me = _place()
        if not self.gather:
            return [], [_peer_copy(self._src(ins, t, peer), lands, t, k, me, peer, send_sems, recv_sems)
                        for t in range(self.n) for k, peer in enumerate(_peers(me))]
        mine = [pltpu.make_async_copy(ins[t], lands[t].at[_slot(me)], local_sems.at[t]) for t in range(self.n)]
        first = []
        for t in range(self.n):
            first.append(self._hop(ins, lands, sems, t, 0, me, (x, y, 1 - c), ins[t]))
            first += [self._hop(ins, lands, sems, t, 1 + j, me, (*chip, c), ins[t]) for j, chip in enumerate(self._chips())]
        return mine, first

    def _chips(self):
        x, y, _ = _place()
        return [(1 - x, y), (x, 1 - y), (1 - x, 1 - y)]

    def _hop(self, ins, lands, sems, t, k, block, to, source=None):
        dst = lands[t].at[_slot(block)]
        return pltpu.make_async_remote_copy(
            src_ref=dst if source is None else source, dst_ref=dst, send_sem=sems[0].at[N_PEERS * t + k],
            recv_sem=sems[1].at[N_PEERS * t + k], device_id=to, device_id_type=MESH)

    def start(self, ins, lands, sems):
        mine, sends = self._copies(ins, lands, sems)
        for cp in mine + sends:
            cp.start()

    def finish(self, ins, lands, sems):
        mine, sends = self._copies(ins, lands, sems)
        x, y, c = me = _place()
        if self.gather:
            sibling = (x, y, 1 - c)
            for j, chip in enumerate(self._chips()):
                for t in range(self.n):
                    self._hop(ins, lands, sems, t, 1 + j, (*chip, c), me).wait_recv()
                    cp = self._hop(ins, lands, sems, t, 4 + j, (*chip, c), sibling)
                    cp.start()
                    sends.append(cp)
            for t in range(self.n):
                self._hop(ins, lands, sems, t, 0, sibling, me).wait_recv()
                for j, chip in enumerate(self._chips()):
                    self._hop(ins, lands, sems, t, 4 + j, (*chip, 1 - c), me).wait_recv()
        else:
            for t in range(self.n):
                for k, peer in enumerate(_peers(me)):
                    _peer_copy(self._src(ins, t, peer), lands, t, k, peer, peer, sems[0], sems[1]).wait_recv()
        for cp in sends:
            cp.wait_send()
        for cp in mine:
            cp.wait()


def _place_own(land, shard, *, name):
    n, K = shard.shape
    tr = n
    for cand in (1024, 736, 512, 256, 128):
        if n % cand == 0:
            tr = cand
            break

    def body(s_ref, land_ref, o_ref):
        del land_ref
        o_ref[0] = s_ref[...]

    return pl.pallas_call(
        body, name=name, grid=(n // tr,),
        in_specs=[pl.BlockSpec((tr, K), lambda i: (i, 0)), _ANY],
        out_specs=pl.BlockSpec((1, tr, K), lambda i: (_slot(_place()), i, 0)),
        out_shape=jax.ShapeDtypeStruct(land.shape, land.dtype),
        input_output_aliases={1: 0},
        compiler_params=_params(("arbitrary",)),
    )(shard, land)


def _rope_tables(T):
    half = HEAD_DIM // 2
    inv_freq = ROPE_THETA ** (-jnp.arange(half, dtype=F32) / half)
    ang = jnp.arange(T, dtype=F32)[:, None] * inv_freq[None, :]
    cos, sin = jnp.cos(ang), jnp.sin(ang)
    return jnp.concatenate([cos, cos], axis=-1), jnp.concatenate([-sin, sin], axis=-1)


def _gain_table(q_gain, k_gain):
    return jnp.tile(jnp.concatenate([q_gain, k_gain], axis=0), (1, 4))[:, None, :]


def _sb_heads(w_in):
    n_in, d_model = w_in.shape
    return (n_in - N_DIL_BLOCKS * GROUP_W - 2 * d_model) // (3 * HEAD_DIM)


def _carry(rides, key, args, call):
    make = rides.get(key) if rides else None
    if make is None:
        return call(None)
    ride, on_landed = make(*args)
    res, lands = call(ride)
    on_landed(lands)
    return res


def _layer_fwd(x, p, cos, sin, tag, rides=None):
    sb_heads = _sb_heads(p["w_in"])
    n_sb_blocks = 3 * sb_heads * HEAD_DIM // GROUP_W
    s = {"x": x}
    s["h"] = _rmsnorm_fwd(x, p["norm1"], name=f"norm1_fwd{tag}")
    s["proj"] = _carry(rides, "proj_fwd", (), lambda ride: _matmul(
        s["h"], p["w_in"], mode="nt", out_dtype=F32, name=f"proj_fwd{tag}", ride=ride))
    *s["qkv_d"], s["qkv_s"] = _prep_fwd(s["proj"], p["gains"], cos, sin, n_sb_blocks, name=f"prep_fwd{tag}")
    outs = [_dil_fwd(s["qkv_d"][g], g, name=f"dil{g}_fwd{tag}") for g in range(N_GROUPS)]
    s["o"], s["ld"] = [o for o, _ in outs], [ld for _, ld in outs]
    s["y_dil"] = _merge_fwd(s["o"], s["ld"], name=f"merge_fwd{tag}")
    s["y_sb32"], s["y_sb"] = _sb_fwd(s["qkv_s"], sb_heads, 0, name=f"sb_fwd{tag}")
    s["mixed"] = _gate_fwd(s["y_dil"], s["y_sb"], p["w_up_dil"], p["w_up_sb"], s["proj"], p["gate_b"],
                           (N_DIL_BLOCKS + n_sb_blocks) * GROUP_W, name=f"gate_fwd{tag}")
    s["x1"] = _matmul(s["mixed"], p["w_out"], mode="nn", out_dtype=F32, epilogue="add", extra=x, name=f"out_fwd{tag}")
    s["h2"] = _rmsnorm_fwd(s["x1"], p["norm2"], name=f"norm2_fwd{tag}")
    s["f"], s["a"] = _carry(rides, "ff1_fwd", (), lambda ride: _matmul(
        s["h2"], p["w_ff1"], mode="nt", out_dtype=BF16, epilogue="relu2", name=f"ff1_fwd{tag}", ride=ride))
    x2 = _carry(rides, "ff2_fwd", (), lambda ride: _matmul(
        s["a"], p["w_ff2"], mode="nn", out_dtype=F32, epilogue="add", extra=s["x1"], name=f"ff2_fwd{tag}", ride=ride))
    return x2, s


def _layer_bwd(dx2, dx2_b, p, s, cos, sin, tag, rides=None, done=None):
    sb_heads = _sb_heads(p["w_in"])
    gate_col0 = N_DIL_BLOCKS * GROUP_W + 3 * sb_heads * HEAD_DIM
    g = {}
    df = _carry(rides, "ff2_bwd", (g, done), lambda ride: _matmul(
        dx2_b, p["w_ff2"], mode="nt", out_dtype=BF16, epilogue="relu2_bwd", extra=s["f"], name=f"ff2_bwd{tag}", ride=ride))
    g["w_ff2"] = _matmul(s["a"], dx2_b, mode="tn", out_dtype=BF16, name=f"ff2_wgrad{tag}")
    dh2 = _carry(rides, "ff1_bwd", (g, done), lambda ride: _matmul(
        df, p["w_ff1"], mode="nn", out_dtype=F32, name=f"ff1_bwd{tag}", ride=ride))
    g["w_ff1"] = _matmul(df, s["h2"], mode="tn", out_dtype=BF16, name=f"ff1_wgrad{tag}")
    dx1, dx1_b, g["norm2"] = _rmsnorm_bwd(s["x1"], p["norm2"], dh2, dx2, name=f"norm2_bwd{tag}")
    dmixed = _matmul(dx1_b, p["w_out"], mode="nt", out_dtype=F32, name=f"out_bwd{tag}")
    g["w_out"] = _matmul(s["mixed"], dx1_b, mode="tn", out_dtype=BF16, name=f"out_wgrad{tag}")
    gate_b = p["gate_b"]
    dup_dil, dgp0, db0 = _gate_bwd(s["y_dil"], p["w_up_dil"], s["proj"], gate_b, dmixed, 0, gate_col0, name=f"gate0_bwd{tag}")
    dup_sb, dgp1, db1 = _gate_bwd(s["y_sb"], p["w_up_sb"], s["proj"], gate_b, dmixed, 1, gate_col0, name=f"gate1_bwd{tag}")
    g["gate_b"] = jnp.concatenate([db0, db1], axis=0)
    dy_dil = _matmul(dup_dil, p["w_up_dil"], mode="nn", out_dtype=F32, name=f"updil_bwd{tag}")
    g["w_up_dil"] = _matmul(dup_dil, s["y_dil"], mode="tn", out_dtype=BF16, name=f"updil_wgrad{tag}")
    dy_sb = _matmul(dup_sb, p["w_up_sb"], mode="nn", out_dtype=BF16, name=f"upsb_bwd{tag}")
    g["w_up_sb"] = _matmul(dup_sb, s["y_sb"], mode="tn", out_dtype=BF16, name=f"upsb_wgrad{tag}")
    dos, dterms = _merge_bwd(s["o"], s["ld"], dy_dil, name=f"merge_bwd{tag}")
    dqkv = [_dil_bwd(s["qkv_d"][grp], dos[grp], s["ld"][grp], dterms[grp], grp, name=f"dil{grp}_bwd{tag}")
            for grp in range(N_GROUPS)]
    dproj_d, dgain = _prep_bwd(s["proj"], dqkv, p["gains"], cos, sin, name=f"prep_bwd{tag}")
    g["q_gain"], g["k_gain"] = dgain[:N_GROUPS, 0], dgain[N_GROUPS:, 0]
    dq_s, dk_s, dv_s = _sb_bwd(s["qkv_s"], s["y_sb32"], dy_sb, sb_heads, 0, name=f"sb_bwd{tag}")
    dproj = jnp.concatenate([dproj_d, dq_s, dk_s, dv_s, dgp0, dgp1], axis=1)
    dh = _carry(rides, "proj_bwd", (g, done), lambda ride: _matmul(
        dproj, p["w_in"], mode="nn", out_dtype=F32, name=f"proj_bwd{tag}", ride=ride))
    g["w_in"] = _matmul(dproj, s["h"], mode="tn", out_dtype=BF16, name=f"proj_wgrad{tag}",
                        tn=_pick(s["h"].shape[1], (2048, 1024, 512, 256, 128)), tk=_pick(s["h"].shape[0], (1024, 512)))
    dx, dx_b, g["norm1"] = _rmsnorm_bwd(s["x"], p["norm1"], dh, dx1, name=f"norm1_bwd{tag}")
    return dx, dx_b, g


def _local_step(x, target, layers, fwd_rides=None, bwd_rides=None):
    depth = len(layers)
    fwd_rides, bwd_rides = (r or [None] * depth for r in (fwd_rides, bwd_rides))
    cos, sin = _rope_tables(x.shape[0])
    saved = []
    for l, p in enumerate(layers):
        x, s = _layer_fwd(x, p, cos, sin, f"_l{l}", fwd_rides[l])
        saved.append(s)
    dx, dx_b, loss = _loss_head(x, target, name="loss_head")
    grads = [None] * depth
    for l in reversed(range(depth)):
        dx, dx_b, grads[l] = _layer_bwd(dx, dx_b, layers[l], saved[l], cos, sin, f"_l{l}", bwd_rides[l], grads)
    return loss, dx, grads


_MATRICES = ("w_in", "w_up_dil", "w_up_sb", "w_out", "w_ff1", "w_ff2")
_TRANSPOSED = ("w_in", "w_up_dil", "w_up_sb", "w_ff1")
_SMALL = ("norm1_g", "norm2_g", "q_norm_g", "k_norm_g")


def _unshard(blocks, name):
    if name == "gate_b":
        return jnp.transpose(blocks, (1, 0, 2)).reshape(blocks.shape[1], N_DEV * blocks.shape[2])
    return blocks.reshape(N_DEV * blocks.shape[1], blocks.shape[2])


def _to_chunks(full, name):
    if name == "gate_b":
        r, cols = full.shape
        return jnp.transpose(full.reshape(r, N_DEV, cols // N_DEV), (1, 0, 2))
    return full.reshape(N_DEV, full.shape[0] // N_DEV, full.shape[1])


def _lane_pad(n):
    return -n % HEAD_DIM


def _pack_small(norm1, norm2, qg, kg):
    flat = jnp.concatenate([t.reshape(-1, HEAD_DIM) for t in (norm1, norm2, qg, kg)], axis=0)
    return jnp.pad(flat, ((0, -flat.shape[0] % 8), (0, 0)))


def _unpack_small(packed, shapes):
    out, row = [], 0
    for shape in shapes:
        rows = math.prod(shape) // HEAD_DIM
        out.append(packed[row:row + rows].reshape(shape))
        row += rows
    return out


def kernel(x, norm1_g, w_in, q_norm_g, k_norm_g, w_up_dil, w_up_sb, gate_b, w_out, norm2_g, w_ff1, w_ff2, loss_target, m_norm1_g, m_w_in, m_q_norm_g, m_k_norm_g, m_w_up_dil, m_w_up_sb, m_gate_b, m_w_out, m_norm2_g, m_w_ff1, m_w_ff2, v_norm1_g, v_w_in, v_q_norm_g, v_k_norm_g, v_w_up_dil, v_w_up_sb, v_gate_b, v_w_out, v_norm2_g, v_w_ff1, v_w_ff2):
    names = ("norm1_g", "w_in", "q_norm_g", "k_norm_g", "w_up_dil", "w_up_sb", "gate_b", "w_out", "norm2_g", "w_ff1", "w_ff2")
    w = dict(zip(names, (norm1_g, w_in, q_norm_g, k_norm_g, w_up_dil, w_up_sb, gate_b, w_out, norm2_g, w_ff1, w_ff2)))
    m = dict(zip(names, (m_norm1_g, m_w_in, m_q_norm_g, m_k_norm_g, m_w_up_dil, m_w_up_sb, m_gate_b, m_w_out, m_norm2_g, m_w_ff1, m_w_ff2)))
    v = dict(zip(names, (v_norm1_g, v_w_in, v_q_norm_g, v_k_norm_g, v_w_up_dil, v_w_up_sb, v_gate_b, v_w_out, v_norm2_g, v_w_ff1, v_w_ff2)))
    depth = norm1_g.shape[0]
    assert depth == 2, "the exchange schedule below is written for two layers"
    sharded = _MATRICES + ("gate_b",)

    def shards(layer, which):
        out = []
        for n in which:
            shard = w[n][layer]
            if n in _TRANSPOSED:
                cols = shard.shape[1]
                padded = jnp.pad(shard, ((0, 0), (0, _lane_pad(cols))))
                out.append(_transpose_cast(padded, name=f"shard_t_{n}_l{layer}")[:cols])
            else:
                out.append(shard if n == "gate_b" else shard.astype(BF16))
        return out

    assert depth == 2, "the schedule of exchanges below is written for two layers"
    layers = [{"norm1": norm1_g[l][None], "norm2": norm2_g[l][None], "gains": _gain_table(q_norm_g[l], k_norm_g[l])}
              for l in range(depth)]
    rest = sharded[1:]

    def gather_on(layer, which):
        def make():
            ride = _Ride([], shards(layer, which))
            return ride, lambda lands: layers[layer].update({n: _unshard(b, n) for n, b in zip(which, lands)})
        return make

    (w_in_0,) = _exchange([], shards(0, sharded[:1]), name="gather_w_in_l0")
    layers[0]["w_in"] = _unshard(w_in_0, "w_in")
    fwd_rides = [{"proj_fwd": gather_on(0, rest), "ff1_fwd": gather_on(1, sharded[:1]), "ff2_fwd": gather_on(1, rest)}, None]

    sent, landed = {}, {}

    def exchange_on(items):
        def make(g, done):
            chunks = [_to_chunks((g if done[layer] is None else done[layer])[n], n) for layer, n in items]

            def on_landed(lands):
                for item, chunk, land in zip(items, chunks, lands):
                    sent[item], landed[item] = chunk, land
            return _Ride(chunks, []), on_landed
        return make

    others = ("w_ff1", "w_out", "w_up_dil", "w_up_sb", "gate_b")
    bwd_rides = [
        {"ff2_bwd": exchange_on([(1, "w_in")]), "ff1_bwd": exchange_on([(0, "w_ff2")]),
         "proj_bwd": exchange_on([(0, n) for n in others])},
        {"ff1_bwd": exchange_on([(1, "w_ff2")]), "proj_bwd": exchange_on([(1, n) for n in others])},
    ]
    loss_part, dx, grads = _local_step(x[0], loss_target[0], layers, fwd_rides, bwd_rides)
    loss = lax.psum(loss_part[0, 0], ("x", "y", "c"))

    small = _pack_small(jnp.concatenate([g["norm1"] for g in grads]), jnp.concatenate([g["norm2"] for g in grads]),
                        jnp.stack([g["q_gain"] for g in grads]), jnp.stack([g["k_gain"] for g in grads]))
    sent[(0, "w_in")] = _to_chunks(grads[0]["w_in"], "w_in")
    landed[(0, "w_in")], small_parts = _exchange([sent[(0, "w_in")]], [small], name="exchange_last")

    out = {}
    for n in sharded:
        rows, cols = depth * w[n].shape[1], w[n].shape[2]
        own, parts = ([moved[(l, n)] for l in range(depth)] for moved in (sent, landed))
        state = [t.reshape(rows, cols) for t in (w[n], m[n], v[n])]
        if n in _TRANSPOSED:
            pad = _lane_pad(cols)
            res = _reduce_adamw_t(parts, own, *(jnp.pad(t, ((0, 0), (0, pad))) for t in state), name=f"adamw_{n}")
            res = [t[:, :cols] for t in res]
        else:
            if w[n].shape[1] % 8:
                own, parts = [jnp.concatenate(own, axis=1)], [jnp.concatenate(parts, axis=1)]
            res = _reduce_adamw(parts, own, *state, name=f"adamw_{n}")
        out[n] = [t.reshape(w[n].shape) for t in res]
    small_res = _reduce_adamw([small_parts], [small], _pack_small(*(w[n] for n in _SMALL)),
                              _pack_small(*(m[n] for n in _SMALL)), _pack_small(*(v[n] for n in _SMALL)),
                              own_chunked=False, name="adamw_small")
    small_shapes = [w[n].shape for n in _SMALL]
    for k, t in enumerate(small_res):
        for n, arr in zip(_SMALL, _unpack_small(t, small_shapes)):
            out.setdefault(n, [None] * 4)[k] = arr
    return (loss, dx[None], *(out[n][0] for n in names), *(out[n][1] for n in names), *(out[n][2] for n in names),
            *(out[n][3] for n in names))
```

```python
import functools
import math

import jax
import jax.numpy as jnp
from jax import lax
from jax.experimental import pallas as pl
from jax.experimental.pallas import tpu as pltpu

F32 = jnp.float32
BF16 = jnp.bfloat16

HEAD_DIM = 128
BLOCK = 128
N_GROUPS = 3
DILATIONS = (1, 4, 16)
ROPE_THETA = 10000.0
EPS = 1e-6
ADAM_LR = 0.001
ADAM_B1 = 0.9
ADAM_B2 = 0.999
ADAM_EPS = 1e-08
ADAM_WD = 0.01
ADAM_STEP = 10
N_DEV = 8
MESH = pl.DeviceIdType.MESH
VMEM_LIMIT_BYTES = 48 * 1024 * 1024
NEG = -1e30


def _params(sem):
    return pltpu.CompilerParams(dimension_semantics=sem, vmem_limit_bytes=VMEM_LIMIT_BYTES)


def _pick(n, options):
    for o in options:
        if n % o == 0:
            return o
    return n


_DIMS = {"nn": (((1,), (0,)), ((), ())), "nt": (((1,), (1,)), ((), ())), "tn": (((0,), (0,)), ((), ()))}


def _matmul(a, b, *, mode, out_dtype, name, epilogue=None, extra=None, tm=None, tn=None, tk=None, ride=None):
    if mode == "nn":
        (M, K), (K2, N) = a.shape, b.shape
    elif mode == "nt":
        (M, K), (N, K2) = a.shape, b.shape
    else:
        (K, M), (K2, N) = a.shape, b.shape
    assert K == K2, (a.shape, b.shape, mode)
    tm = tm or _pick(M, (1024, 512, 256, 128))
    tn = tn or _pick(N, (512, 256, 128))
    tk = tk or _pick(K, (2048, 2944, 1024, 512, 256, 128))
    nk = K // tk
    dims = _DIMS[mode]
    n_extra = 0 if extra is None else 1
    n_out = 2 if epilogue == "relu2" else 1
    n_ride = 0 if ride is None else ride.n
    grid = (M // tm, N // tn, nk)

    def body(*refs):
        a_ref, b_ref = refs[0], refs[1]
        extra_ref = refs[2] if n_extra else None
        first_out = 2 + n_extra + n_ride
        outs = refs[first_out:first_out + n_out]
        acc_ref = refs[first_out + n_out + n_ride] if nk > 1 else None
        if ride is not None:
            ride_refs = (refs[2 + n_extra:first_out], refs[first_out + n_out:first_out + n_out + n_ride], refs[-3:])
            at = [pl.program_id(d) for d in range(3)]

            @pl.when(jnp.logical_and(jnp.logical_and(at[0] == 0, at[1] == 0), at[2] == 0))
            def _():
                ride.start(*ride_refs)

        def finish(acc):
            if epilogue is None:
                outs[0][...] = acc.astype(outs[0].dtype)
            elif epilogue == "add":
                outs[0][...] = (acc + extra_ref[...]).astype(outs[0].dtype)
            elif epilogue == "relu2":
                r = jnp.maximum(acc, 0.0)
                outs[0][...] = r.astype(outs[0].dtype)
                outs[1][...] = (r * r).astype(outs[1].dtype)
            else:
                outs[0][...] = (acc * (2.0 * extra_ref[...].astype(F32))).astype(outs[0].dtype)

        prod = lax.dot_general(a_ref[...], b_ref[...], dims, preferred_element_type=F32)
        if nk == 1:
            finish(prod)
        else:
            k = pl.program_id(2)

            @pl.when(k == 0)
            def _():
                acc_ref[...] = prod

            @pl.when(k > 0)
            def _():
                acc_ref[...] += prod

            @pl.when(k == nk - 1)
            def _():
                finish(acc_ref[...])

        if ride is not None:
            @pl.when(jnp.logical_and(jnp.logical_and(at[0] == grid[0] - 1, at[1] == grid[1] - 1), at[2] == nk - 1))
            def _():
                ride.finish(*ride_refs)

    if mode == "nn":
        a_spec = pl.BlockSpec((tm, tk), lambda i, j, k: (i, k))
        b_spec = pl.BlockSpec((tk, tn), lambda i, j, k: (k, j))
    elif mode == "nt":
        a_spec = pl.BlockSpec((tm, tk), lambda i, j, k: (i, k))
        b_spec = pl.BlockSpec((tn, tk), lambda i, j, k: (j, k))
    else:
        a_spec = pl.BlockSpec((tk, tm), lambda i, j, k: (k, i))
        b_spec = pl.BlockSpec((tk, tn), lambda i, j, k: (k, j))
    o_spec = pl.BlockSpec((tm, tn), lambda i, j, k: (i, j))
    in_specs = [a_spec, b_spec] + ([o_spec] if n_extra else []) + [_ANY] * n_ride
    out_shape = [jax.ShapeDtypeStruct((M, N), out_dtype)] * n_out + (ride.land_shapes() if ride else [])
    res = pl.pallas_call(
        body,
        name=name,
        grid=grid,
        in_specs=in_specs,
        out_specs=[o_spec] * n_out + [_ANY] * n_ride,
        out_shape=out_shape,
        scratch_shapes=([pltpu.VMEM((tm, tn), F32)] if nk > 1 else []) + (ride.semaphores() if ride else []),
        compiler_params=_params(("arbitrary",) * 3 if ride else ("parallel", "parallel", "arbitrary")),
    )(a, b, *([extra] if n_extra else []), *(ride.arrays if ride else []))
    if ride is None:
        return res if n_out > 1 else res[0]
    return (res[:n_out] if n_out > 1 else res[0]), res[n_out:]


def _rmsnorm_fwd(x, g, *, name):
    T, D = x.shape
    tm = _pick(T, (512, 256, 128))

    def body(x_ref, g_ref, o_ref):
        xf = x_ref[...]
        r = lax.rsqrt(jnp.mean(xf * xf, axis=-1, keepdims=True) + EPS)
        o_ref[...] = ((xf * r) * g_ref[...]).astype(o_ref.dtype)

    return pl.pallas_call(
        body,
        name=name,
        grid=(T // tm,),
        in_specs=[pl.BlockSpec((tm, D), lambda i: (i, 0)), pl.BlockSpec((1, D), lambda i: (0, 0))],
        out_specs=pl.BlockSpec((tm, D), lambda i: (i, 0)),
        out_shape=jax.ShapeDtypeStruct((T, D), BF16),
        compiler_params=_params(("parallel",)),
    )(x, g)


def _rmsnorm_bwd(x, g, dh, dres, *, name):
    T, D = x.shape
    tm = _pick(T, (256, 128))

    def body(x_ref, g_ref, dh_ref, dres_ref, dx_ref, dxb_ref, dg_ref):
        i = pl.program_id(0)
        xf = x_ref[...]
        r = lax.rsqrt(jnp.mean(xf * xf, axis=-1, keepdims=True) + EPS)
        y = xf * r
        dh_v = dh_ref[...]
        dy = dh_v * g_ref[...]
        c = jnp.mean(dy * y, axis=-1, keepdims=True)
        dx = r * (dy - y * c) + dres_ref[...]
        dx_ref[...] = dx
        dxb_ref[...] = dx.astype(BF16)
        part = jnp.sum(dh_v * y, axis=0, keepdims=True)

        @pl.when(i == 0)
        def _():
            dg_ref[...] = part

        @pl.when(i > 0)
        def _():
            dg_ref[...] += part

    row = pl.BlockSpec((tm, D), lambda i: (i, 0))
    vec = pl.BlockSpec((1, D), lambda i: (0, 0))
    return pl.pallas_call(
        body,
        name=name,
        grid=(T // tm,),
        in_specs=[row, vec, row, row],
        out_specs=[row, row, vec],
        out_shape=[jax.ShapeDtypeStruct((T, D), F32), jax.ShapeDtypeStruct((T, D), BF16), jax.ShapeDtypeStruct((1, D), F32)],
        compiler_params=_params(("arbitrary",)),
    )(x, g, dh, dres)


GROUP_W = 4 * HEAD_DIM
N_DIL_BLOCKS = 3 * N_GROUPS
N_NORMED = 2 * N_GROUPS


def _kind_of_group(j, g):
    return jnp.clip((j - g) // N_GROUPS, 0, 2)


def _head_rstd(xh):
    return lax.rsqrt(jnp.mean(xh * xh, axis=-1, keepdims=True) + EPS)


def _prep_fwd(proj, gains, cos, sin, n_sb_blocks, *, name):
    T = proj.shape[0]
    tm = _pick(T, (512, 256, 128))

    def body(p_ref, gain_ref, cos_ref, sin_ref, o0_ref, o1_ref, o2_ref, os_ref):
        j = pl.program_id(1)
        for g, o_ref in enumerate((o0_ref, o1_ref, o2_ref)):
            @pl.when(jnp.logical_and(j % N_GROUPS == g, j < N_NORMED))
            def _():
                cos_v, sin_v = cos_ref[...], sin_ref[...]
                for hh in range(4):
                    sl = slice(hh * HEAD_DIM, (hh + 1) * HEAD_DIM)
                    xh = p_ref[:, sl]
                    y = (xh * _head_rstd(xh)) * gain_ref[0, :, sl]
                    o_ref[:, sl] = (y * cos_v + pltpu.roll(y, HEAD_DIM // 2, 1) * sin_v).astype(BF16)

            @pl.when(j == N_NORMED + g)
            def _():
                o_ref[...] = p_ref[...].astype(BF16)

        @pl.when(j >= N_DIL_BLOCKS)
        def _():
            os_ref[...] = p_ref[...].astype(BF16)

    def group_spec(g):
        return pl.BlockSpec((tm, GROUP_W), lambda i, j: (i, _kind_of_group(j, g)))

    return pl.pallas_call(
        body,
        name=name,
        grid=(T // tm, N_DIL_BLOCKS + n_sb_blocks),
        in_specs=[
            pl.BlockSpec((tm, GROUP_W), lambda i, j: (i, j)),
            pl.BlockSpec((1, 1, GROUP_W), lambda i, j: (jnp.minimum(j, N_NORMED - 1), 0, 0)),
            pl.BlockSpec((tm, HEAD_DIM), lambda i, j: (i, 0)),
            pl.BlockSpec((tm, HEAD_DIM), lambda i, j: (i, 0)),
        ],
        out_specs=[group_spec(0), group_spec(1), group_spec(2),
                   pl.BlockSpec((tm, GROUP_W), lambda i, j: (i, jnp.maximum(j - N_DIL_BLOCKS, 0)))],
        out_shape=[jax.ShapeDtypeStruct((T, 3 * GROUP_W), BF16)] * N_GROUPS
        + [jax.ShapeDtypeStruct((T, n_sb_blocks * GROUP_W), BF16)],
        compiler_params=_params(("parallel", "arbitrary")),
    )(proj, gains, cos, sin)


def _prep_bwd(proj, dqkv, gains, cos, sin, *, name):
    T = proj.shape[0]
    tm = _pick(T, (512, 256, 128))

    def body(p_ref, d0_ref, d1_ref, d2_ref, gain_ref, cos_ref, sin_ref, o_ref, dgain_ref):
        j, i = pl.program_id(0), pl.program_id(1)

        def normed_bwd(d_ref):
            cos_v, sin_v = cos_ref[...], sin_ref[...]
            part = jnp.zeros((1, HEAD_DIM), F32)
            for hh in range(4):
                sl = slice(hh * HEAD_DIM, (hh + 1) * HEAD_DIM)
                xh = p_ref[:, sl]
                r = _head_rstd(xh)
                y0 = xh * r
                d_out = d_ref[:, sl]
                d_yg = d_out * cos_v + pltpu.roll(d_out * sin_v, HEAD_DIM // 2, 1)
                part = part + jnp.sum(d_yg * y0, axis=0, keepdims=True)
                dy0 = d_yg * gain_ref[0, :, sl]
                c = jnp.mean(dy0 * y0, axis=-1, keepdims=True)
                o_ref[:, sl] = (r * (dy0 - y0 * c)).astype(BF16)

            @pl.when(i == 0)
            def _():
                dgain_ref[0] = part

            @pl.when(i > 0)
            def _():
                dgain_ref[0] += part

        for g, d_ref in enumerate((d0_ref, d1_ref, d2_ref)):
            @pl.when(jnp.logical_and(j % N_GROUPS == g, j < N_NORMED))
            def _():
                normed_bwd(d_ref)

            @pl.when(j == N_NORMED + g)
            def _():
                o_ref[...] = d_ref[...].astype(BF16)

    gain_row = lambda j, i: (jnp.minimum(j, N_NORMED - 1), 0, 0)

    def grad_spec(g):
        return pl.BlockSpec((tm, GROUP_W), lambda j, i: (i, _kind_of_group(j, g)))

    return pl.pallas_call(
        body,
        name=name,
        grid=(N_DIL_BLOCKS, T // tm),
        in_specs=[
            pl.BlockSpec((tm, GROUP_W), lambda j, i: (i, j)),
            grad_spec(0), grad_spec(1), grad_spec(2),
            pl.BlockSpec((1, 1, GROUP_W), gain_row),
            pl.BlockSpec((tm, HEAD_DIM), lambda j, i: (i, 0)),
            pl.BlockSpec((tm, HEAD_DIM), lambda j, i: (i, 0)),
        ],
        out_specs=[pl.BlockSpec((tm, GROUP_W), lambda j, i: (i, j)), pl.BlockSpec((1, 1, HEAD_DIM), gain_row)],
        out_shape=[jax.ShapeDtypeStruct((T, N_DIL_BLOCKS * GROUP_W), BF16),
                   jax.ShapeDtypeStruct((2 * N_GROUPS, 1, HEAD_DIM), F32)],
        compiler_params=_params(("arbitrary", "arbitrary")),
    )(proj, *dqkv, gains, cos, sin)


def _nt(a, b):
    return lax.dot_general(a, b, _DIMS["nt"], preferred_element_type=F32)


def _tn(a, b):
    return lax.dot_general(a, b, _DIMS["tn"], preferred_element_type=F32)


def _nn(a, b):
    return jnp.dot(a, b, preferred_element_type=F32)


def _window_masks():
    row = lax.broadcasted_iota(jnp.int32, (BLOCK, BLOCK), 0)
    col = lax.broadcasted_iota(jnp.int32, (BLOCK, BLOCK), 1)
    return row >= col, col >= row


def _heads():
    return [slice(hh * HEAD_DIM, (hh + 1) * HEAD_DIM) for hh in range(GROUP_W // HEAD_DIM)]


def _dil_fwd(qkv, g, *, name):
    T = qkv.shape[0]
    r = DILATIONS[g]
    L = T // r
    nb = L // BLOCK
    scale = 1.0 / math.sqrt(HEAD_DIM)
    view = qkv.reshape(L, r * 3 * GROUP_W)

    def body(q_ref, kc_ref, kp_ref, vc_ref, vp_ref, o_ref, ld_ref):
        n = pl.program_id(1)
        m_cur, m_prev = _window_masks()
        m_prev = jnp.logical_and(m_prev, n > 0)
        for sl in _heads():
            q = q_ref[:, sl]
            s_c = jnp.where(m_cur, _nt(q, kc_ref[:, sl]) * scale, NEG)
            s_p = jnp.where(m_prev, _nt(q, kp_ref[:, sl]) * scale, NEG)
            m = jnp.maximum(jnp.max(s_c, axis=-1, keepdims=True), jnp.max(s_p, axis=-1, keepdims=True))
            p_c = jnp.exp(s_c - m)
            p_p = jnp.exp(s_p - m)
            l = jnp.sum(p_c, axis=-1, keepdims=True) + jnp.sum(p_p, axis=-1, keepdims=True)
            inv = 1.0 / l
            o_ref[:, sl] = _nn((p_c * inv).astype(BF16), vc_ref[:, sl]) + _nn((p_p * inv).astype(BF16), vp_ref[:, sl])
            ld_ref[:, sl] = jnp.broadcast_to(m + jnp.log(l), (BLOCK, HEAD_DIM))

    blk = (BLOCK, GROUP_W)
    cur = lambda kind: pl.BlockSpec(blk, lambda c, n: (n, 3 * c + kind))
    prev = lambda kind: pl.BlockSpec(blk, lambda c, n: (jnp.maximum(n - 1, 0), 3 * c + kind))
    out_spec = pl.BlockSpec(blk, lambda c, n: (n, c))
    o, ld = pl.pallas_call(
        body,
        name=name,
        grid=(r, nb),
        in_specs=[cur(0), cur(1), prev(1), cur(2), prev(2)],
        out_specs=[out_spec, out_spec],
        out_shape=[jax.ShapeDtypeStruct((L, r * GROUP_W), F32)] * 2,
        compiler_params=_params(("parallel", "arbitrary")),
    )(view, view, view, view, view)
    return o.reshape(T, GROUP_W), ld.reshape(T, GROUP_W)


def _dil_bwd(qkv, do, ld, dterm, g, *, name):
    T = qkv.shape[0]
    r = DILATIONS[g]
    L = T // r
    nb = L // BLOCK
    scale = 1.0 / math.sqrt(HEAD_DIM)
    view = qkv.reshape(L, r * 3 * GROUP_W)
    do_v, ld_v, dt_v = (t.reshape(L, r * GROUP_W) for t in (do, ld, dterm))

    def body(q_ref, qn_ref, kc_ref, kp_ref, vc_ref, vp_ref, do_ref, don_ref, ld_ref, ldn_ref, dt_ref, dtn_ref, out_ref):
        n = pl.program_id(1)
        m_cur, m_prev = _window_masks()
        has_prev, has_next = jnp.logical_and(m_prev, n > 0), jnp.logical_and(m_prev, n < nb - 1)

        def tile(q, k, v, do_t, ld_t, dt_t, mask):
            s = _nt(q, k) * scale
            p = jnp.where(mask, jnp.exp(s - ld_t[:, 0:1]), 0.0)
            ds = p * (_nt(do_t, v) + dt_t[:, 0:1]) * scale
            return p.astype(BF16), ds.astype(BF16)

        for hh, sl in enumerate(_heads()):
            kc, vc, kp = kc_ref[:, sl], vc_ref[:, sl], kp_ref[:, sl]
            q, do_t, ld_t, dt_t = q_ref[:, sl], do_ref[:, sl], ld_ref[:, sl], dt_ref[:, sl]
            p_cc, ds_cc = tile(q, kc, vc, do_t, ld_t, dt_t, m_cur)
            _, ds_cp = tile(q, kp, vp_ref[:, sl], do_t, ld_t, dt_t, has_prev)
            qn, don = qn_ref[:, sl], don_ref[:, sl]
            p_nc, ds_nc = tile(qn, kc, vc, don, ldn_ref[:, sl], dtn_ref[:, sl], has_next)
            at = lambda kind: slice(kind * GROUP_W + hh * HEAD_DIM, kind * GROUP_W + (hh + 1) * HEAD_DIM)
            out_ref[:, at(0)] = _nn(ds_cc, kc) + _nn(ds_cp, kp)
            out_ref[:, at(1)] = _tn(ds_cc, q) + _tn(ds_nc, qn)
            out_ref[:, at(2)] = _tn(p_cc, do_t) + _tn(p_nc, don)

    blk = (BLOCK, GROUP_W)
    qkv_spec = lambda kind, shift: pl.BlockSpec(blk, lambda c, n: (jnp.clip(n + shift, 0, nb - 1), 3 * c + kind))
    row_spec = lambda shift: pl.BlockSpec(blk, lambda c, n: (jnp.clip(n + shift, 0, nb - 1), c))
    out = pl.pallas_call(
        body,
        name=name,
        grid=(r, nb),
        in_specs=[qkv_spec(0, 0), qkv_spec(0, 1), qkv_spec(1, 0), qkv_spec(1, -1), qkv_spec(2, 0), qkv_spec(2, -1),
                  row_spec(0), row_spec(1), row_spec(0), row_spec(1), row_spec(0), row_spec(1)],
        out_specs=pl.BlockSpec((BLOCK, 3 * GROUP_W), lambda c, n: (n, c)),
        out_shape=jax.ShapeDtypeStruct((L, r * 3 * GROUP_W), F32),
        compiler_params=_params(("parallel", "arbitrary")),
    )(view, view, view, view, view, view, do_v, do_v, ld_v, ld_v, dt_v, dt_v)
    return out.reshape(T, 3 * GROUP_W)


def _group_weights(ld_refs):
    lds = [r[...] for r in ld_refs]
    m = jnp.maximum(jnp.maximum(lds[0], lds[1]), lds[2])
    es = [jnp.exp(v - m) for v in lds]
    inv = 1.0 / (es[0] + es[1] + es[2])
    return [e * inv for e in es]


def _merge_fwd(os_, lds, *, name):
    T = os_[0].shape[0]
    tm = _pick(T, (1024, 512, 256, 128))

    def body(o0, o1, o2, l0, l1, l2, y_ref):
        w = _group_weights((l0, l1, l2))
        y_ref[...] = (w[0] * o0[...] + w[1] * o1[...] + w[2] * o2[...]).astype(BF16)

    spec = pl.BlockSpec((tm, GROUP_W), lambda i: (i, 0))
    return pl.pallas_call(
        body, name=name, grid=(T // tm,), in_specs=[spec] * 6, out_specs=spec,
        out_shape=jax.ShapeDtypeStruct((T, GROUP_W), BF16), compiler_params=_params(("parallel",)),
    )(*os_, *lds)


def _merge_bwd(os_, lds, dy, *, name):
    T = dy.shape[0]
    tm = _pick(T, (512, 256, 128))

    def body(o0, o1, o2, l0, l1, l2, dy_ref, do0, do1, do2, dt0, dt1, dt2):
        w = _group_weights((l0, l1, l2))
        dy_v = dy_ref[...]
        y = w[0] * o0[...] + w[1] * o1[...] + w[2] * o2[...]
        prod = dy_v * y
        for hh in range(4):
            sl = slice(hh * HEAD_DIM, (hh + 1) * HEAD_DIM)
            s = jnp.sum(prod[:, sl], axis=-1, keepdims=True)
            for wg, dt in zip(w, (dt0, dt1, dt2)):
                dt[:, sl] = -wg[:, sl] * s
        for wg, do in zip(w, (do0, do1, do2)):
            do[...] = (wg * dy_v).astype(BF16)

    spec = pl.BlockSpec((tm, GROUP_W), lambda i: (i, 0))
    outs = pl.pallas_call(
        body, name=name, grid=(T // tm,), in_specs=[spec] * 7, out_specs=[spec] * 6,
        out_shape=[jax.ShapeDtypeStruct((T, GROUP_W), BF16)] * 3 + [jax.ShapeDtypeStruct((T, GROUP_W), F32)] * 3,
        compiler_params=_params(("parallel",)),
    )(*os_, *lds, dy)
    return outs[:3], outs[3:]


SB_ROWS = 512
SB_KEYS = 256


def _sum_matrix(inclusive):
    j = lax.broadcasted_iota(jnp.int32, (2 * BLOCK, 2 * BLOCK), 0) % BLOCK
    s = lax.broadcasted_iota(jnp.int32, (2 * BLOCK, 2 * BLOCK), 1)
    later = (j >= s) if inclusive else (j > s)
    return jnp.logical_or(s >= BLOCK, later).astype(BF16)


def _block_sums(x, mat):
    hi = x.astype(BF16)
    lo = (x - hi.astype(F32)).astype(BF16)
    r = _nn(jnp.concatenate([hi, lo], axis=1), mat)
    return r[:, :BLOCK], r[:, BLOCK:]


def _log_terms(z):
    t = jnp.log(1.0 + jnp.exp(-jnp.abs(z)))
    return -(jnp.maximum(z, 0.0) + t), jnp.minimum(z, 0.0) - t


SB_DEAD = -105.0


def _sb_alive(n_chunks, state):
    return jnp.logical_and(state[0] < n_chunks, jnp.max(state[1]) > SB_DEAD)


def _causal_mask(rows, cols, first_col):
    row = lax.broadcasted_iota(jnp.int32, (rows, cols), 0)
    col = lax.broadcasted_iota(jnp.int32, (rows, cols), 1)
    return col + first_col < row


def _sb_fwd(qkv, n_heads, col0, *, name):
    T = qkv.shape[0]
    tq = _pick(T, (SB_ROWS, BLOCK))
    kc = _pick(tq, (SB_KEYS, BLOCK))
    nq, nsub, per_tile = T // tq, kc // BLOCK, tq // kc
    scale = 1.0 / math.sqrt(HEAD_DIM)

    def body(q_ref, k_ref, v_ref, o_ref, ob_ref):
        i = pl.program_id(1)
        q = q_ref[...]
        mat = _sum_matrix(False)

        def chunk(j, carry, acc, mask):
            rows = pl.ds(pl.multiple_of(j * kc, kc), kc)
            z = _nt(q, k_ref[rows, :]) * scale
            lk, ls = _log_terms(z)
            if mask is not None:
                lk = jnp.where(mask, lk, 0.0)
            a = []
            for b in reversed(range(nsub)):
                sl = slice(b * BLOCK, (b + 1) * BLOCK)
                later, total = _block_sums(lk[:, sl], mat)
                a.append(jnp.exp(ls[:, sl] + (later + carry)))
                carry = carry + total
            a = jnp.concatenate(a[::-1], axis=1)
            if mask is not None:
                a = jnp.where(mask, a, 0.0)
            return carry, acc + _nn(a.astype(BF16), v_ref[rows, :])

        carry = acc = jnp.zeros((tq, HEAD_DIM), F32)
        for d in reversed(range(per_tile)):
            carry, acc = chunk(i * per_tile + d, carry, acc, _causal_mask(tq, kc, d * kc))

        def step(state):
            carry, acc = chunk(i * per_tile - 1 - state[0], state[1], state[2], None)
            return state[0] + 1, carry, acc

        _, carry, acc = lax.while_loop(functools.partial(_sb_alive, i * per_tile), step, (0, carry, acc))
        o_ref[...] = acc
        ob_ref[...] = acc.astype(BF16)

    blk = (tq, HEAD_DIM)
    out_spec = pl.BlockSpec(blk, lambda h, i: (i, h))
    return pl.pallas_call(
        body,
        name=name,
        grid=(n_heads, nq),
        in_specs=[
            pl.BlockSpec(blk, lambda h, i: (i, col0 + h)),
            pl.BlockSpec((T, HEAD_DIM), lambda h, i: (0, col0 + n_heads + h)),
            pl.BlockSpec((T, HEAD_DIM), lambda h, i: (0, col0 + 2 * n_heads + h)),
        ],
        out_specs=[out_spec, out_spec],
        out_shape=[jax.ShapeDtypeStruct((T, n_heads * HEAD_DIM), F32), jax.ShapeDtypeStruct((T, n_heads * HEAD_DIM), BF16)],
        compiler_params=_params(("parallel", "arbitrary")),
    )(qkv, qkv, qkv)


def _sb_bwd(qkv, o32, do, n_heads, col0, *, name):
    T = qkv.shape[0]
    tq = _pick(T, (SB_ROWS, BLOCK))
    kc = _pick(tq, (SB_KEYS, BLOCK))
    nq, nsub, per_tile = T // tq, kc // BLOCK, tq // kc
    scale = 1.0 / math.sqrt(HEAD_DIM)

    def body(q_ref, k_ref, v_ref, o_ref, do_ref, dq_ref, dk_ref, dv_ref, dk_acc, dv_acc):
        i = pl.program_id(1)

        @pl.when(i == 0)
        def _():
            dk_acc[...] = jnp.zeros_like(dk_acc)
            dv_acc[...] = jnp.zeros_like(dv_acc)

        q, do_t = q_ref[...], do_ref[...]
        delta = jnp.broadcast_to(jnp.sum(do_t.astype(F32) * o_ref[...], axis=-1, keepdims=True), (tq, HEAD_DIM))
        mat, mat_incl = _sum_matrix(False), _sum_matrix(True)

        def chunk(j, carry_b, carry_g, dq, mask):
            rows = pl.ds(pl.multiple_of(j * kc, kc), kc)
            k_t, v_t = k_ref[rows, :], v_ref[rows, :]
            z = _nt(q, k_t) * scale
            lk, ls = _log_terms(z)
            if mask is not None:
                lk = jnp.where(mask, lk, 0.0)
            d_a = _nt(do_t, v_t)
            a_parts, dz_parts = [], []
            for b in reversed(range(nsub)):
                sl = slice(b * BLOCK, (b + 1) * BLOCK)
                later, total = _block_sums(lk[:, sl], mat)
                a = jnp.exp(ls[:, sl] + (later + carry_b))
                carry_b = carry_b + total
                if mask is not None:
                    a = jnp.where(mask[:, sl], a, 0.0)
                a_b = a.astype(BF16)
                g = a_b.astype(F32) * d_a[:, sl]
                from_here, total_g = _block_sums(g, mat_incl)
                before = delta - (from_here + carry_g)
                carry_g = carry_g + total_g
                sig = jnp.exp(ls[:, sl])
                dz = (g - sig * (g + before)) * scale
                a_parts.append(a_b)
                dz_parts.append(dz)
            dz = jnp.concatenate(dz_parts[::-1], axis=1)
            if mask is not None:
                dz = jnp.where(mask, dz, 0.0)
            dz_b = dz.astype(BF16)
            dk_acc[rows, :] += _tn(dz_b, q)
            dv_acc[rows, :] += _tn(jnp.concatenate(a_parts[::-1], axis=1), do_t)
            return carry_b, carry_g, dq + _nn(dz_b, k_t)

        zero = jnp.zeros((tq, HEAD_DIM), F32)
        state = (zero, zero, zero)
        for d in reversed(range(per_tile)):
            state = chunk(i * per_tile + d, *state, _causal_mask(tq, kc, d * kc))

        def step(st):
            return (st[0] + 1,) + chunk(i * per_tile - 1 - st[0], st[1], st[2], st[3], None)

        state = lax.while_loop(functools.partial(_sb_alive, i * per_tile), step, (0,) + state)
        dq_ref[...] = state[3].astype(BF16)

        @pl.when(i == nq - 1)
        def _():
            dk_ref[...] = dk_acc[...].astype(BF16)
            dv_ref[...] = dv_acc[...].astype(BF16)

    blk = (tq, HEAD_DIM)
    full = (T, HEAD_DIM)
    dshape = jax.ShapeDtypeStruct((T, n_heads * HEAD_DIM), BF16)
    return pl.pallas_call(
        body,
        name=name,
        grid=(n_heads, nq),
        in_specs=[
            pl.BlockSpec(blk, lambda h, i: (i, col0 + h)),
            pl.BlockSpec(full, lambda h, i: (0, col0 + n_heads + h)),
            pl.BlockSpec(full, lambda h, i: (0, col0 + 2 * n_heads + h)),
            pl.BlockSpec(blk, lambda h, i: (i, h)),
            pl.BlockSpec(blk, lambda h, i: (i, h)),
        ],
        out_specs=[pl.BlockSpec(blk, lambda h, i: (i, h)), pl.BlockSpec(full, lambda h, i: (0, h)),
                   pl.BlockSpec(full, lambda h, i: (0, h))],
        out_shape=[dshape, dshape, dshape],
        scratch_shapes=[pltpu.VMEM(full, F32), pltpu.VMEM(full, F32)],
        compiler_params=_params(("arbitrary", "arbitrary")),
    )(qkv, qkv, qkv, o32, do)


def _gate_fwd(y_dil, y_sb, w_up_dil, w_up_sb, proj, gate_b, gate_col0, *, name):
    T, D = y_dil.shape[0], w_up_dil.shape[0]
    tm = _pick(T, (512, 256, 128))
    tn = _pick(D, (512, 256, 128))
    c0, nbr = gate_col0 // tn, D // tn

    def body(yd_ref, ys_ref, wd_ref, ws_ref, gp0_ref, gp1_ref, b_ref, o_ref):
        g0 = jax.nn.sigmoid(gp0_ref[...] + b_ref[0:1, :])
        g1 = jax.nn.sigmoid(gp1_ref[...] + b_ref[1:2, :])
        o_ref[...] = (g0 * _nt(yd_ref[...], wd_ref[...]) + g1 * _nt(ys_ref[...], ws_ref[...])).astype(BF16)

    return pl.pallas_call(
        body,
        name=name,
        grid=(T // tm, nbr),
        in_specs=[
            pl.BlockSpec((tm, y_dil.shape[1]), lambda i, j: (i, 0)),
            pl.BlockSpec((tm, y_sb.shape[1]), lambda i, j: (i, 0)),
            pl.BlockSpec((tn, w_up_dil.shape[1]), lambda i, j: (j, 0)),
            pl.BlockSpec((tn, w_up_sb.shape[1]), lambda i, j: (j, 0)),
            pl.BlockSpec((tm, tn), lambda i, j: (i, c0 + j)),
            pl.BlockSpec((tm, tn), lambda i, j: (i, c0 + nbr + j)),
            pl.BlockSpec((2, tn), lambda i, j: (0, j)),
        ],
        out_specs=pl.BlockSpec((tm, tn), lambda i, j: (i, j)),
        out_shape=jax.ShapeDtypeStruct((T, D), BF16),
        compiler_params=_params(("parallel", "parallel")),
    )(y_dil, y_sb, w_up_dil, w_up_sb, proj, proj, gate_b)


def _gate_bwd(y, w_up, proj, gate_b, dmixed, branch, gate_col0, *, name):
    T, D = y.shape[0], w_up.shape[0]
    tm = _pick(T, (512, 256, 128))
    tn = _pick(D, (512, 256, 128))
    c0 = gate_col0 // tn + branch * (D // tn)

    def body(y_ref, w_ref, gp_ref, b_ref, dm_ref, dup_ref, dgp_ref, db_ref):
        i = pl.program_id(1)
        g = jax.nn.sigmoid(gp_ref[...] + b_ref[branch:branch + 1, :])
        dm = dm_ref[...]
        dup_ref[...] = (dm * g).astype(BF16)
        dgp = (dm * _nt(y_ref[...], w_ref[...])) * (g * (1.0 - g))
        dgp_ref[...] = dgp.astype(BF16)
        part = jnp.sum(dgp, axis=0, keepdims=True)

        @pl.when(i == 0)
        def _():
            db_ref[...] = part

        @pl.when(i > 0)
        def _():
            db_ref[...] += part

    tile = pl.BlockSpec((tm, tn), lambda j, i: (i, j))
    return pl.pallas_call(
        body,
        name=name,
        grid=(D // tn, T // tm),
        in_specs=[
            pl.BlockSpec((tm, y.shape[1]), lambda j, i: (i, 0)),
            pl.BlockSpec((tn, w_up.shape[1]), lambda j, i: (j, 0)),
            pl.BlockSpec((tm, tn), lambda j, i: (i, c0 + j)),
            pl.BlockSpec((2, tn), lambda j, i: (0, j)),
            tile,
        ],
        out_specs=[tile, tile, pl.BlockSpec((1, tn), lambda j, i: (0, j))],
        out_shape=[jax.ShapeDtypeStruct((T, D), BF16), jax.ShapeDtypeStruct((T, D), BF16), jax.ShapeDtypeStruct((1, D), F32)],
        compiler_params=_params(("parallel", "arbitrary")),
    )(y, w_up, proj, gate_b, dmixed)


def _loss_head(y, target, *, name):
    T, D = y.shape
    tm = _pick(T, (256, 128))

    def body(y_ref, t_ref, dy_ref, dyb_ref, l_ref):
        i = pl.program_id(0)
        err = y_ref[...] - t_ref[...]
        dy = err * (1.0 / D)
        dy_ref[...] = dy
        dyb_ref[...] = dy.astype(BF16)
        part = 0.5 * jnp.sum(jnp.mean(err * err, axis=-1, keepdims=True), axis=0, keepdims=True)

        @pl.when(i == 0)
        def _():
            l_ref[...] = part

        @pl.when(i > 0)
        def _():
            l_ref[...] += part

    row = pl.BlockSpec((tm, D), lambda i: (i, 0))
    return pl.pallas_call(
        body, name=name, grid=(T // tm,), in_specs=[row, row],
        out_specs=[row, row, pl.BlockSpec((1, 1), lambda i: (0, 0))],
        out_shape=[jax.ShapeDtypeStruct((T, D), F32), jax.ShapeDtypeStruct((T, D), BF16), jax.ShapeDtypeStruct((1, 1), F32)],
        compiler_params=_params(("arbitrary",)),
    )(y, target)


def _sum_parts(p_ref, own):
    slot = _slot(_place())
    g = jnp.where(slot == 0, own, p_ref[0].astype(F32))
    for s in range(1, N_DEV):
        g = g + jnp.where(slot == s, own, p_ref[s].astype(F32))
    return g


def _held(step, l, last):
    layer, i = step
    return jnp.where(layer == l, i, last * (layer > l))


def _reduce_adamw(parts, own, w, m, v, *, own_chunked=True, name):
    n_layers = len(parts)
    C = w.shape[1]
    R = w.shape[0] // n_layers
    tr = R
    for cand in (1024, 512, 256, 128, 64, 32, 16, 8):
        if R % cand == 0 and cand * C <= 256 * 1024:
            tr = cand
            break
    nr = R // tr

    def body(*refs):
        p_refs, o_refs = refs[:n_layers], refs[n_layers:2 * n_layers]
        w_ref, m_ref, v_ref, g_ref, d_ref, nm_ref, nv_ref = refs[2 * n_layers:]
        layer = pl.program_id(0)
        for l, (p_ref, o_ref) in enumerate(zip(p_refs, o_refs)):
            @pl.when(layer == l)
            def _():
                g_ref[...] = _sum_parts(p_ref, (o_ref[0] if own_chunked else o_ref[...]).astype(F32))

        _adamw_update(g_ref[...], w_ref, m_ref, v_ref, d_ref, nm_ref, nv_ref)

    def part_spec(l):
        return pl.BlockSpec((N_DEV, tr, C), lambda *step: (0, _held(step, l, nr - 1), 0))

    def own_spec(l):
        if own_chunked:
            return pl.BlockSpec((1, tr, C), lambda *step: (_slot(_place()), _held(step, l, nr - 1), 0))
        return pl.BlockSpec((tr, C), lambda *step: (_held(step, l, nr - 1), 0))

    row = pl.BlockSpec((tr, C), lambda layer, i: (layer * nr + i, 0))
    return pl.pallas_call(
        body, name=name, grid=(n_layers, nr),
        in_specs=[part_spec(l) for l in range(n_layers)] + [own_spec(l) for l in range(n_layers)] + [row, row, row],
        out_specs=[row] * 4, out_shape=[jax.ShapeDtypeStruct(w.shape, F32)] * 4,
        compiler_params=_params(("arbitrary", "arbitrary")),
    )(*parts, *own, w, m, v)


def _adamw_update(g, w_ref, m_ref, v_ref, d_ref, nm_ref, nv_ref):
    m_new = ADAM_B1 * m_ref[...] + (1.0 - ADAM_B1) * g
    v_new = ADAM_B2 * v_ref[...] + (1.0 - ADAM_B2) * (g * g)
    m_hat = m_new / (1.0 - ADAM_B1 ** ADAM_STEP)
    v_hat = v_new / (1.0 - ADAM_B2 ** ADAM_STEP)
    d_ref[...] = -ADAM_LR * (m_hat / (jnp.sqrt(v_hat) + ADAM_EPS) + ADAM_WD * w_ref[...])
    nm_ref[...] = m_new
    nv_ref[...] = v_new


def _reduce_adamw_t(parts, own, w, m, v, *, name):
    n_layers = len(parts)
    _, n, K = parts[0].shape
    n_pad = w.shape[1]
    tm = _pick(K, (128,))
    nr = K // tm

    def body(*refs):
        p_refs, o_refs = refs[:n_layers], refs[n_layers:2 * n_layers]
        w_ref, m_ref, v_ref, g_ref, d_ref, nm_ref, nv_ref = refs[2 * n_layers:]
        layer = pl.program_id(0)
        for l, (p_ref, o_ref) in enumerate(zip(p_refs, o_refs)):
            @pl.when(layer == l)
            def _():
                g_t = _sum_parts(p_ref, o_ref[0].astype(F32))
                if n_pad > n:
                    g_t = jnp.concatenate([g_t, jnp.zeros((n_pad - n, tm), F32)], axis=0)
                g_ref[...] = g_t.T

        _adamw_update(g_ref[...], w_ref, m_ref, v_ref, d_ref, nm_ref, nv_ref)

    def part_spec(l):
        return pl.BlockSpec((N_DEV, n, tm), lambda *step: (0, 0, _held(step, l, nr - 1)))

    def own_spec(l):
        return pl.BlockSpec((1, n, tm), lambda *step: (_slot(_place()), 0, _held(step, l, nr - 1)))

    row = pl.BlockSpec((tm, n_pad), lambda layer, i: (layer * nr + i, 0))
    return pl.pallas_call(
        body, name=name, grid=(n_layers, nr),
        in_specs=[part_spec(l) for l in range(n_layers)] + [own_spec(l) for l in range(n_layers)] + [row, row, row],
        out_specs=[row] * 4, out_shape=[jax.ShapeDtypeStruct(w.shape, F32)] * 4,
        compiler_params=_params(("arbitrary", "arbitrary")),
    )(*parts, *own, w, m, v)


def _transpose_cast(x, *, name):
    R, C = x.shape
    tr, tc = _pick(R, (512, 256, 128)), _pick(C, (512, 256, 128))

    def body(x_ref, o_ref):
        o_ref[...] = x_ref[...].astype(F32).T.astype(BF16)

    return pl.pallas_call(
        body, name=name, grid=(R // tr, C // tc),
        in_specs=[pl.BlockSpec((tr, tc), lambda i, j: (i, j))],
        out_specs=pl.BlockSpec((tc, tr), lambda i, j: (j, i)),
        out_shape=jax.ShapeDtypeStruct((C, R), BF16),
        compiler_params=_params(("parallel", "parallel")),
    )(x)


_ANY = pl.BlockSpec(memory_space=pl.ANY)


def _place():
    return lax.axis_index("x"), lax.axis_index("y"), lax.axis_index("c")


def _slot(p):
    return 4 * p[0] + 2 * p[1] + p[2]


_HBM = pl.BlockSpec(memory_space=pltpu.HBM)
_SEM = pl.BlockSpec(memory_space=pltpu.SEMAPHORE)
_EFFECT = pltpu.SideEffectType.DATAFLOW_SIDE_EFFECTING
N_PEERS = N_DEV - 1


def _peers(me):
    flips = [(fx, fy, fc) for fx in (0, 1) for fy in (0, 1) for fc in (0, 1)][1:]
    return [tuple(1 - v if f else v for v, f in zip(me, flip)) for flip in flips]


def _peer_copy(src, lands, t, k, sender, to, send_sems, recv_sems):
    return pltpu.make_async_remote_copy(
        src_ref=src, dst_ref=lands[t].at[_slot(sender)], send_sem=send_sems.at[N_PEERS * t + k],
        recv_sem=recv_sems.at[N_PEERS * t + k], device_id=to, device_id_type=MESH)


class _Exchange:
    def __init__(self, chunked, whole, name):
        self.arrays = [pltpu.with_memory_space_constraint(a, pltpu.HBM) for a in list(chunked) + list(whole)]
        self.n, self.n_chunked, self.name = len(self.arrays), len(chunked), name

    def _src(self, ins, t, dest):
        return ins[t].at[_slot(dest)] if t < self.n_chunked else ins[t]

    def _land_shape(self, t):
        a = self.arrays[t]
        return a.shape if t < self.n_chunked else (N_DEV,) + a.shape

    def start(self, after=None):
        n = self.n

        def body(*refs):
            ins, lands = refs[:n], refs[n:2 * n]
            send_sems, recv_sems = refs[-2 * n - 3], refs[-2 * n - 2]
            token = refs[-1]
            me = _place()
            for t in range(n):
                for k, peer in enumerate(_peers(me)):
                    _peer_copy(self._src(ins, t, peer), lands, t, k, me, peer, send_sems, recv_sems).start()
            token[...] = jnp.zeros_like(token)

        lands = [pltpu.with_memory_space_constraint(lax.empty(self._land_shape(t), a.dtype), pltpu.HBM)
                 for t, a in enumerate(self.arrays)]
        sems = pltpu.SemaphoreType.DMA((N_PEERS * n,))
        outs = pl.pallas_call(
            body,
            name=self.name + "_start",
            in_specs=[_HBM] * (2 * n) + ([_ANY] if after is not None else []),
            out_specs=[_SEM, _SEM] + [_HBM] * (2 * n) + [pl.BlockSpec(memory_space=pltpu.VMEM)],
            out_shape=[sems, sems] + [pltpu.HBM(a.shape, a.dtype) for a in self.arrays + lands]
            + [jax.ShapeDtypeStruct((8, 128), F32)],
            input_output_aliases={t: 2 + t for t in range(2 * n)},
            compiler_params=pltpu.CompilerParams(has_side_effects=_EFFECT),
        )(*self.arrays, *lands, *([after] if after is not None else []))
        self.sems, self.thru, self.lands, self.token = outs[:2], outs[2:2 + n], outs[2 + n:2 + 2 * n], outs[-1]
        return self.token

    def finish(self, after):
        n = self.n

        def wait_body(*refs):
            ins, lands, (send_sems, recv_sems) = refs[:n], refs[n:2 * n], refs[2 * n:2 * n + 2]
            me = _place()
            for t in range(n):
                for k, peer in enumerate(_peers(me)):
                    cp = _peer_copy(self._src(ins, t, peer), lands, t, k, peer, peer, send_sems, recv_sems)
                    cp.wait_send()
                    cp.wait_recv()

        outs = pl.pallas_call(
            wait_body,
            name=self.name + "_wait",
            in_specs=[_HBM] * (2 * n) + [_SEM, _SEM, _ANY],
            out_specs=[_HBM] * (2 * n),
            out_shape=[pltpu.HBM(a.shape, a.dtype) for a in self.thru + self.lands],
            input_output_aliases={t: t for t in range(2 * n)},
            compiler_params=pltpu.CompilerParams(has_side_effects=_EFFECT),
        )(*self.thru, *self.lands, *self.sems, after)
        return outs[:n], outs[n:]


def _exchange(chunked, whole, *, name):
    ride = _Ride(chunked, whole)
    n = ride.n

    def body(*refs):
        ride.start(refs[:n], refs[n:2 * n], refs[2 * n:])
        ride.finish(refs[:n], refs[n:2 * n], refs[2 * n:])

    return pl.pallas_call(
        body,
        name=name,
        in_specs=[_ANY] * n,
        out_specs=[_ANY] * n,
        out_shape=ride.land_shapes(),
        scratch_shapes=ride.semaphores(),
        compiler_params=pltpu.CompilerParams(has_side_effects=True),
    )(*ride.arrays)


class _Ride:
    def __init__(self, chunked, whole):
        self.arrays = list(chunked) + list(whole)
        self.n, self.n_chunked = len(self.arrays), len(chunked)
        self.gather = self.n_chunked == 0

    def land_shapes(self):
        return [jax.ShapeDtypeStruct(a.shape if t < self.n_chunked else (N_DEV,) + a.shape, a.dtype)
                for t, a in enumerate(self.arrays)]

    def semaphores(self):
        sems = pltpu.SemaphoreType.DMA((N_PEERS * self.n,))
        return [sems, sems, pltpu.SemaphoreType.DMA((self.n,))]

    def _src(self, ins, t, dest):
        return ins[t].at[_slot(dest)] if t < self.n_chunked else ins[t]

    def _copies(self, ins, lands, sems):
        send_sems, recv_sems, local_sems = sems
        x, y, c = me = _place()
        if not self.gather:
            return [], [_peer_copy(self._src(ins, t, peer), lands, t, k, me, peer, send_sems, recv_sems)
                        for t in range(self.n) for k, peer in enumerate(_peers(me))]
        mine = [pltpu.make_async_copy(ins[t], lands[t].at[_slot(me)], local_sems.at[t]) for t in range(self.n)]
        first = []
        for t in range(self.n):
            first.append(self._hop(ins, lands, sems, t, 0, me, (x, y, 1 - c), ins[t]))
            first += [self._hop(ins, lands, sems, t, 1 + j, me, (*chip, c), ins[t]) for j, chip in enumerate(self._chips())]
        return mine, first

    def _chips(self):
        x, y, _ = _place()
        return [(1 - x, y), (x, 1 - y), (1 - x, 1 - y)]

    def _hop(self, ins, lands, sems, t, k, block, to, source=None):
        dst = lands[t].at[_slot(block)]
        return pltpu.make_async_remote_copy(
            src_ref=dst if source is None else source, dst_ref=dst, send_sem=sems[0].at[N_PEERS * t + k],
            recv_sem=sems[1].at[N_PEERS * t + k], device_id=to, device_id_type=MESH)

    def start(self, ins, lands, sems):
        mine, sends = self._copies(ins, lands, sems)
        for cp in mine + sends:
            cp.start()

    def finish(self, ins, lands, sems):
        mine, sends = self._copies(ins, lands, sems)
        x, y, c = me = _place()
        if self.gather:
            sibling = (x, y, 1 - c)
            for j, chip in enumerate(self._chips()):
                for t in range(self.n):
                    self._hop(ins, lands, sems, t, 1 + j, (*chip, c), me).wait_recv()
                    cp = self._hop(ins, lands, sems, t, 4 + j, (*chip, c), sibling)
                    cp.start()
                    sends.append(cp)
            for t in range(self.n):
                self._hop(ins, lands, sems, t, 0, sibling, me).wait_recv()
                for j, chip in enumerate(self._chips()):
                    self._hop(ins, lands, sems, t, 4 + j, (*chip, 1 - c), me).wait_recv()
        else:
            for t in range(self.n):
                for k, peer in enumerate(_peers(me)):
                    _peer_copy(self._src(ins, t, peer), lands, t, k, peer, peer, sems[0], sems[1]).wait_recv()
        for cp in sends:
            cp.wait_send()
        for cp in mine:
            cp.wait()


def _place_own(land, shard, *, name):
    n, K = shard.shape
    tr = n
    for cand in (1024, 736, 512, 256, 128):
        if n % cand == 0:
            tr = cand
            break

    def body(s_ref, land_ref, o_ref):
        del land_ref
        o_ref[0] = s_ref[...]

    return pl.pallas_call(
        body, name=name, grid=(n // tr,),
        in_specs=[pl.BlockSpec((tr, K), lambda i: (i, 0)), _ANY],
        out_specs=pl.BlockSpec((1, tr, K), lambda i: (_slot(_place()), i, 0)),
        out_shape=jax.ShapeDtypeStruct(land.shape, land.dtype),
        input_output_aliases={1: 0},
        compiler_params=_params(("arbitrary",)),
    )(shard, land)


def _rope_tables(T):
    half = HEAD_DIM // 2
    inv_freq = ROPE_THETA ** (-jnp.arange(half, dtype=F32) / half)
    ang = jnp.arange(T, dtype=F32)[:, None] * inv_freq[None, :]
    cos, sin = jnp.cos(ang), jnp.sin(ang)
    return jnp.concatenate([cos, cos], axis=-1), jnp.concatenate([-sin, sin], axis=-1)


def _gain_table(q_gain, k_gain):
    return jnp.tile(jnp.concatenate([q_gain, k_gain], axis=0), (1, 4))[:, None, :]


def _sb_heads(w_in):
    n_in, d_model = w_in.shape
    return (n_in - N_DIL_BLOCKS * GROUP_W - 2 * d_model) // (3 * HEAD_DIM)


def _carry(rides, key, args, call):
    make = rides.get(key) if rides else None
    if make is None:
        return call(None)
    ride, on_landed = make(*args)
    res, lands = call(ride)
    on_landed(lands)
    return res


def _layer_fwd(x, p, cos, sin, tag, rides=None):
    sb_heads = _sb_heads(p["w_in"])
    n_sb_blocks = 3 * sb_heads * HEAD_DIM // GROUP_W
    s = {"x": x}
    s["h"] = _rmsnorm_fwd(x, p["norm1"], name=f"norm1_fwd{tag}")
    s["proj"] = _carry(rides, "proj_fwd", (), lambda ride: _matmul(
        s["h"], p["w_in"], mode="nt", out_dtype=F32, name=f"proj_fwd{tag}", ride=ride))
    *s["qkv_d"], s["qkv_s"] = _prep_fwd(s["proj"], p["gains"], cos, sin, n_sb_blocks, name=f"prep_fwd{tag}")
    outs = [_dil_fwd(s["qkv_d"][g], g, name=f"dil{g}_fwd{tag}") for g in range(N_GROUPS)]
    s["o"], s["ld"] = [o for o, _ in outs], [ld for _, ld in outs]
    s["y_dil"] = _merge_fwd(s["o"], s["ld"], name=f"merge_fwd{tag}")
    s["y_sb32"], s["y_sb"] = _sb_fwd(s["qkv_s"], sb_heads, 0, name=f"sb_fwd{tag}")
    s["mixed"] = _gate_fwd(s["y_dil"], s["y_sb"], p["w_up_dil"], p["w_up_sb"], s["proj"], p["gate_b"],
                           (N_DIL_BLOCKS + n_sb_blocks) * GROUP_W, name=f"gate_fwd{tag}")
    s["x1"] = _matmul(s["mixed"], p["w_out"], mode="nn", out_dtype=F32, epilogue="add", extra=x, name=f"out_fwd{tag}")
    s["h2"] = _rmsnorm_fwd(s["x1"], p["norm2"], name=f"norm2_fwd{tag}")
    s["f"], s["a"] = _carry(rides, "ff1_fwd", (), lambda ride: _matmul(
        s["h2"], p["w_ff1"], mode="nt", out_dtype=BF16, epilogue="relu2", name=f"ff1_fwd{tag}", ride=ride))
    x2 = _carry(rides, "ff2_fwd", (), lambda ride: _matmul(
        s["a"], p["w_ff2"], mode="nn", out_dtype=F32, epilogue="add", extra=s["x1"], name=f"ff2_fwd{tag}", ride=ride))
    return x2, s


def _layer_bwd(dx2, dx2_b, p, s, cos, sin, tag, rides=None, done=None):
    sb_heads = _sb_heads(p["w_in"])
    gate_col0 = N_DIL_BLOCKS * GROUP_W + 3 * sb_heads * HEAD_DIM
    g = {}
    df = _carry(rides, "ff2_bwd", (g, done), lambda ride: _matmul(
        dx2_b, p["w_ff2"], mode="nt", out_dtype=BF16, epilogue="relu2_bwd", extra=s["f"], name=f"ff2_bwd{tag}", ride=ride))
    g["w_ff2"] = _matmul(s["a"], dx2_b, mode="tn", out_dtype=BF16, name=f"ff2_wgrad{tag}")
    dh2 = _carry(rides, "ff1_bwd", (g, done), lambda ride: _matmul(
        df, p["w_ff1"], mode="nn", out_dtype=F32, name=f"ff1_bwd{tag}", ride=ride))
    g["w_ff1"] = _matmul(df, s["h2"], mode="tn", out_dtype=BF16, name=f"ff1_wgrad{tag}")
    dx1, dx1_b, g["norm2"] = _rmsnorm_bwd(s["x1"], p["norm2"], dh2, dx2, name=f"norm2_bwd{tag}")
    dmixed = _matmul(dx1_b, p["w_out"], mode="nt", out_dtype=F32, name=f"out_bwd{tag}")
    g["w_out"] = _matmul(s["mixed"], dx1_b, mode="tn", out_dtype=BF16, name=f"out_wgrad{tag}")
    gate_b = p["gate_b"]
    dup_dil, dgp0, db0 = _gate_bwd(s["y_dil"], p["w_up_dil"], s["proj"], gate_b, dmixed, 0, gate_col0, name=f"gate0_bwd{tag}")
    dup_sb, dgp1, db1 = _gate_bwd(s["y_sb"], p["w_up_sb"], s["proj"], gate_b, dmixed, 1, gate_col0, name=f"gate1_bwd{tag}")
    g["gate_b"] = jnp.concatenate([db0, db1], axis=0)
    dy_dil = _matmul(dup_dil, p["w_up_dil"], mode="nn", out_dtype=F32, name=f"updil_bwd{tag}")
    g["w_up_dil"] = _matmul(dup_dil, s["y_dil"], mode="tn", out_dtype=BF16, name=f"updil_wgrad{tag}")
    dy_sb = _matmul(dup_sb, p["w_up_sb"], mode="nn", out_dtype=BF16, name=f"upsb_bwd{tag}")
    g["w_up_sb"] = _matmul(dup_sb, s["y_sb"], mode="tn", out_dtype=BF16, name=f"upsb_wgrad{tag}")
    dos, dterms = _merge_bwd(s["o"], s["ld"], dy_dil, name=f"merge_bwd{tag}")
    dqkv = [_dil_bwd(s["qkv_d"][grp], dos[grp], s["ld"][grp], dterms[grp], grp, name=f"dil{grp}_bwd{tag}")
            for grp in range(N_GROUPS)]
    dproj_d, dgain = _prep_bwd(s["proj"], dqkv, p["gains"], cos, sin, name=f"prep_bwd{tag}")
    g["q_gain"], g["k_gain"] = dgain[:N_GROUPS, 0], dgain[N_GROUPS:, 0]
    dq_s, dk_s, dv_s = _sb_bwd(s["qkv_s"], s["y_sb32"], dy_sb, sb_heads, 0, name=f"sb_bwd{tag}")
    dproj = jnp.concatenate([dproj_d, dq_s, dk_s, dv_s, dgp0, dgp1], axis=1)
    g["w_in"] = _carry(rides, "proj_wgrad", (g, done), lambda ride: _matmul(
        dproj, s["h"], mode="tn", out_dtype=BF16, name=f"proj_wgrad{tag}", ride=ride,
        tn=_pick(s["h"].shape[1], (2048, 1024, 512, 256, 128)), tk=_pick(s["h"].shape[0], (1024, 512))))
    dh = _carry(rides, "proj_bwd", (g, done), lambda ride: _matmul(
        dproj, p["w_in"], mode="nn", out_dtype=F32, name=f"proj_bwd{tag}", ride=ride))
    dx, dx_b, g["norm1"] = _rmsnorm_bwd(s["x"], p["norm1"], dh, dx1, name=f"norm1_bwd{tag}")
    return dx, dx_b, g


def _local_step(x, target, layers, fwd_rides=None, bwd_rides=None):
    depth = len(layers)
    fwd_rides, bwd_rides = (r or [None] * depth for r in (fwd_rides, bwd_rides))
    cos, sin = _rope_tables(x.shape[0])
    saved = []
    for l, p in enumerate(layers):
        x, s = _layer_fwd(x, p, cos, sin, f"_l{l}", fwd_rides[l])
        saved.append(s)
    dx, dx_b, loss = _loss_head(x, target, name="loss_head")
    grads = [None] * depth
    for l in reversed(range(depth)):
        dx, dx_b, grads[l] = _layer_bwd(dx, dx_b, layers[l], saved[l], cos, sin, f"_l{l}", bwd_rides[l], grads)
    return loss, dx, grads


_MATRICES = ("w_in", "w_up_dil", "w_up_sb", "w_out", "w_ff1", "w_ff2")
_TRANSPOSED = ("w_in", "w_up_dil", "w_up_sb", "w_ff1")
_SMALL = ("norm1_g", "norm2_g", "q_norm_g", "k_norm_g")


def _unshard(blocks, name):
    if name == "gate_b":
        return jnp.transpose(blocks, (1, 0, 2)).reshape(blocks.shape[1], N_DEV * blocks.shape[2])
    return blocks.reshape(N_DEV * blocks.shape[1], blocks.shape[2])


def _to_chunks(full, name):
    if name == "gate_b":
        r, cols = full.shape
        return jnp.transpose(full.reshape(r, N_DEV, cols // N_DEV), (1, 0, 2))
    return full.reshape(N_DEV, full.shape[0] // N_DEV, full.shape[1])


def _lane_pad(n):
    return -n % HEAD_DIM


def _pack_small(norm1, norm2, qg, kg):
    flat = jnp.concatenate([t.reshape(-1, HEAD_DIM) for t in (norm1, norm2, qg, kg)], axis=0)
    return jnp.pad(flat, ((0, -flat.shape[0] % 8), (0, 0)))


def _unpack_small(packed, shapes):
    out, row = [], 0
    for shape in shapes:
        rows = math.prod(shape) // HEAD_DIM
        out.append(packed[row:row + rows].reshape(shape))
        row += rows
    return out


def kernel(x, norm1_g, w_in, q_norm_g, k_norm_g, w_up_dil, w_up_sb, gate_b, w_out, norm2_g, w_ff1, w_ff2, loss_target, m_norm1_g, m_w_in, m_q_norm_g, m_k_norm_g, m_w_up_dil, m_w_up_sb, m_gate_b, m_w_out, m_norm2_g, m_w_ff1, m_w_ff2, v_norm1_g, v_w_in, v_q_norm_g, v_k_norm_g, v_w_up_dil, v_w_up_sb, v_gate_b, v_w_out, v_norm2_g, v_w_ff1, v_w_ff2):
    names = ("norm1_g", "w_in", "q_norm_g", "k_norm_g", "w_up_dil", "w_up_sb", "gate_b", "w_out", "norm2_g", "w_ff1", "w_ff2")
    w = dict(zip(names, (norm1_g, w_in, q_norm_g, k_norm_g, w_up_dil, w_up_sb, gate_b, w_out, norm2_g, w_ff1, w_ff2)))
    m = dict(zip(names, (m_norm1_g, m_w_in, m_q_norm_g, m_k_norm_g, m_w_up_dil, m_w_up_sb, m_gate_b, m_w_out, m_norm2_g, m_w_ff1, m_w_ff2)))
    v = dict(zip(names, (v_norm1_g, v_w_in, v_q_norm_g, v_k_norm_g, v_w_up_dil, v_w_up_sb, v_gate_b, v_w_out, v_norm2_g, v_w_ff1, v_w_ff2)))
    depth = norm1_g.shape[0]
    assert depth == 2, "the exchange schedule below is written for two layers"
    sharded = _MATRICES + ("gate_b",)

    def shards(layer, which):
        out = []
        for n in which:
            shard = w[n][layer]
            if n in _TRANSPOSED:
                cols = shard.shape[1]
                padded = jnp.pad(shard, ((0, 0), (0, _lane_pad(cols))))
                out.append(_transpose_cast(padded, name=f"shard_t_{n}_l{layer}")[:cols])
            else:
                out.append(shard if n == "gate_b" else shard.astype(BF16))
        return out

    assert depth == 2, "the schedule of exchanges below is written for two layers"
    layers = [{"norm1": norm1_g[l][None], "norm2": norm2_g[l][None], "gains": _gain_table(q_norm_g[l], k_norm_g[l])}
              for l in range(depth)]
    rest = sharded[1:]

    def gather_on(layer, which):
        def make():
            ride = _Ride([], shards(layer, which))
            return ride, lambda lands: layers[layer].update({n: _unshard(b, n) for n, b in zip(which, lands)})
        return make

    (w_in_0,) = _exchange([], shards(0, sharded[:1]), name="gather_w_in_l0")
    layers[0]["w_in"] = _unshard(w_in_0, "w_in")
    fwd_rides = [{"proj_fwd": gather_on(0, rest), "ff1_fwd": gather_on(1, sharded[:1]), "ff2_fwd": gather_on(1, rest)}, None]

    sent, landed = {}, {}

    def exchange_on(items):
        def make(g, done):
            chunks = [_to_chunks((g if done[layer] is None else done[layer])[n], n) for layer, n in items]

            def on_landed(lands):
                for item, chunk, land in zip(items, chunks, lands):
                    sent[item], landed[item] = chunk, land
            return _Ride(chunks, []), on_landed
        return make

    others = ("w_ff1", "w_out", "w_up_dil", "w_up_sb", "gate_b")
    bwd_rides = [{"ff1_bwd": exchange_on([(l, "w_ff2")]), "proj_wgrad": exchange_on([(l, n) for n in others]),
                  "proj_bwd": exchange_on([(l, "w_in")])} for l in range(depth)]
    loss_part, dx, grads = _local_step(x[0], loss_target[0], layers, fwd_rides, bwd_rides)
    loss = lax.psum(loss_part[0, 0], ("x", "y", "c"))

    small = _pack_small(jnp.concatenate([g["norm1"] for g in grads]), jnp.concatenate([g["norm2"] for g in grads]),
                        jnp.stack([g["q_gain"] for g in grads]), jnp.stack([g["k_gain"] for g in grads]))
    (small_parts,) = _exchange([], [small], name="exchange_small")

    out = {}
    for n in sharded:
        rows, cols = depth * w[n].shape[1], w[n].shape[2]
        own, parts = ([moved[(l, n)] for l in range(depth)] for moved in (sent, landed))
        state = [t.reshape(rows, cols) for t in (w[n], m[n], v[n])]
        if n in _TRANSPOSED:
            pad = _lane_pad(cols)
            res = _reduce_adamw_t(parts, own, *(jnp.pad(t, ((0, 0), (0, pad))) for t in state), name=f"adamw_{n}")
            res = [t[:, :cols] for t in res]
        else:
            if w[n].shape[1] % 8:
                own, parts = [jnp.concatenate(own, axis=1)], [jnp.concatenate(parts, axis=1)]
            res = _reduce_adamw(parts, own, *state, name=f"adamw_{n}")
        out[n] = [t.reshape(w[n].shape) for t in res]
    small_res = _reduce_adamw([small_parts], [small], _pack_small(*(w[n] for n in _SMALL)),
                              _pack_small(*(m[n] for n in _SMALL)), _pack_small(*(v[n] for n in _SMALL)),
                              own_chunked=False, name="adamw_small")
    small_shapes = [w[n].shape for n in _SMALL]
    for k, t in enumerate(small_res):
        for n, arr in zip(_SMALL, _unpack_small(t, small_shapes)):
            out.setdefault(n, [None] * 4)[k] = arr
    return (loss, dx[None], *(out[n][0] for n in names), *(out[n][1] for n in names), *(out[n][2] for n in names),
            *(out[n][3] for n in names))
```

```python
import functools
import math

import jax
import jax.numpy as jnp
from jax import lax
from jax.experimental import pallas as pl
from jax.experimental.pallas import tpu as pltpu

F32 = jnp.float32
BF16 = jnp.bfloat16

HEAD_DIM = 128
BLOCK = 128
N_GROUPS = 3
DILATIONS = (1, 4, 16)
ROPE_THETA = 10000.0
EPS = 1e-6
ADAM_LR = 0.001
ADAM_B1 = 0.9
ADAM_B2 = 0.999
ADAM_EPS = 1e-08
ADAM_WD = 0.01
ADAM_STEP = 10
N_DEV = 8
MESH = pl.DeviceIdType.MESH
VMEM_LIMIT_BYTES = 48 * 1024 * 1024
NEG = -1e30


def _params(sem):
    return pltpu.CompilerParams(dimension_semantics=sem, vmem_limit_bytes=VMEM_LIMIT_BYTES)


def _pick(n, options):
    for o in options:
        if n % o == 0:
            return o
    return n


_DIMS = {"nn": (((1,), (0,)), ((), ())), "nt": (((1,), (1,)), ((), ())), "tn": (((0,), (0,)), ((), ()))}


def _matmul(a, b, *, mode, out_dtype, name, epilogue=None, extra=None, tm=None, tn=None, tk=None, ride=None):
    if mode == "nn":
        (M, K), (K2, N) = a.shape, b.shape
    elif mode == "nt":
        (M, K), (N, K2) = a.shape, b.shape
    else:
        (K, M), (K2, N) = a.shape, b.shape
    assert K == K2, (a.shape, b.shape, mode)
    tm = tm or _pick(M, (1024, 512, 256, 128))
    tn = tn or _pick(N, (512, 256, 128))
    tk = tk or _pick(K, (2048, 2944, 1024, 512, 256, 128))
    nk = K // tk
    dims = _DIMS[mode]
    n_extra = 0 if extra is None else 1
    n_out = 2 if epilogue == "relu2" else 1
    n_ride = 0 if ride is None else ride.n
    grid = (M // tm, N // tn, nk)

    def body(*refs):
        a_ref, b_ref = refs[0], refs[1]
        extra_ref = refs[2] if n_extra else None
        first_out = 2 + n_extra + n_ride
        outs = refs[first_out:first_out + n_out]
        acc_ref = refs[first_out + n_out + n_ride] if nk > 1 else None
        if ride is not None:
            ride_refs = (refs[2 + n_extra:first_out], refs[first_out + n_out:first_out + n_out + n_ride], refs[-3:])
            at = [pl.program_id(d) for d in range(3)]

            @pl.when(jnp.logical_and(jnp.logical_and(at[0] == 0, at[1] == 0), at[2] == 0))
            def _():
                ride.start(*ride_refs)

            if ride.gather:
                @pl.when(jnp.logical_and(jnp.logical_and(at[0] == (7 * grid[0]) // 8, at[1] == 0), at[2] == 0))
                def _():
                    ride.forward(*ride_refs)

        def finish(acc):
            if epilogue is None:
                outs[0][...] = acc.astype(outs[0].dtype)
            elif epilogue == "add":
                outs[0][...] = (acc + extra_ref[...]).astype(outs[0].dtype)
            elif epilogue == "relu2":
                r = jnp.maximum(acc, 0.0)
                outs[0][...] = r.astype(outs[0].dtype)
                outs[1][...] = (r * r).astype(outs[1].dtype)
            else:
                outs[0][...] = (acc * (2.0 * extra_ref[...].astype(F32))).astype(outs[0].dtype)

        prod = lax.dot_general(a_ref[...], b_ref[...], dims, preferred_element_type=F32)
        if nk == 1:
            finish(prod)
        else:
            k = pl.program_id(2)

            @pl.when(k == 0)
            def _():
                acc_ref[...] = prod

            @pl.when(k > 0)
            def _():
                acc_ref[...] += prod

            @pl.when(k == nk - 1)
            def _():
                finish(acc_ref[...])

        if ride is not None:
            @pl.when(jnp.logical_and(jnp.logical_and(at[0] == grid[0] - 1, at[1] == grid[1] - 1), at[2] == nk - 1))
            def _():
                ride.finish(*ride_refs)

    if mode == "nn":
        a_spec = pl.BlockSpec((tm, tk), lambda i, j, k: (i, k))
        b_spec = pl.BlockSpec((tk, tn), lambda i, j, k: (k, j))
    elif mode == "nt":
        a_spec = pl.BlockSpec((tm, tk), lambda i, j, k: (i, k))
        b_spec = pl.BlockSpec((tn, tk), lambda i, j, k: (j, k))
    else:
        a_spec = pl.BlockSpec((tk, tm), lambda i, j, k: (k, i))
        b_spec = pl.BlockSpec((tk, tn), lambda i, j, k: (k, j))
    o_spec = pl.BlockSpec((tm, tn), lambda i, j, k: (i, j))
    in_specs = [a_spec, b_spec] + ([o_spec] if n_extra else []) + [_ANY] * n_ride
    out_shape = [jax.ShapeDtypeStruct((M, N), out_dtype)] * n_out + (ride.land_shapes() if ride else [])
    res = pl.pallas_call(
        body,
        name=name,
        grid=grid,
        in_specs=in_specs,
        out_specs=[o_spec] * n_out + [_ANY] * n_ride,
        out_shape=out_shape,
        scratch_shapes=([pltpu.VMEM((tm, tn), F32)] if nk > 1 else []) + (ride.semaphores() if ride else []),
        compiler_params=_params(("arbitrary",) * 3 if ride else ("parallel", "parallel", "arbitrary")),
    )(a, b, *([extra] if n_extra else []), *(ride.arrays if ride else []))
    if ride is None:
        return res if n_out > 1 else res[0]
    return (res[:n_out] if n_out > 1 else res[0]), res[n_out:]


def _rmsnorm_fwd(x, g, *, name):
    T, D = x.shape
    tm = _pick(T, (512, 256, 128))

    def body(x_ref, g_ref, o_ref):
        xf = x_ref[...]
        r = lax.rsqrt(jnp.mean(xf * xf, axis=-1, keepdims=True) + EPS)
        o_ref[...] = ((xf * r) * g_ref[...]).astype(o_ref.dtype)

    return pl.pallas_call(
        body,
        name=name,
        grid=(T // tm,),
        in_specs=[pl.BlockSpec((tm, D), lambda i: (i, 0)), pl.BlockSpec((1, D), lambda i: (0, 0))],
        out_specs=pl.BlockSpec((tm, D), lambda i: (i, 0)),
        out_shape=jax.ShapeDtypeStruct((T, D), BF16),
        compiler_params=_params(("parallel",)),
    )(x, g)


def _rmsnorm_bwd(x, g, dh, dres, *, name):
    T, D = x.shape
    tm = _pick(T, (256, 128))

    def body(x_ref, g_ref, dh_ref, dres_ref, dx_ref, dxb_ref, dg_ref):
        i = pl.program_id(0)
        xf = x_ref[...]
        r = lax.rsqrt(jnp.mean(xf * xf, axis=-1, keepdims=True) + EPS)
        y = xf * r
        dh_v = dh_ref[...]
        dy = dh_v * g_ref[...]
        c = jnp.mean(dy * y, axis=-1, keepdims=True)
        dx = r * (dy - y * c) + dres_ref[...]
        dx_ref[...] = dx
        dxb_ref[...] = dx.astype(BF16)
        part = jnp.sum(dh_v * y, axis=0, keepdims=True)

        @pl.when(i == 0)
        def _():
            dg_ref[...] = part

        @pl.when(i > 0)
        def _():
            dg_ref[...] += part

    row = pl.BlockSpec((tm, D), lambda i: (i, 0))
    vec = pl.BlockSpec((1, D), lambda i: (0, 0))
    return pl.pallas_call(
        body,
        name=name,
        grid=(T // tm,),
        in_specs=[row, vec, row, row],
        out_specs=[row, row, vec],
        out_shape=[jax.ShapeDtypeStruct((T, D), F32), jax.ShapeDtypeStruct((T, D), BF16), jax.ShapeDtypeStruct((1, D), F32)],
        compiler_params=_params(("arbitrary",)),
    )(x, g, dh, dres)


GROUP_W = 4 * HEAD_DIM
N_DIL_BLOCKS = 3 * N_GROUPS
N_NORMED = 2 * N_GROUPS


def _kind_of_group(j, g):
    return jnp.clip((j - g) // N_GROUPS, 0, 2)


def _head_rstd(xh):
    return lax.rsqrt(jnp.mean(xh * xh, axis=-1, keepdims=True) + EPS)


def _prep_fwd(proj, gains, cos, sin, n_sb_blocks, *, name):
    T = proj.shape[0]
    tm = _pick(T, (512, 256, 128))

    def body(p_ref, gain_ref, cos_ref, sin_ref, o0_ref, o1_ref, o2_ref, os_ref):
        j = pl.program_id(1)
        for g, o_ref in enumerate((o0_ref, o1_ref, o2_ref)):
            @pl.when(jnp.logical_and(j % N_GROUPS == g, j < N_NORMED))
            def _():
                cos_v, sin_v = cos_ref[...], sin_ref[...]
                for hh in range(4):
                    sl = slice(hh * HEAD_DIM, (hh + 1) * HEAD_DIM)
                    xh = p_ref[:, sl]
                    y = (xh * _head_rstd(xh)) * gain_ref[0, :, sl]
                    o_ref[:, sl] = (y * cos_v + pltpu.roll(y, HEAD_DIM // 2, 1) * sin_v).astype(BF16)

            @pl.when(j == N_NORMED + g)
            def _():
                o_ref[...] = p_ref[...].astype(BF16)

        @pl.when(j >= N_DIL_BLOCKS)
        def _():
            os_ref[...] = p_ref[...].astype(BF16)

    def group_spec(g):
        return pl.BlockSpec((tm, GROUP_W), lambda i, j: (i, _kind_of_group(j, g)))

    return pl.pallas_call(
        body,
        name=name,
        grid=(T // tm, N_DIL_BLOCKS + n_sb_blocks),
        in_specs=[
            pl.BlockSpec((tm, GROUP_W), lambda i, j: (i, j)),
            pl.BlockSpec((1, 1, GROUP_W), lambda i, j: (jnp.minimum(j, N_NORMED - 1), 0, 0)),
            pl.BlockSpec((tm, HEAD_DIM), lambda i, j: (i, 0)),
            pl.BlockSpec((tm, HEAD_DIM), lambda i, j: (i, 0)),
        ],
        out_specs=[group_spec(0), group_spec(1), group_spec(2),
                   pl.BlockSpec((tm, GROUP_W), lambda i, j: (i, jnp.maximum(j - N_DIL_BLOCKS, 0)))],
        out_shape=[jax.ShapeDtypeStruct((T, 3 * GROUP_W), BF16)] * N_GROUPS
        + [jax.ShapeDtypeStruct((T, n_sb_blocks * GROUP_W), BF16)],
        compiler_params=_params(("parallel", "arbitrary")),
    )(proj, gains, cos, sin)


def _prep_bwd(proj, dqkv, gains, cos, sin, *, name):
    T = proj.shape[0]
    tm = _pick(T, (512, 256, 128))

    def body(p_ref, d0_ref, d1_ref, d2_ref, gain_ref, cos_ref, sin_ref, o_ref, dgain_ref):
        j, i = pl.program_id(0), pl.program_id(1)

        def normed_bwd(d_ref):
            cos_v, sin_v = cos_ref[...], sin_ref[...]
            part = jnp.zeros((1, HEAD_DIM), F32)
            for hh in range(4):
                sl = slice(hh * HEAD_DIM, (hh + 1) * HEAD_DIM)
                xh = p_ref[:, sl]
                r = _head_rstd(xh)
                y0 = xh * r
                d_out = d_ref[:, sl]
                d_yg = d_out * cos_v + pltpu.roll(d_out * sin_v, HEAD_DIM // 2, 1)
                part = part + jnp.sum(d_yg * y0, axis=0, keepdims=True)
                dy0 = d_yg * gain_ref[0, :, sl]
                c = jnp.mean(dy0 * y0, axis=-1, keepdims=True)
                o_ref[:, sl] = (r * (dy0 - y0 * c)).astype(BF16)

            @pl.when(i == 0)
            def _():
                dgain_ref[0] = part

            @pl.when(i > 0)
            def _():
                dgain_ref[0] += part

        for g, d_ref in enumerate((d0_ref, d1_ref, d2_ref)):
            @pl.when(jnp.logical_and(j % N_GROUPS == g, j < N_NORMED))
            def _():
                normed_bwd(d_ref)

            @pl.when(j == N_NORMED + g)
            def _():
                o_ref[...] = d_ref[...].astype(BF16)

    gain_row = lambda j, i: (jnp.minimum(j, N_NORMED - 1), 0, 0)

    def grad_spec(g):
        return pl.BlockSpec((tm, GROUP_W), lambda j, i: (i, _kind_of_group(j, g)))

    return pl.pallas_call(
        body,
        name=name,
        grid=(N_DIL_BLOCKS, T // tm),
        in_specs=[
            pl.BlockSpec((tm, GROUP_W), lambda j, i: (i, j)),
            grad_spec(0), grad_spec(1), grad_spec(2),
            pl.BlockSpec((1, 1, GROUP_W), gain_row),
            pl.BlockSpec((tm, HEAD_DIM), lambda j, i: (i, 0)),
            pl.BlockSpec((tm, HEAD_DIM), lambda j, i: (i, 0)),
        ],
        out_specs=[pl.BlockSpec((tm, GROUP_W), lambda j, i: (i, j)), pl.BlockSpec((1, 1, HEAD_DIM), gain_row)],
        out_shape=[jax.ShapeDtypeStruct((T, N_DIL_BLOCKS * GROUP_W), BF16),
                   jax.ShapeDtypeStruct((2 * N_GROUPS, 1, HEAD_DIM), F32)],
        compiler_params=_params(("arbitrary", "arbitrary")),
    )(proj, *dqkv, gains, cos, sin)


def _nt(a, b):
    return lax.dot_general(a, b, _DIMS["nt"], preferred_element_type=F32)


def _tn(a, b):
    return lax.dot_general(a, b, _DIMS["tn"], preferred_element_type=F32)


def _nn(a, b):
    return jnp.dot(a, b, preferred_element_type=F32)


def _window_masks():
    row = lax.broadcasted_iota(jnp.int32, (BLOCK, BLOCK), 0)
    col = lax.broadcasted_iota(jnp.int32, (BLOCK, BLOCK), 1)
    return row >= col, col >= row


def _heads():
    return [slice(hh * HEAD_DIM, (hh + 1) * HEAD_DIM) for hh in range(GROUP_W // HEAD_DIM)]


def _dil_fwd(qkv, g, *, name):
    T = qkv.shape[0]
    r = DILATIONS[g]
    L = T // r
    nb = L // BLOCK
    scale = 1.0 / math.sqrt(HEAD_DIM)
    view = qkv.reshape(L, r * 3 * GROUP_W)

    def body(q_ref, kc_ref, kp_ref, vc_ref, vp_ref, o_ref, ld_ref):
        n = pl.program_id(1)
        m_cur, m_prev = _window_masks()
        m_prev = jnp.logical_and(m_prev, n > 0)
        heads = _heads()
        qs = [q_ref[:, sl] for sl in heads]
        s_c = [jnp.where(m_cur, _nt(q, kc_ref[:, sl]) * scale, NEG) for q, sl in zip(qs, heads)]
        s_p = [jnp.where(m_prev, _nt(q, kp_ref[:, sl]) * scale, NEG) for q, sl in zip(qs, heads)]
        m = [jnp.maximum(jnp.max(c, axis=-1, keepdims=True), jnp.max(p, axis=-1, keepdims=True)) for c, p in zip(s_c, s_p)]
        p_c = [jnp.exp(c - mx) for c, mx in zip(s_c, m)]
        p_p = [jnp.exp(p - mx) for p, mx in zip(s_p, m)]
        l = [jnp.sum(c, axis=-1, keepdims=True) + jnp.sum(p, axis=-1, keepdims=True) for c, p in zip(p_c, p_p)]
        inv = [1.0 / v for v in l]
        outs = [_nn((c * r).astype(BF16), vc_ref[:, sl]) + _nn((p * r).astype(BF16), vp_ref[:, sl])
                for c, p, r, sl in zip(p_c, p_p, inv, heads)]
        for sl, o, mx, v in zip(heads, outs, m, l):
            o_ref[:, sl] = o
            ld_ref[:, sl] = jnp.broadcast_to(mx + jnp.log(v), (BLOCK, HEAD_DIM))

    blk = (BLOCK, GROUP_W)
    cur = lambda kind: pl.BlockSpec(blk, lambda c, n: (n, 3 * c + kind))
    prev = lambda kind: pl.BlockSpec(blk, lambda c, n: (jnp.maximum(n - 1, 0), 3 * c + kind))
    out_spec = pl.BlockSpec(blk, lambda c, n: (n, c))
    o, ld = pl.pallas_call(
        body,
        name=name,
        grid=(r, nb),
        in_specs=[cur(0), cur(1), prev(1), cur(2), prev(2)],
        out_specs=[out_spec, out_spec],
        out_shape=[jax.ShapeDtypeStruct((L, r * GROUP_W), F32)] * 2,
        compiler_params=_params(("parallel", "arbitrary")),
    )(view, view, view, view, view)
    return o.reshape(T, GROUP_W), ld.reshape(T, GROUP_W)


def _dil_bwd(qkv, do, ld, dterm, g, *, name):
    T = qkv.shape[0]
    r = DILATIONS[g]
    L = T // r
    nb = L // BLOCK
    scale = 1.0 / math.sqrt(HEAD_DIM)
    view = qkv.reshape(L, r * 3 * GROUP_W)
    do_v, ld_v, dt_v = (t.reshape(L, r * GROUP_W) for t in (do, ld, dterm))

    def body(q_ref, qn_ref, kc_ref, kp_ref, vc_ref, vp_ref, do_ref, don_ref, ld_ref, ldn_ref, dt_ref, dtn_ref, out_ref):
        n = pl.program_id(1)
        m_cur, m_prev = _window_masks()
        has_prev, has_next = jnp.logical_and(m_prev, n > 0), jnp.logical_and(m_prev, n < nb - 1)

        def tile(q, k, v, do_t, ld_t, dt_t, mask):
            s = _nt(q, k) * scale
            p = jnp.where(mask, jnp.exp(s - ld_t[:, 0:1]), 0.0)
            ds = p * (_nt(do_t, v) + dt_t[:, 0:1]) * scale
            return p.astype(BF16), ds.astype(BF16)

        heads = _heads()
        kc, vc, kp = ([ref[:, sl] for sl in heads] for ref in (kc_ref, vc_ref, kp_ref))
        q, do_t, qn, don = ([ref[:, sl] for sl in heads] for ref in (q_ref, do_ref, qn_ref, don_ref))
        cc = [tile(q[h], kc[h], vc[h], do_t[h], ld_ref[:, sl], dt_ref[:, sl], m_cur) for h, sl in enumerate(heads)]
        cp = [tile(q[h], kp[h], vp_ref[:, sl], do_t[h], ld_ref[:, sl], dt_ref[:, sl], has_prev) for h, sl in enumerate(heads)]
        nc = [tile(qn[h], kc[h], vc[h], don[h], ldn_ref[:, sl], dtn_ref[:, sl], has_next) for h, sl in enumerate(heads)]
        dq = [_nn(cc[h][1], kc[h]) + _nn(cp[h][1], kp[h]) for h in range(len(heads))]
        dk = [_tn(cc[h][1], q[h]) + _tn(nc[h][1], qn[h]) for h in range(len(heads))]
        dv = [_tn(cc[h][0], do_t[h]) + _tn(nc[h][0], don[h]) for h in range(len(heads))]
        for kind, grads in enumerate((dq, dk, dv)):
            for h, grad in enumerate(grads):
                out_ref[:, kind * GROUP_W + h * HEAD_DIM:kind * GROUP_W + (h + 1) * HEAD_DIM] = grad

    blk = (BLOCK, GROUP_W)
    qkv_spec = lambda kind, shift: pl.BlockSpec(blk, lambda c, n: (jnp.clip(n + shift, 0, nb - 1), 3 * c + kind))
    row_spec = lambda shift: pl.BlockSpec(blk, lambda c, n: (jnp.clip(n + shift, 0, nb - 1), c))
    out = pl.pallas_call(
        body,
        name=name,
        grid=(r, nb),
        in_specs=[qkv_spec(0, 0), qkv_spec(0, 1), qkv_spec(1, 0), qkv_spec(1, -1), qkv_spec(2, 0), qkv_spec(2, -1),
                  row_spec(0), row_spec(1), row_spec(0), row_spec(1), row_spec(0), row_spec(1)],
        out_specs=pl.BlockSpec((BLOCK, 3 * GROUP_W), lambda c, n: (n, c)),
        out_shape=jax.ShapeDtypeStruct((L, r * 3 * GROUP_W), F32),
        compiler_params=_params(("parallel", "arbitrary")),
    )(view, view, view, view, view, view, do_v, do_v, ld_v, ld_v, dt_v, dt_v)
    return out.reshape(T, 3 * GROUP_W)


def _group_weights(ld_refs):
    lds = [r[...] for r in ld_refs]
    m = jnp.maximum(jnp.maximum(lds[0], lds[1]), lds[2])
    es = [jnp.exp(v - m) for v in lds]
    inv = 1.0 / (es[0] + es[1] + es[2])
    return [e * inv for e in es]


def _merge_fwd(os_, lds, *, name):
    T = os_[0].shape[0]
    tm = _pick(T, (1024, 512, 256, 128))

    def body(o0, o1, o2, l0, l1, l2, y_ref):
        w = _group_weights((l0, l1, l2))
        y_ref[...] = (w[0] * o0[...] + w[1] * o1[...] + w[2] * o2[...]).astype(BF16)

    spec = pl.BlockSpec((tm, GROUP_W), lambda i: (i, 0))
    return pl.pallas_call(
        body, name=name, grid=(T // tm,), in_specs=[spec] * 6, out_specs=spec,
        out_shape=jax.ShapeDtypeStruct((T, GROUP_W), BF16), compiler_params=_params(("parallel",)),
    )(*os_, *lds)


def _merge_bwd(os_, lds, dy, *, name):
    T = dy.shape[0]
    tm = _pick(T, (512, 256, 128))

    def body(o0, o1, o2, l0, l1, l2, dy_ref, do0, do1, do2, dt0, dt1, dt2):
        w = _group_weights((l0, l1, l2))
        dy_v = dy_ref[...]
        y = w[0] * o0[...] + w[1] * o1[...] + w[2] * o2[...]
        prod = dy_v * y
        for hh in range(4):
            sl = slice(hh * HEAD_DIM, (hh + 1) * HEAD_DIM)
            s = jnp.sum(prod[:, sl], axis=-1, keepdims=True)
            for wg, dt in zip(w, (dt0, dt1, dt2)):
                dt[:, sl] = -wg[:, sl] * s
        for wg, do in zip(w, (do0, do1, do2)):
            do[...] = (wg * dy_v).astype(BF16)

    spec = pl.BlockSpec((tm, GROUP_W), lambda i: (i, 0))
    outs = pl.pallas_call(
        body, name=name, grid=(T // tm,), in_specs=[spec] * 7, out_specs=[spec] * 6,
        out_shape=[jax.ShapeDtypeStruct((T, GROUP_W), BF16)] * 3 + [jax.ShapeDtypeStruct((T, GROUP_W), F32)] * 3,
        compiler_params=_params(("parallel",)),
    )(*os_, *lds, dy)
    return outs[:3], outs[3:]


SB_ROWS = 512
SB_KEYS = 256


def _sum_matrix(inclusive):
    j = lax.broadcasted_iota(jnp.int32, (2 * BLOCK, 2 * BLOCK), 0) % BLOCK
    s = lax.broadcasted_iota(jnp.int32, (2 * BLOCK, 2 * BLOCK), 1)
    later = (j >= s) if inclusive else (j > s)
    return jnp.logical_or(s >= BLOCK, later).astype(BF16)


def _block_sums(x, mat):
    hi = x.astype(BF16)
    lo = (x - hi.astype(F32)).astype(BF16)
    r = _nn(jnp.concatenate([hi, lo], axis=1), mat)
    return r[:, :BLOCK], r[:, BLOCK:]


def _log_terms(z):
    t = jnp.log(1.0 + jnp.exp(-jnp.abs(z)))
    return -(jnp.maximum(z, 0.0) + t), jnp.minimum(z, 0.0) - t


SB_DEAD = -105.0


def _sb_alive(n_chunks, state):
    return jnp.logical_and(state[0] < n_chunks, jnp.max(state[1]) > SB_DEAD)


def _causal_mask(rows, cols, first_col):
    row = lax.broadcasted_iota(jnp.int32, (rows, cols), 0)
    col = lax.broadcasted_iota(jnp.int32, (rows, cols), 1)
    return col + first_col < row


def _sb_fwd(qkv, n_heads, col0, *, name):
    T = qkv.shape[0]
    tq = _pick(T, (SB_ROWS, BLOCK))
    kc = _pick(tq, (SB_KEYS, BLOCK))
    nq, nsub, per_tile = T // tq, kc // BLOCK, tq // kc
    scale = 1.0 / math.sqrt(HEAD_DIM)

    def body(q_ref, k_ref, v_ref, o_ref, ob_ref):
        i = pl.program_id(1)
        q = q_ref[...]
        mat = _sum_matrix(False)

        def chunk(j, carry, acc, mask):
            rows = pl.ds(pl.multiple_of(j * kc, kc), kc)
            z = _nt(q, k_ref[rows, :]) * scale
            lk, ls = _log_terms(z)
            if mask is not None:
                lk = jnp.where(mask, lk, 0.0)
            a = []
            for b in reversed(range(nsub)):
                sl = slice(b * BLOCK, (b + 1) * BLOCK)
                later, total = _block_sums(lk[:, sl], mat)
                a.append(jnp.exp(ls[:, sl] + (later + carry)))
                carry = carry + total
            a = jnp.concatenate(a[::-1], axis=1)
            if mask is not None:
                a = jnp.where(mask, a, 0.0)
            return carry, acc + _nn(a.astype(BF16), v_ref[rows, :])

        carry = acc = jnp.zeros((tq, HEAD_DIM), F32)
        for d in reversed(range(per_tile)):
            carry, acc = chunk(i * per_tile + d, carry, acc, _causal_mask(tq, kc, d * kc))

        def step(state):
            carry, acc = chunk(i * per_tile - 1 - state[0], state[1], state[2], None)
            return state[0] + 1, carry, acc

        _, carry, acc = lax.while_loop(functools.partial(_sb_alive, i * per_tile), step, (0, carry, acc))
        o_ref[...] = acc
        ob_ref[...] = acc.astype(BF16)

    blk = (tq, HEAD_DIM)
    out_spec = pl.BlockSpec(blk, lambda h, i: (i, h))
    return pl.pallas_call(
        body,
        name=name,
        grid=(n_heads, nq),
        in_specs=[
            pl.BlockSpec(blk, lambda h, i: (i, col0 + h)),
            pl.BlockSpec((T, HEAD_DIM), lambda h, i: (0, col0 + n_heads + h)),
            pl.BlockSpec((T, HEAD_DIM), lambda h, i: (0, col0 + 2 * n_heads + h)),
        ],
        out_specs=[out_spec, out_spec],
        out_shape=[jax.ShapeDtypeStruct((T, n_heads * HEAD_DIM), F32), jax.ShapeDtypeStruct((T, n_heads * HEAD_DIM), BF16)],
        compiler_params=_params(("parallel", "arbitrary")),
    )(qkv, qkv, qkv)


def _sb_bwd(qkv, o32, do, n_heads, col0, *, name):
    T = qkv.shape[0]
    tq = _pick(T, (SB_ROWS, BLOCK))
    kc = _pick(tq, (SB_KEYS, BLOCK))
    nq, nsub, per_tile = T // tq, kc // BLOCK, tq // kc
    scale = 1.0 / math.sqrt(HEAD_DIM)

    def body(q_ref, k_ref, v_ref, o_ref, do_ref, dq_ref, dk_ref, dv_ref, dk_acc, dv_acc):
        i = pl.program_id(1)

        @pl.when(i == 0)
        def _():
            dk_acc[...] = jnp.zeros_like(dk_acc)
            dv_acc[...] = jnp.zeros_like(dv_acc)

        q, do_t = q_ref[...], do_ref[...]
        delta = jnp.broadcast_to(jnp.sum(do_t.astype(F32) * o_ref[...], axis=-1, keepdims=True), (tq, HEAD_DIM))
        mat, mat_incl = _sum_matrix(False), _sum_matrix(True)

        def chunk(j, carry_b, carry_g, dq, mask):
            rows = pl.ds(pl.multiple_of(j * kc, kc), kc)
            k_t, v_t = k_ref[rows, :], v_ref[rows, :]
            z = _nt(q, k_t) * scale
            lk, ls = _log_terms(z)
            if mask is not None:
                lk = jnp.where(mask, lk, 0.0)
            d_a = _nt(do_t, v_t)
            a_parts, dz_parts = [], []
            for b in reversed(range(nsub)):
                sl = slice(b * BLOCK, (b + 1) * BLOCK)
                later, total = _block_sums(lk[:, sl], mat)
                a = jnp.exp(ls[:, sl] + (later + carry_b))
                carry_b = carry_b + total
                if mask is not None:
                    a = jnp.where(mask[:, sl], a, 0.0)
                a_b = a.astype(BF16)
                g = a_b.astype(F32) * d_a[:, sl]
                from_here, total_g = _block_sums(g, mat_incl)
                before = delta - (from_here + carry_g)
                carry_g = carry_g + total_g
                sig = jnp.exp(ls[:, sl])
                dz = (g - sig * (g + before)) * scale
                a_parts.append(a_b)
                dz_parts.append(dz)
            dz = jnp.concatenate(dz_parts[::-1], axis=1)
            if mask is not None:
                dz = jnp.where(mask, dz, 0.0)
            dz_b = dz.astype(BF16)
            dk_acc[rows, :] += _tn(dz_b, q)
            dv_acc[rows, :] += _tn(jnp.concatenate(a_parts[::-1], axis=1), do_t)
            return carry_b, carry_g, dq + _nn(dz_b, k_t)

        zero = jnp.zeros((tq, HEAD_DIM), F32)
        state = (zero, zero, zero)
        for d in reversed(range(per_tile)):
            state = chunk(i * per_tile + d, *state, _causal_mask(tq, kc, d * kc))

        def step(st):
            return (st[0] + 1,) + chunk(i * per_tile - 1 - st[0], st[1], st[2], st[3], None)

        state = lax.while_loop(functools.partial(_sb_alive, i * per_tile), step, (0,) + state)
        dq_ref[...] = state[3].astype(BF16)

        @pl.when(i == nq - 1)
        def _():
            dk_ref[...] = dk_acc[...].astype(BF16)
            dv_ref[...] = dv_acc[...].astype(BF16)

    blk = (tq, HEAD_DIM)
    full = (T, HEAD_DIM)
    dshape = jax.ShapeDtypeStruct((T, n_heads * HEAD_DIM), BF16)
    return pl.pallas_call(
        body,
        name=name,
        grid=(n_heads, nq),
        in_specs=[
            pl.BlockSpec(blk, lambda h, i: (i, col0 + h)),
            pl.BlockSpec(full, lambda h, i: (0, col0 + n_heads + h)),
            pl.BlockSpec(full, lambda h, i: (0, col0 + 2 * n_heads + h)),
            pl.BlockSpec(blk, lambda h, i: (i, h)),
            pl.BlockSpec(blk, lambda h, i: (i, h)),
        ],
        out_specs=[pl.BlockSpec(blk, lambda h, i: (i, h)), pl.BlockSpec(full, lambda h, i: (0, h)),
                   pl.BlockSpec(full, lambda h, i: (0, h))],
        out_shape=[dshape, dshape, dshape],
        scratch_shapes=[pltpu.VMEM(full, F32), pltpu.VMEM(full, F32)],
        compiler_params=_params(("arbitrary", "arbitrary")),
    )(qkv, qkv, qkv, o32, do)


def _gate_fwd(y_dil, y_sb, w_up_dil, w_up_sb, proj, gate_b, gate_col0, *, name):
    T, D = y_dil.shape[0], w_up_dil.shape[0]
    tm = _pick(T, (512, 256, 128))
    tn = _pick(D, (512, 256, 128))
    c0, nbr = gate_col0 // tn, D // tn

    def body(yd_ref, ys_ref, wd_ref, ws_ref, gp0_ref, gp1_ref, b_ref, o_ref):
        g0 = jax.nn.sigmoid(gp0_ref[...] + b_ref[0:1, :])
        g1 = jax.nn.sigmoid(gp1_ref[...] + b_ref[1:2, :])
        o_ref[...] = (g0 * _nt(yd_ref[...], wd_ref[...]) + g1 * _nt(ys_ref[...], ws_ref[...])).astype(BF16)

    return pl.pallas_call(
        body,
        name=name,
        grid=(T // tm, nbr),
        in_specs=[
            pl.BlockSpec((tm, y_dil.shape[1]), lambda i, j: (i, 0)),
            pl.BlockSpec((tm, y_sb.shape[1]), lambda i, j: (i, 0)),
            pl.BlockSpec((tn, w_up_dil.shape[1]), lambda i, j: (j, 0)),
            pl.BlockSpec((tn, w_up_sb.shape[1]), lambda i, j: (j, 0)),
            pl.BlockSpec((tm, tn), lambda i, j: (i, c0 + j)),
            pl.BlockSpec((tm, tn), lambda i, j: (i, c0 + nbr + j)),
            pl.BlockSpec((2, tn), lambda i, j: (0, j)),
        ],
        out_specs=pl.BlockSpec((tm, tn), lambda i, j: (i, j)),
        out_shape=jax.ShapeDtypeStruct((T, D), BF16),
        compiler_params=_params(("parallel", "parallel")),
    )(y_dil, y_sb, w_up_dil, w_up_sb, proj, proj, gate_b)


def _gate_bwd(y, w_up, proj, gate_b, dmixed, branch, gate_col0, *, name):
    T, D = y.shape[0], w_up.shape[0]
    tm = _pick(T, (512, 256, 128))
    tn = _pick(D, (512, 256, 128))
    c0 = gate_col0 // tn + branch * (D // tn)

    def body(y_ref, w_ref, gp_ref, b_ref, dm_ref, dup_ref, dgp_ref, db_ref):
        i = pl.program_id(1)
        g = jax.nn.sigmoid(gp_ref[...] + b_ref[branch:branch + 1, :])
        dm = dm_ref[...]
        dup_ref[...] = (dm * g).astype(BF16)
        dgp = (dm * _nt(y_ref[...], w_ref[...])) * (g * (1.0 - g))
        dgp_ref[...] = dgp.astype(BF16)
        part = jnp.sum(dgp, axis=0, keepdims=True)

        @pl.when(i == 0)
        def _():
            db_ref[...] = part

        @pl.when(i > 0)
        def _():
            db_ref[...] += part

    tile = pl.BlockSpec((tm, tn), lambda j, i: (i, j))
    return pl.pallas_call(
        body,
        name=name,
        grid=(D // tn, T // tm),
        in_specs=[
            pl.BlockSpec((tm, y.shape[1]), lambda j, i: (i, 0)),
            pl.BlockSpec((tn, w_up.shape[1]), lambda j, i: (j, 0)),
            pl.BlockSpec((tm, tn), lambda j, i: (i, c0 + j)),
            pl.BlockSpec((2, tn), lambda j, i: (0, j)),
            tile,
        ],
        out_specs=[tile, tile, pl.BlockSpec((1, tn), lambda j, i: (0, j))],
        out_shape=[jax.ShapeDtypeStruct((T, D), BF16), jax.ShapeDtypeStruct((T, D), BF16), jax.ShapeDtypeStruct((1, D), F32)],
        compiler_params=_params(("parallel", "arbitrary")),
    )(y, w_up, proj, gate_b, dmixed)


def _loss_head(y, target, *, name):
    T, D = y.shape
    tm = _pick(T, (256, 128))

    def body(y_ref, t_ref, dy_ref, dyb_ref, l_ref):
        i = pl.program_id(0)
        err = y_ref[...] - t_ref[...]
        dy = err * (1.0 / D)
        dy_ref[...] = dy
        dyb_ref[...] = dy.astype(BF16)
        part = 0.5 * jnp.sum(jnp.mean(err * err, axis=-1, keepdims=True), axis=0, keepdims=True)

        @pl.when(i == 0)
        def _():
            l_ref[...] = part

        @pl.when(i > 0)
        def _():
            l_ref[...] += part

    row = pl.BlockSpec((tm, D), lambda i: (i, 0))
    return pl.pallas_call(
        body, name=name, grid=(T // tm,), in_specs=[row, row],
        out_specs=[row, row, pl.BlockSpec((1, 1), lambda i: (0, 0))],
        out_shape=[jax.ShapeDtypeStruct((T, D), F32), jax.ShapeDtypeStruct((T, D), BF16), jax.ShapeDtypeStruct((1, 1), F32)],
        compiler_params=_params(("arbitrary",)),
    )(y, target)


def _sum_parts(p_ref, own):
    slot = _slot(_place())
    g = jnp.where(slot == 0, own, p_ref[0].astype(F32))
    for s in range(1, N_DEV):
        g = g + jnp.where(slot == s, own, p_ref[s].astype(F32))
    return g


def _held(step, l, last):
    layer, i = step
    return jnp.where(layer == l, i, last * (layer > l))


def _reduce_adamw(parts, own, w, m, v, *, own_chunked=True, name):
    n_layers = len(parts)
    C = w.shape[1]
    R = w.shape[0] // n_layers
    tr = R
    for cand in (1024, 512, 256, 128, 64, 32, 16, 8):
        if R % cand == 0 and cand * C <= 256 * 1024:
            tr = cand
            break
    nr = R // tr

    def body(*refs):
        p_refs, o_refs = refs[:n_layers], refs[n_layers:2 * n_layers]
        w_ref, m_ref, v_ref, g_ref, d_ref, nm_ref, nv_ref = refs[2 * n_layers:]
        layer = pl.program_id(0)
        for l, (p_ref, o_ref) in enumerate(zip(p_refs, o_refs)):
            @pl.when(layer == l)
            def _():
                g_ref[...] = _sum_parts(p_ref, (o_ref[0] if own_chunked else o_ref[...]).astype(F32))

        _adamw_update(g_ref[...], w_ref, m_ref, v_ref, d_ref, nm_ref, nv_ref)

    def part_spec(l):
        return pl.BlockSpec((N_DEV, tr, C), lambda *step: (0, _held(step, l, nr - 1), 0))

    def own_spec(l):
        if own_chunked:
            return pl.BlockSpec((1, tr, C), lambda *step: (_slot(_place()), _held(step, l, nr - 1), 0))
        return pl.BlockSpec((tr, C), lambda *step: (_held(step, l, nr - 1), 0))

    row = pl.BlockSpec((tr, C), lambda layer, i: (layer * nr + i, 0))
    return pl.pallas_call(
        body, name=name, grid=(n_layers, nr),
        in_specs=[part_spec(l) for l in range(n_layers)] + [own_spec(l) for l in range(n_layers)] + [row, row, row],
        out_specs=[row] * 4, out_shape=[jax.ShapeDtypeStruct(w.shape, F32)] * 4,
        compiler_params=_params(("arbitrary", "arbitrary")),
    )(*parts, *own, w, m, v)


def _adamw_update(g, w_ref, m_ref, v_ref, d_ref, nm_ref, nv_ref):
    m_new = ADAM_B1 * m_ref[...] + (1.0 - ADAM_B1) * g
    v_new = ADAM_B2 * v_ref[...] + (1.0 - ADAM_B2) * (g * g)
    m_hat = m_new / (1.0 - ADAM_B1 ** ADAM_STEP)
    v_hat = v_new / (1.0 - ADAM_B2 ** ADAM_STEP)
    d_ref[...] = -ADAM_LR * (m_hat / (jnp.sqrt(v_hat) + ADAM_EPS) + ADAM_WD * w_ref[...])
    nm_ref[...] = m_new
    nv_ref[...] = v_new


def _reduce_adamw_t(parts, own, w, m, v, *, name):
    n_layers = len(parts)
    _, n, K = parts[0].shape
    n_pad = w.shape[1]
    tm = _pick(K, (128,))
    nr = K // tm

    def body(*refs):
        p_refs, o_refs = refs[:n_layers], refs[n_layers:2 * n_layers]
        w_ref, m_ref, v_ref, g_ref, d_ref, nm_ref, nv_ref = refs[2 * n_layers:]
        layer = pl.program_id(0)
        for l, (p_ref, o_ref) in enumerate(zip(p_refs, o_refs)):
            @pl.when(layer == l)
            def _():
                g_t = _sum_parts(p_ref, o_ref[0].astype(F32))
                if n_pad > n:
                    g_t = jnp.concatenate([g_t, jnp.zeros((n_pad - n, tm), F32)], axis=0)
                g_ref[...] = g_t.T

        _adamw_update(g_ref[...], w_ref, m_ref, v_ref, d_ref, nm_ref, nv_ref)

    def part_spec(l):
        return pl.BlockSpec((N_DEV, n, tm), lambda *step: (0, 0, _held(step, l, nr - 1)))

    def own_spec(l):
        return pl.BlockSpec((1, n, tm), lambda *step: (_slot(_place()), 0, _held(step, l, nr - 1)))

    row = pl.BlockSpec((tm, n_pad), lambda layer, i: (layer * nr + i, 0))
    return pl.pallas_call(
        body, name=name, grid=(n_layers, nr),
        in_specs=[part_spec(l) for l in range(n_layers)] + [own_spec(l) for l in range(n_layers)] + [row, row, row],
        out_specs=[row] * 4, out_shape=[jax.ShapeDtypeStruct(w.shape, F32)] * 4,
        compiler_params=_params(("arbitrary", "arbitrary")),
    )(*parts, *own, w, m, v)


def _transpose_cast(x, *, name):
    R, C = x.shape
    tr, tc = _pick(R, (512, 256, 128)), _pick(C, (512, 256, 128))

    def body(x_ref, o_ref):
        o_ref[...] = x_ref[...].astype(F32).T.astype(BF16)

    return pl.pallas_call(
        body, name=name, grid=(R // tr, C // tc),
        in_specs=[pl.BlockSpec((tr, tc), lambda i, j: (i, j))],
        out_specs=pl.BlockSpec((tc, tr), lambda i, j: (j, i)),
        out_shape=jax.ShapeDtypeStruct((C, R), BF16),
        compiler_params=_params(("parallel", "parallel")),
    )(x)


_ANY = pl.BlockSpec(memory_space=pl.ANY)


def _place():
    return lax.axis_index("x"), lax.axis_index("y"), lax.axis_index("c")


def _slot(p):
    return 4 * p[0] + 2 * p[1] + p[2]


_HBM = pl.BlockSpec(memory_space=pltpu.HBM)
_SEM = pl.BlockSpec(memory_space=pltpu.SEMAPHORE)
_EFFECT = pltpu.SideEffectType.DATAFLOW_SIDE_EFFECTING
N_PEERS = N_DEV - 1


def _peers(me):
    flips = [(fx, fy, fc) for fx in (0, 1) for fy in (0, 1) for fc in (0, 1)][1:]
    return [tuple(1 - v if f else v for v, f in zip(me, flip)) for flip in flips]


def _peer_copy(src, lands, t, k, sender, to, send_sems, recv_sems):
    return pltpu.make_async_remote_copy(
        src_ref=src, dst_ref=lands[t].at[_slot(sender)], send_sem=send_sems.at[N_PEERS * t + k],
        recv_sem=recv_sems.at[N_PEERS * t + k], device_id=to, device_id_type=MESH)


class _Exchange:
    def __init__(self, chunked, whole, name):
        self.arrays = [pltpu.with_memory_space_constraint(a, pltpu.HBM) for a in list(chunked) + list(whole)]
        self.n, self.n_chunked, self.name = len(self.arrays), len(chunked), name

    def _src(self, ins, t, dest):
        return ins[t].at[_slot(dest)] if t < self.n_chunked else ins[t]

    def _land_shape(self, t):
        a = self.arrays[t]
        return a.shape if t < self.n_chunked else (N_DEV,) + a.shape

    def start(self, after=None):
        n = self.n

        def body(*refs):
            ins, lands = refs[:n], refs[n:2 * n]
            send_sems, recv_sems = refs[-2 * n - 3], refs[-2 * n - 2]
            token = refs[-1]
            me = _place()
            for t in range(n):
                for k, peer in enumerate(_peers(me)):
                    _peer_copy(self._src(ins, t, peer), lands, t, k, me, peer, send_sems, recv_sems).start()
            token[...] = jnp.zeros_like(token)

        lands = [pltpu.with_memory_space_constraint(lax.empty(self._land_shape(t), a.dtype), pltpu.HBM)
                 for t, a in enumerate(self.arrays)]
        sems = pltpu.SemaphoreType.DMA((N_PEERS * n,))
        outs = pl.pallas_call(
            body,
            name=self.name + "_start",
            in_specs=[_HBM] * (2 * n) + ([_ANY] if after is not None else []),
            out_specs=[_SEM, _SEM] + [_HBM] * (2 * n) + [pl.BlockSpec(memory_space=pltpu.VMEM)],
            out_shape=[sems, sems] + [pltpu.HBM(a.shape, a.dtype) for a in self.arrays + lands]
            + [jax.ShapeDtypeStruct((8, 128), F32)],
            input_output_aliases={t: 2 + t for t in range(2 * n)},
            compiler_params=pltpu.CompilerParams(has_side_effects=_EFFECT),
        )(*self.arrays, *lands, *([after] if after is not None else []))
        self.sems, self.thru, self.lands, self.token = outs[:2], outs[2:2 + n], outs[2 + n:2 + 2 * n], outs[-1]
        return self.token

    def finish(self, after):
        n = self.n

        def wait_body(*refs):
            ins, lands, (send_sems, recv_sems) = refs[:n], refs[n:2 * n], refs[2 * n:2 * n + 2]
            me = _place()
            for t in range(n):
                for k, peer in enumerate(_peers(me)):
                    cp = _peer_copy(self._src(ins, t, peer), lands, t, k, peer, peer, send_sems, recv_sems)
                    cp.wait_send()
                    cp.wait_recv()

        outs = pl.pallas_call(
            wait_body,
            name=self.name + "_wait",
            in_specs=[_HBM] * (2 * n) + [_SEM, _SEM, _ANY],
            out_specs=[_HBM] * (2 * n),
            out_shape=[pltpu.HBM(a.shape, a.dtype) for a in self.thru + self.lands],
            input_output_aliases={t: t for t in range(2 * n)},
            compiler_params=pltpu.CompilerParams(has_side_effects=_EFFECT),
        )(*self.thru, *self.lands, *self.sems, after)
        return outs[:n], outs[n:]


def _exchange(chunked, whole, *, name):
    ride = _Ride(chunked, whole)
    n = ride.n

    def body(*refs):
        ride.start(refs[:n], refs[n:2 * n], refs[2 * n:])
        ride.forward(refs[:n], refs[n:2 * n], refs[2 * n:])
        ride.finish(refs[:n], refs[n:2 * n], refs[2 * n:])

    return pl.pallas_call(
        body,
        name=name,
        in_specs=[_ANY] * n,
        out_specs=[_ANY] * n,
        out_shape=ride.land_shapes(),
        scratch_shapes=ride.semaphores(),
        compiler_params=pltpu.CompilerParams(has_side_effects=True),
    )(*ride.arrays)


class _Ride:
    def __init__(self, chunked, whole):
        self.arrays = list(chunked) + list(whole)
        self.n, self.n_chunked = len(self.arrays), len(chunked)
        self.gather = self.n_chunked == 0

    def land_shapes(self):
        return [jax.ShapeDtypeStruct(a.shape if t < self.n_chunked else (N_DEV,) + a.shape, a.dtype)
                for t, a in enumerate(self.arrays)]

    def semaphores(self):
        sems = pltpu.SemaphoreType.DMA((N_PEERS * self.n,))
        return [sems, sems, pltpu.SemaphoreType.DMA((self.n,))]

    def _src(self, ins, t, dest):
        return ins[t].at[_slot(dest)] if t < self.n_chunked else ins[t]

    def _copies(self, ins, lands, sems):
        send_sems, recv_sems, local_sems = sems
        x, y, c = me = _place()
        if not self.gather:
            return [], [_peer_copy(self._src(ins, t, peer), lands, t, k, me, peer, send_sems, recv_sems)
                        for t in range(self.n) for k, peer in enumerate(_peers(me))]
        mine = [pltpu.make_async_copy(ins[t], lands[t].at[_slot(me)], local_sems.at[t]) for t in range(self.n)]
        first = []
        for t in range(self.n):
            first.append(self._hop(ins, lands, sems, t, 0, me, (x, y, 1 - c), ins[t]))
            first += [self._hop(ins, lands, sems, t, 1 + j, me, (*chip, c), ins[t]) for j, chip in enumerate(self._chips())]
        return mine, first

    def _chips(self):
        x, y, _ = _place()
        return [(1 - x, y), (x, 1 - y), (1 - x, 1 - y)]

    def _hop(self, ins, lands, sems, t, k, block, to, source=None):
        dst = lands[t].at[_slot(block)]
        return pltpu.make_async_remote_copy(
            src_ref=dst if source is None else source, dst_ref=dst, send_sem=sems[0].at[N_PEERS * t + k],
            recv_sem=sems[1].at[N_PEERS * t + k], device_id=to, device_id_type=MESH)

    def start(self, ins, lands, sems):
        mine, sends = self._copies(ins, lands, sems)
        for cp in mine + sends:
            cp.start()

    def _passed_on(self, ins, lands, sems):
        x, y, c = _place()
        return [self._hop(ins, lands, sems, t, 4 + j, (*chip, c), (x, y, 1 - c))
                for j, chip in enumerate(self._chips()) for t in range(self.n)]

    def forward(self, ins, lands, sems):
        if not self.gather:
            return
        x, y, c = me = _place()
        for j, chip in enumerate(self._chips()):
            for t in range(self.n):
                self._hop(ins, lands, sems, t, 1 + j, (*chip, c), me).wait_recv()
        for cp in self._passed_on(ins, lands, sems):
            cp.start()

    def finish(self, ins, lands, sems):
        mine, sends = self._copies(ins, lands, sems)
        x, y, c = me = _place()
        if self.gather:
            sibling = (x, y, 1 - c)
            sends = sends + self._passed_on(ins, lands, sems)
            for t in range(self.n):
                self._hop(ins, lands, sems, t, 0, sibling, me).wait_recv()
                for j, chip in enumerate(self._chips()):
                    self._hop(ins, lands, sems, t, 4 + j, (*chip, 1 - c), me).wait_recv()
        else:
            for t in range(self.n):
                for k, peer in enumerate(_peers(me)):
                    _peer_copy(self._src(ins, t, peer), lands, t, k, peer, peer, sems[0], sems[1]).wait_recv()
        for cp in sends:
            cp.wait_send()
        for cp in mine:
            cp.wait()


def _place_own(land, shard, *, name):
    n, K = shard.shape
    tr = n
    for cand in (1024, 736, 512, 256, 128):
        if n % cand == 0:
            tr = cand
            break

    def body(s_ref, land_ref, o_ref):
        del land_ref
        o_ref[0] = s_ref[...]

    return pl.pallas_call(
        body, name=name, grid=(n // tr,),
        in_specs=[pl.BlockSpec((tr, K), lambda i: (i, 0)), _ANY],
        out_specs=pl.BlockSpec((1, tr, K), lambda i: (_slot(_place()), i, 0)),
        out_shape=jax.ShapeDtypeStruct(land.shape, land.dtype),
        input_output_aliases={1: 0},
        compiler_params=_params(("arbitrary",)),
    )(shard, land)


def _rope_tables(T):
    half = HEAD_DIM // 2
    inv_freq = ROPE_THETA ** (-jnp.arange(half, dtype=F32) / half)
    ang = jnp.arange(T, dtype=F32)[:, None] * inv_freq[None, :]
    cos, sin = jnp.cos(ang), jnp.sin(ang)
    return jnp.concatenate([cos, cos], axis=-1), jnp.concatenate([-sin, sin], axis=-1)


def _gain_table(q_gain, k_gain):
    return jnp.tile(jnp.concatenate([q_gain, k_gain], axis=0), (1, 4))[:, None, :]


def _sb_heads(w_in):
    n_in, d_model = w_in.shape
    return (n_in - N_DIL_BLOCKS * GROUP_W - 2 * d_model) // (3 * HEAD_DIM)


def _carry(rides, key, args, call):
    make = rides.get(key) if rides else None
    if make is None:
        return call(None)
    ride, on_landed = make(*args)
    res, lands = call(ride)
    on_landed(lands)
    return res


def _layer_fwd(x, p, cos, sin, tag, rides=None):
    sb_heads = _sb_heads(p["w_in"])
    n_sb_blocks = 3 * sb_heads * HEAD_DIM // GROUP_W
    s = {"x": x}
    s["h"] = _rmsnorm_fwd(x, p["norm1"], name=f"norm1_fwd{tag}")
    s["proj"] = _carry(rides, "proj_fwd", (), lambda ride: _matmul(
        s["h"], p["w_in"], mode="nt", out_dtype=F32, name=f"proj_fwd{tag}", ride=ride))
    *s["qkv_d"], s["qkv_s"] = _prep_fwd(s["proj"], p["gains"], cos, sin, n_sb_blocks, name=f"prep_fwd{tag}")
    outs = [_dil_fwd(s["qkv_d"][g], g, name=f"dil{g}_fwd{tag}") for g in range(N_GROUPS)]
    s["o"], s["ld"] = [o for o, _ in outs], [ld for _, ld in outs]
    s["y_dil"] = _merge_fwd(s["o"], s["ld"], name=f"merge_fwd{tag}")
    s["y_sb32"], s["y_sb"] = _sb_fwd(s["qkv_s"], sb_heads, 0, name=f"sb_fwd{tag}")
    s["mixed"] = _gate_fwd(s["y_dil"], s["y_sb"], p["w_up_dil"], p["w_up_sb"], s["proj"], p["gate_b"],
                           (N_DIL_BLOCKS + n_sb_blocks) * GROUP_W, name=f"gate_fwd{tag}")
    s["x1"] = _matmul(s["mixed"], p["w_out"], mode="nn", out_dtype=F32, epilogue="add", extra=x, name=f"out_fwd{tag}")
    s["h2"] = _rmsnorm_fwd(s["x1"], p["norm2"], name=f"norm2_fwd{tag}")
    s["f"], s["a"] = _carry(rides, "ff1_fwd", (), lambda ride: _matmul(
        s["h2"], p["w_ff1"], mode="nt", out_dtype=BF16, epilogue="relu2", name=f"ff1_fwd{tag}", ride=ride))
    x2 = _carry(rides, "ff2_fwd", (), lambda ride: _matmul(
        s["a"], p["w_ff2"], mode="nn", out_dtype=F32, epilogue="add", extra=s["x1"], name=f"ff2_fwd{tag}", ride=ride))
    return x2, s


def _layer_bwd(dx2, dx2_b, p, s, cos, sin, tag, rides=None, done=None):
    sb_heads = _sb_heads(p["w_in"])
    gate_col0 = N_DIL_BLOCKS * GROUP_W + 3 * sb_heads * HEAD_DIM
    g = {}
    df = _carry(rides, "ff2_bwd", (g, done), lambda ride: _matmul(
        dx2_b, p["w_ff2"], mode="nt", out_dtype=BF16, epilogue="relu2_bwd", extra=s["f"], name=f"ff2_bwd{tag}", ride=ride))
    g["w_ff2"] = _matmul(s["a"], dx2_b, mode="tn", out_dtype=BF16, name=f"ff2_wgrad{tag}")
    dh2 = _carry(rides, "ff1_bwd", (g, done), lambda ride: _matmul(
        df, p["w_ff1"], mode="nn", out_dtype=F32, name=f"ff1_bwd{tag}", ride=ride))
    g["w_ff1"] = _matmul(df, s["h2"], mode="tn", out_dtype=BF16, name=f"ff1_wgrad{tag}")
    dx1, dx1_b, g["norm2"] = _rmsnorm_bwd(s["x1"], p["norm2"], dh2, dx2, name=f"norm2_bwd{tag}")
    dmixed = _matmul(dx1_b, p["w_out"], mode="nt", out_dtype=F32, name=f"out_bwd{tag}")
    g["w_out"] = _matmul(s["mixed"], dx1_b, mode="tn", out_dtype=BF16, name=f"out_wgrad{tag}")
    gate_b = p["gate_b"]
    dup_dil, dgp0, db0 = _gate_bwd(s["y_dil"], p["w_up_dil"], s["proj"], gate_b, dmixed, 0, gate_col0, name=f"gate0_bwd{tag}")
    dup_sb, dgp1, db1 = _gate_bwd(s["y_sb"], p["w_up_sb"], s["proj"], gate_b, dmixed, 1, gate_col0, name=f"gate1_bwd{tag}")
    g["gate_b"] = jnp.concatenate([db0, db1], axis=0)
    dy_dil = _matmul(dup_dil, p["w_up_dil"], mode="nn", out_dtype=F32, name=f"updil_bwd{tag}")
    g["w_up_dil"] = _matmul(dup_dil, s["y_dil"], mode="tn", out_dtype=BF16, name=f"updil_wgrad{tag}")
    dy_sb = _matmul(dup_sb, p["w_up_sb"], mode="nn", out_dtype=BF16, name=f"upsb_bwd{tag}")
    g["w_up_sb"] = _matmul(dup_sb, s["y_sb"], mode="tn", out_dtype=BF16, name=f"upsb_wgrad{tag}")
    dos, dterms = _merge_bwd(s["o"], s["ld"], dy_dil, name=f"merge_bwd{tag}")
    dqkv = [_dil_bwd(s["qkv_d"][grp], dos[grp], s["ld"][grp], dterms[grp], grp, name=f"dil{grp}_bwd{tag}")
            for grp in range(N_GROUPS)]
    dproj_d, dgain = _prep_bwd(s["proj"], dqkv, p["gains"], cos, sin, name=f"prep_bwd{tag}")
    g["q_gain"], g["k_gain"] = dgain[:N_GROUPS, 0], dgain[N_GROUPS:, 0]
    dq_s, dk_s, dv_s = _sb_bwd(s["qkv_s"], s["y_sb32"], dy_sb, sb_heads, 0, name=f"sb_bwd{tag}")
    dproj = jnp.concatenate([dproj_d, dq_s, dk_s, dv_s, dgp0, dgp1], axis=1)
    g["w_in"] = _carry(rides, "proj_wgrad", (g, done), lambda ride: _matmul(
        dproj, s["h"], mode="tn", out_dtype=BF16, name=f"proj_wgrad{tag}", ride=ride,
        tn=_pick(s["h"].shape[1], (2048, 1024, 512, 256, 128)), tk=_pick(s["h"].shape[0], (1024, 512))))
    dh = _carry(rides, "proj_bwd", (g, done), lambda ride: _matmul(
        dproj, p["w_in"], mode="nn", out_dtype=F32, name=f"proj_bwd{tag}", ride=ride))
    dx, dx_b, g["norm1"] = _rmsnorm_bwd(s["x"], p["norm1"], dh, dx1, name=f"norm1_bwd{tag}")
    return dx, dx_b, g


def _local_step(x, target, layers, fwd_rides=None, bwd_rides=None):
    depth = len(layers)
    fwd_rides, bwd_rides = (r or [None] * depth for r in (fwd_rides, bwd_rides))
    cos, sin = _rope_tables(x.shape[0])
    saved = []
    for l, p in enumerate(layers):
        x, s = _layer_fwd(x, p, cos, sin, f"_l{l}", fwd_rides[l])
        saved.append(s)
    dx, dx_b, loss = _loss_head(x, target, name="loss_head")
    grads = [None] * depth
    for l in reversed(range(depth)):
        dx, dx_b, grads[l] = _layer_bwd(dx, dx_b, layers[l], saved[l], cos, sin, f"_l{l}", bwd_rides[l], grads)
    return loss, dx, grads


_MATRICES = ("w_in", "w_up_dil", "w_up_sb", "w_out", "w_ff1", "w_ff2")
_TRANSPOSED = ("w_in", "w_up_dil", "w_up_sb", "w_ff1")
_SMALL = ("norm1_g", "norm2_g", "q_norm_g", "k_norm_g")


def _unshard(blocks, name):
    if name == "gate_b":
        return jnp.transpose(blocks, (1, 0, 2)).reshape(blocks.shape[1], N_DEV * blocks.shape[2])
    return blocks.reshape(N_DEV * blocks.shape[1], blocks.shape[2])


def _to_chunks(full, name):
    if name == "gate_b":
        r, cols = full.shape
        return jnp.transpose(full.reshape(r, N_DEV, cols // N_DEV), (1, 0, 2))
    return full.reshape(N_DEV, full.shape[0] // N_DEV, full.shape[1])


def _lane_pad(n):
    return -n % HEAD_DIM


def _pack_small(norm1, norm2, qg, kg):
    flat = jnp.concatenate([t.reshape(-1, HEAD_DIM) for t in (norm1, norm2, qg, kg)], axis=0)
    return jnp.pad(flat, ((0, -flat.shape[0] % 8), (0, 0)))


def _unpack_small(packed, shapes):
    out, row = [], 0
    for shape in shapes:
        rows = math.prod(shape) // HEAD_DIM
        out.append(packed[row:row + rows].reshape(shape))
        row += rows
    return out


def kernel(x, norm1_g, w_in, q_norm_g, k_norm_g, w_up_dil, w_up_sb, gate_b, w_out, norm2_g, w_ff1, w_ff2, loss_target, m_norm1_g, m_w_in, m_q_norm_g, m_k_norm_g, m_w_up_dil, m_w_up_sb, m_gate_b, m_w_out, m_norm2_g, m_w_ff1, m_w_ff2, v_norm1_g, v_w_in, v_q_norm_g, v_k_norm_g, v_w_up_dil, v_w_up_sb, v_gate_b, v_w_out, v_norm2_g, v_w_ff1, v_w_ff2):
    names = ("norm1_g", "w_in", "q_norm_g", "k_norm_g", "w_up_dil", "w_up_sb", "gate_b", "w_out", "norm2_g", "w_ff1", "w_ff2")
    w = dict(zip(names, (norm1_g, w_in, q_norm_g, k_norm_g, w_up_dil, w_up_sb, gate_b, w_out, norm2_g, w_ff1, w_ff2)))
    m = dict(zip(names, (m_norm1_g, m_w_in, m_q_norm_g, m_k_norm_g, m_w_up_dil, m_w_up_sb, m_gate_b, m_w_out, m_norm2_g, m_w_ff1, m_w_ff2)))
    v = dict(zip(names, (v_norm1_g, v_w_in, v_q_norm_g, v_k_norm_g, v_w_up_dil, v_w_up_sb, v_gate_b, v_w_out, v_norm2_g, v_w_ff1, v_w_ff2)))
    depth = norm1_g.shape[0]
    assert depth == 2, "the exchange schedule below is written for two layers"
    sharded = _MATRICES + ("gate_b",)

    def shards(layer, which):
        out = []
        for n in which:
            shard = w[n][layer]
            if n in _TRANSPOSED:
                cols = shard.shape[1]
                padded = jnp.pad(shard, ((0, 0), (0, _lane_pad(cols))))
                out.append(_transpose_cast(padded, name=f"shard_t_{n}_l{layer}")[:cols])
            else:
                out.append(shard if n == "gate_b" else shard.astype(BF16))
        return out

    assert depth == 2, "the schedule of exchanges below is written for two layers"
    layers = [{"norm1": norm1_g[l][None], "norm2": norm2_g[l][None], "gains": _gain_table(q_norm_g[l], k_norm_g[l])}
              for l in range(depth)]
    rest = sharded[1:]

    def gather_on(layer, which):
        def make():
            ride = _Ride([], shards(layer, which))
            return ride, lambda lands: layers[layer].update({n: _unshard(b, n) for n, b in zip(which, lands)})
        return make

    (w_in_0,) = _exchange([], shards(0, sharded[:1]), name="gather_w_in_l0")
    layers[0]["w_in"] = _unshard(w_in_0, "w_in")
    fwd_rides = [{"proj_fwd": gather_on(0, rest), "ff1_fwd": gather_on(1, sharded[:1]), "ff2_fwd": gather_on(1, rest)}, None]

    sent, landed = {}, {}

    def exchange_on(items):
        def make(g, done):
            chunks = [_to_chunks((g if done[layer] is None else done[layer])[n], n) for layer, n in items]

            def on_landed(lands):
                for item, chunk, land in zip(items, chunks, lands):
                    sent[item], landed[item] = chunk, land
            return _Ride(chunks, []), on_landed
        return make

    others = ("w_ff1", "w_out", "w_up_dil", "w_up_sb", "gate_b")
    bwd_rides = [{"ff1_bwd": exchange_on([(l, "w_ff2")]), "proj_wgrad": exchange_on([(l, n) for n in others]),
                  "proj_bwd": exchange_on([(l, "w_in")])} for l in range(depth)]
    loss_part, dx, grads = _local_step(x[0], loss_target[0], layers, fwd_rides, bwd_rides)
    loss = lax.psum(loss_part[0, 0], ("x", "y", "c"))

    small = _pack_small(jnp.concatenate([g["norm1"] for g in grads]), jnp.concatenate([g["norm2"] for g in grads]),
                        jnp.stack([g["q_gain"] for g in grads]), jnp.stack([g["k_gain"] for g in grads]))
    (small_parts,) = _exchange([], [small], name="exchange_small")

    out = {}
    for n in sharded:
        rows, cols = depth * w[n].shape[1], w[n].shape[2]
        own, parts = ([moved[(l, n)] for l in range(depth)] for moved in (sent, landed))
        state = [t.reshape(rows, cols) for t in (w[n], m[n], v[n])]
        if n in _TRANSPOSED:
            pad = _lane_pad(cols)
            res = _reduce_adamw_t(parts, own, *(jnp.pad(t, ((0, 0), (0, pad))) for t in state), name=f"adamw_{n}")
            res = [t[:, :cols] for t in res]
        else:
            if w[n].shape[1] % 8:
                own, parts = [jnp.concatenate(own, axis=1)], [jnp.concatenate(parts, axis=1)]
            res = _reduce_adamw(parts, own, *state, name=f"adamw_{n}")
        out[n] = [t.reshape(w[n].shape) for t in res]
    small_res = _reduce_adamw([small_parts], [small], _pack_small(*(w[n] for n in _SMALL)),
                              _pack_small(*(m[n] for n in _SMALL)), _pack_small(*(v[n] for n in _SMALL)),
                              own_chunked=False, name="adamw_small")
    small_shapes = [w[n].shape for n in _SMALL]
    for k, t in enumerate(small_res):
        for n, arr in zip(_SMALL, _unpack_small(t, small_shapes)):
            out.setdefault(n, [None] * 4)[k] = arr
    return (loss, dx[None], *(out[n][0] for n in names), *(out[n][1] for n in names), *(out[n][2] for n in names),
            *(out[n][3] for n in names))
```

```python
import functools
import math

import jax
import jax.numpy as jnp
from jax import lax
from jax.experimental import pallas as pl
from jax.experimental.pallas import tpu as pltpu

F32 = jnp.float32
BF16 = jnp.bfloat16

HEAD_DIM = 128
BLOCK = 128
N_GROUPS = 3
DILATIONS = (1, 4, 16)
ROPE_THETA = 10000.0
EPS = 1e-6
ADAM_LR = 0.001
ADAM_B1 = 0.9
ADAM_B2 = 0.999
ADAM_EPS = 1e-08
ADAM_WD = 0.01
ADAM_STEP = 10
N_DEV = 8
MESH = pl.DeviceIdType.MESH
VMEM_LIMIT_BYTES = 48 * 1024 * 1024
NEG = -1e30


def _params(sem):
    return pltpu.CompilerParams(dimension_semantics=sem, vmem_limit_bytes=VMEM_LIMIT_BYTES)


def _pick(n, options):
    for o in options:
        if n % o == 0:
            return o
    return n


_DIMS = {"nn": (((1,), (0,)), ((), ())), "nt": (((1,), (1,)), ((), ())), "tn": (((0,), (0,)), ((), ()))}


def _matmul(a, b, *, mode, out_dtype, name, epilogue=None, extra=None, tm=None, tn=None, tk=None, ride=None):
    if mode == "nn":
        (M, K), (K2, N) = a.shape, b.shape
    elif mode == "nt":
        (M, K), (N, K2) = a.shape, b.shape
    else:
        (K, M), (K2, N) = a.shape, b.shape
    assert K == K2, (a.shape, b.shape, mode)
    tm = tm or _pick(M, (1024, 512, 256, 128))
    tn = tn or _pick(N, (512, 256, 128))
    tk = tk or _pick(K, (2048, 2944, 1024, 512, 256, 128))
    nk = K // tk
    dims = _DIMS[mode]
    n_extra = 0 if extra is None else 1
    n_out = 2 if epilogue == "relu2" else 1
    n_ride = 0 if ride is None else ride.n
    grid = (M // tm, N // tn, nk)

    def body(*refs):
        a_ref, b_ref = refs[0], refs[1]
        extra_ref = refs[2] if n_extra else None
        first_out = 2 + n_extra + n_ride
        outs = refs[first_out:first_out + n_out]
        acc_ref = refs[first_out + n_out + n_ride] if nk > 1 else None
        if ride is not None:
            ride_refs = (refs[2 + n_extra:first_out], refs[first_out + n_out:first_out + n_out + n_ride], refs[-3:])
            at = [pl.program_id(d) for d in range(3)]

            @pl.when(jnp.logical_and(jnp.logical_and(at[0] == 0, at[1] == 0), at[2] == 0))
            def _():
                ride.start(*ride_refs)

            if ride.gather:
                @pl.when(jnp.logical_and(jnp.logical_and(at[0] == (7 * grid[0]) // 8, at[1] == 0), at[2] == 0))
                def _():
                    ride.forward(*ride_refs)

        def finish(acc):
            if epilogue is None:
                outs[0][...] = acc.astype(outs[0].dtype)
            elif epilogue == "add":
                outs[0][...] = (acc + extra_ref[...]).astype(outs[0].dtype)
            elif epilogue == "relu2":
                r = jnp.maximum(acc, 0.0)
                outs[0][...] = r.astype(outs[0].dtype)
                outs[1][...] = (r * r).astype(outs[1].dtype)
            else:
                outs[0][...] = (acc * (2.0 * extra_ref[...].astype(F32))).astype(outs[0].dtype)

        prod = lax.dot_general(a_ref[...], b_ref[...], dims, preferred_element_type=F32)
        if nk == 1:
            finish(prod)
        else:
            k = pl.program_id(2)

            @pl.when(k == 0)
            def _():
                acc_ref[...] = prod

            @pl.when(k > 0)
            def _():
                acc_ref[...] += prod

            @pl.when(k == nk - 1)
            def _():
                finish(acc_ref[...])

        if ride is not None:
            @pl.when(jnp.logical_and(jnp.logical_and(at[0] == grid[0] - 1, at[1] == grid[1] - 1), at[2] == nk - 1))
            def _():
                ride.finish(*ride_refs)

    if mode == "nn":
        a_spec = pl.BlockSpec((tm, tk), lambda i, j, k: (i, k))
        b_spec = pl.BlockSpec((tk, tn), lambda i, j, k: (k, j))
    elif mode == "nt":
        a_spec = pl.BlockSpec((tm, tk), lambda i, j, k: (i, k))
        b_spec = pl.BlockSpec((tn, tk), lambda i, j, k: (j, k))
    else:
        a_spec = pl.BlockSpec((tk, tm), lambda i, j, k: (k, i))
        b_spec = pl.BlockSpec((tk, tn), lambda i, j, k: (k, j))
    o_spec = pl.BlockSpec((tm, tn), lambda i, j, k: (i, j))
    in_specs = [a_spec, b_spec] + ([o_spec] if n_extra else []) + [_ANY] * n_ride
    out_shape = [jax.ShapeDtypeStruct((M, N), out_dtype)] * n_out + (ride.land_shapes() if ride else [])
    res = pl.pallas_call(
        body,
        name=name,
        grid=grid,
        in_specs=in_specs,
        out_specs=[o_spec] * n_out + [_ANY] * n_ride,
        out_shape=out_shape,
        scratch_shapes=([pltpu.VMEM((tm, tn), F32)] if nk > 1 else []) + (ride.semaphores() if ride else []),
        compiler_params=_params(("arbitrary",) * 3 if ride else ("parallel", "parallel", "arbitrary")),
    )(a, b, *([extra] if n_extra else []), *(ride.arrays if ride else []))
    if ride is None:
        return res if n_out > 1 else res[0]
    return (res[:n_out] if n_out > 1 else res[0]), res[n_out:]


def _rmsnorm_fwd(x, g, *, name):
    T, D = x.shape
    tm = _pick(T, (512, 256, 128))

    def body(x_ref, g_ref, o_ref):
        xf = x_ref[...]
        r = lax.rsqrt(jnp.mean(xf * xf, axis=-1, keepdims=True) + EPS)
        o_ref[...] = ((xf * r) * g_ref[...]).astype(o_ref.dtype)

    return pl.pallas_call(
        body,
        name=name,
        grid=(T // tm,),
        in_specs=[pl.BlockSpec((tm, D), lambda i: (i, 0)), pl.BlockSpec((1, D), lambda i: (0, 0))],
        out_specs=pl.BlockSpec((tm, D), lambda i: (i, 0)),
        out_shape=jax.ShapeDtypeStruct((T, D), BF16),
        compiler_params=_params(("parallel",)),
    )(x, g)


def _rmsnorm_bwd(x, g, dh, dres, *, name):
    T, D = x.shape
    tm = _pick(T, (256, 128))

    def body(x_ref, g_ref, dh_ref, dres_ref, dx_ref, dxb_ref, dg_ref):
        i = pl.program_id(0)
        xf = x_ref[...]
        r = lax.rsqrt(jnp.mean(xf * xf, axis=-1, keepdims=True) + EPS)
        y = xf * r
        dh_v = dh_ref[...]
        dy = dh_v * g_ref[...]
        c = jnp.mean(dy * y, axis=-1, keepdims=True)
        dx = r * (dy - y * c) + dres_ref[...]
        dx_ref[...] = dx
        dxb_ref[...] = dx.astype(BF16)
        part = jnp.sum(dh_v * y, axis=0, keepdims=True)

        @pl.when(i == 0)
        def _():
            dg_ref[...] = part

        @pl.when(i > 0)
        def _():
            dg_ref[...] += part

    row = pl.BlockSpec((tm, D), lambda i: (i, 0))
    vec = pl.BlockSpec((1, D), lambda i: (0, 0))
    return pl.pallas_call(
        body,
        name=name,
        grid=(T // tm,),
        in_specs=[row, vec, row, row],
        out_specs=[row, row, vec],
        out_shape=[jax.ShapeDtypeStruct((T, D), F32), jax.ShapeDtypeStruct((T, D), BF16), jax.ShapeDtypeStruct((1, D), F32)],
        compiler_params=_params(("arbitrary",)),
    )(x, g, dh, dres)


GROUP_W = 4 * HEAD_DIM
N_DIL_BLOCKS = 3 * N_GROUPS
N_NORMED = 2 * N_GROUPS


def _kind_of_group(j, g):
    return jnp.clip((j - g) // N_GROUPS, 0, 2)


def _head_rstd(xh):
    return lax.rsqrt(jnp.mean(xh * xh, axis=-1, keepdims=True) + EPS)


def _prep_fwd(proj, gains, cos, sin, n_sb_blocks, *, name):
    T = proj.shape[0]
    tm = _pick(T, (512, 256, 128))

    def body(p_ref, gain_ref, cos_ref, sin_ref, o0_ref, o1_ref, o2_ref, os_ref):
        j = pl.program_id(1)
        for g, o_ref in enumerate((o0_ref, o1_ref, o2_ref)):
            @pl.when(jnp.logical_and(j % N_GROUPS == g, j < N_NORMED))
            def _():
                cos_v, sin_v = cos_ref[...], sin_ref[...]
                for hh in range(4):
                    sl = slice(hh * HEAD_DIM, (hh + 1) * HEAD_DIM)
                    xh = p_ref[:, sl]
                    y = (xh * _head_rstd(xh)) * gain_ref[0, :, sl]
                    o_ref[:, sl] = (y * cos_v + pltpu.roll(y, HEAD_DIM // 2, 1) * sin_v).astype(BF16)

            @pl.when(j == N_NORMED + g)
            def _():
                o_ref[...] = p_ref[...].astype(BF16)

        @pl.when(j >= N_DIL_BLOCKS)
        def _():
            os_ref[...] = p_ref[...].astype(BF16)

    def group_spec(g):
        return pl.BlockSpec((tm, GROUP_W), lambda i, j: (i, _kind_of_group(j, g)))

    return pl.pallas_call(
        body,
        name=name,
        grid=(T // tm, N_DIL_BLOCKS + n_sb_blocks),
        in_specs=[
            pl.BlockSpec((tm, GROUP_W), lambda i, j: (i, j)),
            pl.BlockSpec((1, 1, GROUP_W), lambda i, j: (jnp.minimum(j, N_NORMED - 1), 0, 0)),
            pl.BlockSpec((tm, HEAD_DIM), lambda i, j: (i, 0)),
            pl.BlockSpec((tm, HEAD_DIM), lambda i, j: (i, 0)),
        ],
        out_specs=[group_spec(0), group_spec(1), group_spec(2),
                   pl.BlockSpec((tm, GROUP_W), lambda i, j: (i, jnp.maximum(j - N_DIL_BLOCKS, 0)))],
        out_shape=[jax.ShapeDtypeStruct((T, 3 * GROUP_W), BF16)] * N_GROUPS
        + [jax.ShapeDtypeStruct((T, n_sb_blocks * GROUP_W), BF16)],
        compiler_params=_params(("parallel", "arbitrary")),
    )(proj, gains, cos, sin)


def _prep_bwd(proj, dqkv, gains, cos, sin, *, name):
    T = proj.shape[0]
    tm = _pick(T, (512, 256, 128))

    def body(p_ref, d0_ref, d1_ref, d2_ref, gain_ref, cos_ref, sin_ref, o_ref, dgain_ref):
        j, i = pl.program_id(0), pl.program_id(1)

        def normed_bwd(d_ref):
            cos_v, sin_v = cos_ref[...], sin_ref[...]
            part = jnp.zeros((1, HEAD_DIM), F32)
            for hh in range(4):
                sl = slice(hh * HEAD_DIM, (hh + 1) * HEAD_DIM)
                xh = p_ref[:, sl]
                r = _head_rstd(xh)
                y0 = xh * r
                d_out = d_ref[:, sl]
                d_yg = d_out * cos_v + pltpu.roll(d_out * sin_v, HEAD_DIM // 2, 1)
                part = part + jnp.sum(d_yg * y0, axis=0, keepdims=True)
                dy0 = d_yg * gain_ref[0, :, sl]
                c = jnp.mean(dy0 * y0, axis=-1, keepdims=True)
                o_ref[:, sl] = (r * (dy0 - y0 * c)).astype(BF16)

            @pl.when(i == 0)
            def _():
                dgain_ref[0] = part

            @pl.when(i > 0)
            def _():
                dgain_ref[0] += part

        for g, d_ref in enumerate((d0_ref, d1_ref, d2_ref)):
            @pl.when(jnp.logical_and(j % N_GROUPS == g, j < N_NORMED))
            def _():
                normed_bwd(d_ref)

            @pl.when(j == N_NORMED + g)
            def _():
                o_ref[...] = d_ref[...].astype(BF16)

    gain_row = lambda j, i: (jnp.minimum(j, N_NORMED - 1), 0, 0)

    def grad_spec(g):
        return pl.BlockSpec((tm, GROUP_W), lambda j, i: (i, _kind_of_group(j, g)))

    return pl.pallas_call(
        body,
        name=name,
        grid=(N_DIL_BLOCKS, T // tm),
        in_specs=[
            pl.BlockSpec((tm, GROUP_W), lambda j, i: (i, j)),
            grad_spec(0), grad_spec(1), grad_spec(2),
            pl.BlockSpec((1, 1, GROUP_W), gain_row),
            pl.BlockSpec((tm, HEAD_DIM), lambda j, i: (i, 0)),
            pl.BlockSpec((tm, HEAD_DIM), lambda j, i: (i, 0)),
        ],
        out_specs=[pl.BlockSpec((tm, GROUP_W), lambda j, i: (i, j)), pl.BlockSpec((1, 1, HEAD_DIM), gain_row)],
        out_shape=[jax.ShapeDtypeStruct((T, N_DIL_BLOCKS * GROUP_W), BF16),
                   jax.ShapeDtypeStruct((2 * N_GROUPS, 1, HEAD_DIM), F32)],
        compiler_params=_params(("arbitrary", "arbitrary")),
    )(proj, *dqkv, gains, cos, sin)


def _nt(a, b):
    return lax.dot_general(a, b, _DIMS["nt"], preferred_element_type=F32)


def _tn(a, b):
    return lax.dot_general(a, b, _DIMS["tn"], preferred_element_type=F32)


def _nn(a, b):
    return jnp.dot(a, b, preferred_element_type=F32)


def _window_masks():
    row = lax.broadcasted_iota(jnp.int32, (BLOCK, BLOCK), 0)
    col = lax.broadcasted_iota(jnp.int32, (BLOCK, BLOCK), 1)
    return row >= col, col >= row


def _heads():
    return [slice(hh * HEAD_DIM, (hh + 1) * HEAD_DIM) for hh in range(GROUP_W // HEAD_DIM)]


def _dil_fwd(qkv, g, *, name):
    T = qkv.shape[0]
    r = DILATIONS[g]
    L = T // r
    nb = L // BLOCK
    scale = 1.0 / math.sqrt(HEAD_DIM)
    view = qkv.reshape(L, r * 3 * GROUP_W)

    def body(q_ref, kc_ref, kp_ref, vc_ref, vp_ref, o_ref, ld_ref):
        n = pl.program_id(1)
        m_cur, m_prev = _window_masks()
        m_prev = jnp.logical_and(m_prev, n > 0)
        heads = _heads()
        qs = [q_ref[:, sl] for sl in heads]
        s_c = [jnp.where(m_cur, _nt(q, kc_ref[:, sl]) * scale, NEG) for q, sl in zip(qs, heads)]
        s_p = [jnp.where(m_prev, _nt(q, kp_ref[:, sl]) * scale, NEG) for q, sl in zip(qs, heads)]
        m = [jnp.maximum(jnp.max(c, axis=-1, keepdims=True), jnp.max(p, axis=-1, keepdims=True)) for c, p in zip(s_c, s_p)]
        p_c = [jnp.exp(c - mx) for c, mx in zip(s_c, m)]
        p_p = [jnp.exp(p - mx) for p, mx in zip(s_p, m)]
        l = [jnp.sum(c, axis=-1, keepdims=True) + jnp.sum(p, axis=-1, keepdims=True) for c, p in zip(p_c, p_p)]
        inv = [1.0 / v for v in l]
        outs = [_nn((c * r).astype(BF16), vc_ref[:, sl]) + _nn((p * r).astype(BF16), vp_ref[:, sl])
                for c, p, r, sl in zip(p_c, p_p, inv, heads)]
        for sl, o, mx, v in zip(heads, outs, m, l):
            o_ref[:, sl] = o
            ld_ref[:, sl] = jnp.broadcast_to(mx + jnp.log(v), (BLOCK, HEAD_DIM))

    blk = (BLOCK, GROUP_W)
    cur = lambda kind: pl.BlockSpec(blk, lambda c, n: (n, 3 * c + kind))
    prev = lambda kind: pl.BlockSpec(blk, lambda c, n: (jnp.maximum(n - 1, 0), 3 * c + kind))
    out_spec = pl.BlockSpec(blk, lambda c, n: (n, c))
    o, ld = pl.pallas_call(
        body,
        name=name,
        grid=(r, nb),
        in_specs=[cur(0), cur(1), prev(1), cur(2), prev(2)],
        out_specs=[out_spec, out_spec],
        out_shape=[jax.ShapeDtypeStruct((L, r * GROUP_W), F32)] * 2,
        compiler_params=_params(("parallel", "arbitrary")),
    )(view, view, view, view, view)
    return o.reshape(T, GROUP_W), ld.reshape(T, GROUP_W)


def _dil_bwd(qkv, do, ld, dterm, g, *, name):
    T = qkv.shape[0]
    r = DILATIONS[g]
    L = T // r
    nb = L // BLOCK
    scale = 1.0 / math.sqrt(HEAD_DIM)
    view = qkv.reshape(L, r * 3 * GROUP_W)
    do_v, ld_v, dt_v = (t.reshape(L, r * GROUP_W) for t in (do, ld, dterm))

    def body(q_ref, qn_ref, kc_ref, kp_ref, vc_ref, vp_ref, do_ref, don_ref, ld_ref, ldn_ref, dt_ref, dtn_ref, out_ref):
        n = pl.program_id(1)
        m_cur, m_prev = _window_masks()
        has_prev, has_next = jnp.logical_and(m_prev, n > 0), jnp.logical_and(m_prev, n < nb - 1)

        def tile(q, k, v, do_t, ld_t, dt_t, mask):
            s = _nt(q, k) * scale
            p = jnp.where(mask, jnp.exp(s - ld_t[:, 0:1]), 0.0)
            ds = p * (_nt(do_t, v) + dt_t[:, 0:1]) * scale
            return p.astype(BF16), ds.astype(BF16)

        heads = _heads()
        kc, vc, kp = ([ref[:, sl] for sl in heads] for ref in (kc_ref, vc_ref, kp_ref))
        q, do_t, qn, don = ([ref[:, sl] for sl in heads] for ref in (q_ref, do_ref, qn_ref, don_ref))
        cc = [tile(q[h], kc[h], vc[h], do_t[h], ld_ref[:, sl], dt_ref[:, sl], m_cur) for h, sl in enumerate(heads)]
        cp = [tile(q[h], kp[h], vp_ref[:, sl], do_t[h], ld_ref[:, sl], dt_ref[:, sl], has_prev) for h, sl in enumerate(heads)]
        nc = [tile(qn[h], kc[h], vc[h], don[h], ldn_ref[:, sl], dtn_ref[:, sl], has_next) for h, sl in enumerate(heads)]
        dq = [_nn(cc[h][1], kc[h]) + _nn(cp[h][1], kp[h]) for h in range(len(heads))]
        dk = [_tn(cc[h][1], q[h]) + _tn(nc[h][1], qn[h]) for h in range(len(heads))]
        dv = [_tn(cc[h][0], do_t[h]) + _tn(nc[h][0], don[h]) for h in range(len(heads))]
        for kind, grads in enumerate((dq, dk, dv)):
            for h, grad in enumerate(grads):
                out_ref[:, kind * GROUP_W + h * HEAD_DIM:kind * GROUP_W + (h + 1) * HEAD_DIM] = grad

    blk = (BLOCK, GROUP_W)
    qkv_spec = lambda kind, shift: pl.BlockSpec(blk, lambda c, n: (jnp.clip(n + shift, 0, nb - 1), 3 * c + kind))
    row_spec = lambda shift: pl.BlockSpec(blk, lambda c, n: (jnp.clip(n + shift, 0, nb - 1), c))
    out = pl.pallas_call(
        body,
        name=name,
        grid=(r, nb),
        in_specs=[qkv_spec(0, 0), qkv_spec(0, 1), qkv_spec(1, 0), qkv_spec(1, -1), qkv_spec(2, 0), qkv_spec(2, -1),
                  row_spec(0), row_spec(1), row_spec(0), row_spec(1), row_spec(0), row_spec(1)],
        out_specs=pl.BlockSpec((BLOCK, 3 * GROUP_W), lambda c, n: (n, c)),
        out_shape=jax.ShapeDtypeStruct((L, r * 3 * GROUP_W), F32),
        compiler_params=_params(("parallel", "arbitrary")),
    )(view, view, view, view, view, view, do_v, do_v, ld_v, ld_v, dt_v, dt_v)
    return out.reshape(T, 3 * GROUP_W)


def _group_weights(ld_refs):
    lds = [r[...] for r in ld_refs]
    m = jnp.maximum(jnp.maximum(lds[0], lds[1]), lds[2])
    es = [jnp.exp(v - m) for v in lds]
    inv = 1.0 / (es[0] + es[1] + es[2])
    return [e * inv for e in es]


def _merge_fwd(os_, lds, *, name):
    T = os_[0].shape[0]
    tm = _pick(T, (1024, 512, 256, 128))

    def body(o0, o1, o2, l0, l1, l2, y_ref):
        w = _group_weights((l0, l1, l2))
        y_ref[...] = (w[0] * o0[...] + w[1] * o1[...] + w[2] * o2[...]).astype(BF16)

    spec = pl.BlockSpec((tm, GROUP_W), lambda i: (i, 0))
    return pl.pallas_call(
        body, name=name, grid=(T // tm,), in_specs=[spec] * 6, out_specs=spec,
        out_shape=jax.ShapeDtypeStruct((T, GROUP_W), BF16), compiler_params=_params(("parallel",)),
    )(*os_, *lds)


def _merge_bwd(os_, lds, dy, *, name):
    T = dy.shape[0]
    tm = _pick(T, (512, 256, 128))

    def body(o0, o1, o2, l0, l1, l2, dy_ref, do0, do1, do2, dt0, dt1, dt2):
        w = _group_weights((l0, l1, l2))
        dy_v = dy_ref[...]
        y = w[0] * o0[...] + w[1] * o1[...] + w[2] * o2[...]
        prod = dy_v * y
        for hh in range(4):
            sl = slice(hh * HEAD_DIM, (hh + 1) * HEAD_DIM)
            s = jnp.sum(prod[:, sl], axis=-1, keepdims=True)
            for wg, dt in zip(w, (dt0, dt1, dt2)):
                dt[:, sl] = -wg[:, sl] * s
        for wg, do in zip(w, (do0, do1, do2)):
            do[...] = (wg * dy_v).astype(BF16)

    spec = pl.BlockSpec((tm, GROUP_W), lambda i: (i, 0))
    outs = pl.pallas_call(
        body, name=name, grid=(T // tm,), in_specs=[spec] * 7, out_specs=[spec] * 6,
        out_shape=[jax.ShapeDtypeStruct((T, GROUP_W), BF16)] * 3 + [jax.ShapeDtypeStruct((T, GROUP_W), F32)] * 3,
        compiler_params=_params(("parallel",)),
    )(*os_, *lds, dy)
    return outs[:3], outs[3:]


SB_ROWS = 512
SB_KEYS = 256


def _sum_matrix(inclusive):
    j = lax.broadcasted_iota(jnp.int32, (2 * BLOCK, 2 * BLOCK), 0) % BLOCK
    s = lax.broadcasted_iota(jnp.int32, (2 * BLOCK, 2 * BLOCK), 1)
    later = (j >= s) if inclusive else (j > s)
    return jnp.logical_or(s >= BLOCK, later).astype(BF16)


def _block_sums(x, mat):
    hi = x.astype(BF16)
    lo = (x - hi.astype(F32)).astype(BF16)
    r = _nn(jnp.concatenate([hi, lo], axis=1), mat)
    return r[:, :BLOCK], r[:, BLOCK:]


def _log_terms(z):
    t = jnp.log(1.0 + jnp.exp(-jnp.abs(z)))
    return -(jnp.maximum(z, 0.0) + t), jnp.minimum(z, 0.0) - t


SB_DEAD = -105.0


def _sb_alive(n_chunks, state):
    return jnp.logical_and(state[0] < n_chunks, jnp.max(state[1]) > SB_DEAD)


def _causal_mask(rows, cols, first_col):
    row = lax.broadcasted_iota(jnp.int32, (rows, cols), 0)
    col = lax.broadcasted_iota(jnp.int32, (rows, cols), 1)
    return col + first_col < row


def _sb_fwd(qkv, n_heads, col0, *, name):
    T = qkv.shape[0]
    tq = _pick(T, (SB_ROWS, BLOCK))
    kc = _pick(tq, (SB_KEYS, BLOCK))
    nq, nsub, per_tile = T // tq, kc // BLOCK, tq // kc
    scale = 1.0 / math.sqrt(HEAD_DIM)

    def body(q_ref, k_ref, v_ref, o_ref, ob_ref):
        i = pl.program_id(1)
        q = q_ref[...]
        mat = _sum_matrix(False)

        def chunk(j, carry, acc, mask):
            rows = pl.ds(pl.multiple_of(j * kc, kc), kc)
            z = _nt(q, k_ref[rows, :]) * scale
            lk, ls = _log_terms(z)
            if mask is not None:
                lk = jnp.where(mask, lk, 0.0)
            a = []
            for b in reversed(range(nsub)):
                sl = slice(b * BLOCK, (b + 1) * BLOCK)
                later, total = _block_sums(lk[:, sl], mat)
                a.append(jnp.exp(ls[:, sl] + (later + carry)))
                carry = carry + total
            a = jnp.concatenate(a[::-1], axis=1)
            if mask is not None:
                a = jnp.where(mask, a, 0.0)
            return carry, acc + _nn(a.astype(BF16), v_ref[rows, :])

        carry = acc = jnp.zeros((tq, HEAD_DIM), F32)
        for d in reversed(range(per_tile)):
            carry, acc = chunk(i * per_tile + d, carry, acc, _causal_mask(tq, kc, d * kc))

        def step(state):
            carry, acc = chunk(i * per_tile - 1 - state[0], state[1], state[2], None)
            return state[0] + 1, carry, acc

        _, carry, acc = lax.while_loop(functools.partial(_sb_alive, i * per_tile), step, (0, carry, acc))
        o_ref[...] = acc
        ob_ref[...] = acc.astype(BF16)

    blk = (tq, HEAD_DIM)
    out_spec = pl.BlockSpec(blk, lambda h, i: (i, h))
    return pl.pallas_call(
        body,
        name=name,
        grid=(n_heads, nq),
        in_specs=[
            pl.BlockSpec(blk, lambda h, i: (i, col0 + h)),
            pl.BlockSpec((T, HEAD_DIM), lambda h, i: (0, col0 + n_heads + h)),
            pl.BlockSpec((T, HEAD_DIM), lambda h, i: (0, col0 + 2 * n_heads + h)),
        ],
        out_specs=[out_spec, out_spec],
        out_shape=[jax.ShapeDtypeStruct((T, n_heads * HEAD_DIM), F32), jax.ShapeDtypeStruct((T, n_heads * HEAD_DIM), BF16)],
        compiler_params=_params(("parallel", "arbitrary")),
    )(qkv, qkv, qkv)


def _sb_bwd(qkv, o32, do, n_heads, col0, *, name):
    T = qkv.shape[0]
    tq = _pick(T, (SB_ROWS, BLOCK))
    kc = _pick(tq, (SB_KEYS, BLOCK))
    nq, nsub, per_tile = T // tq, kc // BLOCK, tq // kc
    scale = 1.0 / math.sqrt(HEAD_DIM)

    def body(q_ref, k_ref, v_ref, o_ref, do_ref, dq_ref, dk_ref, dv_ref, dk_acc, dv_acc):
        i = pl.program_id(1)

        @pl.when(i == 0)
        def _():
            dk_acc[...] = jnp.zeros_like(dk_acc)
            dv_acc[...] = jnp.zeros_like(dv_acc)

        q, do_t = q_ref[...], do_ref[...]
        delta = jnp.broadcast_to(jnp.sum(do_t.astype(F32) * o_ref[...], axis=-1, keepdims=True), (tq, HEAD_DIM))
        mat, mat_incl = _sum_matrix(False), _sum_matrix(True)

        def chunk(j, carry_b, carry_g, dq, mask):
            rows = pl.ds(pl.multiple_of(j * kc, kc), kc)
            k_t, v_t = k_ref[rows, :], v_ref[rows, :]
            z = _nt(q, k_t) * scale
            lk, ls = _log_terms(z)
            if mask is not None:
                lk = jnp.where(mask, lk, 0.0)
            d_a = _nt(do_t, v_t)
            a_parts, dz_parts = [], []
            for b in reversed(range(nsub)):
                sl = slice(b * BLOCK, (b + 1) * BLOCK)
                later, total = _block_sums(lk[:, sl], mat)
                a = jnp.exp(ls[:, sl] + (later + carry_b))
                carry_b = carry_b + total
                if mask is not None:
                    a = jnp.where(mask[:, sl], a, 0.0)
                a_b = a.astype(BF16)
                g = a_b.astype(F32) * d_a[:, sl]
                from_here, total_g = _block_sums(g, mat_incl)
                before = delta - (from_here + carry_g)
                carry_g = carry_g + total_g
                sig = jnp.exp(ls[:, sl])
                dz = (g - sig * (g + before)) * scale
                a_parts.append(a_b)
                dz_parts.append(dz)
            dz = jnp.concatenate(dz_parts[::-1], axis=1)
            if mask is not None:
                dz = jnp.where(mask, dz, 0.0)
            dz_b = dz.astype(BF16)
            dk_acc[rows, :] += _tn(dz_b, q)
            dv_acc[rows, :] += _tn(jnp.concatenate(a_parts[::-1], axis=1), do_t)
            return carry_b, carry_g, dq + _nn(dz_b, k_t)

        zero = jnp.zeros((tq, HEAD_DIM), F32)
        state = (zero, zero, zero)
        for d in reversed(range(per_tile)):
            state = chunk(i * per_tile + d, *state, _causal_mask(tq, kc, d * kc))

        def step(st):
            return (st[0] + 1,) + chunk(i * per_tile - 1 - st[0], st[1], st[2], st[3], None)

        state = lax.while_loop(functools.partial(_sb_alive, i * per_tile), step, (0,) + state)
        dq_ref[...] = state[3].astype(BF16)

        @pl.when(i == nq - 1)
        def _():
            dk_ref[...] = dk_acc[...].astype(BF16)
            dv_ref[...] = dv_acc[...].astype(BF16)

    blk = (tq, HEAD_DIM)
    full = (T, HEAD_DIM)
    dshape = jax.ShapeDtypeStruct((T, n_heads * HEAD_DIM), BF16)
    return pl.pallas_call(
        body,
        name=name,
        grid=(n_heads, nq),
        in_specs=[
            pl.BlockSpec(blk, lambda h, i: (i, col0 + h)),
            pl.BlockSpec(full, lambda h, i: (0, col0 + n_heads + h)),
            pl.BlockSpec(full, lambda h, i: (0, col0 + 2 * n_heads + h)),
            pl.BlockSpec(blk, lambda h, i: (i, h)),
            pl.BlockSpec(blk, lambda h, i: (i, h)),
        ],
        out_specs=[pl.BlockSpec(blk, lambda h, i: (i, h)), pl.BlockSpec(full, lambda h, i: (0, h)),
                   pl.BlockSpec(full, lambda h, i: (0, h))],
        out_shape=[dshape, dshape, dshape],
        scratch_shapes=[pltpu.VMEM(full, F32), pltpu.VMEM(full, F32)],
        compiler_params=_params(("arbitrary", "arbitrary")),
    )(qkv, qkv, qkv, o32, do)


def _gate_fwd(y_dil, y_sb, w_up_dil, w_up_sb, proj, gate_b, gate_col0, *, name):
    T, D = y_dil.shape[0], w_up_dil.shape[0]
    tm = _pick(T, (512, 256, 128))
    tn = _pick(D, (512, 256, 128))
    c0, nbr = gate_col0 // tn, D // tn

    def body(yd_ref, ys_ref, wd_ref, ws_ref, gp0_ref, gp1_ref, b_ref, o_ref):
        g0 = jax.nn.sigmoid(gp0_ref[...] + b_ref[0:1, :])
        g1 = jax.nn.sigmoid(gp1_ref[...] + b_ref[1:2, :])
        o_ref[...] = (g0 * _nt(yd_ref[...], wd_ref[...]) + g1 * _nt(ys_ref[...], ws_ref[...])).astype(BF16)

    return pl.pallas_call(
        body,
        name=name,
        grid=(T // tm, nbr),
        in_specs=[
            pl.BlockSpec((tm, y_dil.shape[1]), lambda i, j: (i, 0)),
            pl.BlockSpec((tm, y_sb.shape[1]), lambda i, j: (i, 0)),
            pl.BlockSpec((tn, w_up_dil.shape[1]), lambda i, j: (j, 0)),
            pl.BlockSpec((tn, w_up_sb.shape[1]), lambda i, j: (j, 0)),
            pl.BlockSpec((tm, tn), lambda i, j: (i, c0 + j)),
            pl.BlockSpec((tm, tn), lambda i, j: (i, c0 + nbr + j)),
            pl.BlockSpec((2, tn), lambda i, j: (0, j)),
        ],
        out_specs=pl.BlockSpec((tm, tn), lambda i, j: (i, j)),
        out_shape=jax.ShapeDtypeStruct((T, D), BF16),
        compiler_params=_params(("parallel", "parallel")),
    )(y_dil, y_sb, w_up_dil, w_up_sb, proj, proj, gate_b)


def _gate_bwd(y, w_up, proj, gate_b, dmixed, branch, gate_col0, *, name):
    T, D = y.shape[0], w_up.shape[0]
    tm = _pick(T, (512, 256, 128))
    tn = _pick(D, (512, 256, 128))
    c0 = gate_col0 // tn + branch * (D // tn)

    def body(y_ref, w_ref, gp_ref, b_ref, dm_ref, dup_ref, dgp_ref, db_ref):
        i = pl.program_id(1)
        g = jax.nn.sigmoid(gp_ref[...] + b_ref[branch:branch + 1, :])
        dm = dm_ref[...]
        dup_ref[...] = (dm * g).astype(BF16)
        dgp = (dm * _nt(y_ref[...], w_ref[...])) * (g * (1.0 - g))
        dgp_ref[...] = dgp.astype(BF16)
        part = jnp.sum(dgp, axis=0, keepdims=True)

        @pl.when(i == 0)
        def _():
            db_ref[...] = part

        @pl.when(i > 0)
        def _():
            db_ref[...] += part

    tile = pl.BlockSpec((tm, tn), lambda j, i: (i, j))
    return pl.pallas_call(
        body,
        name=name,
        grid=(D // tn, T // tm),
        in_specs=[
            pl.BlockSpec((tm, y.shape[1]), lambda j, i: (i, 0)),
            pl.BlockSpec((tn, w_up.shape[1]), lambda j, i: (j, 0)),
            pl.BlockSpec((tm, tn), lambda j, i: (i, c0 + j)),
            pl.BlockSpec((2, tn), lambda j, i: (0, j)),
            tile,
        ],
        out_specs=[tile, tile, pl.BlockSpec((1, tn), lambda j, i: (0, j))],
        out_shape=[jax.ShapeDtypeStruct((T, D), BF16), jax.ShapeDtypeStruct((T, D), BF16), jax.ShapeDtypeStruct((1, D), F32)],
        compiler_params=_params(("parallel", "arbitrary")),
    )(y, w_up, proj, gate_b, dmixed)


def _loss_head(y, target, *, name):
    T, D = y.shape
    tm = _pick(T, (256, 128))

    def body(y_ref, t_ref, dy_ref, dyb_ref, l_ref):
        i = pl.program_id(0)
        err = y_ref[...] - t_ref[...]
        dy = err * (1.0 / D)
        dy_ref[...] = dy
        dyb_ref[...] = dy.astype(BF16)
        part = 0.5 * jnp.sum(jnp.mean(err * err, axis=-1, keepdims=True), axis=0, keepdims=True)

        @pl.when(i == 0)
        def _():
            l_ref[...] = part

        @pl.when(i > 0)
        def _():
            l_ref[...] += part

    row = pl.BlockSpec((tm, D), lambda i: (i, 0))
    return pl.pallas_call(
        body, name=name, grid=(T // tm,), in_specs=[row, row],
        out_specs=[row, row, pl.BlockSpec((1, 1), lambda i: (0, 0))],
        out_shape=[jax.ShapeDtypeStruct((T, D), F32), jax.ShapeDtypeStruct((T, D), BF16), jax.ShapeDtypeStruct((1, 1), F32)],
        compiler_params=_params(("arbitrary",)),
    )(y, target)


def _sum_parts(p_ref, own):
    slot = _slot(_place())
    g = jnp.where(slot == 0, own, p_ref[0].astype(F32))
    for s in range(1, N_DEV):
        g = g + jnp.where(slot == s, own, p_ref[s].astype(F32))
    return g


def _held(step, l, last):
    layer, i = step
    return jnp.where(layer == l, i, last * (layer > l))


def _reduce_adamw(parts, own, w, m, v, *, own_chunked=True, name):
    n_layers = len(parts)
    C = w.shape[1]
    R = w.shape[0] // n_layers
    tr = R
    for cand in (1024, 512, 256, 128, 64, 32, 16, 8):
        if R % cand == 0 and cand * C <= 256 * 1024:
            tr = cand
            break
    nr = R // tr

    def body(*refs):
        p_refs, o_refs = refs[:n_layers], refs[n_layers:2 * n_layers]
        w_ref, m_ref, v_ref, g_ref, d_ref, nm_ref, nv_ref = refs[2 * n_layers:]
        layer = pl.program_id(0)
        for l, (p_ref, o_ref) in enumerate(zip(p_refs, o_refs)):
            @pl.when(layer == l)
            def _():
                g_ref[...] = _sum_parts(p_ref, (o_ref[0] if own_chunked else o_ref[...]).astype(F32))

        _adamw_update(g_ref[...], w_ref, m_ref, v_ref, d_ref, nm_ref, nv_ref)

    def part_spec(l):
        return pl.BlockSpec((N_DEV, tr, C), lambda *step: (0, _held(step, l, nr - 1), 0))

    def own_spec(l):
        if own_chunked:
            return pl.BlockSpec((1, tr, C), lambda *step: (_slot(_place()), _held(step, l, nr - 1), 0))
        return pl.BlockSpec((tr, C), lambda *step: (_held(step, l, nr - 1), 0))

    row = pl.BlockSpec((tr, C), lambda layer, i: (layer * nr + i, 0))
    return pl.pallas_call(
        body, name=name, grid=(n_layers, nr),
        in_specs=[part_spec(l) for l in range(n_layers)] + [own_spec(l) for l in range(n_layers)] + [row, row, row],
        out_specs=[row] * 4, out_shape=[jax.ShapeDtypeStruct(w.shape, F32)] * 4,
        compiler_params=_params(("arbitrary", "arbitrary")),
    )(*parts, *own, w, m, v)


def _adamw_update(g, w_ref, m_ref, v_ref, d_ref, nm_ref, nv_ref):
    m_new = ADAM_B1 * m_ref[...] + (1.0 - ADAM_B1) * g
    v_new = ADAM_B2 * v_ref[...] + (1.0 - ADAM_B2) * (g * g)
    m_hat = m_new / (1.0 - ADAM_B1 ** ADAM_STEP)
    v_hat = v_new / (1.0 - ADAM_B2 ** ADAM_STEP)
    d_ref[...] = -ADAM_LR * (m_hat / (jnp.sqrt(v_hat) + ADAM_EPS) + ADAM_WD * w_ref[...])
    nm_ref[...] = m_new
    nv_ref[...] = v_new


def _reduce_adamw_t(parts, own, w, m, v, *, name):
    n_layers = len(parts)
    _, n, K = parts[0].shape
    n_pad = w.shape[1]
    tm = _pick(K, (128,))
    nr = K // tm

    def body(*refs):
        p_refs, o_refs = refs[:n_layers], refs[n_layers:2 * n_layers]
        w_ref, m_ref, v_ref, g_ref, d_ref, nm_ref, nv_ref = refs[2 * n_layers:]
        layer = pl.program_id(0)
        for l, (p_ref, o_ref) in enumerate(zip(p_refs, o_refs)):
            @pl.when(layer == l)
            def _():
                g_t = _sum_parts(p_ref, o_ref[0].astype(F32))
                if n_pad > n:
                    g_t = jnp.concatenate([g_t, jnp.zeros((n_pad - n, tm), F32)], axis=0)
                g_ref[...] = g_t.T

        _adamw_update(g_ref[...], w_ref, m_ref, v_ref, d_ref, nm_ref, nv_ref)

    def part_spec(l):
        return pl.BlockSpec((N_DEV, n, tm), lambda *step: (0, 0, _held(step, l, nr - 1)))

    def own_spec(l):
        return pl.BlockSpec((1, n, tm), lambda *step: (_slot(_place()), 0, _held(step, l, nr - 1)))

    row = pl.BlockSpec((tm, n_pad), lambda layer, i: (layer * nr + i, 0))
    return pl.pallas_call(
        body, name=name, grid=(n_layers, nr),
        in_specs=[part_spec(l) for l in range(n_layers)] + [own_spec(l) for l in range(n_layers)] + [row, row, row],
        out_specs=[row] * 4, out_shape=[jax.ShapeDtypeStruct(w.shape, F32)] * 4,
        compiler_params=_params(("arbitrary", "arbitrary")),
    )(*parts, *own, w, m, v)


def _transpose_cast(x, *, name):
    R, C = x.shape
    tr, tc = _pick(R, (512, 256, 128)), _pick(C, (512, 256, 128))

    def body(x_ref, o_ref):
        o_ref[...] = x_ref[...].astype(F32).T.astype(BF16)

    return pl.pallas_call(
        body, name=name, grid=(R // tr, C // tc),
        in_specs=[pl.BlockSpec((tr, tc), lambda i, j: (i, j))],
        out_specs=pl.BlockSpec((tc, tr), lambda i, j: (j, i)),
        out_shape=jax.ShapeDtypeStruct((C, R), BF16),
        compiler_params=_params(("parallel", "parallel")),
    )(x)


_ANY = pl.BlockSpec(memory_space=pl.ANY)


def _place():
    return lax.axis_index("x"), lax.axis_index("y"), lax.axis_index("c")


def _slot(p):
    return 4 * p[0] + 2 * p[1] + p[2]


N_PEERS = N_DEV - 1


def _peers(me):
    flips = [(fx, fy, fc) for fx in (0, 1) for fy in (0, 1) for fc in (0, 1)][1:]
    return [tuple(1 - v if f else v for v, f in zip(me, flip)) for flip in flips]


def _peer_copy(src, lands, t, k, sender, to, send_sems, recv_sems):
    return pltpu.make_async_remote_copy(
        src_ref=src, dst_ref=lands[t].at[_slot(sender)], send_sem=send_sems.at[N_PEERS * t + k],
        recv_sem=recv_sems.at[N_PEERS * t + k], device_id=to, device_id_type=MESH)


def _exchange(chunked, whole, *, name):
    ride = _Ride(chunked, whole)
    n = ride.n

    def body(*refs):
        ride.start(refs[:n], refs[n:2 * n], refs[2 * n:])
        ride.forward(refs[:n], refs[n:2 * n], refs[2 * n:])
        ride.finish(refs[:n], refs[n:2 * n], refs[2 * n:])

    return pl.pallas_call(
        body,
        name=name,
        in_specs=[_ANY] * n,
        out_specs=[_ANY] * n,
        out_shape=ride.land_shapes(),
        scratch_shapes=ride.semaphores(),
        compiler_params=pltpu.CompilerParams(has_side_effects=True),
    )(*ride.arrays)


class _Ride:
    def __init__(self, chunked, whole):
        self.arrays = list(chunked) + list(whole)
        self.n, self.n_chunked = len(self.arrays), len(chunked)
        self.gather = self.n_chunked == 0

    def land_shapes(self):
        return [jax.ShapeDtypeStruct(a.shape if t < self.n_chunked else (N_DEV,) + a.shape, a.dtype)
                for t, a in enumerate(self.arrays)]

    def semaphores(self):
        sems = pltpu.SemaphoreType.DMA((N_PEERS * self.n,))
        return [sems, sems, pltpu.SemaphoreType.DMA((self.n,))]

    def _src(self, ins, t, dest):
        return ins[t].at[_slot(dest)] if t < self.n_chunked else ins[t]

    def _copies(self, ins, lands, sems):
        send_sems, recv_sems, local_sems = sems
        x, y, c = me = _place()
        if not self.gather:
            return [], [_peer_copy(self._src(ins, t, peer), lands, t, k, me, peer, send_sems, recv_sems)
                        for t in range(self.n) for k, peer in enumerate(_peers(me))]
        mine = [pltpu.make_async_copy(ins[t], lands[t].at[_slot(me)], local_sems.at[t]) for t in range(self.n)]
        first = []
        for t in range(self.n):
            first.append(self._hop(ins, lands, sems, t, 0, me, (x, y, 1 - c), ins[t]))
            first += [self._hop(ins, lands, sems, t, 1 + j, me, (*chip, c), ins[t]) for j, chip in enumerate(self._chips())]
        return mine, first

    def _chips(self):
        x, y, _ = _place()
        return [(1 - x, y), (x, 1 - y), (1 - x, 1 - y)]

    def _hop(self, ins, lands, sems, t, k, block, to, source=None):
        dst = lands[t].at[_slot(block)]
        return pltpu.make_async_remote_copy(
            src_ref=dst if source is None else source, dst_ref=dst, send_sem=sems[0].at[N_PEERS * t + k],
            recv_sem=sems[1].at[N_PEERS * t + k], device_id=to, device_id_type=MESH)

    def start(self, ins, lands, sems):
        mine, sends = self._copies(ins, lands, sems)
        for cp in mine + sends:
            cp.start()

    def _passed_on(self, ins, lands, sems):
        x, y, c = _place()
        return [self._hop(ins, lands, sems, t, 4 + j, (*chip, c), (x, y, 1 - c))
                for j, chip in enumerate(self._chips()) for t in range(self.n)]

    def forward(self, ins, lands, sems):
        if not self.gather:
            return
        x, y, c = me = _place()
        for j, chip in enumerate(self._chips()):
            for t in range(self.n):
                self._hop(ins, lands, sems, t, 1 + j, (*chip, c), me).wait_recv()
        for cp in self._passed_on(ins, lands, sems):
            cp.start()

    def finish(self, ins, lands, sems):
        mine, sends = self._copies(ins, lands, sems)
        x, y, c = me = _place()
        if self.gather:
            sibling = (x, y, 1 - c)
            sends = sends + self._passed_on(ins, lands, sems)
            for t in range(self.n):
                self._hop(ins, lands, sems, t, 0, sibling, me).wait_recv()
                for j, chip in enumerate(self._chips()):
                    self._hop(ins, lands, sems, t, 4 + j, (*chip, 1 - c), me).wait_recv()
        else:
            for t in range(self.n):
                for k, peer in enumerate(_peers(me)):
                    _peer_copy(self._src(ins, t, peer), lands, t, k, peer, peer, sems[0], sems[1]).wait_recv()
        for cp in sends:
            cp.wait_send()
        for cp in mine:
            cp.wait()


def _rope_tables(T):
    half = HEAD_DIM // 2
    inv_freq = ROPE_THETA ** (-jnp.arange(half, dtype=F32) / half)
    ang = jnp.arange(T, dtype=F32)[:, None] * inv_freq[None, :]
    cos, sin = jnp.cos(ang), jnp.sin(ang)
    return jnp.concatenate([cos, cos], axis=-1), jnp.concatenate([-sin, sin], axis=-1)


def _gain_table(q_gain, k_gain):
    return jnp.tile(jnp.concatenate([q_gain, k_gain], axis=0), (1, 4))[:, None, :]


def _sb_heads(w_in):
    n_in, d_model = w_in.shape
    return (n_in - N_DIL_BLOCKS * GROUP_W - 2 * d_model) // (3 * HEAD_DIM)


def _carry(rides, key, args, call):
    make = rides.get(key) if rides else None
    if make is None:
        return call(None)
    ride, on_landed = make(*args)
    res, lands = call(ride)
    on_landed(lands)
    return res


def _layer_fwd(x, p, cos, sin, tag, rides=None):
    sb_heads = _sb_heads(p["w_in"])
    n_sb_blocks = 3 * sb_heads * HEAD_DIM // GROUP_W
    s = {"x": x}
    s["h"] = _rmsnorm_fwd(x, p["norm1"], name=f"norm1_fwd{tag}")
    s["proj"] = _carry(rides, "proj_fwd", (), lambda ride: _matmul(
        s["h"], p["w_in"], mode="nt", out_dtype=F32, name=f"proj_fwd{tag}", ride=ride))
    *s["qkv_d"], s["qkv_s"] = _prep_fwd(s["proj"], p["gains"], cos, sin, n_sb_blocks, name=f"prep_fwd{tag}")
    outs = [_dil_fwd(s["qkv_d"][g], g, name=f"dil{g}_fwd{tag}") for g in range(N_GROUPS)]
    s["o"], s["ld"] = [o for o, _ in outs], [ld for _, ld in outs]
    s["y_dil"] = _merge_fwd(s["o"], s["ld"], name=f"merge_fwd{tag}")
    s["y_sb32"], s["y_sb"] = _sb_fwd(s["qkv_s"], sb_heads, 0, name=f"sb_fwd{tag}")
    s["mixed"] = _gate_fwd(s["y_dil"], s["y_sb"], p["w_up_dil"], p["w_up_sb"], s["proj"], p["gate_b"],
                           (N_DIL_BLOCKS + n_sb_blocks) * GROUP_W, name=f"gate_fwd{tag}")
    s["x1"] = _carry(rides, "out_fwd", (), lambda ride: _matmul(
        s["mixed"], p["w_out"], mode="nn", out_dtype=F32, epilogue="add", extra=x, name=f"out_fwd{tag}", ride=ride))
    s["h2"] = _rmsnorm_fwd(s["x1"], p["norm2"], name=f"norm2_fwd{tag}")
    s["f"], s["a"] = _carry(rides, "ff1_fwd", (), lambda ride: _matmul(
        s["h2"], p["w_ff1"], mode="nt", out_dtype=BF16, epilogue="relu2", name=f"ff1_fwd{tag}", ride=ride))
    x2 = _carry(rides, "ff2_fwd", (), lambda ride: _matmul(
        s["a"], p["w_ff2"], mode="nn", out_dtype=F32, epilogue="add", extra=s["x1"], name=f"ff2_fwd{tag}", ride=ride))
    return x2, s


def _layer_bwd(dx2, dx2_b, p, s, cos, sin, tag, rides=None, done=None):
    sb_heads = _sb_heads(p["w_in"])
    gate_col0 = N_DIL_BLOCKS * GROUP_W + 3 * sb_heads * HEAD_DIM
    g = {}
    df = _carry(rides, "ff2_bwd", (g, done), lambda ride: _matmul(
        dx2_b, p["w_ff2"], mode="nt", out_dtype=BF16, epilogue="relu2_bwd", extra=s["f"], name=f"ff2_bwd{tag}", ride=ride))
    g["w_ff2"] = _matmul(s["a"], dx2_b, mode="tn", out_dtype=BF16, name=f"ff2_wgrad{tag}")
    dh2 = _carry(rides, "ff1_bwd", (g, done), lambda ride: _matmul(
        df, p["w_ff1"], mode="nn", out_dtype=F32, name=f"ff1_bwd{tag}", ride=ride))
    g["w_ff1"] = _matmul(df, s["h2"], mode="tn", out_dtype=BF16, name=f"ff1_wgrad{tag}")
    dx1, dx1_b, g["norm2"] = _rmsnorm_bwd(s["x1"], p["norm2"], dh2, dx2, name=f"norm2_bwd{tag}")
    dmixed = _matmul(dx1_b, p["w_out"], mode="nt", out_dtype=F32, name=f"out_bwd{tag}")
    g["w_out"] = _matmul(s["mixed"], dx1_b, mode="tn", out_dtype=BF16, name=f"out_wgrad{tag}")
    gate_b = p["gate_b"]
    dup_dil, dgp0, db0 = _gate_bwd(s["y_dil"], p["w_up_dil"], s["proj"], gate_b, dmixed, 0, gate_col0, name=f"gate0_bwd{tag}")
    dup_sb, dgp1, db1 = _gate_bwd(s["y_sb"], p["w_up_sb"], s["proj"], gate_b, dmixed, 1, gate_col0, name=f"gate1_bwd{tag}")
    g["gate_b"] = jnp.concatenate([db0, db1], axis=0)
    dy_dil = _matmul(dup_dil, p["w_up_dil"], mode="nn", out_dtype=F32, name=f"updil_bwd{tag}")
    g["w_up_dil"] = _matmul(dup_dil, s["y_dil"], mode="tn", out_dtype=BF16, name=f"updil_wgrad{tag}")
    dy_sb = _matmul(dup_sb, p["w_up_sb"], mode="nn", out_dtype=BF16, name=f"upsb_bwd{tag}")
    g["w_up_sb"] = _matmul(dup_sb, s["y_sb"], mode="tn", out_dtype=BF16, name=f"upsb_wgrad{tag}")
    dos, dterms = _merge_bwd(s["o"], s["ld"], dy_dil, name=f"merge_bwd{tag}")
    dqkv = [_dil_bwd(s["qkv_d"][grp], dos[grp], s["ld"][grp], dterms[grp], grp, name=f"dil{grp}_bwd{tag}")
            for grp in range(N_GROUPS)]
    dproj_d, dgain = _prep_bwd(s["proj"], dqkv, p["gains"], cos, sin, name=f"prep_bwd{tag}")
    g["q_gain"], g["k_gain"] = dgain[:N_GROUPS, 0], dgain[N_GROUPS:, 0]
    dq_s, dk_s, dv_s = _sb_bwd(s["qkv_s"], s["y_sb32"], dy_sb, sb_heads, 0, name=f"sb_bwd{tag}")
    dproj = jnp.concatenate([dproj_d, dq_s, dk_s, dv_s, dgp0, dgp1], axis=1)
    g["w_in"] = _carry(rides, "proj_wgrad", (g, done), lambda ride: _matmul(
        dproj, s["h"], mode="tn", out_dtype=BF16, name=f"proj_wgrad{tag}", ride=ride,
        tn=_pick(s["h"].shape[1], (2048, 1024, 512, 256, 128)), tk=_pick(s["h"].shape[0], (1024, 512))))
    dh = _carry(rides, "proj_bwd", (g, done), lambda ride: _matmul(
        dproj, p["w_in"], mode="nn", out_dtype=F32, name=f"proj_bwd{tag}", ride=ride))
    dx, dx_b, g["norm1"] = _rmsnorm_bwd(s["x"], p["norm1"], dh, dx1, name=f"norm1_bwd{tag}")
    return dx, dx_b, g


def _local_step(x, target, layers, fwd_rides=None, bwd_rides=None):
    depth = len(layers)
    fwd_rides, bwd_rides = (r or [None] * depth for r in (fwd_rides, bwd_rides))
    cos, sin = _rope_tables(x.shape[0])
    saved = []
    for l, p in enumerate(layers):
        x, s = _layer_fwd(x, p, cos, sin, f"_l{l}", fwd_rides[l])
        saved.append(s)
    dx, dx_b, loss = _loss_head(x, target, name="loss_head")
    grads = [None] * depth
    for l in reversed(range(depth)):
        dx, dx_b, grads[l] = _layer_bwd(dx, dx_b, layers[l], saved[l], cos, sin, f"_l{l}", bwd_rides[l], grads)
    return loss, dx, grads


_MATRICES = ("w_in", "w_up_dil", "w_up_sb", "w_out", "w_ff1", "w_ff2")
_TRANSPOSED = ("w_in", "w_up_dil", "w_up_sb", "w_ff1")
_SMALL = ("norm1_g", "norm2_g", "q_norm_g", "k_norm_g")


def _unshard(blocks, name):
    if name == "gate_b":
        return jnp.transpose(blocks, (1, 0, 2)).reshape(blocks.shape[1], N_DEV * blocks.shape[2])
    return blocks.reshape(N_DEV * blocks.shape[1], blocks.shape[2])


def _to_chunks(full, name):
    if name == "gate_b":
        r, cols = full.shape
        return jnp.transpose(full.reshape(r, N_DEV, cols // N_DEV), (1, 0, 2))
    return full.reshape(N_DEV, full.shape[0] // N_DEV, full.shape[1])


def _lane_pad(n):
    return -n % HEAD_DIM


def _pack_small(norm1, norm2, qg, kg):
    flat = jnp.concatenate([t.reshape(-1, HEAD_DIM) for t in (norm1, norm2, qg, kg)], axis=0)
    return jnp.pad(flat, ((0, -flat.shape[0] % 8), (0, 0)))


def _unpack_small(packed, shapes):
    out, row = [], 0
    for shape in shapes:
        rows = math.prod(shape) // HEAD_DIM
        out.append(packed[row:row + rows].reshape(shape))
        row += rows
    return out


def kernel(x, norm1_g, w_in, q_norm_g, k_norm_g, w_up_dil, w_up_sb, gate_b, w_out, norm2_g, w_ff1, w_ff2, loss_target, m_norm1_g, m_w_in, m_q_norm_g, m_k_norm_g, m_w_up_dil, m_w_up_sb, m_gate_b, m_w_out, m_norm2_g, m_w_ff1, m_w_ff2, v_norm1_g, v_w_in, v_q_norm_g, v_k_norm_g, v_w_up_dil, v_w_up_sb, v_gate_b, v_w_out, v_norm2_g, v_w_ff1, v_w_ff2):
    names = ("norm1_g", "w_in", "q_norm_g", "k_norm_g", "w_up_dil", "w_up_sb", "gate_b", "w_out", "norm2_g", "w_ff1", "w_ff2")
    w = dict(zip(names, (norm1_g, w_in, q_norm_g, k_norm_g, w_up_dil, w_up_sb, gate_b, w_out, norm2_g, w_ff1, w_ff2)))
    m = dict(zip(names, (m_norm1_g, m_w_in, m_q_norm_g, m_k_norm_g, m_w_up_dil, m_w_up_sb, m_gate_b, m_w_out, m_norm2_g, m_w_ff1, m_w_ff2)))
    v = dict(zip(names, (v_norm1_g, v_w_in, v_q_norm_g, v_k_norm_g, v_w_up_dil, v_w_up_sb, v_gate_b, v_w_out, v_norm2_g, v_w_ff1, v_w_ff2)))
    depth = norm1_g.shape[0]
    assert depth == 2, "the exchange schedule below is written for two layers"
    sharded = _MATRICES + ("gate_b",)

    def shards(layer, which):
        out = []
        for n in which:
            shard = w[n][layer]
            if n in _TRANSPOSED:
                cols = shard.shape[1]
                padded = jnp.pad(shard, ((0, 0), (0, _lane_pad(cols))))
                out.append(_transpose_cast(padded, name=f"shard_t_{n}_l{layer}")[:cols])
            else:
                out.append(shard if n == "gate_b" else shard.astype(BF16))
        return out

    assert depth == 2, "the schedule of exchanges below is written for two layers"
    layers = [{"norm1": norm1_g[l][None], "norm2": norm2_g[l][None], "gains": _gain_table(q_norm_g[l], k_norm_g[l])}
              for l in range(depth)]
    rest = sharded[1:]

    def gather_on(layer, which):
        def make():
            ride = _Ride([], shards(layer, which))
            return ride, lambda lands: layers[layer].update({n: _unshard(b, n) for n, b in zip(which, lands)})
        return make

    (w_in_0,) = _exchange([], shards(0, sharded[:1]), name="gather_w_in_l0")
    layers[0]["w_in"] = _unshard(w_in_0, "w_in")
    fwd_rides = [{"proj_fwd": gather_on(0, rest), "out_fwd": gather_on(1, ("w_up_dil", "w_up_sb", "w_out", "gate_b")),
                  "ff1_fwd": gather_on(1, ("w_in",)), "ff2_fwd": gather_on(1, ("w_ff1", "w_ff2"))}, None]

    sent, landed = {}, {}

    def exchange_on(items):
        def make(g, done):
            chunks = [_to_chunks((g if done[layer] is None else done[layer])[n], n) for layer, n in items]

            def on_landed(lands):
                for item, chunk, land in zip(items, chunks, lands):
                    sent[item], landed[item] = chunk, land
            return _Ride(chunks, []), on_landed
        return make

    others = ("w_ff1", "w_out", "w_up_dil", "w_up_sb", "gate_b")
    bwd_rides = [{"ff1_bwd": exchange_on([(l, "w_ff2")]), "proj_wgrad": exchange_on([(l, n) for n in others]),
                  "proj_bwd": exchange_on([(l, "w_in")])} for l in range(depth)]
    loss_part, dx, grads = _local_step(x[0], loss_target[0], layers, fwd_rides, bwd_rides)
    loss = lax.psum(loss_part[0, 0], ("x", "y", "c"))

    small = _pack_small(jnp.concatenate([g["norm1"] for g in grads]), jnp.concatenate([g["norm2"] for g in grads]),
                        jnp.stack([g["q_gain"] for g in grads]), jnp.stack([g["k_gain"] for g in grads]))
    (small_parts,) = _exchange([], [small], name="exchange_small")

    out = {}
    for n in sharded:
        rows, cols = depth * w[n].shape[1], w[n].shape[2]
        own, parts = ([moved[(l, n)] for l in range(depth)] for moved in (sent, landed))
        state = [t.reshape(rows, cols) for t in (w[n], m[n], v[n])]
        if n in _TRANSPOSED:
            pad = _lane_pad(cols)
            res = _reduce_adamw_t(parts, own, *(jnp.pad(t, ((0, 0), (0, pad))) for t in state), name=f"adamw_{n}")
            res = [t[:, :cols] for t in res]
        else:
            if w[n].shape[1] % 8:
                own, parts = [jnp.concatenate(own, axis=1)], [jnp.concatenate(parts, axis=1)]
            res = _reduce_adamw(parts, own, *state, name=f"adamw_{n}")
        out[n] = [t.reshape(w[n].shape) for t in res]
    small_res = _reduce_adamw([small_parts], [small], _pack_small(*(w[n] for n in _SMALL)),
                              _pack_small(*(m[n] for n in _SMALL)), _pack_small(*(v[n] for n in _SMALL)),
                              own_chunked=False, name="adamw_small")
    small_shapes = [w[n].shape for n in _SMALL]
    for k, t in enumerate(small_res):
        for n, arr in zip(_SMALL, _unpack_small(t, small_shapes)):
            out.setdefault(n, [None] * 4)[k] = arr
    return (loss, dx[None], *(out[n][0] for n in names), *(out[n][1] for n in names), *(out[n][2] for n in names),
            *(out[n][3] for n in names))
```

```python
import functools
import math

import jax
import jax.numpy as jnp
from jax import lax
from jax.experimental import pallas as pl
from jax.experimental.pallas import tpu as pltpu

F32 = jnp.float32
BF16 = jnp.bfloat16

HEAD_DIM = 128
BLOCK = 128
N_GROUPS = 3
DILATIONS = (1, 4, 16)
ROPE_THETA = 10000.0
EPS = 1e-6
ADAM_LR = 0.001
ADAM_B1 = 0.9
ADAM_B2 = 0.999
ADAM_EPS = 1e-08
ADAM_WD = 0.01
ADAM_STEP = 10
N_DEV = 8
MESH = pl.DeviceIdType.MESH
VMEM_LIMIT_BYTES = 48 * 1024 * 1024
NEG = -1e30


def _params(sem):
    return pltpu.CompilerParams(dimension_semantics=sem, vmem_limit_bytes=VMEM_LIMIT_BYTES)


def _pick(n, options):
    for o in options:
        if n % o == 0:
            return o
    return n


_DIMS = {"nn": (((1,), (0,)), ((), ())), "nt": (((1,), (1,)), ((), ())), "tn": (((0,), (0,)), ((), ()))}


def _matmul(a, b, *, mode, out_dtype, name, epilogue=None, extra=None, tm=None, tn=None, tk=None, ride=None):
    if mode == "nn":
        (M, K), (K2, N) = a.shape, b.shape
    elif mode == "nt":
        (M, K), (N, K2) = a.shape, b.shape
    else:
        (K, M), (K2, N) = a.shape, b.shape
    assert K == K2, (a.shape, b.shape, mode)
    tm = tm or _pick(M, (1024, 512, 256, 128))
    tn = tn or _pick(N, (512, 256, 128))
    tk = tk or _pick(K, (2048, 2944, 1024, 512, 256, 128))
    nk = K // tk
    dims = _DIMS[mode]
    n_extra = 0 if extra is None else 1
    n_out = 2 if epilogue == "relu2" else 1
    n_ride = 0 if ride is None else ride.n
    grid = (M // tm, N // tn, nk)

    def body(*refs):
        a_ref, b_ref = refs[0], refs[1]
        extra_ref = refs[2] if n_extra else None
        first_out = 2 + n_extra + n_ride
        outs = refs[first_out:first_out + n_out]
        acc_ref = refs[first_out + n_out + n_ride] if nk > 1 else None
        if ride is not None:
            ride_refs = (refs[2 + n_extra:first_out], refs[first_out + n_out:first_out + n_out + n_ride], refs[-3:])
            at = [pl.program_id(d) for d in range(3)]

            @pl.when(jnp.logical_and(jnp.logical_and(at[0] == 0, at[1] == 0), at[2] == 0))
            def _():
                ride.start(*ride_refs)

            if ride.gather:
                @pl.when(jnp.logical_and(jnp.logical_and(at[0] == (7 * grid[0]) // 8, at[1] == 0), at[2] == 0))
                def _():
                    ride.forward(*ride_refs)

        def finish(acc):
            if epilogue is None:
                outs[0][...] = acc.astype(outs[0].dtype)
            elif epilogue == "add":
                outs[0][...] = (acc + extra_ref[...]).astype(outs[0].dtype)
            elif epilogue == "relu2":
                r = jnp.maximum(acc, 0.0)
                outs[0][...] = r.astype(outs[0].dtype)
                outs[1][...] = (r * r).astype(outs[1].dtype)
            else:
                outs[0][...] = (acc * (2.0 * extra_ref[...].astype(F32))).astype(outs[0].dtype)

        prod = lax.dot_general(a_ref[...], b_ref[...], dims, preferred_element_type=F32)
        if nk == 1:
            finish(prod)
        else:
            k = pl.program_id(2)

            @pl.when(k == 0)
            def _():
                acc_ref[...] = prod

            @pl.when(k > 0)
            def _():
                acc_ref[...] += prod

            @pl.when(k == nk - 1)
            def _():
                finish(acc_ref[...])

        if ride is not None:
            @pl.when(jnp.logical_and(jnp.logical_and(at[0] == grid[0] - 1, at[1] == grid[1] - 1), at[2] == nk - 1))
            def _():
                ride.finish(*ride_refs)

    if mode == "nn":
        a_spec = pl.BlockSpec((tm, tk), lambda i, j, k: (i, k))
        b_spec = pl.BlockSpec((tk, tn), lambda i, j, k: (k, j))
    elif mode == "nt":
        a_spec = pl.BlockSpec((tm, tk), lambda i, j, k: (i, k))
        b_spec = pl.BlockSpec((tn, tk), lambda i, j, k: (j, k))
    else:
        a_spec = pl.BlockSpec((tk, tm), lambda i, j, k: (k, i))
        b_spec = pl.BlockSpec((tk, tn), lambda i, j, k: (k, j))
    o_spec = pl.BlockSpec((tm, tn), lambda i, j, k: (i, j))
    in_specs = [a_spec, b_spec] + ([o_spec] if n_extra else []) + [_ANY] * n_ride
    out_shape = [jax.ShapeDtypeStruct((M, N), out_dtype)] * n_out + (ride.land_shapes() if ride else [])
    res = pl.pallas_call(
        body,
        name=name,
        grid=grid,
        in_specs=in_specs,
        out_specs=[o_spec] * n_out + [_ANY] * n_ride,
        out_shape=out_shape,
        scratch_shapes=([pltpu.VMEM((tm, tn), F32)] if nk > 1 else []) + (ride.semaphores() if ride else []),
        compiler_params=_params(("arbitrary",) * 3 if ride else ("parallel", "parallel", "arbitrary")),
    )(a, b, *([extra] if n_extra else []), *(ride.arrays if ride else []))
    if ride is None:
        return res if n_out > 1 else res[0]
    return (res[:n_out] if n_out > 1 else res[0]), res[n_out:]


def _rmsnorm_fwd(x, g, *, name):
    T, D = x.shape
    tm = _pick(T, (512, 256, 128))

    def body(x_ref, g_ref, o_ref):
        xf = x_ref[...]
        r = lax.rsqrt(jnp.mean(xf * xf, axis=-1, keepdims=True) + EPS)
        o_ref[...] = ((xf * r) * g_ref[...]).astype(o_ref.dtype)

    return pl.pallas_call(
        body,
        name=name,
        grid=(T // tm,),
        in_specs=[pl.BlockSpec((tm, D), lambda i: (i, 0)), pl.BlockSpec((1, D), lambda i: (0, 0))],
        out_specs=pl.BlockSpec((tm, D), lambda i: (i, 0)),
        out_shape=jax.ShapeDtypeStruct((T, D), BF16),
        compiler_params=_params(("parallel",)),
    )(x, g)


def _rmsnorm_bwd(x, g, dh, dres, *, name):
    T, D = x.shape
    tm = _pick(T, (256, 128))

    def body(x_ref, g_ref, dh_ref, dres_ref, dx_ref, dxb_ref, dg_ref):
        i = pl.program_id(0)
        xf = x_ref[...]
        r = lax.rsqrt(jnp.mean(xf * xf, axis=-1, keepdims=True) + EPS)
        y = xf * r
        dh_v = dh_ref[...]
        dy = dh_v * g_ref[...]
        c = jnp.mean(dy * y, axis=-1, keepdims=True)
        dx = r * (dy - y * c) + dres_ref[...]
        dx_ref[...] = dx
        dxb_ref[...] = dx.astype(BF16)
        part = jnp.sum(dh_v * y, axis=0, keepdims=True)

        @pl.when(i == 0)
        def _():
            dg_ref[...] = part

        @pl.when(i > 0)
        def _():
            dg_ref[...] += part

    row = pl.BlockSpec((tm, D), lambda i: (i, 0))
    vec = pl.BlockSpec((1, D), lambda i: (0, 0))
    return pl.pallas_call(
        body,
        name=name,
        grid=(T // tm,),
        in_specs=[row, vec, row, row],
        out_specs=[row, row, vec],
        out_shape=[jax.ShapeDtypeStruct((T, D), F32), jax.ShapeDtypeStruct((T, D), BF16), jax.ShapeDtypeStruct((1, D), F32)],
        compiler_params=_params(("arbitrary",)),
    )(x, g, dh, dres)


GROUP_W = 4 * HEAD_DIM
N_DIL_BLOCKS = 3 * N_GROUPS
N_NORMED = 2 * N_GROUPS


def _kind_of_group(j, g):
    return jnp.clip((j - g) // N_GROUPS, 0, 2)


def _head_rstd(xh):
    return lax.rsqrt(jnp.mean(xh * xh, axis=-1, keepdims=True) + EPS)


def _prep_fwd(proj, gains, cos, sin, n_sb_blocks, *, name):
    T = proj.shape[0]
    tm = _pick(T, (1024, 512, 256, 128))

    def body(p_ref, gain_ref, cos_ref, sin_ref, o0_ref, o1_ref, o2_ref, os_ref):
        j = pl.program_id(1)
        for g, o_ref in enumerate((o0_ref, o1_ref, o2_ref)):
            @pl.when(jnp.logical_and(j % N_GROUPS == g, j < N_NORMED))
            def _():
                cos_v, sin_v = cos_ref[...], sin_ref[...]
                for hh in range(4):
                    sl = slice(hh * HEAD_DIM, (hh + 1) * HEAD_DIM)
                    xh = p_ref[:, sl]
                    y = (xh * _head_rstd(xh)) * gain_ref[0, :, sl]
                    o_ref[:, sl] = (y * cos_v + pltpu.roll(y, HEAD_DIM // 2, 1) * sin_v).astype(BF16)

            @pl.when(j == N_NORMED + g)
            def _():
                o_ref[...] = p_ref[...].astype(BF16)

        @pl.when(j >= N_DIL_BLOCKS)
        def _():
            os_ref[...] = p_ref[...].astype(BF16)

    def group_spec(g):
        return pl.BlockSpec((tm, GROUP_W), lambda i, j: (i, _kind_of_group(j, g)))

    return pl.pallas_call(
        body,
        name=name,
        grid=(T // tm, N_DIL_BLOCKS + n_sb_blocks),
        in_specs=[
            pl.BlockSpec((tm, GROUP_W), lambda i, j: (i, j)),
            pl.BlockSpec((1, 1, GROUP_W), lambda i, j: (jnp.minimum(j, N_NORMED - 1), 0, 0)),
            pl.BlockSpec((tm, HEAD_DIM), lambda i, j: (i, 0)),
            pl.BlockSpec((tm, HEAD_DIM), lambda i, j: (i, 0)),
        ],
        out_specs=[group_spec(0), group_spec(1), group_spec(2),
                   pl.BlockSpec((tm, GROUP_W), lambda i, j: (i, jnp.maximum(j - N_DIL_BLOCKS, 0)))],
        out_shape=[jax.ShapeDtypeStruct((T, 3 * GROUP_W), BF16)] * N_GROUPS
        + [jax.ShapeDtypeStruct((T, n_sb_blocks * GROUP_W), BF16)],
        compiler_params=_params(("parallel", "arbitrary")),
    )(proj, gains, cos, sin)


def _prep_bwd(proj, dqkv, gains, cos, sin, *, name):
    T = proj.shape[0]
    tm = _pick(T, (1024, 512, 256, 128))

    def body(p_ref, d0_ref, d1_ref, d2_ref, gain_ref, cos_ref, sin_ref, o_ref, dgain_ref):
        j, i = pl.program_id(0), pl.program_id(1)

        def normed_bwd(d_ref):
            cos_v, sin_v = cos_ref[...], sin_ref[...]
            part = jnp.zeros((1, HEAD_DIM), F32)
            for hh in range(4):
                sl = slice(hh * HEAD_DIM, (hh + 1) * HEAD_DIM)
                xh = p_ref[:, sl]
                r = _head_rstd(xh)
                y0 = xh * r
                d_out = d_ref[:, sl]
                d_yg = d_out * cos_v + pltpu.roll(d_out * sin_v, HEAD_DIM // 2, 1)
                part = part + jnp.sum(d_yg * y0, axis=0, keepdims=True)
                dy0 = d_yg * gain_ref[0, :, sl]
                c = jnp.mean(dy0 * y0, axis=-1, keepdims=True)
                o_ref[:, sl] = (r * (dy0 - y0 * c)).astype(BF16)

            @pl.when(i == 0)
            def _():
                dgain_ref[0] = part

            @pl.when(i > 0)
            def _():
                dgain_ref[0] += part

        for g, d_ref in enumerate((d0_ref, d1_ref, d2_ref)):
            @pl.when(jnp.logical_and(j % N_GROUPS == g, j < N_NORMED))
            def _():
                normed_bwd(d_ref)

            @pl.when(j == N_NORMED + g)
            def _():
                o_ref[...] = d_ref[...].astype(BF16)

    gain_row = lambda j, i: (jnp.minimum(j, N_NORMED - 1), 0, 0)

    def grad_spec(g):
        return pl.BlockSpec((tm, GROUP_W), lambda j, i: (i, _kind_of_group(j, g)))

    return pl.pallas_call(
        body,
        name=name,
        grid=(N_DIL_BLOCKS, T // tm),
        in_specs=[
            pl.BlockSpec((tm, GROUP_W), lambda j, i: (i, j)),
            grad_spec(0), grad_spec(1), grad_spec(2),
            pl.BlockSpec((1, 1, GROUP_W), gain_row),
            pl.BlockSpec((tm, HEAD_DIM), lambda j, i: (i, 0)),
            pl.BlockSpec((tm, HEAD_DIM), lambda j, i: (i, 0)),
        ],
        out_specs=[pl.BlockSpec((tm, GROUP_W), lambda j, i: (i, j)), pl.BlockSpec((1, 1, HEAD_DIM), gain_row)],
        out_shape=[jax.ShapeDtypeStruct((T, N_DIL_BLOCKS * GROUP_W), BF16),
                   jax.ShapeDtypeStruct((2 * N_GROUPS, 1, HEAD_DIM), F32)],
        compiler_params=_params(("arbitrary", "arbitrary")),
    )(proj, *dqkv, gains, cos, sin)


def _nt(a, b):
    return lax.dot_general(a, b, _DIMS["nt"], preferred_element_type=F32)


def _tn(a, b):
    return lax.dot_general(a, b, _DIMS["tn"], preferred_element_type=F32)


def _nn(a, b):
    return jnp.dot(a, b, preferred_element_type=F32)


def _window_masks():
    row = lax.broadcasted_iota(jnp.int32, (BLOCK, BLOCK), 0)
    col = lax.broadcasted_iota(jnp.int32, (BLOCK, BLOCK), 1)
    return row >= col, col >= row


def _heads():
    return [slice(hh * HEAD_DIM, (hh + 1) * HEAD_DIM) for hh in range(GROUP_W // HEAD_DIM)]


def _dil_fwd(qkv, g, *, name):
    T = qkv.shape[0]
    r = DILATIONS[g]
    L = T // r
    nb = L // BLOCK
    scale = 1.0 / math.sqrt(HEAD_DIM)
    view = qkv.reshape(L, r * 3 * GROUP_W)

    def body(q_ref, kc_ref, kp_ref, vc_ref, vp_ref, o_ref, ld_ref):
        n = pl.program_id(1)
        m_cur, m_prev = _window_masks()
        m_prev = jnp.logical_and(m_prev, n > 0)
        heads = _heads()
        qs = [q_ref[:, sl] for sl in heads]
        s_c = [jnp.where(m_cur, _nt(q, kc_ref[:, sl]) * scale, NEG) for q, sl in zip(qs, heads)]
        s_p = [jnp.where(m_prev, _nt(q, kp_ref[:, sl]) * scale, NEG) for q, sl in zip(qs, heads)]
        m = [jnp.maximum(jnp.max(c, axis=-1, keepdims=True), jnp.max(p, axis=-1, keepdims=True)) for c, p in zip(s_c, s_p)]
        p_c = [jnp.exp(c - mx) for c, mx in zip(s_c, m)]
        p_p = [jnp.exp(p - mx) for p, mx in zip(s_p, m)]
        l = [jnp.sum(c, axis=-1, keepdims=True) + jnp.sum(p, axis=-1, keepdims=True) for c, p in zip(p_c, p_p)]
        inv = [1.0 / v for v in l]
        outs = [_nn((c * r).astype(BF16), vc_ref[:, sl]) + _nn((p * r).astype(BF16), vp_ref[:, sl])
                for c, p, r, sl in zip(p_c, p_p, inv, heads)]
        for sl, o, mx, v in zip(heads, outs, m, l):
            o_ref[:, sl] = o
            ld_ref[:, sl] = jnp.broadcast_to(mx + jnp.log(v), (BLOCK, HEAD_DIM))

    blk = (BLOCK, GROUP_W)
    cur = lambda kind: pl.BlockSpec(blk, lambda c, n: (n, 3 * c + kind))
    prev = lambda kind: pl.BlockSpec(blk, lambda c, n: (jnp.maximum(n - 1, 0), 3 * c + kind))
    out_spec = pl.BlockSpec(blk, lambda c, n: (n, c))
    o, ld = pl.pallas_call(
        body,
        name=name,
        grid=(r, nb),
        in_specs=[cur(0), cur(1), prev(1), cur(2), prev(2)],
        out_specs=[out_spec, out_spec],
        out_shape=[jax.ShapeDtypeStruct((L, r * GROUP_W), F32)] * 2,
        compiler_params=_params(("parallel", "arbitrary")),
    )(view, view, view, view, view)
    return o.reshape(T, GROUP_W), ld.reshape(T, GROUP_W)


def _dil_bwd(qkv, do, ld, dterm, g, *, name):
    T = qkv.shape[0]
    r = DILATIONS[g]
    L = T // r
    nb = L // BLOCK
    scale = 1.0 / math.sqrt(HEAD_DIM)
    view = qkv.reshape(L, r * 3 * GROUP_W)
    do_v, ld_v, dt_v = (t.reshape(L, r * GROUP_W) for t in (do, ld, dterm))

    def body(q_ref, qn_ref, kc_ref, kp_ref, vc_ref, vp_ref, do_ref, don_ref, ld_ref, ldn_ref, dt_ref, dtn_ref, out_ref):
        n = pl.program_id(1)
        m_cur, m_prev = _window_masks()
        has_prev, has_next = jnp.logical_and(m_prev, n > 0), jnp.logical_and(m_prev, n < nb - 1)

        def tile(q, k, v, do_t, ld_t, dt_t, mask):
            s = _nt(q, k) * scale
            p = jnp.where(mask, jnp.exp(s - ld_t[:, 0:1]), 0.0)
            ds = p * (_nt(do_t, v) + dt_t[:, 0:1]) * scale
            return p.astype(BF16), ds.astype(BF16)

        heads = _heads()
        kc, vc, kp = ([ref[:, sl] for sl in heads] for ref in (kc_ref, vc_ref, kp_ref))
        q, do_t, qn, don = ([ref[:, sl] for sl in heads] for ref in (q_ref, do_ref, qn_ref, don_ref))
        cc = [tile(q[h], kc[h], vc[h], do_t[h], ld_ref[:, sl], dt_ref[:, sl], m_cur) for h, sl in enumerate(heads)]
        cp = [tile(q[h], kp[h], vp_ref[:, sl], do_t[h], ld_ref[:, sl], dt_ref[:, sl], has_prev) for h, sl in enumerate(heads)]
        nc = [tile(qn[h], kc[h], vc[h], don[h], ldn_ref[:, sl], dtn_ref[:, sl], has_next) for h, sl in enumerate(heads)]
        dq = [_nn(cc[h][1], kc[h]) + _nn(cp[h][1], kp[h]) for h in range(len(heads))]
        dk = [_tn(cc[h][1], q[h]) + _tn(nc[h][1], qn[h]) for h in range(len(heads))]
        dv = [_tn(cc[h][0], do_t[h]) + _tn(nc[h][0], don[h]) for h in range(len(heads))]
        for kind, grads in enumerate((dq, dk, dv)):
            for h, grad in enumerate(grads):
                out_ref[:, kind * GROUP_W + h * HEAD_DIM:kind * GROUP_W + (h + 1) * HEAD_DIM] = grad

    blk = (BLOCK, GROUP_W)
    qkv_spec = lambda kind, shift: pl.BlockSpec(blk, lambda c, n: (jnp.clip(n + shift, 0, nb - 1), 3 * c + kind))
    row_spec = lambda shift: pl.BlockSpec(blk, lambda c, n: (jnp.clip(n + shift, 0, nb - 1), c))
    out = pl.pallas_call(
        body,
        name=name,
        grid=(r, nb),
        in_specs=[qkv_spec(0, 0), qkv_spec(0, 1), qkv_spec(1, 0), qkv_spec(1, -1), qkv_spec(2, 0), qkv_spec(2, -1),
                  row_spec(0), row_spec(1), row_spec(0), row_spec(1), row_spec(0), row_spec(1)],
        out_specs=pl.BlockSpec((BLOCK, 3 * GROUP_W), lambda c, n: (n, c)),
        out_shape=jax.ShapeDtypeStruct((L, r * 3 * GROUP_W), F32),
        compiler_params=_params(("parallel", "arbitrary")),
    )(view, view, view, view, view, view, do_v, do_v, ld_v, ld_v, dt_v, dt_v)
    return out.reshape(T, 3 * GROUP_W)


def _group_weights(ld_refs):
    lds = [r[...] for r in ld_refs]
    m = jnp.maximum(jnp.maximum(lds[0], lds[1]), lds[2])
    es = [jnp.exp(v - m) for v in lds]
    inv = 1.0 / (es[0] + es[1] + es[2])
    return [e * inv for e in es]


def _merge_fwd(os_, lds, *, name):
    T = os_[0].shape[0]
    tm = _pick(T, (1024, 512, 256, 128))

    def body(o0, o1, o2, l0, l1, l2, y_ref):
        w = _group_weights((l0, l1, l2))
        y_ref[...] = (w[0] * o0[...] + w[1] * o1[...] + w[2] * o2[...]).astype(BF16)

    spec = pl.BlockSpec((tm, GROUP_W), lambda i: (i, 0))
    return pl.pallas_call(
        body, name=name, grid=(T // tm,), in_specs=[spec] * 6, out_specs=spec,
        out_shape=jax.ShapeDtypeStruct((T, GROUP_W), BF16), compiler_params=_params(("parallel",)),
    )(*os_, *lds)


def _merge_bwd(os_, lds, dy, *, name):
    T = dy.shape[0]
    tm = _pick(T, (512, 256, 128))

    def body(o0, o1, o2, l0, l1, l2, dy_ref, do0, do1, do2, dt0, dt1, dt2):
        w = _group_weights((l0, l1, l2))
        dy_v = dy_ref[...]
        y = w[0] * o0[...] + w[1] * o1[...] + w[2] * o2[...]
        prod = dy_v * y
        for hh in range(4):
            sl = slice(hh * HEAD_DIM, (hh + 1) * HEAD_DIM)
            s = jnp.sum(prod[:, sl], axis=-1, keepdims=True)
            for wg, dt in zip(w, (dt0, dt1, dt2)):
                dt[:, sl] = -wg[:, sl] * s
        for wg, do in zip(w, (do0, do1, do2)):
            do[...] = (wg * dy_v).astype(BF16)

    spec = pl.BlockSpec((tm, GROUP_W), lambda i: (i, 0))
    outs = pl.pallas_call(
        body, name=name, grid=(T // tm,), in_specs=[spec] * 7, out_specs=[spec] * 6,
        out_shape=[jax.ShapeDtypeStruct((T, GROUP_W), BF16)] * 3 + [jax.ShapeDtypeStruct((T, GROUP_W), F32)] * 3,
        compiler_params=_params(("parallel",)),
    )(*os_, *lds, dy)
    return outs[:3], outs[3:]


SB_ROWS = 512
SB_KEYS = 256


def _sum_matrix(inclusive):
    j = lax.broadcasted_iota(jnp.int32, (2 * BLOCK, 2 * BLOCK), 0) % BLOCK
    s = lax.broadcasted_iota(jnp.int32, (2 * BLOCK, 2 * BLOCK), 1)
    later = (j >= s) if inclusive else (j > s)
    return jnp.logical_or(s >= BLOCK, later).astype(BF16)


def _block_sums(x, mat):
    hi = x.astype(BF16)
    lo = (x - hi.astype(F32)).astype(BF16)
    r = _nn(jnp.concatenate([hi, lo], axis=1), mat)
    return r[:, :BLOCK], r[:, BLOCK:]


def _log_terms(z):
    t = jnp.log(1.0 + jnp.exp(-jnp.abs(z)))
    return -(jnp.maximum(z, 0.0) + t), jnp.minimum(z, 0.0) - t


SB_DEAD = -105.0


def _sb_alive(n_chunks, state):
    return jnp.logical_and(state[0] < n_chunks, jnp.max(state[1]) > SB_DEAD)


def _causal_mask(rows, cols, first_col):
    row = lax.broadcasted_iota(jnp.int32, (rows, cols), 0)
    col = lax.broadcasted_iota(jnp.int32, (rows, cols), 1)
    return col + first_col < row


def _sb_fwd(qkv, n_heads, col0, *, name):
    T = qkv.shape[0]
    tq = _pick(T, (SB_ROWS, BLOCK))
    kc = _pick(tq, (SB_KEYS, BLOCK))
    nq, nsub, per_tile = T // tq, kc // BLOCK, tq // kc
    scale = 1.0 / math.sqrt(HEAD_DIM)

    def body(q_ref, k_ref, v_ref, o_ref, ob_ref):
        i = pl.program_id(1)
        q = q_ref[...]
        mat = _sum_matrix(False)

        def chunk(j, carry, acc, mask):
            rows = pl.ds(pl.multiple_of(j * kc, kc), kc)
            z = _nt(q, k_ref[rows, :]) * scale
            lk, ls = _log_terms(z)
            if mask is not None:
                lk = jnp.where(mask, lk, 0.0)
            a = []
            for b in reversed(range(nsub)):
                sl = slice(b * BLOCK, (b + 1) * BLOCK)
                later, total = _block_sums(lk[:, sl], mat)
                a.append(jnp.exp(ls[:, sl] + (later + carry)))
                carry = carry + total
            a = jnp.concatenate(a[::-1], axis=1)
            if mask is not None:
                a = jnp.where(mask, a, 0.0)
            return carry, acc + _nn(a.astype(BF16), v_ref[rows, :])

        carry = acc = jnp.zeros((tq, HEAD_DIM), F32)
        for d in reversed(range(per_tile)):
            carry, acc = chunk(i * per_tile + d, carry, acc, _causal_mask(tq, kc, d * kc))

        def step(state):
            carry, acc = chunk(i * per_tile - 1 - state[0], state[1], state[2], None)
            return state[0] + 1, carry, acc

        _, carry, acc = lax.while_loop(functools.partial(_sb_alive, i * per_tile), step, (0, carry, acc))
        o_ref[...] = acc
        ob_ref[...] = acc.astype(BF16)

    blk = (tq, HEAD_DIM)
    out_spec = pl.BlockSpec(blk, lambda h, i: (i, h))
    return pl.pallas_call(
        body,
        name=name,
        grid=(n_heads, nq),
        in_specs=[
            pl.BlockSpec(blk, lambda h, i: (i, col0 + h)),
            pl.BlockSpec((T, HEAD_DIM), lambda h, i: (0, col0 + n_heads + h)),
            pl.BlockSpec((T, HEAD_DIM), lambda h, i: (0, col0 + 2 * n_heads + h)),
        ],
        out_specs=[out_spec, out_spec],
        out_shape=[jax.ShapeDtypeStruct((T, n_heads * HEAD_DIM), F32), jax.ShapeDtypeStruct((T, n_heads * HEAD_DIM), BF16)],
        compiler_params=_params(("parallel", "arbitrary")),
    )(qkv, qkv, qkv)


def _sb_bwd(qkv, o32, do, n_heads, col0, *, name):
    T = qkv.shape[0]
    tq = _pick(T, (SB_ROWS, BLOCK))
    kc = _pick(tq, (SB_KEYS, BLOCK))
    nq, nsub, per_tile = T // tq, kc // BLOCK, tq // kc
    scale = 1.0 / math.sqrt(HEAD_DIM)

    def body(q_ref, k_ref, v_ref, o_ref, do_ref, dq_ref, dk_ref, dv_ref, dk_acc, dv_acc):
        i = pl.program_id(1)

        @pl.when(i == 0)
        def _():
            dk_acc[...] = jnp.zeros_like(dk_acc)
            dv_acc[...] = jnp.zeros_like(dv_acc)

        q, do_t = q_ref[...], do_ref[...]
        delta = jnp.broadcast_to(jnp.sum(do_t.astype(F32) * o_ref[...], axis=-1, keepdims=True), (tq, HEAD_DIM))
        mat, mat_incl = _sum_matrix(False), _sum_matrix(True)

        def chunk(j, carry_b, carry_g, dq, mask):
            rows = pl.ds(pl.multiple_of(j * kc, kc), kc)
            k_t, v_t = k_ref[rows, :], v_ref[rows, :]
            z = _nt(q, k_t) * scale
            lk, ls = _log_terms(z)
            if mask is not None:
                lk = jnp.where(mask, lk, 0.0)
            d_a = _nt(do_t, v_t)
            a_parts, dz_parts = [], []
            for b in reversed(range(nsub)):
                sl = slice(b * BLOCK, (b + 1) * BLOCK)
                later, total = _block_sums(lk[:, sl], mat)
                a = jnp.exp(ls[:, sl] + (later + carry_b))
                carry_b = carry_b + total
                if mask is not None:
                    a = jnp.where(mask[:, sl], a, 0.0)
                a_b = a.astype(BF16)
                g = a_b.astype(F32) * d_a[:, sl]
                from_here, total_g = _block_sums(g, mat_incl)
                before = delta - (from_here + carry_g)
                carry_g = carry_g + total_g
                sig = jnp.exp(ls[:, sl])
                dz = (g - sig * (g + before)) * scale
                a_parts.append(a_b)
                dz_parts.append(dz)
            dz = jnp.concatenate(dz_parts[::-1], axis=1)
            if mask is not None:
                dz = jnp.where(mask, dz, 0.0)
            dz_b = dz.astype(BF16)
            dk_acc[rows, :] += _tn(dz_b, q)
            dv_acc[rows, :] += _tn(jnp.concatenate(a_parts[::-1], axis=1), do_t)
            return carry_b, carry_g, dq + _nn(dz_b, k_t)

        zero = jnp.zeros((tq, HEAD_DIM), F32)
        state = (zero, zero, zero)
        for d in reversed(range(per_tile)):
            state = chunk(i * per_tile + d, *state, _causal_mask(tq, kc, d * kc))

        def step(st):
            return (st[0] + 1,) + chunk(i * per_tile - 1 - st[0], st[1], st[2], st[3], None)

        state = lax.while_loop(functools.partial(_sb_alive, i * per_tile), step, (0,) + state)
        dq_ref[...] = state[3].astype(BF16)

        @pl.when(i == nq - 1)
        def _():
            dk_ref[...] = dk_acc[...].astype(BF16)
            dv_ref[...] = dv_acc[...].astype(BF16)

    blk = (tq, HEAD_DIM)
    full = (T, HEAD_DIM)
    dshape = jax.ShapeDtypeStruct((T, n_heads * HEAD_DIM), BF16)
    return pl.pallas_call(
        body,
        name=name,
        grid=(n_heads, nq),
        in_specs=[
            pl.BlockSpec(blk, lambda h, i: (i, col0 + h)),
            pl.BlockSpec(full, lambda h, i: (0, col0 + n_heads + h)),
            pl.BlockSpec(full, lambda h, i: (0, col0 + 2 * n_heads + h)),
            pl.BlockSpec(blk, lambda h, i: (i, h)),
            pl.BlockSpec(blk, lambda h, i: (i, h)),
        ],
        out_specs=[pl.BlockSpec(blk, lambda h, i: (i, h)), pl.BlockSpec(full, lambda h, i: (0, h)),
                   pl.BlockSpec(full, lambda h, i: (0, h))],
        out_shape=[dshape, dshape, dshape],
        scratch_shapes=[pltpu.VMEM(full, F32), pltpu.VMEM(full, F32)],
        compiler_params=_params(("arbitrary", "arbitrary")),
    )(qkv, qkv, qkv, o32, do)


def _gate_fwd(y_dil, y_sb, w_up_dil, w_up_sb, proj, gate_b, gate_col0, *, name):
    T, D = y_dil.shape[0], w_up_dil.shape[0]
    tm = _pick(T, (512, 256, 128))
    tn = _pick(D, (512, 256, 128))
    c0, nbr = gate_col0 // tn, D // tn

    def body(yd_ref, ys_ref, wd_ref, ws_ref, gp0_ref, gp1_ref, b_ref, o_ref):
        g0 = jax.nn.sigmoid(gp0_ref[...] + b_ref[0:1, :])
        g1 = jax.nn.sigmoid(gp1_ref[...] + b_ref[1:2, :])
        o_ref[...] = (g0 * _nt(yd_ref[...], wd_ref[...]) + g1 * _nt(ys_ref[...], ws_ref[...])).astype(BF16)

    return pl.pallas_call(
        body,
        name=name,
        grid=(T // tm, nbr),
        in_specs=[
            pl.BlockSpec((tm, y_dil.shape[1]), lambda i, j: (i, 0)),
            pl.BlockSpec((tm, y_sb.shape[1]), lambda i, j: (i, 0)),
            pl.BlockSpec((tn, w_up_dil.shape[1]), lambda i, j: (j, 0)),
            pl.BlockSpec((tn, w_up_sb.shape[1]), lambda i, j: (j, 0)),
            pl.BlockSpec((tm, tn), lambda i, j: (i, c0 + j)),
            pl.BlockSpec((tm, tn), lambda i, j: (i, c0 + nbr + j)),
            pl.BlockSpec((2, tn), lambda i, j: (0, j)),
        ],
        out_specs=pl.BlockSpec((tm, tn), lambda i, j: (i, j)),
        out_shape=jax.ShapeDtypeStruct((T, D), BF16),
        compiler_params=_params(("parallel", "parallel")),
    )(y_dil, y_sb, w_up_dil, w_up_sb, proj, proj, gate_b)


def _gate_bwd(y, w_up, proj, gate_b, dmixed, branch, gate_col0, *, name):
    T, D = y.shape[0], w_up.shape[0]
    tm = _pick(T, (512, 256, 128))
    tn = _pick(D, (512, 256, 128))
    c0 = gate_col0 // tn + branch * (D // tn)

    def body(y_ref, w_ref, gp_ref, b_ref, dm_ref, dup_ref, dgp_ref, db_ref):
        i = pl.program_id(1)
        g = jax.nn.sigmoid(gp_ref[...] + b_ref[branch:branch + 1, :])
        dm = dm_ref[...]
        dup_ref[...] = (dm * g).astype(BF16)
        dgp = (dm * _nt(y_ref[...], w_ref[...])) * (g * (1.0 - g))
        dgp_ref[...] = dgp.astype(BF16)
        part = jnp.sum(dgp, axis=0, keepdims=True)

        @pl.when(i == 0)
        def _():
            db_ref[...] = part

        @pl.when(i > 0)
        def _():
            db_ref[...] += part

    tile = pl.BlockSpec((tm, tn), lambda j, i: (i, j))
    return pl.pallas_call(
        body,
        name=name,
        grid=(D // tn, T // tm),
        in_specs=[
            pl.BlockSpec((tm, y.shape[1]), lambda j, i: (i, 0)),
            pl.BlockSpec((tn, w_up.shape[1]), lambda j, i: (j, 0)),
            pl.BlockSpec((tm, tn), lambda j, i: (i, c0 + j)),
            pl.BlockSpec((2, tn), lambda j, i: (0, j)),
            tile,
        ],
        out_specs=[tile, tile, pl.BlockSpec((1, tn), lambda j, i: (0, j))],
        out_shape=[jax.ShapeDtypeStruct((T, D), BF16), jax.ShapeDtypeStruct((T, D), BF16), jax.ShapeDtypeStruct((1, D), F32)],
        compiler_params=_params(("parallel", "arbitrary")),
    )(y, w_up, proj, gate_b, dmixed)


def _loss_head(y, target, *, name):
    T, D = y.shape
    tm = _pick(T, (256, 128))

    def body(y_ref, t_ref, dy_ref, dyb_ref, l_ref):
        i = pl.program_id(0)
        err = y_ref[...] - t_ref[...]
        dy = err * (1.0 / D)
        dy_ref[...] = dy
        dyb_ref[...] = dy.astype(BF16)
        part = 0.5 * jnp.sum(jnp.mean(err * err, axis=-1, keepdims=True), axis=0, keepdims=True)

        @pl.when(i == 0)
        def _():
            l_ref[...] = part

        @pl.when(i > 0)
        def _():
            l_ref[...] += part

    row = pl.BlockSpec((tm, D), lambda i: (i, 0))
    return pl.pallas_call(
        body, name=name, grid=(T // tm,), in_specs=[row, row],
        out_specs=[row, row, pl.BlockSpec((1, 1), lambda i: (0, 0))],
        out_shape=[jax.ShapeDtypeStruct((T, D), F32), jax.ShapeDtypeStruct((T, D), BF16), jax.ShapeDtypeStruct((1, 1), F32)],
        compiler_params=_params(("arbitrary",)),
    )(y, target)


def _sum_parts(p_ref, own):
    slot = _slot(_place())
    g = jnp.where(slot == 0, own, p_ref[0].astype(F32))
    for s in range(1, N_DEV):
        g = g + jnp.where(slot == s, own, p_ref[s].astype(F32))
    return g


def _held(step, l, last):
    layer, i = step
    return jnp.where(layer == l, i, last * (layer > l))


def _reduce_adamw(parts, own, w, m, v, *, own_chunked=True, name):
    n_layers = len(parts)
    C = w.shape[1]
    R = w.shape[0] // n_layers
    tr = R
    for cand in (1024, 512, 256, 128, 64, 32, 16, 8):
        if R % cand == 0 and cand * C <= 256 * 1024:
            tr = cand
            break
    nr = R // tr

    def body(*refs):
        p_refs, o_refs = refs[:n_layers], refs[n_layers:2 * n_layers]
        w_ref, m_ref, v_ref, g_ref, d_ref, nm_ref, nv_ref = refs[2 * n_layers:]
        layer = pl.program_id(0)
        for l, (p_ref, o_ref) in enumerate(zip(p_refs, o_refs)):
            @pl.when(layer == l)
            def _():
                g_ref[...] = _sum_parts(p_ref, (o_ref[0] if own_chunked else o_ref[...]).astype(F32))

        _adamw_update(g_ref[...], w_ref, m_ref, v_ref, d_ref, nm_ref, nv_ref)

    def part_spec(l):
        return pl.BlockSpec((N_DEV, tr, C), lambda *step: (0, _held(step, l, nr - 1), 0))

    def own_spec(l):
        if own_chunked:
            return pl.BlockSpec((1, tr, C), lambda *step: (_slot(_place()), _held(step, l, nr - 1), 0))
        return pl.BlockSpec((tr, C), lambda *step: (_held(step, l, nr - 1), 0))

    row = pl.BlockSpec((tr, C), lambda layer, i: (layer * nr + i, 0))
    return pl.pallas_call(
        body, name=name, grid=(n_layers, nr),
        in_specs=[part_spec(l) for l in range(n_layers)] + [own_spec(l) for l in range(n_layers)] + [row, row, row],
        out_specs=[row] * 4, out_shape=[jax.ShapeDtypeStruct(w.shape, F32)] * 4,
        compiler_params=_params(("arbitrary", "arbitrary")),
    )(*parts, *own, w, m, v)


def _adamw_update(g, w_ref, m_ref, v_ref, d_ref, nm_ref, nv_ref):
    m_new = ADAM_B1 * m_ref[...] + (1.0 - ADAM_B1) * g
    v_new = ADAM_B2 * v_ref[...] + (1.0 - ADAM_B2) * (g * g)
    m_hat = m_new / (1.0 - ADAM_B1 ** ADAM_STEP)
    v_hat = v_new / (1.0 - ADAM_B2 ** ADAM_STEP)
    d_ref[...] = -ADAM_LR * (m_hat / (jnp.sqrt(v_hat) + ADAM_EPS) + ADAM_WD * w_ref[...])
    nm_ref[...] = m_new
    nv_ref[...] = v_new


def _reduce_adamw_t(parts, own, w, m, v, *, name):
    n_layers = len(parts)
    _, n, K = parts[0].shape
    n_pad = w.shape[1]
    tm = _pick(K, (128,))
    nr = K // tm

    def body(*refs):
        p_refs, o_refs = refs[:n_layers], refs[n_layers:2 * n_layers]
        w_ref, m_ref, v_ref, g_ref, d_ref, nm_ref, nv_ref = refs[2 * n_layers:]
        layer = pl.program_id(0)
        for l, (p_ref, o_ref) in enumerate(zip(p_refs, o_refs)):
            @pl.when(layer == l)
            def _():
                g_t = _sum_parts(p_ref, o_ref[0].astype(F32))
                if n_pad > n:
                    g_t = jnp.concatenate([g_t, jnp.zeros((n_pad - n, tm), F32)], axis=0)
                g_ref[...] = g_t.T

        _adamw_update(g_ref[...], w_ref, m_ref, v_ref, d_ref, nm_ref, nv_ref)

    def part_spec(l):
        return pl.BlockSpec((N_DEV, n, tm), lambda *step: (0, 0, _held(step, l, nr - 1)))

    def own_spec(l):
        return pl.BlockSpec((1, n, tm), lambda *step: (_slot(_place()), 0, _held(step, l, nr - 1)))

    row = pl.BlockSpec((tm, n_pad), lambda layer, i: (layer * nr + i, 0))
    return pl.pallas_call(
        body, name=name, grid=(n_layers, nr),
        in_specs=[part_spec(l) for l in range(n_layers)] + [own_spec(l) for l in range(n_layers)] + [row, row, row],
        out_specs=[row] * 4, out_shape=[jax.ShapeDtypeStruct(w.shape, F32)] * 4,
        compiler_params=_params(("arbitrary", "arbitrary")),
    )(*parts, *own, w, m, v)


def _transpose_cast(x, *, name):
    R, C = x.shape
    tr, tc = _pick(R, (512, 256, 128)), _pick(C, (512, 256, 128))

    def body(x_ref, o_ref):
        o_ref[...] = x_ref[...].astype(F32).T.astype(BF16)

    return pl.pallas_call(
        body, name=name, grid=(R // tr, C // tc),
        in_specs=[pl.BlockSpec((tr, tc), lambda i, j: (i, j))],
        out_specs=pl.BlockSpec((tc, tr), lambda i, j: (j, i)),
        out_shape=jax.ShapeDtypeStruct((C, R), BF16),
        compiler_params=_params(("parallel", "parallel")),
    )(x)


_ANY = pl.BlockSpec(memory_space=pl.ANY)


def _place():
    return lax.axis_index("x"), lax.axis_index("y"), lax.axis_index("c")


def _slot(p):
    return 4 * p[0] + 2 * p[1] + p[2]


N_PEERS = N_DEV - 1


def _peers(me):
    flips = [(fx, fy, fc) for fx in (0, 1) for fy in (0, 1) for fc in (0, 1)][1:]
    return [tuple(1 - v if f else v for v, f in zip(me, flip)) for flip in flips]


def _peer_copy(src, lands, t, k, sender, to, send_sems, recv_sems):
    return pltpu.make_async_remote_copy(
        src_ref=src, dst_ref=lands[t].at[_slot(sender)], send_sem=send_sems.at[N_PEERS * t + k],
        recv_sem=recv_sems.at[N_PEERS * t + k], device_id=to, device_id_type=MESH)


def _exchange(chunked, whole, *, name):
    ride = _Ride(chunked, whole)
    n = ride.n

    def body(*refs):
        ride.start(refs[:n], refs[n:2 * n], refs[2 * n:])
        ride.forward(refs[:n], refs[n:2 * n], refs[2 * n:])
        ride.finish(refs[:n], refs[n:2 * n], refs[2 * n:])

    return pl.pallas_call(
        body,
        name=name,
        in_specs=[_ANY] * n,
        out_specs=[_ANY] * n,
        out_shape=ride.land_shapes(),
        scratch_shapes=ride.semaphores(),
        compiler_params=pltpu.CompilerParams(has_side_effects=True),
    )(*ride.arrays)


class _Ride:
    def __init__(self, chunked, whole):
        self.arrays = list(chunked) + list(whole)
        self.n, self.n_chunked = len(self.arrays), len(chunked)
        self.gather = self.n_chunked == 0

    def land_shapes(self):
        return [jax.ShapeDtypeStruct(a.shape if t < self.n_chunked else (N_DEV,) + a.shape, a.dtype)
                for t, a in enumerate(self.arrays)]

    def semaphores(self):
        sems = pltpu.SemaphoreType.DMA((N_PEERS * self.n,))
        return [sems, sems, pltpu.SemaphoreType.DMA((self.n,))]

    def _src(self, ins, t, dest):
        return ins[t].at[_slot(dest)] if t < self.n_chunked else ins[t]

    def _copies(self, ins, lands, sems):
        send_sems, recv_sems, local_sems = sems
        x, y, c = me = _place()
        if not self.gather:
            return [], [_peer_copy(self._src(ins, t, peer), lands, t, k, me, peer, send_sems, recv_sems)
                        for t in range(self.n) for k, peer in enumerate(_peers(me))]
        mine = [pltpu.make_async_copy(ins[t], lands[t].at[_slot(me)], local_sems.at[t]) for t in range(self.n)]
        first = []
        for t in range(self.n):
            first.append(self._hop(ins, lands, sems, t, 0, me, (x, y, 1 - c), ins[t]))
            first += [self._hop(ins, lands, sems, t, 1 + j, me, (*chip, c), ins[t]) for j, chip in enumerate(self._chips())]
        return mine, first

    def _chips(self):
        x, y, _ = _place()
        return [(1 - x, y), (x, 1 - y), (1 - x, 1 - y)]

    def _hop(self, ins, lands, sems, t, k, block, to, source=None):
        dst = lands[t].at[_slot(block)]
        return pltpu.make_async_remote_copy(
            src_ref=dst if source is None else source, dst_ref=dst, send_sem=sems[0].at[N_PEERS * t + k],
            recv_sem=sems[1].at[N_PEERS * t + k], device_id=to, device_id_type=MESH)

    def start(self, ins, lands, sems):
        mine, sends = self._copies(ins, lands, sems)
        for cp in mine + sends:
            cp.start()

    def _passed_on(self, ins, lands, sems):
        x, y, c = _place()
        return [self._hop(ins, lands, sems, t, 4 + j, (*chip, c), (x, y, 1 - c))
                for j, chip in enumerate(self._chips()) for t in range(self.n)]

    def forward(self, ins, lands, sems):
        if not self.gather:
            return
        x, y, c = me = _place()
        for j, chip in enumerate(self._chips()):
            for t in range(self.n):
                self._hop(ins, lands, sems, t, 1 + j, (*chip, c), me).wait_recv()
        for cp in self._passed_on(ins, lands, sems):
            cp.start()

    def finish(self, ins, lands, sems):
        mine, sends = self._copies(ins, lands, sems)
        x, y, c = me = _place()
        if self.gather:
            sibling = (x, y, 1 - c)
            sends = sends + self._passed_on(ins, lands, sems)
            for t in range(self.n):
                self._hop(ins, lands, sems, t, 0, sibling, me).wait_recv()
                for j, chip in enumerate(self._chips()):
                    self._hop(ins, lands, sems, t, 4 + j, (*chip, 1 - c), me).wait_recv()
        else:
            for t in range(self.n):
                for k, peer in enumerate(_peers(me)):
                    _peer_copy(self._src(ins, t, peer), lands, t, k, peer, peer, sems[0], sems[1]).wait_recv()
        for cp in sends:
            cp.wait_send()
        for cp in mine:
            cp.wait()


def _rope_tables(T):
    half = HEAD_DIM // 2
    inv_freq = ROPE_THETA ** (-jnp.arange(half, dtype=F32) / half)
    ang = jnp.arange(T, dtype=F32)[:, None] * inv_freq[None, :]
    cos, sin = jnp.cos(ang), jnp.sin(ang)
    return jnp.concatenate([cos, cos], axis=-1), jnp.concatenate([-sin, sin], axis=-1)


def _gain_table(q_gain, k_gain):
    return jnp.tile(jnp.concatenate([q_gain, k_gain], axis=0), (1, 4))[:, None, :]


def _sb_heads(w_in):
    n_in, d_model = w_in.shape
    return (n_in - N_DIL_BLOCKS * GROUP_W - 2 * d_model) // (3 * HEAD_DIM)


def _carry(rides, key, args, call):
    make = rides.get(key) if rides else None
    if make is None:
        return call(None)
    ride, on_landed = make(*args)
    res, lands = call(ride)
    on_landed(lands)
    return res


def _layer_fwd(x, p, cos, sin, tag, rides=None):
    sb_heads = _sb_heads(p["w_in"])
    n_sb_blocks = 3 * sb_heads * HEAD_DIM // GROUP_W
    s = {"x": x}
    s["h"] = _rmsnorm_fwd(x, p["norm1"], name=f"norm1_fwd{tag}")
    s["proj"] = _carry(rides, "proj_fwd", (), lambda ride: _matmul(
        s["h"], p["w_in"], mode="nt", out_dtype=F32, name=f"proj_fwd{tag}", ride=ride))
    *s["qkv_d"], s["qkv_s"] = _prep_fwd(s["proj"], p["gains"], cos, sin, n_sb_blocks, name=f"prep_fwd{tag}")
    outs = [_dil_fwd(s["qkv_d"][g], g, name=f"dil{g}_fwd{tag}") for g in range(N_GROUPS)]
    s["o"], s["ld"] = [o for o, _ in outs], [ld for _, ld in outs]
    s["y_dil"] = _merge_fwd(s["o"], s["ld"], name=f"merge_fwd{tag}")
    s["y_sb32"], s["y_sb"] = _sb_fwd(s["qkv_s"], sb_heads, 0, name=f"sb_fwd{tag}")
    s["mixed"] = _gate_fwd(s["y_dil"], s["y_sb"], p["w_up_dil"], p["w_up_sb"], s["proj"], p["gate_b"],
                           (N_DIL_BLOCKS + n_sb_blocks) * GROUP_W, name=f"gate_fwd{tag}")
    s["x1"] = _carry(rides, "out_fwd", (), lambda ride: _matmul(
        s["mixed"], p["w_out"], mode="nn", out_dtype=F32, epilogue="add", extra=x, name=f"out_fwd{tag}", ride=ride))
    s["h2"] = _rmsnorm_fwd(s["x1"], p["norm2"], name=f"norm2_fwd{tag}")
    s["f"], s["a"] = _carry(rides, "ff1_fwd", (), lambda ride: _matmul(
        s["h2"], p["w_ff1"], mode="nt", out_dtype=BF16, epilogue="relu2", name=f"ff1_fwd{tag}", ride=ride))
    x2 = _carry(rides, "ff2_fwd", (), lambda ride: _matmul(
        s["a"], p["w_ff2"], mode="nn", out_dtype=F32, epilogue="add", extra=s["x1"], name=f"ff2_fwd{tag}", ride=ride))
    return x2, s


def _layer_bwd(dx2, dx2_b, p, s, cos, sin, tag, rides=None, done=None):
    sb_heads = _sb_heads(p["w_in"])
    gate_col0 = N_DIL_BLOCKS * GROUP_W + 3 * sb_heads * HEAD_DIM
    g = {}
    df = _carry(rides, "ff2_bwd", (g, done), lambda ride: _matmul(
        dx2_b, p["w_ff2"], mode="nt", out_dtype=BF16, epilogue="relu2_bwd", extra=s["f"], name=f"ff2_bwd{tag}", ride=ride))
    g["w_ff2"] = _matmul(s["a"], dx2_b, mode="tn", out_dtype=BF16, name=f"ff2_wgrad{tag}")
    dh2 = _carry(rides, "ff1_bwd", (g, done), lambda ride: _matmul(
        df, p["w_ff1"], mode="nn", out_dtype=F32, name=f"ff1_bwd{tag}", ride=ride))
    g["w_ff1"] = _matmul(df, s["h2"], mode="tn", out_dtype=BF16, name=f"ff1_wgrad{tag}")
    dx1, dx1_b, g["norm2"] = _rmsnorm_bwd(s["x1"], p["norm2"], dh2, dx2, name=f"norm2_bwd{tag}")
    dmixed = _matmul(dx1_b, p["w_out"], mode="nt", out_dtype=F32, name=f"out_bwd{tag}")
    g["w_out"] = _matmul(s["mixed"], dx1_b, mode="tn", out_dtype=BF16, name=f"out_wgrad{tag}")
    gate_b = p["gate_b"]
    dup_dil, dgp0, db0 = _gate_bwd(s["y_dil"], p["w_up_dil"], s["proj"], gate_b, dmixed, 0, gate_col0, name=f"gate0_bwd{tag}")
    dup_sb, dgp1, db1 = _gate_bwd(s["y_sb"], p["w_up_sb"], s["proj"], gate_b, dmixed, 1, gate_col0, name=f"gate1_bwd{tag}")
    g["gate_b"] = jnp.concatenate([db0, db1], axis=0)
    dy_dil = _matmul(dup_dil, p["w_up_dil"], mode="nn", out_dtype=F32, name=f"updil_bwd{tag}")
    g["w_up_dil"] = _matmul(dup_dil, s["y_dil"], mode="tn", out_dtype=BF16, name=f"updil_wgrad{tag}")
    dy_sb = _matmul(dup_sb, p["w_up_sb"], mode="nn", out_dtype=BF16, name=f"upsb_bwd{tag}")
    g["w_up_sb"] = _matmul(dup_sb, s["y_sb"], mode="tn", out_dtype=BF16, name=f"upsb_wgrad{tag}")
    dos, dterms = _merge_bwd(s["o"], s["ld"], dy_dil, name=f"merge_bwd{tag}")
    dqkv = [_dil_bwd(s["qkv_d"][grp], dos[grp], s["ld"][grp], dterms[grp], grp, name=f"dil{grp}_bwd{tag}")
            for grp in range(N_GROUPS)]
    dproj_d, dgain = _prep_bwd(s["proj"], dqkv, p["gains"], cos, sin, name=f"prep_bwd{tag}")
    g["q_gain"], g["k_gain"] = dgain[:N_GROUPS, 0], dgain[N_GROUPS:, 0]
    dq_s, dk_s, dv_s = _sb_bwd(s["qkv_s"], s["y_sb32"], dy_sb, sb_heads, 0, name=f"sb_bwd{tag}")
    dproj = jnp.concatenate([dproj_d, dq_s, dk_s, dv_s, dgp0, dgp1], axis=1)
    g["w_in"] = _carry(rides, "proj_wgrad", (g, done), lambda ride: _matmul(
        dproj, s["h"], mode="tn", out_dtype=BF16, name=f"proj_wgrad{tag}", ride=ride,
        tn=_pick(s["h"].shape[1], (2048, 1024, 512, 256, 128)), tk=_pick(s["h"].shape[0], (1024, 512))))
    dh = _carry(rides, "proj_bwd", (g, done), lambda ride: _matmul(
        dproj, p["w_in"], mode="nn", out_dtype=F32, name=f"proj_bwd{tag}", ride=ride))
    dx, dx_b, g["norm1"] = _rmsnorm_bwd(s["x"], p["norm1"], dh, dx1, name=f"norm1_bwd{tag}")
    return dx, dx_b, g


def _local_step(x, target, layers, fwd_rides=None, bwd_rides=None):
    depth = len(layers)
    fwd_rides, bwd_rides = (r or [None] * depth for r in (fwd_rides, bwd_rides))
    cos, sin = _rope_tables(x.shape[0])
    saved = []
    for l, p in enumerate(layers):
        x, s = _layer_fwd(x, p, cos, sin, f"_l{l}", fwd_rides[l])
        saved.append(s)
    dx, dx_b, loss = _loss_head(x, target, name="loss_head")
    grads = [None] * depth
    for l in reversed(range(depth)):
        dx, dx_b, grads[l] = _layer_bwd(dx, dx_b, layers[l], saved[l], cos, sin, f"_l{l}", bwd_rides[l], grads)
    return loss, dx, grads


_MATRICES = ("w_in", "w_up_dil", "w_up_sb", "w_out", "w_ff1", "w_ff2")
_TRANSPOSED = ("w_in", "w_up_dil", "w_up_sb", "w_ff1")
_SMALL = ("norm1_g", "norm2_g", "q_norm_g", "k_norm_g")


def _unshard(blocks, name):
    if name == "gate_b":
        return jnp.transpose(blocks, (1, 0, 2)).reshape(blocks.shape[1], N_DEV * blocks.shape[2])
    return blocks.reshape(N_DEV * blocks.shape[1], blocks.shape[2])


def _to_chunks(full, name):
    if name == "gate_b":
        r, cols = full.shape
        return jnp.transpose(full.reshape(r, N_DEV, cols // N_DEV), (1, 0, 2))
    return full.reshape(N_DEV, full.shape[0] // N_DEV, full.shape[1])


def _lane_pad(n):
    return -n % HEAD_DIM


def _pack_small(norm1, norm2, qg, kg):
    flat = jnp.concatenate([t.reshape(-1, HEAD_DIM) for t in (norm1, norm2, qg, kg)], axis=0)
    return jnp.pad(flat, ((0, -flat.shape[0] % 8), (0, 0)))


def _unpack_small(packed, shapes):
    out, row = [], 0
    for shape in shapes:
        rows = math.prod(shape) // HEAD_DIM
        out.append(packed[row:row + rows].reshape(shape))
        row += rows
    return out


def kernel(x, norm1_g, w_in, q_norm_g, k_norm_g, w_up_dil, w_up_sb, gate_b, w_out, norm2_g, w_ff1, w_ff2, loss_target, m_norm1_g, m_w_in, m_q_norm_g, m_k_norm_g, m_w_up_dil, m_w_up_sb, m_gate_b, m_w_out, m_norm2_g, m_w_ff1, m_w_ff2, v_norm1_g, v_w_in, v_q_norm_g, v_k_norm_g, v_w_up_dil, v_w_up_sb, v_gate_b, v_w_out, v_norm2_g, v_w_ff1, v_w_ff2):
    names = ("norm1_g", "w_in", "q_norm_g", "k_norm_g", "w_up_dil", "w_up_sb", "gate_b", "w_out", "norm2_g", "w_ff1", "w_ff2")
    w = dict(zip(names, (norm1_g, w_in, q_norm_g, k_norm_g, w_up_dil, w_up_sb, gate_b, w_out, norm2_g, w_ff1, w_ff2)))
    m = dict(zip(names, (m_norm1_g, m_w_in, m_q_norm_g, m_k_norm_g, m_w_up_dil, m_w_up_sb, m_gate_b, m_w_out, m_norm2_g, m_w_ff1, m_w_ff2)))
    v = dict(zip(names, (v_norm1_g, v_w_in, v_q_norm_g, v_k_norm_g, v_w_up_dil, v_w_up_sb, v_gate_b, v_w_out, v_norm2_g, v_w_ff1, v_w_ff2)))
    depth = norm1_g.shape[0]
    assert depth == 2, "the exchange schedule below is written for two layers"
    sharded = _MATRICES + ("gate_b",)

    def shards(layer, which):
        out = []
        for n in which:
            shard = w[n][layer]
            if n in _TRANSPOSED:
                cols = shard.shape[1]
                padded = jnp.pad(shard, ((0, 0), (0, _lane_pad(cols))))
                out.append(_transpose_cast(padded, name=f"shard_t_{n}_l{layer}")[:cols])
            else:
                out.append(shard if n == "gate_b" else shard.astype(BF16))
        return out

    assert depth == 2, "the schedule of exchanges below is written for two layers"
    layers = [{"norm1": norm1_g[l][None], "norm2": norm2_g[l][None], "gains": _gain_table(q_norm_g[l], k_norm_g[l])}
              for l in range(depth)]
    rest = sharded[1:]

    def gather_on(layer, which):
        def make():
            ride = _Ride([], shards(layer, which))
            return ride, lambda lands: layers[layer].update({n: _unshard(b, n) for n, b in zip(which, lands)})
        return make

    (w_in_0,) = _exchange([], shards(0, sharded[:1]), name="gather_w_in_l0")
    layers[0]["w_in"] = _unshard(w_in_0, "w_in")
    fwd_rides = [{"proj_fwd": gather_on(0, rest), "out_fwd": gather_on(1, ("w_up_dil", "w_up_sb", "w_out", "gate_b")),
                  "ff1_fwd": gather_on(1, ("w_in",)), "ff2_fwd": gather_on(1, ("w_ff1", "w_ff2"))}, None]

    sent, landed = {}, {}

    def exchange_on(items):
        def make(g, done):
            chunks = [_to_chunks((g if done[layer] is None else done[layer])[n], n) for layer, n in items]

            def on_landed(lands):
                for item, chunk, land in zip(items, chunks, lands):
                    sent[item], landed[item] = chunk, land
            return _Ride(chunks, []), on_landed
        return make

    others = ("w_ff1", "w_out", "w_up_dil", "w_up_sb", "gate_b")
    bwd_rides = [{"ff1_bwd": exchange_on([(l, "w_ff2")]), "proj_wgrad": exchange_on([(l, n) for n in others]),
                  "proj_bwd": exchange_on([(l, "w_in")])} for l in range(depth)]
    loss_part, dx, grads = _local_step(x[0], loss_target[0], layers, fwd_rides, bwd_rides)
    loss = lax.psum(loss_part[0, 0], ("x", "y", "c"))

    small = _pack_small(jnp.concatenate([g["norm1"] for g in grads]), jnp.concatenate([g["norm2"] for g in grads]),
                        jnp.stack([g["q_gain"] for g in grads]), jnp.stack([g["k_gain"] for g in grads]))
    (small_parts,) = _exchange([], [small], name="exchange_small")

    out = {}
    for n in sharded:
        rows, cols = depth * w[n].shape[1], w[n].shape[2]
        own, parts = ([moved[(l, n)] for l in range(depth)] for moved in (sent, landed))
        state = [t.reshape(rows, cols) for t in (w[n], m[n], v[n])]
        if n in _TRANSPOSED:
            pad = _lane_pad(cols)
            res = _reduce_adamw_t(parts, own, *(jnp.pad(t, ((0, 0), (0, pad))) for t in state), name=f"adamw_{n}")
            res = [t[:, :cols] for t in res]
        else:
            if w[n].shape[1] % 8:
                own, parts = [jnp.concatenate(own, axis=1)], [jnp.concatenate(parts, axis=1)]
            res = _reduce_adamw(parts, own, *state, name=f"adamw_{n}")
        out[n] = [t.reshape(w[n].shape) for t in res]
    small_res = _reduce_adamw([small_parts], [small], _pack_small(*(w[n] for n in _SMALL)),
                              _pack_small(*(m[n] for n in _SMALL)), _pack_small(*(v[n] for n in _SMALL)),
                              own_chunked=False, name="adamw_small")
    small_shapes = [w[n].shape for n in _SMALL]
    for k, t in enumerate(small_res):
        for n, arr in zip(_SMALL, _unpack_small(t, small_shapes)):
            out.setdefault(n, [None] * 4)[k] = arr
    return (loss, dx[None], *(out[n][0] for n in names), *(out[n][1] for n in names), *(out[n][2] for n in names),
            *(out[n][3] for n in names))
```

```python
import functools
import math

import jax
import jax.numpy as jnp
from jax import lax
from jax.experimental import pallas as pl
from jax.experimental.pallas import tpu as pltpu

F32 = jnp.float32
BF16 = jnp.bfloat16

HEAD_DIM = 128
BLOCK = 128
N_GROUPS = 3
DILATIONS = (1, 4, 16)
ROPE_THETA = 10000.0
EPS = 1e-6
ADAM_LR = 0.001
ADAM_B1 = 0.9
ADAM_B2 = 0.999
ADAM_EPS = 1e-08
ADAM_WD = 0.01
ADAM_STEP = 10
N_DEV = 8
MESH = pl.DeviceIdType.MESH
VMEM_LIMIT_BYTES = 48 * 1024 * 1024
NEG = -1e30


def _params(sem):
    return pltpu.CompilerParams(dimension_semantics=sem, vmem_limit_bytes=VMEM_LIMIT_BYTES)


def _pick(n, options):
    for o in options:
        if n % o == 0:
            return o
    return n


_DIMS = {"nn": (((1,), (0,)), ((), ())), "nt": (((1,), (1,)), ((), ())), "tn": (((0,), (0,)), ((), ()))}


def _matmul(a, b, *, mode, out_dtype, name, epilogue=None, extra=None, tm=None, tn=None, tk=None, ride=None):
    if mode == "nn":
        (M, K), (K2, N) = a.shape, b.shape
    elif mode == "nt":
        (M, K), (N, K2) = a.shape, b.shape
    else:
        (K, M), (K2, N) = a.shape, b.shape
    assert K == K2, (a.shape, b.shape, mode)
    tm = tm or _pick(M, (1024, 512, 256, 128))
    tn = tn or _pick(N, (512, 256, 128))
    tk = tk or _pick(K, (2048, 2944, 1024, 512, 256, 128))
    nk = K // tk
    dims = _DIMS[mode]
    n_extra = 0 if extra is None else 1
    n_out = 2 if epilogue == "relu2" else 1
    n_ride = 0 if ride is None else ride.n
    grid = (M // tm, N // tn, nk)

    def body(*refs):
        a_ref, b_ref = refs[0], refs[1]
        extra_ref = refs[2] if n_extra else None
        first_out = 2 + n_extra + n_ride
        outs = refs[first_out:first_out + n_out]
        acc_ref = refs[first_out + n_out + n_ride] if nk > 1 else None
        if ride is not None:
            ride_refs = (refs[2 + n_extra:first_out], refs[first_out + n_out:first_out + n_out + n_ride], refs[-3:])
            at = [pl.program_id(d) for d in range(3)]

            @pl.when(jnp.logical_and(jnp.logical_and(at[0] == 0, at[1] == 0), at[2] == 0))
            def _():
                ride.start(*ride_refs)

            if ride.gather:
                @pl.when(jnp.logical_and(jnp.logical_and(at[0] == (7 * grid[0]) // 8, at[1] == 0), at[2] == 0))
                def _():
                    ride.forward(*ride_refs)

        def finish(acc):
            if epilogue is None:
                outs[0][...] = acc.astype(outs[0].dtype)
            elif epilogue == "add":
                outs[0][...] = (acc + extra_ref[...]).astype(outs[0].dtype)
            elif epilogue == "relu2":
                r = jnp.maximum(acc, 0.0)
                outs[0][...] = r.astype(outs[0].dtype)
                outs[1][...] = (r * r).astype(outs[1].dtype)
            else:
                outs[0][...] = (acc * (2.0 * extra_ref[...].astype(F32))).astype(outs[0].dtype)

        prod = lax.dot_general(a_ref[...], b_ref[...], dims, preferred_element_type=F32)
        if nk == 1:
            finish(prod)
        else:
            k = pl.program_id(2)

            @pl.when(k == 0)
            def _():
                acc_ref[...] = prod

            @pl.when(k > 0)
            def _():
                acc_ref[...] += prod

            @pl.when(k == nk - 1)
            def _():
                finish(acc_ref[...])

        if ride is not None:
            @pl.when(jnp.logical_and(jnp.logical_and(at[0] == grid[0] - 1, at[1] == grid[1] - 1), at[2] == nk - 1))
            def _():
                ride.finish(*ride_refs)

    if mode == "nn":
        a_spec = pl.BlockSpec((tm, tk), lambda i, j, k: (i, k))
        b_spec = pl.BlockSpec((tk, tn), lambda i, j, k: (k, j))
    elif mode == "nt":
        a_spec = pl.BlockSpec((tm, tk), lambda i, j, k: (i, k))
        b_spec = pl.BlockSpec((tn, tk), lambda i, j, k: (j, k))
    else:
        a_spec = pl.BlockSpec((tk, tm), lambda i, j, k: (k, i))
        b_spec = pl.BlockSpec((tk, tn), lambda i, j, k: (k, j))
    o_spec = pl.BlockSpec((tm, tn), lambda i, j, k: (i, j))
    in_specs = [a_spec, b_spec] + ([o_spec] if n_extra else []) + [_ANY] * n_ride
    out_shape = [jax.ShapeDtypeStruct((M, N), out_dtype)] * n_out + (ride.land_shapes() if ride else [])
    res = pl.pallas_call(
        body,
        name=name,
        grid=grid,
        in_specs=in_specs,
        out_specs=[o_spec] * n_out + [_ANY] * n_ride,
        out_shape=out_shape,
        scratch_shapes=([pltpu.VMEM((tm, tn), F32)] if nk > 1 else []) + (ride.semaphores() if ride else []),
        compiler_params=_params(("arbitrary",) * 3 if ride else ("parallel", "parallel", "arbitrary")),
    )(a, b, *([extra] if n_extra else []), *(ride.arrays if ride else []))
    if ride is None:
        return res if n_out > 1 else res[0]
    return (res[:n_out] if n_out > 1 else res[0]), res[n_out:]


def _rmsnorm_fwd(x, g, *, name):
    T, D = x.shape
    tm = _pick(T, (512, 256, 128))

    def body(x_ref, g_ref, o_ref):
        xf = x_ref[...]
        r = lax.rsqrt(jnp.mean(xf * xf, axis=-1, keepdims=True) + EPS)
        o_ref[...] = ((xf * r) * g_ref[...]).astype(o_ref.dtype)

    return pl.pallas_call(
        body,
        name=name,
        grid=(T // tm,),
        in_specs=[pl.BlockSpec((tm, D), lambda i: (i, 0)), pl.BlockSpec((1, D), lambda i: (0, 0))],
        out_specs=pl.BlockSpec((tm, D), lambda i: (i, 0)),
        out_shape=jax.ShapeDtypeStruct((T, D), BF16),
        compiler_params=_params(("parallel",)),
    )(x, g)


def _rmsnorm_bwd(x, g, dh, dres, *, name):
    T, D = x.shape
    tm = _pick(T, (256, 128))

    def body(x_ref, g_ref, dh_ref, dres_ref, dx_ref, dxb_ref, dg_ref):
        i = pl.program_id(0)
        xf = x_ref[...]
        r = lax.rsqrt(jnp.mean(xf * xf, axis=-1, keepdims=True) + EPS)
        y = xf * r
        dh_v = dh_ref[...]
        dy = dh_v * g_ref[...]
        c = jnp.mean(dy * y, axis=-1, keepdims=True)
        dx = r * (dy - y * c) + dres_ref[...]
        dx_ref[...] = dx
        dxb_ref[...] = dx.astype(BF16)
        part = jnp.sum(dh_v * y, axis=0, keepdims=True)

        @pl.when(i == 0)
        def _():
            dg_ref[...] = part

        @pl.when(i > 0)
        def _():
            dg_ref[...] += part

    row = pl.BlockSpec((tm, D), lambda i: (i, 0))
    vec = pl.BlockSpec((1, D), lambda i: (0, 0))
    return pl.pallas_call(
        body,
        name=name,
        grid=(T // tm,),
        in_specs=[row, vec, row, row],
        out_specs=[row, row, vec],
        out_shape=[jax.ShapeDtypeStruct((T, D), F32), jax.ShapeDtypeStruct((T, D), BF16), jax.ShapeDtypeStruct((1, D), F32)],
        compiler_params=_params(("arbitrary",)),
    )(x, g, dh, dres)


GROUP_W = 4 * HEAD_DIM
N_DIL_BLOCKS = 3 * N_GROUPS
N_NORMED = 2 * N_GROUPS


def _kind_of_group(j, g):
    return jnp.clip((j - g) // N_GROUPS, 0, 2)


def _head_rstd(xh):
    return lax.rsqrt(jnp.mean(xh * xh, axis=-1, keepdims=True) + EPS)


def _prep_fwd(proj, gains, cos, sin, n_sb_blocks, *, name):
    T = proj.shape[0]
    tm = _pick(T, (1024, 512, 256, 128))

    def body(p_ref, gain_ref, cos_ref, sin_ref, o0_ref, o1_ref, o2_ref, os_ref):
        j = pl.program_id(1)
        for g, o_ref in enumerate((o0_ref, o1_ref, o2_ref)):
            @pl.when(jnp.logical_and(j % N_GROUPS == g, j < N_NORMED))
            def _():
                cos_v, sin_v = cos_ref[...], sin_ref[...]
                for hh in range(4):
                    sl = slice(hh * HEAD_DIM, (hh + 1) * HEAD_DIM)
                    xh = p_ref[:, sl]
                    y = (xh * _head_rstd(xh)) * gain_ref[0, :, sl]
                    o_ref[:, sl] = (y * cos_v + pltpu.roll(y, HEAD_DIM // 2, 1) * sin_v).astype(BF16)

            @pl.when(j == N_NORMED + g)
            def _():
                o_ref[...] = p_ref[...].astype(BF16)

        @pl.when(j >= N_DIL_BLOCKS)
        def _():
            os_ref[...] = p_ref[...].astype(BF16)

    def group_spec(g):
        return pl.BlockSpec((tm, GROUP_W), lambda i, j: (i, _kind_of_group(j, g)))

    return pl.pallas_call(
        body,
        name=name,
        grid=(T // tm, N_DIL_BLOCKS + n_sb_blocks),
        in_specs=[
            pl.BlockSpec((tm, GROUP_W), lambda i, j: (i, j)),
            pl.BlockSpec((1, 1, GROUP_W), lambda i, j: (jnp.minimum(j, N_NORMED - 1), 0, 0)),
            pl.BlockSpec((tm, HEAD_DIM), lambda i, j: (i, 0)),
            pl.BlockSpec((tm, HEAD_DIM), lambda i, j: (i, 0)),
        ],
        out_specs=[group_spec(0), group_spec(1), group_spec(2),
                   pl.BlockSpec((tm, GROUP_W), lambda i, j: (i, jnp.maximum(j - N_DIL_BLOCKS, 0)))],
        out_shape=[jax.ShapeDtypeStruct((T, 3 * GROUP_W), BF16)] * N_GROUPS
        + [jax.ShapeDtypeStruct((T, n_sb_blocks * GROUP_W), BF16)],
        compiler_params=_params(("parallel", "arbitrary")),
    )(proj, gains, cos, sin)


def _prep_bwd(proj, dqkv, gains, cos, sin, *, name):
    T = proj.shape[0]
    tm = _pick(T, (1024, 512, 256, 128))

    def body(p_ref, d0_ref, d1_ref, d2_ref, gain_ref, cos_ref, sin_ref, o_ref, dgain_ref):
        j, i = pl.program_id(0), pl.program_id(1)

        def normed_bwd(d_ref):
            cos_v, sin_v = cos_ref[...], sin_ref[...]
            part = jnp.zeros((1, HEAD_DIM), F32)
            for hh in range(4):
                sl = slice(hh * HEAD_DIM, (hh + 1) * HEAD_DIM)
                xh = p_ref[:, sl]
                r = _head_rstd(xh)
                y0 = xh * r
                d_out = d_ref[:, sl]
                d_yg = d_out * cos_v + pltpu.roll(d_out * sin_v, HEAD_DIM // 2, 1)
                part = part + jnp.sum(d_yg * y0, axis=0, keepdims=True)
                dy0 = d_yg * gain_ref[0, :, sl]
                c = jnp.mean(dy0 * y0, axis=-1, keepdims=True)
                o_ref[:, sl] = (r * (dy0 - y0 * c)).astype(BF16)

            @pl.when(i == 0)
            def _():
                dgain_ref[0] = part

            @pl.when(i > 0)
            def _():
                dgain_ref[0] += part

        for g, d_ref in enumerate((d0_ref, d1_ref, d2_ref)):
            @pl.when(jnp.logical_and(j % N_GROUPS == g, j < N_NORMED))
            def _():
                normed_bwd(d_ref)

            @pl.when(j == N_NORMED + g)
            def _():
                o_ref[...] = d_ref[...].astype(BF16)

    gain_row = lambda j, i: (jnp.minimum(j, N_NORMED - 1), 0, 0)

    def grad_spec(g):
        return pl.BlockSpec((tm, GROUP_W), lambda j, i: (i, _kind_of_group(j, g)))

    return pl.pallas_call(
        body,
        name=name,
        grid=(N_DIL_BLOCKS, T // tm),
        in_specs=[
            pl.BlockSpec((tm, GROUP_W), lambda j, i: (i, j)),
            grad_spec(0), grad_spec(1), grad_spec(2),
            pl.BlockSpec((1, 1, GROUP_W), gain_row),
            pl.BlockSpec((tm, HEAD_DIM), lambda j, i: (i, 0)),
            pl.BlockSpec((tm, HEAD_DIM), lambda j, i: (i, 0)),
        ],
        out_specs=[pl.BlockSpec((tm, GROUP_W), lambda j, i: (i, j)), pl.BlockSpec((1, 1, HEAD_DIM), gain_row)],
        out_shape=[jax.ShapeDtypeStruct((T, N_DIL_BLOCKS * GROUP_W), BF16),
                   jax.ShapeDtypeStruct((2 * N_GROUPS, 1, HEAD_DIM), F32)],
        compiler_params=_params(("arbitrary", "arbitrary")),
    )(proj, *dqkv, gains, cos, sin)


def _nt(a, b):
    return lax.dot_general(a, b, _DIMS["nt"], preferred_element_type=F32)


def _tn(a, b):
    return lax.dot_general(a, b, _DIMS["tn"], preferred_element_type=F32)


def _nn(a, b):
    return jnp.dot(a, b, preferred_element_type=F32)


def _window_masks():
    row = lax.broadcasted_iota(jnp.int32, (BLOCK, BLOCK), 0)
    col = lax.broadcasted_iota(jnp.int32, (BLOCK, BLOCK), 1)
    return row >= col, col >= row


def _heads():
    return [slice(hh * HEAD_DIM, (hh + 1) * HEAD_DIM) for hh in range(GROUP_W // HEAD_DIM)]


def _dil_fwd(qkv, g, *, name):
    T = qkv.shape[0]
    r = DILATIONS[g]
    L = T // r
    nb = L // BLOCK
    scale = 1.0 / math.sqrt(HEAD_DIM)
    view = qkv.reshape(L, r * 3 * GROUP_W)

    def body(q_ref, kc_ref, kp_ref, vc_ref, vp_ref, o_ref, ld_ref):
        n = pl.program_id(1)
        m_cur, m_prev = _window_masks()
        m_prev = jnp.logical_and(m_prev, n > 0)
        heads = _heads()
        qs = [q_ref[:, sl] for sl in heads]
        s_c = [jnp.where(m_cur, _nt(q, kc_ref[:, sl]) * scale, NEG) for q, sl in zip(qs, heads)]
        s_p = [jnp.where(m_prev, _nt(q, kp_ref[:, sl]) * scale, NEG) for q, sl in zip(qs, heads)]
        m = [jnp.maximum(jnp.max(c, axis=-1, keepdims=True), jnp.max(p, axis=-1, keepdims=True)) for c, p in zip(s_c, s_p)]
        p_c = [jnp.exp(c - mx) for c, mx in zip(s_c, m)]
        p_p = [jnp.exp(p - mx) for p, mx in zip(s_p, m)]
        l = [jnp.sum(c, axis=-1, keepdims=True) + jnp.sum(p, axis=-1, keepdims=True) for c, p in zip(p_c, p_p)]
        inv = [1.0 / v for v in l]
        outs = [_nn((c * r).astype(BF16), vc_ref[:, sl]) + _nn((p * r).astype(BF16), vp_ref[:, sl])
                for c, p, r, sl in zip(p_c, p_p, inv, heads)]
        for sl, o, mx, v in zip(heads, outs, m, l):
            o_ref[:, sl] = o
            ld_ref[:, sl] = jnp.broadcast_to(mx + jnp.log(v), (BLOCK, HEAD_DIM))

    blk = (BLOCK, GROUP_W)
    cur = lambda kind: pl.BlockSpec(blk, lambda c, n: (n, 3 * c + kind))
    prev = lambda kind: pl.BlockSpec(blk, lambda c, n: (jnp.maximum(n - 1, 0), 3 * c + kind))
    out_spec = pl.BlockSpec(blk, lambda c, n: (n, c))
    o, ld = pl.pallas_call(
        body,
        name=name,
        grid=(r, nb),
        in_specs=[cur(0), cur(1), prev(1), cur(2), prev(2)],
        out_specs=[out_spec, out_spec],
        out_shape=[jax.ShapeDtypeStruct((L, r * GROUP_W), F32)] * 2,
        compiler_params=_params(("parallel", "arbitrary")),
    )(view, view, view, view, view)
    return o.reshape(T, GROUP_W), ld.reshape(T, GROUP_W)


def _dil_bwd(qkv, do, ld, dterm, g, *, name):
    T = qkv.shape[0]
    r = DILATIONS[g]
    L = T // r
    nb = L // BLOCK
    scale = 1.0 / math.sqrt(HEAD_DIM)
    view = qkv.reshape(L, r * 3 * GROUP_W)
    do_v, ld_v, dt_v = (t.reshape(L, r * GROUP_W) for t in (do, ld, dterm))

    def body(q_ref, qn_ref, kc_ref, kp_ref, vc_ref, vp_ref, do_ref, don_ref, ld_ref, ldn_ref, dt_ref, dtn_ref, out_ref):
        n = pl.program_id(1)
        m_cur, m_prev = _window_masks()
        has_prev, has_next = jnp.logical_and(m_prev, n > 0), jnp.logical_and(m_prev, n < nb - 1)

        def tile(q, k, v, do_t, ld_t, dt_t, mask):
            s = _nt(q, k) * scale
            p = jnp.where(mask, jnp.exp(s - ld_t[:, 0:1]), 0.0)
            ds = p * (_nt(do_t, v) + dt_t[:, 0:1]) * scale
            return p.astype(BF16), ds.astype(BF16)

        heads = _heads()
        kc, vc, kp = ([ref[:, sl] for sl in heads] for ref in (kc_ref, vc_ref, kp_ref))
        q, do_t, qn, don = ([ref[:, sl] for sl in heads] for ref in (q_ref, do_ref, qn_ref, don_ref))
        cc = [tile(q[h], kc[h], vc[h], do_t[h], ld_ref[:, sl], dt_ref[:, sl], m_cur) for h, sl in enumerate(heads)]
        cp = [tile(q[h], kp[h], vp_ref[:, sl], do_t[h], ld_ref[:, sl], dt_ref[:, sl], has_prev) for h, sl in enumerate(heads)]
        nc = [tile(qn[h], kc[h], vc[h], don[h], ldn_ref[:, sl], dtn_ref[:, sl], has_next) for h, sl in enumerate(heads)]
        dq = [_nn(cc[h][1], kc[h]) + _nn(cp[h][1], kp[h]) for h in range(len(heads))]
        dk = [_tn(cc[h][1], q[h]) + _tn(nc[h][1], qn[h]) for h in range(len(heads))]
        dv = [_tn(cc[h][0], do_t[h]) + _tn(nc[h][0], don[h]) for h in range(len(heads))]
        for kind, grads in enumerate((dq, dk, dv)):
            for h, grad in enumerate(grads):
                out_ref[:, kind * GROUP_W + h * HEAD_DIM:kind * GROUP_W + (h + 1) * HEAD_DIM] = grad

    blk = (BLOCK, GROUP_W)
    qkv_spec = lambda kind, shift: pl.BlockSpec(blk, lambda c, n: (jnp.clip(n + shift, 0, nb - 1), 3 * c + kind))
    row_spec = lambda shift: pl.BlockSpec(blk, lambda c, n: (jnp.clip(n + shift, 0, nb - 1), c))
    out = pl.pallas_call(
        body,
        name=name,
        grid=(r, nb),
        in_specs=[qkv_spec(0, 0), qkv_spec(0, 1), qkv_spec(1, 0), qkv_spec(1, -1), qkv_spec(2, 0), qkv_spec(2, -1),
                  row_spec(0), row_spec(1), row_spec(0), row_spec(1), row_spec(0), row_spec(1)],
        out_specs=pl.BlockSpec((BLOCK, 3 * GROUP_W), lambda c, n: (n, c)),
        out_shape=jax.ShapeDtypeStruct((L, r * 3 * GROUP_W), F32),
        compiler_params=_params(("parallel", "arbitrary")),
    )(view, view, view, view, view, view, do_v, do_v, ld_v, ld_v, dt_v, dt_v)
    return out.reshape(T, 3 * GROUP_W)


def _group_weights(ld_refs):
    lds = [r[...] for r in ld_refs]
    m = jnp.maximum(jnp.maximum(lds[0], lds[1]), lds[2])
    es = [jnp.exp(v - m) for v in lds]
    inv = 1.0 / (es[0] + es[1] + es[2])
    return [e * inv for e in es]


def _merge_fwd(os_, lds, *, name):
    T = os_[0].shape[0]
    tm = _pick(T, (1024, 512, 256, 128))

    def body(o0, o1, o2, l0, l1, l2, y_ref):
        w = _group_weights((l0, l1, l2))
        y_ref[...] = (w[0] * o0[...] + w[1] * o1[...] + w[2] * o2[...]).astype(BF16)

    spec = pl.BlockSpec((tm, GROUP_W), lambda i: (i, 0))
    return pl.pallas_call(
        body, name=name, grid=(T // tm,), in_specs=[spec] * 6, out_specs=spec,
        out_shape=jax.ShapeDtypeStruct((T, GROUP_W), BF16), compiler_params=_params(("parallel",)),
    )(*os_, *lds)


def _merge_bwd(os_, lds, dy, *, name):
    T = dy.shape[0]
    tm = _pick(T, (512, 256, 128))

    def body(o0, o1, o2, l0, l1, l2, dy_ref, do0, do1, do2, dt0, dt1, dt2):
        w = _group_weights((l0, l1, l2))
        dy_v = dy_ref[...]
        y = w[0] * o0[...] + w[1] * o1[...] + w[2] * o2[...]
        prod = dy_v * y
        for hh in range(4):
            sl = slice(hh * HEAD_DIM, (hh + 1) * HEAD_DIM)
            s = jnp.sum(prod[:, sl], axis=-1, keepdims=True)
            for wg, dt in zip(w, (dt0, dt1, dt2)):
                dt[:, sl] = -wg[:, sl] * s
        for wg, do in zip(w, (do0, do1, do2)):
            do[...] = (wg * dy_v).astype(BF16)

    spec = pl.BlockSpec((tm, GROUP_W), lambda i: (i, 0))
    outs = pl.pallas_call(
        body, name=name, grid=(T // tm,), in_specs=[spec] * 7, out_specs=[spec] * 6,
        out_shape=[jax.ShapeDtypeStruct((T, GROUP_W), BF16)] * 3 + [jax.ShapeDtypeStruct((T, GROUP_W), F32)] * 3,
        compiler_params=_params(("parallel",)),
    )(*os_, *lds, dy)
    return outs[:3], outs[3:]


SB_ROWS = 512
SB_KEYS = 256


def _sum_matrix(inclusive):
    j = lax.broadcasted_iota(jnp.int32, (2 * BLOCK, 2 * BLOCK), 0) % BLOCK
    s = lax.broadcasted_iota(jnp.int32, (2 * BLOCK, 2 * BLOCK), 1)
    later = (j >= s) if inclusive else (j > s)
    return jnp.logical_or(s >= BLOCK, later).astype(BF16)


def _block_sums(x, mat):
    hi = x.astype(BF16)
    lo = (x - hi.astype(F32)).astype(BF16)
    r = _nn(jnp.concatenate([hi, lo], axis=1), mat)
    return r[:, :BLOCK], r[:, BLOCK:]


def _log_terms(z):
    t = jnp.log(1.0 + jnp.exp(-jnp.abs(z)))
    return -(jnp.maximum(z, 0.0) + t), jnp.minimum(z, 0.0) - t


SB_DEAD = -105.0


def _sb_alive(n_chunks, state):
    return jnp.logical_and(state[0] < n_chunks, jnp.max(state[1]) > SB_DEAD)


def _causal_mask(rows, cols, first_col):
    row = lax.broadcasted_iota(jnp.int32, (rows, cols), 0)
    col = lax.broadcasted_iota(jnp.int32, (rows, cols), 1)
    return col + first_col < row


def _sb_fwd(qkv, n_heads, col0, *, name):
    T = qkv.shape[0]
    tq = _pick(T, (SB_ROWS, BLOCK))
    kc = _pick(tq, (SB_KEYS, BLOCK))
    nq, nsub, per_tile = T // tq, kc // BLOCK, tq // kc
    scale = 1.0 / math.sqrt(HEAD_DIM)

    def body(q_ref, k_ref, v_ref, o_ref, ob_ref):
        i = pl.program_id(1)
        q = q_ref[...]
        mat = _sum_matrix(False)

        def chunk(j, carry, acc, mask):
            rows = pl.ds(pl.multiple_of(j * kc, kc), kc)
            z = _nt(q, k_ref[rows, :]) * scale
            lk, ls = _log_terms(z)
            if mask is not None:
                lk = jnp.where(mask, lk, 0.0)
            a = []
            for b in reversed(range(nsub)):
                sl = slice(b * BLOCK, (b + 1) * BLOCK)
                later, total = _block_sums(lk[:, sl], mat)
                a.append(jnp.exp(ls[:, sl] + (later + carry)))
                carry = carry + total
            a = jnp.concatenate(a[::-1], axis=1)
            if mask is not None:
                a = jnp.where(mask, a, 0.0)
            return carry, acc + _nn(a.astype(BF16), v_ref[rows, :])

        carry = acc = jnp.zeros((tq, HEAD_DIM), F32)
        for d in reversed(range(per_tile)):
            carry, acc = chunk(i * per_tile + d, carry, acc, _causal_mask(tq, kc, d * kc))

        def step(state):
            carry, acc = chunk(i * per_tile - 1 - state[0], state[1], state[2], None)
            return state[0] + 1, carry, acc

        _, carry, acc = lax.while_loop(functools.partial(_sb_alive, i * per_tile), step, (0, carry, acc))
        o_ref[...] = acc
        ob_ref[...] = acc.astype(BF16)

    blk = (tq, HEAD_DIM)
    out_spec = pl.BlockSpec(blk, lambda h, i: (i, h))
    return pl.pallas_call(
        body,
        name=name,
        grid=(n_heads, nq),
        in_specs=[
            pl.BlockSpec(blk, lambda h, i: (i, col0 + h)),
            pl.BlockSpec((T, HEAD_DIM), lambda h, i: (0, col0 + n_heads + h)),
            pl.BlockSpec((T, HEAD_DIM), lambda h, i: (0, col0 + 2 * n_heads + h)),
        ],
        out_specs=[out_spec, out_spec],
        out_shape=[jax.ShapeDtypeStruct((T, n_heads * HEAD_DIM), F32), jax.ShapeDtypeStruct((T, n_heads * HEAD_DIM), BF16)],
        compiler_params=_params(("parallel", "arbitrary")),
    )(qkv, qkv, qkv)


def _sb_bwd(qkv, o32, do, n_heads, col0, *, name):
    T = qkv.shape[0]
    tq = _pick(T, (SB_ROWS, BLOCK))
    kc = _pick(tq, (SB_KEYS, BLOCK))
    nq, nsub, per_tile = T // tq, kc // BLOCK, tq // kc
    scale = 1.0 / math.sqrt(HEAD_DIM)

    def body(q_ref, k_ref, v_ref, o_ref, do_ref, dq_ref, dk_ref, dv_ref, dk_acc, dv_acc):
        i = pl.program_id(1)

        @pl.when(i == 0)
        def _():
            dk_acc[...] = jnp.zeros_like(dk_acc)
            dv_acc[...] = jnp.zeros_like(dv_acc)

        q, do_t = q_ref[...], do_ref[...]
        delta = jnp.broadcast_to(jnp.sum(do_t.astype(F32) * o_ref[...], axis=-1, keepdims=True), (tq, HEAD_DIM))
        mat, mat_incl = _sum_matrix(False), _sum_matrix(True)

        def chunk(j, carry_b, carry_g, dq, mask):
            rows = pl.ds(pl.multiple_of(j * kc, kc), kc)
            k_t, v_t = k_ref[rows, :], v_ref[rows, :]
            z = _nt(q, k_t) * scale
            lk, ls = _log_terms(z)
            if mask is not None:
                lk = jnp.where(mask, lk, 0.0)
            d_a = _nt(do_t, v_t)
            a_parts, dz_parts = [], []
            for b in reversed(range(nsub)):
                sl = slice(b * BLOCK, (b + 1) * BLOCK)
                later, total = _block_sums(lk[:, sl], mat)
                a = jnp.exp(ls[:, sl] + (later + carry_b))
                carry_b = carry_b + total
                if mask is not None:
                    a = jnp.where(mask[:, sl], a, 0.0)
                a_b = a.astype(BF16)
                g = a_b.astype(F32) * d_a[:, sl]
                from_here, total_g = _block_sums(g, mat_incl)
                before = delta - (from_here + carry_g)
                carry_g = carry_g + total_g
                sig = jnp.exp(ls[:, sl])
                dz = (g - sig * (g + before)) * scale
                a_parts.append(a_b)
                dz_parts.append(dz)
            dz = jnp.concatenate(dz_parts[::-1], axis=1)
            if mask is not None:
                dz = jnp.where(mask, dz, 0.0)
            dz_b = dz.astype(BF16)
            dk_acc[rows, :] += _tn(dz_b, q)
            dv_acc[rows, :] += _tn(jnp.concatenate(a_parts[::-1], axis=1), do_t)
            return carry_b, carry_g, dq + _nn(dz_b, k_t)

        zero = jnp.zeros((tq, HEAD_DIM), F32)
        state = (zero, zero, zero)
        for d in reversed(range(per_tile)):
            state = chunk(i * per_tile + d, *state, _causal_mask(tq, kc, d * kc))

        def step(st):
            return (st[0] + 1,) + chunk(i * per_tile - 1 - st[0], st[1], st[2], st[3], None)

        state = lax.while_loop(functools.partial(_sb_alive, i * per_tile), step, (0,) + state)
        dq_ref[...] = state[3].astype(BF16)

        @pl.when(i == nq - 1)
        def _():
            dk_ref[...] = dk_acc[...].astype(BF16)
            dv_ref[...] = dv_acc[...].astype(BF16)

    blk = (tq, HEAD_DIM)
    full = (T, HEAD_DIM)
    dshape = jax.ShapeDtypeStruct((T, n_heads * HEAD_DIM), BF16)
    return pl.pallas_call(
        body,
        name=name,
        grid=(n_heads, nq),
        in_specs=[
            pl.BlockSpec(blk, lambda h, i: (i, col0 + h)),
            pl.BlockSpec(full, lambda h, i: (0, col0 + n_heads + h)),
            pl.BlockSpec(full, lambda h, i: (0, col0 + 2 * n_heads + h)),
            pl.BlockSpec(blk, lambda h, i: (i, h)),
            pl.BlockSpec(blk, lambda h, i: (i, h)),
        ],
        out_specs=[pl.BlockSpec(blk, lambda h, i: (i, h)), pl.BlockSpec(full, lambda h, i: (0, h)),
                   pl.BlockSpec(full, lambda h, i: (0, h))],
        out_shape=[dshape, dshape, dshape],
        scratch_shapes=[pltpu.VMEM(full, F32), pltpu.VMEM(full, F32)],
        compiler_params=_params(("arbitrary", "arbitrary")),
    )(qkv, qkv, qkv, o32, do)


def _gate_fwd(y_dil, y_sb, w_up_dil, w_up_sb, proj, gate_b, gate_col0, *, name):
    T, D = y_dil.shape[0], w_up_dil.shape[0]
    tm = _pick(T, (1024, 512, 256, 128))
    tn = _pick(D, (512, 256, 128))
    c0, nbr = gate_col0 // tn, D // tn

    def body(yd_ref, ys_ref, wd_ref, ws_ref, gp0_ref, gp1_ref, b_ref, o_ref):
        g0 = jax.nn.sigmoid(gp0_ref[...] + b_ref[0:1, :])
        g1 = jax.nn.sigmoid(gp1_ref[...] + b_ref[1:2, :])
        o_ref[...] = (g0 * _nt(yd_ref[...], wd_ref[...]) + g1 * _nt(ys_ref[...], ws_ref[...])).astype(BF16)

    return pl.pallas_call(
        body,
        name=name,
        grid=(T // tm, nbr),
        in_specs=[
            pl.BlockSpec((tm, y_dil.shape[1]), lambda i, j: (i, 0)),
            pl.BlockSpec((tm, y_sb.shape[1]), lambda i, j: (i, 0)),
            pl.BlockSpec((tn, w_up_dil.shape[1]), lambda i, j: (j, 0)),
            pl.BlockSpec((tn, w_up_sb.shape[1]), lambda i, j: (j, 0)),
            pl.BlockSpec((tm, tn), lambda i, j: (i, c0 + j)),
            pl.BlockSpec((tm, tn), lambda i, j: (i, c0 + nbr + j)),
            pl.BlockSpec((2, tn), lambda i, j: (0, j)),
        ],
        out_specs=pl.BlockSpec((tm, tn), lambda i, j: (i, j)),
        out_shape=jax.ShapeDtypeStruct((T, D), BF16),
        compiler_params=_params(("parallel", "parallel")),
    )(y_dil, y_sb, w_up_dil, w_up_sb, proj, proj, gate_b)


def _gate_bwd(y, w_up, proj, gate_b, dmixed, branch, gate_col0, *, name):
    T, D = y.shape[0], w_up.shape[0]
    tm = _pick(T, (1024, 512, 256, 128))
    tn = _pick(D, (512, 256, 128))
    c0 = gate_col0 // tn + branch * (D // tn)

    def body(y_ref, w_ref, gp_ref, b_ref, dm_ref, dup_ref, dgp_ref, db_ref):
        i = pl.program_id(1)
        g = jax.nn.sigmoid(gp_ref[...] + b_ref[branch:branch + 1, :])
        dm = dm_ref[...]
        dup_ref[...] = (dm * g).astype(BF16)
        dgp = (dm * _nt(y_ref[...], w_ref[...])) * (g * (1.0 - g))
        dgp_ref[...] = dgp.astype(BF16)
        part = jnp.sum(dgp, axis=0, keepdims=True)

        @pl.when(i == 0)
        def _():
            db_ref[...] = part

        @pl.when(i > 0)
        def _():
            db_ref[...] += part

    tile = pl.BlockSpec((tm, tn), lambda j, i: (i, j))
    return pl.pallas_call(
        body,
        name=name,
        grid=(D // tn, T // tm),
        in_specs=[
            pl.BlockSpec((tm, y.shape[1]), lambda j, i: (i, 0)),
            pl.BlockSpec((tn, w_up.shape[1]), lambda j, i: (j, 0)),
            pl.BlockSpec((tm, tn), lambda j, i: (i, c0 + j)),
            pl.BlockSpec((2, tn), lambda j, i: (0, j)),
            tile,
        ],
        out_specs=[tile, tile, pl.BlockSpec((1, tn), lambda j, i: (0, j))],
        out_shape=[jax.ShapeDtypeStruct((T, D), BF16), jax.ShapeDtypeStruct((T, D), BF16), jax.ShapeDtypeStruct((1, D), F32)],
        compiler_params=_params(("parallel", "arbitrary")),
    )(y, w_up, proj, gate_b, dmixed)


def _loss_head(y, target, *, name):
    T, D = y.shape
    tm = _pick(T, (256, 128))

    def body(y_ref, t_ref, dy_ref, dyb_ref, l_ref):
        i = pl.program_id(0)
        err = y_ref[...] - t_ref[...]
        dy = err * (1.0 / D)
        dy_ref[...] = dy
        dyb_ref[...] = dy.astype(BF16)
        part = 0.5 * jnp.sum(jnp.mean(err * err, axis=-1, keepdims=True), axis=0, keepdims=True)

        @pl.when(i == 0)
        def _():
            l_ref[...] = part

        @pl.when(i > 0)
        def _():
            l_ref[...] += part

    row = pl.BlockSpec((tm, D), lambda i: (i, 0))
    return pl.pallas_call(
        body, name=name, grid=(T // tm,), in_specs=[row, row],
        out_specs=[row, row, pl.BlockSpec((1, 1), lambda i: (0, 0))],
        out_shape=[jax.ShapeDtypeStruct((T, D), F32), jax.ShapeDtypeStruct((T, D), BF16), jax.ShapeDtypeStruct((1, 1), F32)],
        compiler_params=_params(("arbitrary",)),
    )(y, target)


def _sum_parts(p_ref, own):
    slot = _slot(_place())
    g = jnp.where(slot == 0, own, p_ref[0].astype(F32))
    for s in range(1, N_DEV):
        g = g + jnp.where(slot == s, own, p_ref[s].astype(F32))
    return g


def _held(step, l, last):
    layer, i = step
    return jnp.where(layer == l, i, last * (layer > l))


def _reduce_adamw(parts, own, w, m, v, *, own_chunked=True, name):
    n_layers = len(parts)
    C = w.shape[1]
    R = w.shape[0] // n_layers
    tr = R
    for cand in (1024, 512, 256, 128, 64, 32, 16, 8):
        if R % cand == 0 and cand * C <= 256 * 1024:
            tr = cand
            break
    nr = R // tr

    def body(*refs):
        p_refs, o_refs = refs[:n_layers], refs[n_layers:2 * n_layers]
        w_ref, m_ref, v_ref, g_ref, d_ref, nm_ref, nv_ref = refs[2 * n_layers:]
        layer = pl.program_id(0)
        for l, (p_ref, o_ref) in enumerate(zip(p_refs, o_refs)):
            @pl.when(layer == l)
            def _():
                g_ref[...] = _sum_parts(p_ref, (o_ref[0] if own_chunked else o_ref[...]).astype(F32))

        _adamw_update(g_ref[...], w_ref, m_ref, v_ref, d_ref, nm_ref, nv_ref)

    def part_spec(l):
        return pl.BlockSpec((N_DEV, tr, C), lambda *step: (0, _held(step, l, nr - 1), 0))

    def own_spec(l):
        if own_chunked:
            return pl.BlockSpec((1, tr, C), lambda *step: (_slot(_place()), _held(step, l, nr - 1), 0))
        return pl.BlockSpec((tr, C), lambda *step: (_held(step, l, nr - 1), 0))

    row = pl.BlockSpec((tr, C), lambda layer, i: (layer * nr + i, 0))
    return pl.pallas_call(
        body, name=name, grid=(n_layers, nr),
        in_specs=[part_spec(l) for l in range(n_layers)] + [own_spec(l) for l in range(n_layers)] + [row, row, row],
        out_specs=[row] * 4, out_shape=[jax.ShapeDtypeStruct(w.shape, F32)] * 4,
        compiler_params=_params(("arbitrary", "arbitrary")),
    )(*parts, *own, w, m, v)


def _adamw_update(g, w_ref, m_ref, v_ref, d_ref, nm_ref, nv_ref):
    m_new = ADAM_B1 * m_ref[...] + (1.0 - ADAM_B1) * g
    v_new = ADAM_B2 * v_ref[...] + (1.0 - ADAM_B2) * (g * g)
    m_hat = m_new / (1.0 - ADAM_B1 ** ADAM_STEP)
    v_hat = v_new / (1.0 - ADAM_B2 ** ADAM_STEP)
    d_ref[...] = -ADAM_LR * (m_hat / (jnp.sqrt(v_hat) + ADAM_EPS) + ADAM_WD * w_ref[...])
    nm_ref[...] = m_new
    nv_ref[...] = v_new


def _reduce_adamw_t(parts, own, w, m, v, *, name):
    n_layers = len(parts)
    _, n, K = parts[0].shape
    n_pad = w.shape[1]
    tm = _pick(K, (128,))
    nr = K // tm

    def body(*refs):
        p_refs, o_refs = refs[:n_layers], refs[n_layers:2 * n_layers]
        w_ref, m_ref, v_ref, g_ref, d_ref, nm_ref, nv_ref = refs[2 * n_layers:]
        layer = pl.program_id(0)
        for l, (p_ref, o_ref) in enumerate(zip(p_refs, o_refs)):
            @pl.when(layer == l)
            def _():
                g_t = _sum_parts(p_ref, o_ref[0].astype(F32))
                if n_pad > n:
                    g_t = jnp.concatenate([g_t, jnp.zeros((n_pad - n, tm), F32)], axis=0)
                g_ref[...] = g_t.T

        _adamw_update(g_ref[...], w_ref, m_ref, v_ref, d_ref, nm_ref, nv_ref)

    def part_spec(l):
        return pl.BlockSpec((N_DEV, n, tm), lambda *step: (0, 0, _held(step, l, nr - 1)))

    def own_spec(l):
        return pl.BlockSpec((1, n, tm), lambda *step: (_slot(_place()), 0, _held(step, l, nr - 1)))

    row = pl.BlockSpec((tm, n_pad), lambda layer, i: (layer * nr + i, 0))
    return pl.pallas_call(
        body, name=name, grid=(n_layers, nr),
        in_specs=[part_spec(l) for l in range(n_layers)] + [own_spec(l) for l in range(n_layers)] + [row, row, row],
        out_specs=[row] * 4, out_shape=[jax.ShapeDtypeStruct(w.shape, F32)] * 4,
        compiler_params=_params(("arbitrary", "arbitrary")),
    )(*parts, *own, w, m, v)


def _transpose_cast(x, *, name):
    R, C = x.shape
    tr, tc = _pick(R, (512, 256, 128)), _pick(C, (512, 256, 128))

    def body(x_ref, o_ref):
        o_ref[...] = x_ref[...].astype(F32).T.astype(BF16)

    return pl.pallas_call(
        body, name=name, grid=(R // tr, C // tc),
        in_specs=[pl.BlockSpec((tr, tc), lambda i, j: (i, j))],
        out_specs=pl.BlockSpec((tc, tr), lambda i, j: (j, i)),
        out_shape=jax.ShapeDtypeStruct((C, R), BF16),
        compiler_params=_params(("parallel", "parallel")),
    )(x)


_ANY = pl.BlockSpec(memory_space=pl.ANY)


def _place():
    return lax.axis_index("x"), lax.axis_index("y"), lax.axis_index("c")


def _slot(p):
    return 4 * p[0] + 2 * p[1] + p[2]


N_PEERS = N_DEV - 1


def _peers(me):
    flips = [(fx, fy, fc) for fx in (0, 1) for fy in (0, 1) for fc in (0, 1)][1:]
    return [tuple(1 - v if f else v for v, f in zip(me, flip)) for flip in flips]


def _peer_copy(src, lands, t, k, sender, to, send_sems, recv_sems):
    return pltpu.make_async_remote_copy(
        src_ref=src, dst_ref=lands[t].at[_slot(sender)], send_sem=send_sems.at[N_PEERS * t + k],
        recv_sem=recv_sems.at[N_PEERS * t + k], device_id=to, device_id_type=MESH)


def _exchange(chunked, whole, *, name):
    ride = _Ride(chunked, whole)
    n = ride.n

    def body(*refs):
        ride.start(refs[:n], refs[n:2 * n], refs[2 * n:])
        ride.forward(refs[:n], refs[n:2 * n], refs[2 * n:])
        ride.finish(refs[:n], refs[n:2 * n], refs[2 * n:])

    return pl.pallas_call(
        body,
        name=name,
        in_specs=[_ANY] * n,
        out_specs=[_ANY] * n,
        out_shape=ride.land_shapes(),
        scratch_shapes=ride.semaphores(),
        compiler_params=pltpu.CompilerParams(has_side_effects=True),
    )(*ride.arrays)


class _Ride:
    def __init__(self, chunked, whole):
        self.arrays = list(chunked) + list(whole)
        self.n, self.n_chunked = len(self.arrays), len(chunked)
        self.gather = self.n_chunked == 0

    def land_shapes(self):
        return [jax.ShapeDtypeStruct(a.shape if t < self.n_chunked else (N_DEV,) + a.shape, a.dtype)
                for t, a in enumerate(self.arrays)]

    def semaphores(self):
        sems = pltpu.SemaphoreType.DMA((N_PEERS * self.n,))
        return [sems, sems, pltpu.SemaphoreType.DMA((self.n,))]

    def _src(self, ins, t, dest):
        return ins[t].at[_slot(dest)] if t < self.n_chunked else ins[t]

    def _copies(self, ins, lands, sems):
        send_sems, recv_sems, local_sems = sems
        x, y, c = me = _place()
        if not self.gather:
            return [], [_peer_copy(self._src(ins, t, peer), lands, t, k, me, peer, send_sems, recv_sems)
                        for t in range(self.n) for k, peer in enumerate(_peers(me))]
        mine = [pltpu.make_async_copy(ins[t], lands[t].at[_slot(me)], local_sems.at[t]) for t in range(self.n)]
        first = []
        for t in range(self.n):
            first.append(self._hop(ins, lands, sems, t, 0, me, (x, y, 1 - c), ins[t]))
            first += [self._hop(ins, lands, sems, t, 1 + j, me, (*chip, c), ins[t]) for j, chip in enumerate(self._chips())]
        return mine, first

    def _chips(self):
        x, y, _ = _place()
        return [(1 - x, y), (x, 1 - y), (1 - x, 1 - y)]

    def _hop(self, ins, lands, sems, t, k, block, to, source=None):
        dst = lands[t].at[_slot(block)]
        return pltpu.make_async_remote_copy(
            src_ref=dst if source is None else source, dst_ref=dst, send_sem=sems[0].at[N_PEERS * t + k],
            recv_sem=sems[1].at[N_PEERS * t + k], device_id=to, device_id_type=MESH)

    def start(self, ins, lands, sems):
        mine, sends = self._copies(ins, lands, sems)
        for cp in mine + sends:
            cp.start()

    def _passed_on(self, ins, lands, sems):
        x, y, c = _place()
        return [self._hop(ins, lands, sems, t, 4 + j, (*chip, c), (x, y, 1 - c))
                for j, chip in enumerate(self._chips()) for t in range(self.n)]

    def forward(self, ins, lands, sems):
        if not self.gather:
            return
        x, y, c = me = _place()
        for j, chip in enumerate(self._chips()):
            for t in range(self.n):
                self._hop(ins, lands, sems, t, 1 + j, (*chip, c), me).wait_recv()
        for cp in self._passed_on(ins, lands, sems):
            cp.start()

    def finish(self, ins, lands, sems):
        mine, sends = self._copies(ins, lands, sems)
        x, y, c = me = _place()
        if self.gather:
            sibling = (x, y, 1 - c)
            sends = sends + self._passed_on(ins, lands, sems)
            for t in range(self.n):
                self._hop(ins, lands, sems, t, 0, sibling, me).wait_recv()
                for j, chip in enumerate(self._chips()):
                    self._hop(ins, lands, sems, t, 4 + j, (*chip, 1 - c), me).wait_recv()
        else:
            for t in range(self.n):
                for k, peer in enumerate(_peers(me)):
                    _peer_copy(self._src(ins, t, peer), lands, t, k, peer, peer, sems[0], sems[1]).wait_recv()
        for cp in sends:
            cp.wait_send()
        for cp in mine:
            cp.wait()


def _rope_tables(T):
    half = HEAD_DIM // 2
    inv_freq = ROPE_THETA ** (-jnp.arange(half, dtype=F32) / half)
    ang = jnp.arange(T, dtype=F32)[:, None] * inv_freq[None, :]
    cos, sin = jnp.cos(ang), jnp.sin(ang)
    return jnp.concatenate([cos, cos], axis=-1), jnp.concatenate([-sin, sin], axis=-1)


def _gain_table(q_gain, k_gain):
    return jnp.tile(jnp.concatenate([q_gain, k_gain], axis=0), (1, 4))[:, None, :]


def _sb_heads(w_in):
    n_in, d_model = w_in.shape
    return (n_in - N_DIL_BLOCKS * GROUP_W - 2 * d_model) // (3 * HEAD_DIM)


def _carry(rides, key, args, call):
    make = rides.get(key) if rides else None
    if make is None:
        return call(None)
    ride, on_landed = make(*args)
    res, lands = call(ride)
    on_landed(lands)
    return res


def _layer_fwd(x, p, cos, sin, tag, rides=None):
    sb_heads = _sb_heads(p["w_in"])
    n_sb_blocks = 3 * sb_heads * HEAD_DIM // GROUP_W
    s = {"x": x}
    s["h"] = _rmsnorm_fwd(x, p["norm1"], name=f"norm1_fwd{tag}")
    s["proj"] = _carry(rides, "proj_fwd", (), lambda ride: _matmul(
        s["h"], p["w_in"], mode="nt", out_dtype=F32, name=f"proj_fwd{tag}", ride=ride))
    *s["qkv_d"], s["qkv_s"] = _prep_fwd(s["proj"], p["gains"], cos, sin, n_sb_blocks, name=f"prep_fwd{tag}")
    outs = [_dil_fwd(s["qkv_d"][g], g, name=f"dil{g}_fwd{tag}") for g in range(N_GROUPS)]
    s["o"], s["ld"] = [o for o, _ in outs], [ld for _, ld in outs]
    s["y_dil"] = _merge_fwd(s["o"], s["ld"], name=f"merge_fwd{tag}")
    s["y_sb32"], s["y_sb"] = _sb_fwd(s["qkv_s"], sb_heads, 0, name=f"sb_fwd{tag}")
    s["mixed"] = _gate_fwd(s["y_dil"], s["y_sb"], p["w_up_dil"], p["w_up_sb"], s["proj"], p["gate_b"],
                           (N_DIL_BLOCKS + n_sb_blocks) * GROUP_W, name=f"gate_fwd{tag}")
    s["x1"] = _carry(rides, "out_fwd", (), lambda ride: _matmul(
        s["mixed"], p["w_out"], mode="nn", out_dtype=F32, epilogue="add", extra=x, name=f"out_fwd{tag}", ride=ride))
    s["h2"] = _rmsnorm_fwd(s["x1"], p["norm2"], name=f"norm2_fwd{tag}")
    s["f"], s["a"] = _carry(rides, "ff1_fwd", (), lambda ride: _matmul(
        s["h2"], p["w_ff1"], mode="nt", out_dtype=BF16, epilogue="relu2", name=f"ff1_fwd{tag}", ride=ride))
    x2 = _carry(rides, "ff2_fwd", (), lambda ride: _matmul(
        s["a"], p["w_ff2"], mode="nn", out_dtype=F32, epilogue="add", extra=s["x1"], name=f"ff2_fwd{tag}", ride=ride))
    return x2, s


def _layer_bwd(dx2, dx2_b, p, s, cos, sin, tag, rides=None, done=None):
    sb_heads = _sb_heads(p["w_in"])
    gate_col0 = N_DIL_BLOCKS * GROUP_W + 3 * sb_heads * HEAD_DIM
    g = {}
    df = _carry(rides, "ff2_bwd", (g, done), lambda ride: _matmul(
        dx2_b, p["w_ff2"], mode="nt", out_dtype=BF16, epilogue="relu2_bwd", extra=s["f"], name=f"ff2_bwd{tag}", ride=ride))
    g["w_ff2"] = _matmul(s["a"], dx2_b, mode="tn", out_dtype=BF16, name=f"ff2_wgrad{tag}")
    dh2 = _carry(rides, "ff1_bwd", (g, done), lambda ride: _matmul(
        df, p["w_ff1"], mode="nn", out_dtype=F32, name=f"ff1_bwd{tag}", ride=ride))
    g["w_ff1"] = _matmul(df, s["h2"], mode="tn", out_dtype=BF16, name=f"ff1_wgrad{tag}")
    dx1, dx1_b, g["norm2"] = _rmsnorm_bwd(s["x1"], p["norm2"], dh2, dx2, name=f"norm2_bwd{tag}")
    dmixed = _matmul(dx1_b, p["w_out"], mode="nt", out_dtype=F32, name=f"out_bwd{tag}")
    g["w_out"] = _matmul(s["mixed"], dx1_b, mode="tn", out_dtype=BF16, name=f"out_wgrad{tag}")
    gate_b = p["gate_b"]
    dup_dil, dgp0, db0 = _gate_bwd(s["y_dil"], p["w_up_dil"], s["proj"], gate_b, dmixed, 0, gate_col0, name=f"gate0_bwd{tag}")
    dup_sb, dgp1, db1 = _gate_bwd(s["y_sb"], p["w_up_sb"], s["proj"], gate_b, dmixed, 1, gate_col0, name=f"gate1_bwd{tag}")
    g["gate_b"] = jnp.concatenate([db0, db1], axis=0)
    dy_dil = _matmul(dup_dil, p["w_up_dil"], mode="nn", out_dtype=F32, name=f"updil_bwd{tag}")
    g["w_up_dil"] = _matmul(dup_dil, s["y_dil"], mode="tn", out_dtype=BF16, name=f"updil_wgrad{tag}")
    dy_sb = _matmul(dup_sb, p["w_up_sb"], mode="nn", out_dtype=BF16, name=f"upsb_bwd{tag}")
    g["w_up_sb"] = _matmul(dup_sb, s["y_sb"], mode="tn", out_dtype=BF16, name=f"upsb_wgrad{tag}")
    dos, dterms = _merge_bwd(s["o"], s["ld"], dy_dil, name=f"merge_bwd{tag}")
    dqkv = [_dil_bwd(s["qkv_d"][grp], dos[grp], s["ld"][grp], dterms[grp], grp, name=f"dil{grp}_bwd{tag}")
            for grp in range(N_GROUPS)]
    dproj_d, dgain = _prep_bwd(s["proj"], dqkv, p["gains"], cos, sin, name=f"prep_bwd{tag}")
    g["q_gain"], g["k_gain"] = dgain[:N_GROUPS, 0], dgain[N_GROUPS:, 0]
    dq_s, dk_s, dv_s = _sb_bwd(s["qkv_s"], s["y_sb32"], dy_sb, sb_heads, 0, name=f"sb_bwd{tag}")
    dproj = jnp.concatenate([dproj_d, dq_s, dk_s, dv_s, dgp0, dgp1], axis=1)
    g["w_in"] = _carry(rides, "proj_wgrad", (g, done), lambda ride: _matmul(
        dproj, s["h"], mode="tn", out_dtype=BF16, name=f"proj_wgrad{tag}", ride=ride,
        tn=_pick(s["h"].shape[1], (2048, 1024, 512, 256, 128)), tk=_pick(s["h"].shape[0], (1024, 512))))
    dh = _carry(rides, "proj_bwd", (g, done), lambda ride: _matmul(
        dproj, p["w_in"], mode="nn", out_dtype=F32, name=f"proj_bwd{tag}", ride=ride))
    dx, dx_b, g["norm1"] = _rmsnorm_bwd(s["x"], p["norm1"], dh, dx1, name=f"norm1_bwd{tag}")
    return dx, dx_b, g


def _local_step(x, target, layers, fwd_rides=None, bwd_rides=None):
    depth = len(layers)
    fwd_rides, bwd_rides = (r or [None] * depth for r in (fwd_rides, bwd_rides))
    cos, sin = _rope_tables(x.shape[0])
    saved = []
    for l, p in enumerate(layers):
        x, s = _layer_fwd(x, p, cos, sin, f"_l{l}", fwd_rides[l])
        saved.append(s)
    dx, dx_b, loss = _loss_head(x, target, name="loss_head")
    grads = [None] * depth
    for l in reversed(range(depth)):
        dx, dx_b, grads[l] = _layer_bwd(dx, dx_b, layers[l], saved[l], cos, sin, f"_l{l}", bwd_rides[l], grads)
    return loss, dx, grads


_MATRICES = ("w_in", "w_up_dil", "w_up_sb", "w_out", "w_ff1", "w_ff2")
_TRANSPOSED = ("w_in", "w_up_dil", "w_up_sb", "w_ff1")
_SMALL = ("norm1_g", "norm2_g", "q_norm_g", "k_norm_g")


def _unshard(blocks, name):
    if name == "gate_b":
        return jnp.transpose(blocks, (1, 0, 2)).reshape(blocks.shape[1], N_DEV * blocks.shape[2])
    return blocks.reshape(N_DEV * blocks.shape[1], blocks.shape[2])


def _to_chunks(full, name):
    if name == "gate_b":
        r, cols = full.shape
        return jnp.transpose(full.reshape(r, N_DEV, cols // N_DEV), (1, 0, 2))
    return full.reshape(N_DEV, full.shape[0] // N_DEV, full.shape[1])


def _lane_pad(n):
    return -n % HEAD_DIM


def _pack_small(norm1, norm2, qg, kg):
    flat = jnp.concatenate([t.reshape(-1, HEAD_DIM) for t in (norm1, norm2, qg, kg)], axis=0)
    return jnp.pad(flat, ((0, -flat.shape[0] % 8), (0, 0)))


def _unpack_small(packed, shapes):
    out, row = [], 0
    for shape in shapes:
        rows = math.prod(shape) // HEAD_DIM
        out.append(packed[row:row + rows].reshape(shape))
        row += rows
    return out


def kernel(x, norm1_g, w_in, q_norm_g, k_norm_g, w_up_dil, w_up_sb, gate_b, w_out, norm2_g, w_ff1, w_ff2, loss_target, m_norm1_g, m_w_in, m_q_norm_g, m_k_norm_g, m_w_up_dil, m_w_up_sb, m_gate_b, m_w_out, m_norm2_g, m_w_ff1, m_w_ff2, v_norm1_g, v_w_in, v_q_norm_g, v_k_norm_g, v_w_up_dil, v_w_up_sb, v_gate_b, v_w_out, v_norm2_g, v_w_ff1, v_w_ff2):
    names = ("norm1_g", "w_in", "q_norm_g", "k_norm_g", "w_up_dil", "w_up_sb", "gate_b", "w_out", "norm2_g", "w_ff1", "w_ff2")
    w = dict(zip(names, (norm1_g, w_in, q_norm_g, k_norm_g, w_up_dil, w_up_sb, gate_b, w_out, norm2_g, w_ff1, w_ff2)))
    m = dict(zip(names, (m_norm1_g, m_w_in, m_q_norm_g, m_k_norm_g, m_w_up_dil, m_w_up_sb, m_gate_b, m_w_out, m_norm2_g, m_w_ff1, m_w_ff2)))
    v = dict(zip(names, (v_norm1_g, v_w_in, v_q_norm_g, v_k_norm_g, v_w_up_dil, v_w_up_sb, v_gate_b, v_w_out, v_norm2_g, v_w_ff1, v_w_ff2)))
    depth = norm1_g.shape[0]
    assert depth == 2, "the exchange schedule below is written for two layers"
    sharded = _MATRICES + ("gate_b",)

    def shards(layer, which):
        out = []
        for n in which:
            shard = w[n][layer]
            if n in _TRANSPOSED:
                cols = shard.shape[1]
                padded = jnp.pad(shard, ((0, 0), (0, _lane_pad(cols))))
                out.append(_transpose_cast(padded, name=f"shard_t_{n}_l{layer}")[:cols])
            else:
                out.append(shard if n == "gate_b" else shard.astype(BF16))
        return out

    assert depth == 2, "the schedule of exchanges below is written for two layers"
    layers = [{"norm1": norm1_g[l][None], "norm2": norm2_g[l][None], "gains": _gain_table(q_norm_g[l], k_norm_g[l])}
              for l in range(depth)]
    rest = sharded[1:]

    def gather_on(layer, which):
        def make():
            ride = _Ride([], shards(layer, which))
            return ride, lambda lands: layers[layer].update({n: _unshard(b, n) for n, b in zip(which, lands)})
        return make

    (w_in_0,) = _exchange([], shards(0, sharded[:1]), name="gather_w_in_l0")
    layers[0]["w_in"] = _unshard(w_in_0, "w_in")
    fwd_rides = [{"proj_fwd": gather_on(0, rest), "out_fwd": gather_on(1, ("w_up_dil", "w_up_sb", "w_out", "gate_b")),
                  "ff1_fwd": gather_on(1, ("w_in",)), "ff2_fwd": gather_on(1, ("w_ff1", "w_ff2"))}, None]

    sent, landed = {}, {}

    def exchange_on(items):
        def make(g, done):
            chunks = [_to_chunks((g if done[layer] is None else done[layer])[n], n) for layer, n in items]

            def on_landed(lands):
                for item, chunk, land in zip(items, chunks, lands):
                    sent[item], landed[item] = chunk, land
            return _Ride(chunks, []), on_landed
        return make

    others = ("w_ff1", "w_out", "w_up_dil", "w_up_sb", "gate_b")
    bwd_rides = [{"ff1_bwd": exchange_on([(l, "w_ff2")]), "proj_wgrad": exchange_on([(l, n) for n in others]),
                  "proj_bwd": exchange_on([(l, "w_in")])} for l in range(depth)]
    loss_part, dx, grads = _local_step(x[0], loss_target[0], layers, fwd_rides, bwd_rides)
    loss = lax.psum(loss_part[0, 0], ("x", "y", "c"))

    small = _pack_small(jnp.concatenate([g["norm1"] for g in grads]), jnp.concatenate([g["norm2"] for g in grads]),
                        jnp.stack([g["q_gain"] for g in grads]), jnp.stack([g["k_gain"] for g in grads]))
    (small_parts,) = _exchange([], [small], name="exchange_small")

    out = {}
    for n in sharded:
        rows, cols = depth * w[n].shape[1], w[n].shape[2]
        own, parts = ([moved[(l, n)] for l in range(depth)] for moved in (sent, landed))
        state = [t.reshape(rows, cols) for t in (w[n], m[n], v[n])]
        if n in _TRANSPOSED:
            pad = _lane_pad(cols)
            res = _reduce_adamw_t(parts, own, *(jnp.pad(t, ((0, 0), (0, pad))) for t in state), name=f"adamw_{n}")
            res = [t[:, :cols] for t in res]
        else:
            if w[n].shape[1] % 8:
                own, parts = [jnp.concatenate(own, axis=1)], [jnp.concatenate(parts, axis=1)]
            res = _reduce_adamw(parts, own, *state, name=f"adamw_{n}")
        out[n] = [t.reshape(w[n].shape) for t in res]
    small_res = _reduce_adamw([small_parts], [small], _pack_small(*(w[n] for n in _SMALL)),
                              _pack_small(*(m[n] for n in _SMALL)), _pack_small(*(v[n] for n in _SMALL)),
                              own_chunked=False, name="adamw_small")
    small_shapes = [w[n].shape for n in _SMALL]
    for k, t in enumerate(small_res):
        for n, arr in zip(_SMALL, _unpack_small(t, small_shapes)):
            out.setdefault(n, [None] * 4)[k] = arr
    return (loss, dx[None], *(out[n][0] for n in names), *(out[n][1] for n in names), *(out[n][2] for n in names),
            *(out[n][3] for n in names))
```
